```python
import jax, jax.numpy as jnp
from jax import lax
import numpy as np

D_MODEL = 1024
BATCH = 8
SEQ = 4096
DEPTH = 1

HEAD_DIM = 64
ROPE_THETA = 10000.0
BLOCK = 128
RMS_EPS = 1e-6
DIL_GROUPS = ((128, 1), (512, 4), (2048, 16))
N_DIL_GROUPS = len(DIL_GROUPS)
A_HEADS = D_MODEL // (2 * HEAD_DIM)
A_GROUP_WIDTH = A_HEADS * HEAD_DIM
A_QKV_WIDTH = 3 * N_DIL_GROUPS * A_GROUP_WIDTH
B_Q_HEADS = D_MODEL // HEAD_DIM
B_KV_HEADS = max(1, B_Q_HEADS // 8)
B_GROUP = B_Q_HEADS // B_KV_HEADS
B_WINDOW = 128
B_Q_WIDTH = B_Q_HEADS * HEAD_DIM
B_KV_WIDTH = B_KV_HEADS * HEAD_DIM
B_QKV_WIDTH = B_Q_WIDTH + 2 * B_KV_WIDTH
N_BRANCHES = 2
GATE_WIDTH = N_BRANCHES * D_MODEL
IN_WIDTH = A_QKV_WIDTH + B_QKV_WIDTH + GATE_WIDTH
N_EXPERT_GROUPS = 4
EXPERTS_PER_GROUP = 8
N_EXPERTS = N_EXPERT_GROUPS * EXPERTS_PER_GROUP
TOP_K_IN_GROUP = 2
D_EXPERT = D_MODEL // 4

kernel_name = "hybrid_dilated_swa_sink_hmoe"


def rmsnorm(x, g):
    xf = x.astype(jnp.float32)
    y = xf * lax.rsqrt(jnp.mean(xf * xf, axis=-1, keepdims=True) + RMS_EPS)
    return (y * g.astype(jnp.float32)).astype(x.dtype)


def rope(t, cos, sin):
    half = t.shape[-1] // 2
    tf = t.astype(jnp.float32)
    t1, t2 = tf[..., :half], tf[..., half:]
    c, s = cos[:, None, :], sin[:, None, :]
    return jnp.concatenate([t1 * c - t2 * s, t2 * c + t1 * s], axis=-1).astype(t.dtype)


def banded_attention(q, k, v, max_dist, sink=None):
    n, hk, g, L, hd = q.shape
    nb = -(-L // BLOCK)
    lp = nb * BLOCK
    qb = jnp.pad(q, ((0, 0), (0, 0), (0, 0), (0, lp - L), (0, 0))).reshape(n, hk, g, nb, BLOCK, hd)
    kv_pad = ((0, 0), (0, 0), (BLOCK, lp - L), (0, 0))
    kp = jnp.pad(k, kv_pad).reshape(n, hk, nb + 1, BLOCK, hd)
    vp = jnp.pad(v, kv_pad).reshape(n, hk, nb + 1, BLOCK, hd)
    kb = jnp.concatenate([kp[:, :, :-1], kp[:, :, 1:]], axis=3)
    vb = jnp.concatenate([vp[:, :, :-1], vp[:, :, 1:]], axis=3)
    s = jnp.einsum('nhgiqd,nhikd->nhgiqk', qb, kb).astype(jnp.float32) * (hd ** -0.5)
    r = jnp.arange(BLOCK)[:, None]
    c = jnp.arange(2 * BLOCK)[None, :]
    dist = r - c + BLOCK
    kpos = jnp.arange(nb)[:, None, None] * BLOCK - BLOCK + c[None]
    mask = (dist >= 0)[None] & (dist <= max_dist)[None] & (kpos >= 0)
    s = jnp.where(mask, s, -jnp.inf)
    m = jnp.max(s, axis=-1)
    if sink is not None:
        sink_b = sink.astype(jnp.float32)[None, :, :, None, None]
        m = jnp.maximum(m, sink_b)
    p = jnp.exp(s - m[..., None])
    denom = jnp.sum(p, axis=-1)
    if sink is not None:
        denom = denom + jnp.exp(sink_b - m)
    o = jnp.einsum('nhgiqk,nhikd->nhgiqd', p.astype(v.dtype), vb).astype(jnp.float32) / denom[..., None]
    lse = m + jnp.log(denom)
    o = o.astype(q.dtype).reshape(n, hk, g, lp, hd)[:, :, :, :L]
    lse = lse.reshape(n, hk, g, lp)[:, :, :, :L]
    return o, lse


def dilated_group(q, k, v, window, dilation):
    b, s, h, hd = q.shape
    ls = s // dilation

    def to_sub(t):
        return t.reshape(b, ls, dilation, h, hd).transpose(0, 2, 3, 1, 4).reshape(b * dilation, h, ls, hd)

    o, lse = banded_attention(to_sub(q)[:, :, None], to_sub(k), to_sub(v), window // dilation)
    o = o[:, :, 0].reshape(b, dilation, h, ls, hd).transpose(0, 3, 1, 2, 4).reshape(b, s, h, hd)
    lse = lse[:, :, 0].reshape(b, dilation, h, ls).transpose(0, 3, 1, 2).reshape(b, s, h)
    return o, lse


def hier_moe(t, w_rg, b_rg, w_re, b_re, w_eg, w_eu, w_ed):
    n_tok = t.shape[0]
    lg = (t @ w_rg).astype(jnp.float32) + b_rg.astype(jnp.float32)
    pg = jax.nn.softmax(lg, axis=-1)
    gsel = jnp.argmax(lg, axis=-1)
    pg_sel = jnp.take_along_axis(pg, gsel[:, None], axis=-1)[:, 0]
    le = ((t @ w_re).astype(jnp.float32) + b_re.astype(jnp.float32)).reshape(n_tok, N_EXPERT_GROUPS, EXPERTS_PER_GROUP)
    le_sel = jnp.take_along_axis(le, gsel[:, None, None], axis=1)[:, 0]
    pe = jax.nn.softmax(le_sel, axis=-1)
    top_p, top_i = lax.top_k(pe, TOP_K_IN_GROUP)
    wts = pg_sel[:, None] * top_p / jnp.sum(top_p, axis=-1, keepdims=True)
    eidx = gsel[:, None] * EXPERTS_PER_GROUP + top_i
    combine = jnp.sum(jax.nn.one_hot(eidx, N_EXPERTS, dtype=jnp.float32) * wts[..., None], axis=1)
    y = jnp.zeros((n_tok, t.shape[1]), jnp.float32)
    for gi in range(N_EXPERT_GROUPS):
        sl = slice(gi * EXPERTS_PER_GROUP, (gi + 1) * EXPERTS_PER_GROUP)
        hg = jnp.einsum('td,edf->etf', t, w_eg[sl])
        hu = jnp.einsum('td,edf->etf', t, w_eu[sl])
        a = jax.nn.silu(hg) * hu * combine[:, sl].T[:, :, None].astype(t.dtype)
        y = y + jnp.einsum('etf,efd->td', a, w_ed[sl]).astype(jnp.float32)
    return y.astype(t.dtype)


def hybrid_layer(x, cos, sin, w_in, b_in, sinks, w_proj_a, w_proj_b, w_out, g_mix, g_ffn,
                 w_rg, b_rg, w_re, b_re, w_eg, w_eu, w_ed):
    b, s, d = x.shape
    h = rmsnorm(x, g_mix)
    z = h @ w_in + b_in
    za = z[..., :A_QKV_WIDTH]
    zb = z[..., A_QKV_WIDTH:A_QKV_WIDTH + B_QKV_WIDTH]
    zg = z[..., A_QKV_WIDTH + B_QKV_WIDTH:]

    za = za.reshape(b, s, 3, N_DIL_GROUPS, A_HEADS, HEAD_DIM)
    outs, lses = [], []
    for gi, (window, dil) in enumerate(DIL_GROUPS):
        qa = rope(za[:, :, 0, gi], cos, sin)
        ka = rope(za[:, :, 1, gi], cos, sin)
        o, l = dilated_group(qa, ka, za[:, :, 2, gi], window, dil)
        outs.append(o)
        lses.append(l)
    wgt = jax.nn.softmax(jnp.stack(lses), axis=0)
    ya = jnp.einsum('gbsh,gbshd->bshd', wgt.astype(x.dtype), jnp.stack(outs)).reshape(b, s, A_GROUP_WIDTH)

    qb = rope(zb[..., :B_Q_WIDTH].reshape(b, s, B_Q_HEADS, HEAD_DIM), cos, sin)
    kb = rope(zb[..., B_Q_WIDTH:B_Q_WIDTH + B_KV_WIDTH].reshape(b, s, B_KV_HEADS, HEAD_DIM), cos, sin)
    vb = zb[..., B_Q_WIDTH + B_KV_WIDTH:].reshape(b, s, B_KV_HEADS, HEAD_DIM)
    qb = qb.reshape(b, s, B_KV_HEADS, B_GROUP, HEAD_DIM).transpose(0, 2, 3, 1, 4)
    ob, _ = banded_attention(qb, kb.transpose(0, 2, 1, 3), vb.transpose(0, 2, 1, 3),
                             B_WINDOW - 1, sinks.reshape(B_KV_HEADS, B_GROUP))
    yb = ob.transpose(0, 3, 1, 2, 4).reshape(b, s, B_Q_WIDTH)

    gates = jax.nn.sigmoid(zg.reshape(b, s, N_BRANCHES, d))
    merged = gates[:, :, 0] * (ya @ w_proj_a) + gates[:, :, 1] * (yb @ w_proj_b)
    x = x + merged @ w_out

    h2 = rmsnorm(x, g_ffn)
    x = x + hier_moe(h2.reshape(b * s, d), w_rg, b_rg, w_re, b_re, w_eg, w_eu, w_ed).reshape(b, s, d)
    return x


def setup_inputs(seed: int = 0) -> dict:
    key = jax.random.key(seed)
    ks = jax.random.split(key, 18)
    f32 = jnp.float32
    nrm = lambda k, shape, scale: jax.random.normal(k, shape, f32) * scale
    return {
        "x": nrm(ks[0], (BATCH, SEQ, D_MODEL), 1.0),
        "w_in": nrm(ks[1], (DEPTH, D_MODEL, IN_WIDTH), D_MODEL ** -0.5),
        "b_in": nrm(ks[2], (DEPTH, IN_WIDTH), 0.02),
        "sinks": nrm(ks[3], (DEPTH, B_Q_HEADS), 0.5),
        "w_proj_a": nrm(ks[4], (DEPTH, A_GROUP_WIDTH, D_MODEL), A_GROUP_WIDTH ** -0.5),
        "w_proj_b": nrm(ks[5], (DEPTH, B_Q_WIDTH, D_MODEL), B_Q_WIDTH ** -0.5),
        "w_out": nrm(ks[6], (DEPTH, D_MODEL, D_MODEL), D_MODEL ** -0.5),
        "g_mix": 1.0 + nrm(ks[7], (DEPTH, D_MODEL), 0.05),
        "g_ffn": 1.0 + nrm(ks[8], (DEPTH, D_MODEL), 0.05),
        "w_router_group": nrm(ks[9], (DEPTH, D_MODEL, N_EXPERT_GROUPS), D_MODEL ** -0.5),
        "b_router_group": nrm(ks[10], (DEPTH, N_EXPERT_GROUPS), 0.01),
        "w_router_expert": nrm(ks[11], (DEPTH, D_MODEL, N_EXPERTS), D_MODEL ** -0.5),
        "b_router_expert": nrm(ks[12], (DEPTH, N_EXPERTS), 0.01),
        "w_exp_gate": nrm(ks[13], (DEPTH, N_EXPERTS, D_MODEL, D_EXPERT), D_MODEL ** -0.5),
        "w_exp_up": nrm(ks[14], (DEPTH, N_EXPERTS, D_MODEL, D_EXPERT), D_MODEL ** -0.5),
        "w_exp_down": nrm(ks[15], (DEPTH, N_EXPERTS, D_EXPERT, D_MODEL), D_EXPERT ** -0.5),
        "g_final": 1.0 + nrm(ks[16], (D_MODEL,), 0.05),
    }


def reference(x, w_in, b_in, sinks, w_proj_a, w_proj_b, w_out, g_mix, g_ffn,
              w_router_group, b_router_group, w_router_expert, b_router_expert,
              w_exp_gate, w_exp_up, w_exp_down, g_final):
    s = x.shape[1]
    pos = jnp.arange(s, dtype=jnp.float32)
    inv_freq = ROPE_THETA ** (-jnp.arange(0, HEAD_DIM, 2, dtype=jnp.float32) / HEAD_DIM)
    ang = pos[:, None] * inv_freq[None, :]
    cos, sin = jnp.cos(ang), jnp.sin(ang)
    for layer in range(DEPTH):
        x = hybrid_layer(x, cos, sin, w_in[layer], b_in[layer], sinks[layer], w_proj_a[layer],
                         w_proj_b[layer], w_out[layer], g_mix[layer], g_ffn[layer],
                         w_router_group[layer], b_router_group[layer], w_router_expert[layer],
                         b_router_expert[layer], w_exp_gate[layer], w_exp_up[layer], w_exp_down[layer])
    return rmsnorm(x, g_final)
```

```python
import functools

import jax
import jax.numpy as jnp
from jax import lax
from jax.experimental import pallas as pl
from jax.experimental.pallas import tpu as pltpu

F32 = jnp.float32
BF16 = jnp.bfloat16

D_MODEL = 1024
HEAD_DIM = 64
HALF = HEAD_DIM // 2
ROPE_THETA = 10000.0
RMS_EPS = 1e-6
BLOCK = 128
DIL_GROUPS = ((128, 1), (512, 4), (2048, 16))
N_DIL = len(DIL_GROUPS)
A_GROUP_WIDTH = 512
A_QKV_WIDTH = 3 * N_DIL * A_GROUP_WIDTH
B_Q_HEADS = 16
B_KV_HEADS = 2
B_Q_WIDTH = B_Q_HEADS * HEAD_DIM
B_WINDOW = 128
GATE_WIDTH = 2 * D_MODEL
IN_WIDTH = A_QKV_WIDTH + B_Q_WIDTH + 2 * B_KV_HEADS * HEAD_DIM + GATE_WIDTH
N_EXPERT_GROUPS = 4
EXPERTS_PER_GROUP = 8
N_EXPERTS = N_EXPERT_GROUPS * EXPERTS_PER_GROUP
D_EXPERT = D_MODEL // 4

CB = 256
PAIR = 128
N_IN_BLOCKS = IN_WIDTH // CB
ZB_GATE = 0
ZB_QB = 8
ZB_QA = 12
ZB_KA = 18
ZB_VA = 24
ZB_KB = 30
ZB_VB = 31
N_Z_BLOCKS = 32
ROUTER_LANES = 128
EXPERT_LANE0 = N_EXPERT_GROUPS

VMEM_LIMIT = 56 * 1024 * 1024


def _in_proj_plan():
    plan = []
    for c in range(N_IN_BLOCKS):
        col = c * CB
        if col < A_QKV_WIDTH:
            part, rem = divmod(col, N_DIL * A_GROUP_WIDTH)
            dst = (ZB_QA, ZB_KA, ZB_VA)[part] + rem // CB
            kind = ("q", "k", "v")[part]
        elif col < A_QKV_WIDTH + B_Q_WIDTH:
            dst = ZB_QB + (col - A_QKV_WIDTH) // CB
            kind = "q"
        elif col < A_QKV_WIDTH + B_Q_WIDTH + CB:
            dst = -1
            kind = "kvb"
        else:
            dst = ZB_GATE + (col - (A_QKV_WIDTH + B_Q_WIDTH + CB)) // CB
            kind = "v"
        plan.append((dst, kind))
    return tuple(plan)


def _rope(acc, cos, sin_signed, first_half):
    partner = jnp.where(first_half, pltpu.roll(acc, CB - HALF, 1), pltpu.roll(acc, HALF, 1))
    return acc * cos + partner * sin_signed


def _in_proj_kernel(x_ref, g_ref, w_ref, b_ref, cos_ref, sin_ref, z_ref, *, plan):
    x = x_ref[...]
    h = x * lax.rsqrt(jnp.mean(x * x, axis=-1, keepdims=True) + RMS_EPS) * g_ref[...]
    h = h.astype(BF16)
    cos = cos_ref[...]
    sin_signed = sin_ref[...]
    lane = lax.broadcasted_iota(jnp.int32, cos.shape, 1)
    first_half = (lane % HEAD_DIM) < HALF
    for c, (dst, kind) in enumerate(plan):
        cols = slice(c * CB, (c + 1) * CB)
        acc = jnp.dot(h, w_ref[:, cols], preferred_element_type=F32) + b_ref[:, cols]
        if kind == "q":
            z_ref[dst] = (_rope(acc, cos, sin_signed, first_half) * (HEAD_DIM ** -0.5)).astype(BF16)
        elif kind == "k":
            z_ref[dst] = _rope(acc, cos, sin_signed, first_half).astype(BF16)
        elif kind == "v":
            z_ref[dst] = acc.astype(BF16)
        else:
            kr = _rope(acc, cos, sin_signed, first_half)
            r64 = pltpu.roll(kr, HEAD_DIM, 1)
            r128 = pltpu.roll(kr, 2 * HEAD_DIM, 1)
            kdup = jnp.where(lane < HEAD_DIM, kr, jnp.where(lane < 3 * HEAD_DIM, r64, r128))
            a128 = pltpu.roll(acc, 2 * HEAD_DIM, 1)
            a192 = pltpu.roll(acc, 3 * HEAD_DIM, 1)
            vdup = jnp.where(lane < HEAD_DIM, a128, jnp.where(lane < 3 * HEAD_DIM, a192, acc))
            z_ref[ZB_KB] = kdup.astype(BF16)
            z_ref[ZB_VB] = vdup.astype(BF16)


def _in_proj(x2, g_mix, w_in, b_in, cos_t, sin_t, *, seq, tm):
    n_tok = x2.shape[0]
    tiles_per_seq = seq // tm
    const = lambda i: (0, 0)
    return pl.pallas_call(
        functools.partial(_in_proj_kernel, plan=_in_proj_plan()),
        grid=(n_tok // tm,),
        in_specs=[
            pl.BlockSpec((tm, D_MODEL), lambda i: (i, 0)),
            pl.BlockSpec((1, D_MODEL), const),
            pl.BlockSpec((D_MODEL, IN_WIDTH), const, pipeline_mode=pl.Buffered(1)),
            pl.BlockSpec((1, IN_WIDTH), const),
            pl.BlockSpec((tm, CB), lambda i: (i % tiles_per_seq, 0)),
            pl.BlockSpec((tm, CB), lambda i: (i % tiles_per_seq, 0)),
        ],
        out_specs=pl.BlockSpec((N_Z_BLOCKS, tm, CB), lambda i: (0, i, 0)),
        out_shape=jax.ShapeDtypeStruct((N_Z_BLOCKS, n_tok, CB), BF16),
        compiler_params=pltpu.CompilerParams(
            dimension_semantics=("arbitrary",), vmem_limit_bytes=VMEM_LIMIT),
        name="in_proj",
    )(x2, g_mix, w_in, b_in, cos_t, sin_t)


def _attn_kernel(*refs, lq, max_dist, kv_shared, has_sink, want_lse, heads_per_step, seq_axis):
    refs = list(refs)
    sink_ref = refs.pop(0) if has_sink else None
    q_ref, k_ref, v_ref, kp_ref, vp_ref = refs[:5]
    o_ref = refs[5]
    lse_ref = refs[6] if want_lse else None

    row = lax.broadcasted_iota(jnp.int32, (BLOCK, 2 * BLOCK), 0)
    col = lax.broadcasted_iota(jnp.int32, (BLOCK, 2 * BLOCK), 1)
    dist = row - col + BLOCK
    valid = (dist >= 0) & (dist <= max_dist)
    neg_inf = jnp.float32(-jnp.inf)
    bias = jnp.where(valid, 0.0, neg_inf)
    bias_first = jnp.where(valid & (col >= BLOCK), 0.0, neg_inf)
    bias0 = jnp.where(pl.program_id(seq_axis) == 0, bias_first, bias)
    lane_lo = lax.broadcasted_iota(jnp.int32, (BLOCK, PAIR), 1) < HEAD_DIM

    for ib in range(lq // BLOCK):
        rows = slice(ib * BLOCK, (ib + 1) * BLOCK)
        b_ib = bias0 if ib == 0 else bias
        for pp in range(2):
            qcols = slice(pp * PAIR, (pp + 1) * PAIR)
            kcols = slice(0, PAIR) if kv_shared else qcols
            if ib == 0:
                k_win = jnp.concatenate([kp_ref[:, kcols], k_ref[0:BLOCK, kcols]], axis=0)
                v_win = jnp.concatenate([vp_ref[:, kcols], v_ref[0:BLOCK, kcols]], axis=0)
            else:
                win = slice((ib - 1) * BLOCK, (ib + 1) * BLOCK)
                k_win = k_ref[win, kcols]
                v_win = v_ref[win, kcols]
            q_pair = q_ref[rows, qcols]
            outs, lses = [], []
            for hh in range(2):
                q_h = jnp.where(lane_lo == (hh == 0), q_pair, jnp.zeros_like(q_pair))
                s = lax.dot_general(q_h, k_win, (((1,), (1,)), ((), ())), preferred_element_type=F32)
                s = s + b_ib
                m = jnp.max(s, axis=-1, keepdims=True)
                if has_sink:
                    sink = sink_ref[pl.program_id(1) * heads_per_step + pp * 2 + hh]
                    m = jnp.maximum(m, sink)
                p = jnp.exp(s - m)
                den = jnp.sum(p, axis=-1, keepdims=True)
                if has_sink:
                    den = den + jnp.exp(sink - m)
                o_h = jnp.dot(p.astype(BF16), v_win, preferred_element_type=F32)
                outs.append(o_h * (1.0 / den))
                if want_lse:
                    lses.append(jnp.broadcast_to(m + jnp.log(den), (BLOCK, PAIR)))
            o_ref[rows, qcols] = jnp.where(lane_lo, outs[0], outs[1]).astype(BF16)
            if want_lse:
                lse_ref[rows, qcols] = jnp.where(lane_lo, lses[0], lses[1])


def _dilated_attention(z3, group, *, batch, seq, lq):
    window, dil = DIL_GROUPS[group]
    sub_len = seq // dil
    lq = min(lq, sub_len)
    z4 = z3.reshape(N_Z_BLOCKS, batch, sub_len, dil * CB)
    qb, kb, vb = ZB_QA + 2 * group, ZB_KA + 2 * group, ZB_VA + 2 * group
    bpq = lq // BLOCK
    cur = lambda base: pl.BlockSpec((None, None, lq, CB), lambda b, r, jj, i: (base + jj, b, i, r))
    prev = lambda base: pl.BlockSpec(
        (None, None, BLOCK, CB), lambda b, r, jj, i: (base + jj, b, jnp.maximum(i * bpq - 1, 0), r))
    out_spec = pl.BlockSpec((None, None, lq, CB), lambda b, r, jj, i: (jj, b, i, r))
    o, lse = pl.pallas_call(
        functools.partial(_attn_kernel, lq=lq, max_dist=window // dil, kv_shared=False,
                          has_sink=False, want_lse=True, heads_per_step=4, seq_axis=3),
        grid=(batch, dil, 2, sub_len // lq),
        in_specs=[cur(qb), cur(kb), cur(vb), prev(kb), prev(vb)],
        out_specs=[out_spec, out_spec],
        out_shape=[jax.ShapeDtypeStruct((2, batch, sub_len, dil * CB), BF16),
                   jax.ShapeDtypeStruct((2, batch, sub_len, dil * CB), F32)],
        compiler_params=pltpu.CompilerParams(
            dimension_semantics=("arbitrary",) * 4, vmem_limit_bytes=VMEM_LIMIT),
        name=f"dilated_attn_g{group}",
    )(z4, z4, z4, z4, z4)
    n_tok = batch * seq
    return o.reshape(2, n_tok, CB), lse.reshape(2, n_tok, CB)


def _swa_attention(z3, sinks, *, batch, seq, lq):
    z4 = z3.reshape(N_Z_BLOCKS, batch, seq, CB)
    bpq = lq // BLOCK
    n_q_blocks = B_Q_WIDTH // CB
    q_per_kv = n_q_blocks // B_KV_HEADS
    q_spec = pl.BlockSpec((None, None, lq, CB), lambda b, jq, i, s: (ZB_QB + jq, b, i, 0))
    cur = lambda base: pl.BlockSpec((None, None, lq, PAIR), lambda b, jq, i, s: (base, b, i, jq // q_per_kv))
    prev = lambda base: pl.BlockSpec(
        (None, None, BLOCK, PAIR), lambda b, jq, i, s: (base, b, jnp.maximum(i * bpq - 1, 0), jq // q_per_kv))
    o = pl.pallas_call(
        functools.partial(_attn_kernel, lq=lq, max_dist=B_WINDOW - 1, kv_shared=True,
                          has_sink=True, want_lse=False, heads_per_step=4, seq_axis=2),
        grid_spec=pltpu.PrefetchScalarGridSpec(
            num_scalar_prefetch=1,
            grid=(batch, n_q_blocks, seq // lq),
            in_specs=[q_spec, cur(ZB_KB), cur(ZB_VB), prev(ZB_KB), prev(ZB_VB)],
            out_specs=pl.BlockSpec((None, None, lq, CB), lambda b, jq, i, s: (jq, b, i, 0)),
        ),
        out_shape=jax.ShapeDtypeStruct((n_q_blocks, batch, seq, CB), BF16),
        compiler_params=pltpu.CompilerParams(
            dimension_semantics=("arbitrary",) * 3, vmem_limit_bytes=VMEM_LIMIT),
        name="swa_attn",
    )(sinks, z4, z4, z4, z4, z4)
    return o.reshape(n_q_blocks, batch * seq, CB)


def _post_attn_kernel(o0_ref, o1_ref, o2_ref, l0_ref, l1_ref, l2_ref, ob_ref, gate_ref, x_ref,
                      wa_ref, wb_ref, wo_ref, gf_ref, wr_ref, br_ref,
                      x1_ref, h2_ref, comb_ref):
    pa = None
    for cb in range(A_GROUP_WIDTH // CB):
        l0, l1, l2 = l0_ref[cb], l1_ref[cb], l2_ref[cb]
        mx = jnp.maximum(jnp.maximum(l0, l1), l2)
        e0, e1, e2 = jnp.exp(l0 - mx), jnp.exp(l1 - mx), jnp.exp(l2 - mx)
        inv = 1.0 / (e0 + e1 + e2)
        ya = ((e0 * inv) * o0_ref[cb].astype(F32) + (e1 * inv) * o1_ref[cb].astype(F32)
              + (e2 * inv) * o2_ref[cb].astype(F32))
        part = jnp.dot(ya.astype(BF16), wa_ref[cb * CB:(cb + 1) * CB, :], preferred_element_type=F32)
        pa = part if pa is None else pa + part
    pb = None
    for cb in range(B_Q_WIDTH // CB):
        part = jnp.dot(ob_ref[cb], wb_ref[cb * CB:(cb + 1) * CB, :], preferred_element_type=F32)
        pb = part if pb is None else pb + part
    n_gate = D_MODEL // CB
    merged = []
    for cb in range(n_gate):
        cols = slice(cb * CB, (cb + 1) * CB)
        ga = jax.nn.sigmoid(gate_ref[cb].astype(F32))
        gb = jax.nn.sigmoid(gate_ref[n_gate + cb].astype(F32))
        merged.append((ga * pa[:, cols] + gb * pb[:, cols]).astype(BF16))
    merged = jnp.concatenate(merged, axis=1)
    x1 = x_ref[...] + jnp.dot(merged, wo_ref[...], preferred_element_type=F32)
    x1_ref[...] = x1
    h2 = x1 * lax.rsqrt(jnp.mean(x1 * x1, axis=-1, keepdims=True) + RMS_EPS) * gf_ref[...]
    h2_ref[...] = h2.astype(BF16)

    logits = jnp.dot(h2, wr_ref[...], preferred_element_type=F32,
                     precision=lax.Precision.HIGHEST) + br_ref[...]
    lane = lax.broadcasted_iota(jnp.int32, logits.shape, 1)
    neg_inf = jnp.float32(-jnp.inf)
    lg = jnp.where(lane < N_EXPERT_GROUPS, logits, neg_inf)
    mg = jnp.max(lg, axis=-1, keepdims=True)
    gsel = jnp.min(jnp.where(lg == mg, lane, ROUTER_LANES), axis=-1, keepdims=True)
    pg_sel = 1.0 / jnp.sum(jnp.exp(lg - mg), axis=-1, keepdims=True)
    expert = lane - EXPERT_LANE0
    in_group = (expert >= 0) & (expert < N_EXPERTS) & ((expert // EXPERTS_PER_GROUP) == gsel)
    le = jnp.where(in_group, logits, neg_inf)
    me = jnp.max(le, axis=-1, keepdims=True)
    ex = jnp.exp(le - me)
    pe = ex / jnp.sum(ex, axis=-1, keepdims=True)
    p1 = jnp.max(pe, axis=-1, keepdims=True)
    i1 = jnp.min(jnp.where(in_group & (pe == p1), lane, ROUTER_LANES), axis=-1, keepdims=True)
    rest = in_group & (lane != i1)
    p2 = jnp.max(jnp.where(rest, pe, -1.0), axis=-1, keepdims=True)
    i2 = jnp.min(jnp.where(rest & (pe == p2), lane, ROUTER_LANES), axis=-1, keepdims=True)
    norm = pg_sel / (p1 + p2)
    comb_ref[...] = jnp.where(lane == i1, p1 * norm, 0.0) + jnp.where(lane == i2, p2 * norm, 0.0)


def _post_attn(oa, lse, ob, z3, x2, w_proj_a, w_proj_b, w_out, g_ffn, w_router, b_router, *, tm):
    n_tok = x2.shape[0]
    const2 = lambda i: (0, 0)
    blk = lambda nb: pl.BlockSpec((nb, tm, CB), lambda i: (0, i, 0))
    row = lambda width: pl.BlockSpec((tm, width), lambda i: (i, 0))
    resident = lambda shape: pl.BlockSpec(shape, const2, pipeline_mode=pl.Buffered(1))
    return pl.pallas_call(
        _post_attn_kernel,
        grid=(n_tok // tm,),
        in_specs=[blk(2)] * 6 + [blk(4), blk(2 * D_MODEL // CB), row(D_MODEL),
                                 resident((A_GROUP_WIDTH, D_MODEL)), resident((B_Q_WIDTH, D_MODEL)),
                                 resident((D_MODEL, D_MODEL)), resident((1, D_MODEL)),
                                 resident((D_MODEL, ROUTER_LANES)), resident((1, ROUTER_LANES))],
        out_specs=[row(D_MODEL), row(D_MODEL), row(ROUTER_LANES)],
        out_shape=[jax.ShapeDtypeStruct((n_tok, D_MODEL), F32),
                   jax.ShapeDtypeStruct((n_tok, D_MODEL), BF16),
                   jax.ShapeDtypeStruct((n_tok, ROUTER_LANES), F32)],
        compiler_params=pltpu.CompilerParams(
            dimension_semantics=("arbitrary",), vmem_limit_bytes=VMEM_LIMIT),
        name="post_attn",
    )(oa[0], oa[1], oa[2], lse[0], lse[1], lse[2], ob, z3, x2,
      w_proj_a, w_proj_b, w_out, g_ffn, w_router, b_router)


def _moe_kernel(h2_ref, comb_ref, x1_ref, wg_ref, wu_ref, wd_ref, gfin_ref, out_ref, acc_ref):
    e = pl.program_id(1)

    @pl.when(e == 0)
    def _():
        acc_ref[...] = jnp.zeros_like(acc_ref)

    h2 = h2_ref[...]
    comb = comb_ref[...]
    lane = lax.broadcasted_iota(jnp.int32, comb.shape, 1)
    c_e = jnp.sum(jnp.where(lane == e + EXPERT_LANE0, comb, 0.0), axis=-1, keepdims=True)
    hg = jnp.dot(h2, wg_ref[...], preferred_element_type=F32)
    hu = jnp.dot(h2, wu_ref[...], preferred_element_type=F32)
    a = (hg * jax.nn.sigmoid(hg)) * hu * c_e
    acc_ref[...] += jnp.dot(a.astype(BF16), wd_ref[...], preferred_element_type=F32)

    @pl.when(e == N_EXPERTS - 1)
    def _():
        y = x1_ref[...] + acc_ref[...]
        out_ref[...] = y * lax.rsqrt(jnp.mean(y * y, axis=-1, keepdims=True) + RMS_EPS) * gfin_ref[...]


def _moe(h2, comb, x1, w_gate, w_up, w_down, g_final, *, tm):
    n_tok = h2.shape[0]
    row = lambda width: pl.BlockSpec((tm, width), lambda i, e: (i, 0))
    return pl.pallas_call(
        _moe_kernel,
        grid=(n_tok // tm, N_EXPERTS),
        in_specs=[row(D_MODEL), row(ROUTER_LANES), row(D_MODEL),
                  pl.BlockSpec((None, D_MODEL, D_EXPERT), lambda i, e: (e, 0, 0)),
                  pl.BlockSpec((None, D_MODEL, D_EXPERT), lambda i, e: (e, 0, 0)),
                  pl.BlockSpec((None, D_EXPERT, D_MODEL), lambda i, e: (e, 0, 0)),
                  pl.BlockSpec((1, D_MODEL), lambda i, e: (0, 0))],
        out_specs=row(D_MODEL),
        out_shape=jax.ShapeDtypeStruct((n_tok, D_MODEL), F32),
        scratch_shapes=[pltpu.VMEM((tm, D_MODEL), F32)],
        compiler_params=pltpu.CompilerParams(
            dimension_semantics=("arbitrary", "arbitrary"), vmem_limit_bytes=VMEM_LIMIT),
        name="moe",
    )(h2, comb, x1, w_gate, w_up, w_down, g_final)


def _rope_tables(seq):
    pos = jnp.arange(seq, dtype=F32)
    inv_freq = ROPE_THETA ** (-jnp.arange(0, HEAD_DIM, 2, dtype=F32) / HEAD_DIM)
    ang = pos[:, None] * inv_freq[None, :]
    cos, sin = jnp.cos(ang), jnp.sin(ang)
    reps = CB // HEAD_DIM
    cos_t = jnp.tile(jnp.concatenate([cos, cos], axis=-1), (1, reps))
    sin_t = jnp.tile(jnp.concatenate([-sin, sin], axis=-1), (1, reps))
    return cos_t, sin_t


def kernel(x, w_in, b_in, sinks, w_proj_a, w_proj_b, w_out, g_mix, g_ffn, w_router_group, b_router_group,
           w_router_expert, b_router_expert, w_exp_gate, w_exp_up, w_exp_down, g_final):
    batch, seq, d = x.shape
    assert d == D_MODEL and w_in.shape[0] == 1, "single-layer kernel"
    n_tok = batch * seq
    x2 = x.reshape(n_tok, d)
    cos_t, sin_t = _rope_tables(seq)

    z3 = _in_proj(x2, g_mix[0][None, :], w_in[0].astype(BF16), b_in[0][None, :], cos_t, sin_t,
                  seq=seq, tm=512)

    oa, lse = [], []
    for group in range(N_DIL):
        o_g, l_g = _dilated_attention(z3, group, batch=batch, seq=seq, lq=512)
        oa.append(o_g)
        lse.append(l_g)
    ob = _swa_attention(z3, sinks[0], batch=batch, seq=seq, lq=512)

    pad = ROUTER_LANES - N_EXPERT_GROUPS - N_EXPERTS
    w_router = jnp.concatenate(
        [w_router_group[0], w_router_expert[0], jnp.zeros((d, pad), F32)], axis=1)
    b_router = jnp.concatenate(
        [b_router_group[0], b_router_expert[0], jnp.zeros((pad,), F32)])[None, :]
    x1, h2, comb = _post_attn(
        oa, lse, ob, z3, x2, w_proj_a[0].astype(BF16), w_proj_b[0].astype(BF16), w_out[0].astype(BF16),
        g_ffn[0][None, :], w_router, b_router, tm=512)

    out = _moe(h2, comb, x1, w_exp_gate[0].astype(BF16), w_exp_up[0].astype(BF16),
               w_exp_down[0].astype(BF16), g_final[None, :], tm=1024)
    return out.reshape(batch, seq, d)
```

```python
import functools

import jax
import jax.numpy as jnp
from jax import lax
from jax.experimental import pallas as pl
from jax.experimental.pallas import tpu as pltpu

F32 = jnp.float32
BF16 = jnp.bfloat16

D_MODEL = 1024
HEAD_DIM = 64
HALF = HEAD_DIM // 2
ROPE_THETA = 10000.0
RMS_EPS = 1e-6
BLOCK = 128
DIL_GROUPS = ((128, 1), (512, 4), (2048, 16))
N_DIL = len(DIL_GROUPS)
A_GROUP_WIDTH = 512
A_QKV_WIDTH = 3 * N_DIL * A_GROUP_WIDTH
B_Q_HEADS = 16
B_KV_HEADS = 2
B_Q_WIDTH = B_Q_HEADS * HEAD_DIM
B_WINDOW = 128
GATE_WIDTH = 2 * D_MODEL
IN_WIDTH = A_QKV_WIDTH + B_Q_WIDTH + 2 * B_KV_HEADS * HEAD_DIM + GATE_WIDTH
N_EXPERT_GROUPS = 4
EXPERTS_PER_GROUP = 8
N_EXPERTS = N_EXPERT_GROUPS * EXPERTS_PER_GROUP
D_EXPERT = D_MODEL // 4

CB = 256
PAIR = 128
N_IN_BLOCKS = IN_WIDTH // CB
A_BLOCKS = A_GROUP_WIDTH // CB
ZB_GATE = 0
ZB_QB = 8
ZB_KB = 12
ZB_VB = 13
ZB_A0 = 14
N_TOK_BLOCKS = ZB_A0 + 3 * A_BLOCKS
OUT_PERM_BLOCK = 256
LSE_PERM_BLOCK = 128
ROUTER_LANES = 128
EXPERT_LANE0 = N_EXPERT_GROUPS

VMEM_LIMIT = 56 * 1024 * 1024


def _in_proj_plan():
    plan = []
    for c in range(N_IN_BLOCKS):
        col = c * CB
        if col < A_QKV_WIDTH:
            part, rem = divmod(col, N_DIL * A_GROUP_WIDTH)
            group, blk = divmod(rem // CB, A_BLOCKS)
            kind = ("q", "k", "v")[part]
            dil = DIL_GROUPS[group][1]
            if dil == 1:
                plan.append((0, ZB_A0 + part * A_BLOCKS + blk, kind, 1))
            else:
                plan.append((group, part * A_BLOCKS + blk, kind, dil))
        elif col < A_QKV_WIDTH + B_Q_WIDTH:
            plan.append((0, ZB_QB + (col - A_QKV_WIDTH) // CB, "q", 1))
        elif col < A_QKV_WIDTH + B_Q_WIDTH + CB:
            plan.append((0, -1, "kvb", 1))
        else:
            plan.append((0, ZB_GATE + (col - (A_QKV_WIDTH + B_Q_WIDTH + CB)) // CB, "v", 1))
    return tuple(plan)


def _rms(x, g):
    return x * lax.rsqrt(jnp.mean(x * x, axis=-1, keepdims=True) + RMS_EPS) * g


def _rope(acc, cos, sin_signed, first_half):
    partner = jnp.where(first_half, pltpu.roll(acc, CB - HALF, 1), pltpu.roll(acc, HALF, 1))
    return acc * cos + partner * sin_signed


def _in_proj_kernel(x_ref, g_ref, w_ref, b_ref, cos_ref, sin_ref, perm_ref, zt_ref, zd1_ref, zd2_ref,
                    *, plan, tm):
    out_refs = (zt_ref, zd1_ref, zd2_ref)
    lane = lax.broadcasted_iota(jnp.int32, (tm, CB), 1)
    first_half = (lane % HEAD_DIM) < HALF

    h = _rms(x_ref[...], g_ref[...]).astype(BF16)
    h_by_dil, tables = {}, {}
    for slot, (_, dil) in enumerate(DIL_GROUPS):
        tables[dil] = slot
        if dil == 1:
            h_by_dil[dil] = h
        else:
            h_by_dil[dil] = jnp.dot(perm_ref[slot - 1], h, preferred_element_type=F32).astype(BF16)

    for c, (arr, dst, kind, dil) in enumerate(plan):
        cols = slice(c * CB, (c + 1) * CB)
        acc = jnp.dot(h_by_dil[dil], w_ref[:, cols], preferred_element_type=F32) + b_ref[:, cols]
        if kind in ("q", "k", "kvb"):
            slot = tables[dil]
            rot = _rope(acc, cos_ref[slot], sin_ref[slot], first_half)
        if kind == "q":
            val = (rot * (HEAD_DIM ** -0.5)).astype(BF16)
        elif kind == "k":
            val = rot.astype(BF16)
        elif kind == "v":
            val = acc.astype(BF16)
        else:
            r64 = pltpu.roll(rot, HEAD_DIM, 1)
            r128 = pltpu.roll(rot, 2 * HEAD_DIM, 1)
            kdup = jnp.where(lane < HEAD_DIM, rot, jnp.where(lane < 3 * HEAD_DIM, r64, r128))
            a128 = pltpu.roll(acc, 2 * HEAD_DIM, 1)
            a192 = pltpu.roll(acc, 3 * HEAD_DIM, 1)
            vdup = jnp.where(lane < HEAD_DIM, a128, jnp.where(lane < 3 * HEAD_DIM, a192, acc))
            zt_ref[ZB_KB] = kdup.astype(BF16)
            zt_ref[ZB_VB] = vdup.astype(BF16)
            continue
        if dil == 1:
            out_refs[arr][dst] = val
        else:
            n = tm // dil
            for r in range(dil):
                out_refs[arr][dst, :, r * CB:(r + 1) * CB] = val[r * n:(r + 1) * n, :]


def _dilation_perm(block, dil, dtype):
    j = jnp.arange(block)
    src = (j % (block // dil)) * dil + j // (block // dil)
    return (src[:, None] == jnp.arange(block)[None, :]).astype(dtype)


def _in_proj(x2, g_mix, w_in, b_in, cos_t, sin_t, *, seq, tm):
    n_tok = x2.shape[0]
    tiles_per_seq = seq // tm
    const = lambda i: (0, 0)
    table = pl.BlockSpec((N_DIL, tm, CB), lambda i: (0, i % tiles_per_seq, 0))
    d1, d2 = DIL_GROUPS[1][1], DIL_GROUPS[2][1]
    perms = jnp.stack([_dilation_perm(tm, d1, BF16), _dilation_perm(tm, d2, BF16)])
    return pl.pallas_call(
        functools.partial(_in_proj_kernel, plan=_in_proj_plan(), tm=tm),
        grid=(n_tok // tm,),
        in_specs=[
            pl.BlockSpec((tm, D_MODEL), lambda i: (i, 0)),
            pl.BlockSpec((1, D_MODEL), const),
            pl.BlockSpec((D_MODEL, IN_WIDTH), const, pipeline_mode=pl.Buffered(1)),
            pl.BlockSpec((1, IN_WIDTH), const),
            table, table,
            pl.BlockSpec((2, tm, tm), lambda i: (0, 0, 0), pipeline_mode=pl.Buffered(1)),
        ],
        out_specs=[
            pl.BlockSpec((N_TOK_BLOCKS, tm, CB), lambda i: (0, i, 0)),
            pl.BlockSpec((3 * A_BLOCKS, tm // d1, d1 * CB), lambda i: (0, i, 0)),
            pl.BlockSpec((3 * A_BLOCKS, tm // d2, d2 * CB), lambda i: (0, i, 0)),
        ],
        out_shape=[
            jax.ShapeDtypeStruct((N_TOK_BLOCKS, n_tok, CB), BF16),
            jax.ShapeDtypeStruct((3 * A_BLOCKS, n_tok // d1, d1 * CB), BF16),
            jax.ShapeDtypeStruct((3 * A_BLOCKS, n_tok // d2, d2 * CB), BF16),
        ],
        compiler_params=pltpu.CompilerParams(
            dimension_semantics=("arbitrary",), vmem_limit_bytes=VMEM_LIMIT),
        name="in_proj",
    )(x2, g_mix, w_in, b_in, cos_t, sin_t, perms)


def _attn_kernel(*refs, lq, max_dist, kv_shared, has_sink, want_lse, heads_per_step, seq_axis):
    refs = list(refs)
    sink_ref = refs.pop(0) if has_sink else None
    q_ref, k_ref, v_ref, kp_ref, vp_ref = refs[:5]
    o_ref = refs[5]
    lse_ref = refs[6] if want_lse else None

    row = lax.broadcasted_iota(jnp.int32, (BLOCK, 2 * BLOCK), 0)
    col = lax.broadcasted_iota(jnp.int32, (BLOCK, 2 * BLOCK), 1)
    dist = row - col + BLOCK
    valid = (dist >= 0) & (dist <= max_dist)
    neg_inf = jnp.float32(-jnp.inf)
    bias = jnp.where(valid, 0.0, neg_inf)
    bias_first = jnp.where(valid & (col >= BLOCK), 0.0, neg_inf)
    bias0 = jnp.where(pl.program_id(seq_axis) == 0, bias_first, bias)
    lane_lo = lax.broadcasted_iota(jnp.int32, (BLOCK, PAIR), 1) < HEAD_DIM

    for ib in range(lq // BLOCK):
        rows = slice(ib * BLOCK, (ib + 1) * BLOCK)
        b_ib = bias0 if ib == 0 else bias
        for pp in range(2):
            qcols = slice(pp * PAIR, (pp + 1) * PAIR)
            kcols = slice(0, PAIR) if kv_shared else qcols
            if ib == 0:
                k_win = jnp.concatenate([kp_ref[:, kcols], k_ref[0:BLOCK, kcols]], axis=0)
                v_win = jnp.concatenate([vp_ref[:, kcols], v_ref[0:BLOCK, kcols]], axis=0)
            else:
                win = slice((ib - 1) * BLOCK, (ib + 1) * BLOCK)
                k_win = k_ref[win, kcols]
                v_win = v_ref[win, kcols]
            q_pair = q_ref[rows, qcols]
            outs, lses = [], []
            for hh in range(2):
                q_h = jnp.where(lane_lo == (hh == 0), q_pair, jnp.zeros_like(q_pair))
                s = lax.dot_general(q_h, k_win, (((1,), (1,)), ((), ())), preferred_element_type=F32)
                s = s + b_ib
                m = jnp.max(s, axis=-1, keepdims=True)
                if has_sink:
                    sink = sink_ref[pl.program_id(1) * heads_per_step + pp * 2 + hh]
                    m = jnp.maximum(m, sink)
                p = jnp.exp(s - m)
                den = jnp.sum(p, axis=-1, keepdims=True)
                if has_sink:
                    den = den + jnp.exp(sink - m)
                o_h = jnp.dot(p.astype(BF16), v_win, preferred_element_type=F32)
                outs.append(o_h * (1.0 / den))
                if want_lse:
                    lses.append(jnp.broadcast_to(m + jnp.log(den), (BLOCK, PAIR)))
            o_ref[rows, qcols] = jnp.where(lane_lo, outs[0], outs[1]).astype(BF16)
            if want_lse:
                lse_ref[rows, qcols] = jnp.where(lane_lo, lses[0], lses[1])


def _dilated_attention(z, bases, group, *, batch, seq, lq):
    window, dil = DIL_GROUPS[group]
    sub_len = seq // dil
    lq = min(lq, sub_len)
    z4 = z.reshape(z.shape[0], batch, sub_len, dil * CB)
    qb, kb, vb = bases
    bpq = lq // BLOCK
    cur = lambda base: pl.BlockSpec((None, None, lq, CB), lambda b, r, jj, i: (base + jj, b, i, r))
    prev = lambda base: pl.BlockSpec(
        (None, None, BLOCK, CB), lambda b, r, jj, i: (base + jj, b, jnp.maximum(i * bpq - 1, 0), r))
    out_spec = pl.BlockSpec((None, None, lq, CB), lambda b, r, jj, i: (jj, b, i, r))
    o, lse = pl.pallas_call(
        functools.partial(_attn_kernel, lq=lq, max_dist=window // dil, kv_shared=False,
                          has_sink=False, want_lse=True, heads_per_step=4, seq_axis=3),
        grid=(batch, dil, A_BLOCKS, sub_len // lq),
        in_specs=[cur(qb), cur(kb), cur(vb), prev(kb), prev(vb)],
        out_specs=[out_spec, out_spec],
        out_shape=[jax.ShapeDtypeStruct((A_BLOCKS, batch, sub_len, dil * CB), BF16),
                   jax.ShapeDtypeStruct((A_BLOCKS, batch, sub_len, dil * CB), F32)],
        compiler_params=pltpu.CompilerParams(
            dimension_semantics=("arbitrary",) * 4, vmem_limit_bytes=VMEM_LIMIT),
        name=f"dilated_attn_g{group}",
    )(z4, z4, z4, z4, z4)
    rows = batch * sub_len
    return o.reshape(A_BLOCKS, rows, dil * CB), lse.reshape(A_BLOCKS, rows, dil * CB)


def _swa_attention(z_tok, sinks, *, batch, seq, lq):
    z4 = z_tok.reshape(N_TOK_BLOCKS, batch, seq, CB)
    bpq = lq // BLOCK
    n_q_blocks = B_Q_WIDTH // CB
    q_per_kv = n_q_blocks // B_KV_HEADS
    q_spec = pl.BlockSpec((None, None, lq, CB), lambda b, jq, i, s: (ZB_QB + jq, b, i, 0))
    cur = lambda base: pl.BlockSpec((None, None, lq, PAIR), lambda b, jq, i, s: (base, b, i, jq // q_per_kv))
    prev = lambda base: pl.BlockSpec(
        (None, None, BLOCK, PAIR), lambda b, jq, i, s: (base, b, jnp.maximum(i * bpq - 1, 0), jq // q_per_kv))
    o = pl.pallas_call(
        functools.partial(_attn_kernel, lq=lq, max_dist=B_WINDOW - 1, kv_shared=True,
                          has_sink=True, want_lse=False, heads_per_step=4, seq_axis=2),
        grid_spec=pltpu.PrefetchScalarGridSpec(
            num_scalar_prefetch=1,
            grid=(batch, n_q_blocks, seq // lq),
            in_specs=[q_spec, cur(ZB_KB), cur(ZB_VB), prev(ZB_KB), prev(ZB_VB)],
            out_specs=pl.BlockSpec((None, None, lq, CB), lambda b, jq, i, s: (jq, b, i, 0)),
        ),
        out_shape=jax.ShapeDtypeStruct((n_q_blocks, batch, seq, CB), BF16),
        compiler_params=pltpu.CompilerParams(
            dimension_semantics=("arbitrary",) * 3, vmem_limit_bytes=VMEM_LIMIT),
        name="swa_attn",
    )(sinks, z4, z4, z4, z4, z4)
    return o.reshape(n_q_blocks, batch * seq, CB)


def _post_attn_kernel(o0_ref, o1_ref, o2_ref, l0_ref, l1_ref, l2_ref, ob_ref, gate_ref, x_ref,
                      wa_ref, wb_ref, wo_ref, gf_ref, wr_ref, br_ref,
                      po_ref, pl_ref, x1_ref, h2_ref, comb_ref, *, tm):

    def to_token_order(ref, cb, slot, perm_ref, blk, precision):
        dil = DIL_GROUPS[slot + 1][1]
        n = blk // dil
        parts = []
        for tb in range(tm // blk):
            stack = jnp.concatenate(
                [ref[cb, tb * n:(tb + 1) * n, r * CB:(r + 1) * CB] for r in range(dil)], axis=0)
            parts.append(jnp.dot(perm_ref[slot], stack, preferred_element_type=F32, precision=precision))
        return jnp.concatenate(parts, axis=0)

    pa = None
    for cb in range(A_BLOCKS):
        l0 = l0_ref[cb]
        l1 = to_token_order(l1_ref, cb, 0, pl_ref, LSE_PERM_BLOCK, lax.Precision.HIGHEST)
        l2 = to_token_order(l2_ref, cb, 1, pl_ref, LSE_PERM_BLOCK, lax.Precision.HIGHEST)
        o1 = to_token_order(o1_ref, cb, 0, po_ref, OUT_PERM_BLOCK, None)
        o2 = to_token_order(o2_ref, cb, 1, po_ref, OUT_PERM_BLOCK, None)
        mx = jnp.maximum(jnp.maximum(l0, l1), l2)
        e0, e1, e2 = jnp.exp(l0 - mx), jnp.exp(l1 - mx), jnp.exp(l2 - mx)
        inv = 1.0 / (e0 + e1 + e2)
        ya = (e0 * inv) * o0_ref[cb].astype(F32) + (e1 * inv) * o1 + (e2 * inv) * o2
        part = jnp.dot(ya.astype(BF16), wa_ref[cb * CB:(cb + 1) * CB, :], preferred_element_type=F32)
        pa = part if pa is None else pa + part
    pb = None
    for cb in range(B_Q_WIDTH // CB):
        part = jnp.dot(ob_ref[cb], wb_ref[cb * CB:(cb + 1) * CB, :], preferred_element_type=F32)
        pb = part if pb is None else pb + part
    n_gate = D_MODEL // CB
    merged = []
    for cb in range(n_gate):
        cols = slice(cb * CB, (cb + 1) * CB)
        ga = jax.nn.sigmoid(gate_ref[cb].astype(F32))
        gb = jax.nn.sigmoid(gate_ref[n_gate + cb].astype(F32))
        merged.append((ga * pa[:, cols] + gb * pb[:, cols]).astype(BF16))
    merged = jnp.concatenate(merged, axis=1)
    x1 = x_ref[...] + jnp.dot(merged, wo_ref[...], preferred_element_type=F32)
    x1_ref[...] = x1
    h2 = _rms(x1, gf_ref[...])
    h2_ref[...] = h2.astype(BF16)

    logits = jnp.dot(h2, wr_ref[...], preferred_element_type=F32,
                     precision=lax.Precision.HIGHEST) + br_ref[...]
    lane = lax.broadcasted_iota(jnp.int32, logits.shape, 1)
    neg_inf = jnp.float32(-jnp.inf)
    lg = jnp.where(lane < N_EXPERT_GROUPS, logits, neg_inf)
    mg = jnp.max(lg, axis=-1, keepdims=True)
    gsel = jnp.min(jnp.where(lg == mg, lane, ROUTER_LANES), axis=-1, keepdims=True)
    pg_sel = 1.0 / jnp.sum(jnp.exp(lg - mg), axis=-1, keepdims=True)
    expert = lane - EXPERT_LANE0
    in_group = (expert >= 0) & (expert < N_EXPERTS) & ((expert // EXPERTS_PER_GROUP) == gsel)
    le = jnp.where(in_group, logits, neg_inf)
    me = jnp.max(le, axis=-1, keepdims=True)
    ex = jnp.exp(le - me)
    pe = ex / jnp.sum(ex, axis=-1, keepdims=True)
    p1 = jnp.max(pe, axis=-1, keepdims=True)
    i1 = jnp.min(jnp.where(in_group & (pe == p1), lane, ROUTER_LANES), axis=-1, keepdims=True)
    rest = in_group & (lane != i1)
    p2 = jnp.max(jnp.where(rest, pe, -1.0), axis=-1, keepdims=True)
    i2 = jnp.min(jnp.where(rest & (pe == p2), lane, ROUTER_LANES), axis=-1, keepdims=True)
    norm = pg_sel / (p1 + p2)
    comb_ref[...] = jnp.where(lane == i1, p1 * norm, 0.0) + jnp.where(lane == i2, p2 * norm, 0.0)


def _post_attn(oa, lse, ob, z_tok, x2, w_proj_a, w_proj_b, w_out, g_ffn, w_router, b_router, *, tm):
    n_tok = x2.shape[0]
    const2 = lambda i: (0, 0)
    blk = lambda nb: pl.BlockSpec((nb, tm, CB), lambda i: (0, i, 0))
    dil_blk = lambda dil: pl.BlockSpec((A_BLOCKS, tm // dil, dil * CB), lambda i: (0, i, 0))
    row = lambda width: pl.BlockSpec((tm, width), lambda i: (i, 0))
    resident = lambda shape: pl.BlockSpec(shape, const2, pipeline_mode=pl.Buffered(1))
    d1, d2 = DIL_GROUPS[1][1], DIL_GROUPS[2][1]
    perm_o = jnp.stack([_dilation_perm(OUT_PERM_BLOCK, d, BF16).T for d in (d1, d2)])
    perm_l = jnp.stack([_dilation_perm(LSE_PERM_BLOCK, d, F32).T for d in (d1, d2)])
    return pl.pallas_call(
        functools.partial(_post_attn_kernel, tm=tm),
        grid=(n_tok // tm,),
        in_specs=[blk(A_BLOCKS), dil_blk(d1), dil_blk(d2), blk(A_BLOCKS), dil_blk(d1), dil_blk(d2),
                  blk(B_Q_WIDTH // CB), blk(GATE_WIDTH // CB), row(D_MODEL),
                  resident((A_GROUP_WIDTH, D_MODEL)), resident((B_Q_WIDTH, D_MODEL)),
                  resident((D_MODEL, D_MODEL)), resident((1, D_MODEL)),
                  resident((D_MODEL, ROUTER_LANES)), resident((1, ROUTER_LANES)),
                  pl.BlockSpec((2, OUT_PERM_BLOCK, OUT_PERM_BLOCK), lambda i: (0, 0, 0),
                               pipeline_mode=pl.Buffered(1)),
                  pl.BlockSpec((2, LSE_PERM_BLOCK, LSE_PERM_BLOCK), lambda i: (0, 0, 0),
                               pipeline_mode=pl.Buffered(1))],
        out_specs=[row(D_MODEL), row(D_MODEL), row(ROUTER_LANES)],
        out_shape=[jax.ShapeDtypeStruct((n_tok, D_MODEL), F32),
                   jax.ShapeDtypeStruct((n_tok, D_MODEL), BF16),
                   jax.ShapeDtypeStruct((n_tok, ROUTER_LANES), F32)],
        compiler_params=pltpu.CompilerParams(
            dimension_semantics=("arbitrary",), vmem_limit_bytes=VMEM_LIMIT),
        name="post_attn",
    )(oa[0], oa[1], oa[2], lse[0], lse[1], lse[2], ob, z_tok, x2,
      w_proj_a, w_proj_b, w_out, g_ffn, w_router, b_router, perm_o, perm_l)


def _moe_kernel(h2_ref, comb_ref, x1_ref, wg_ref, wu_ref, wd_ref, gfin_ref, out_ref, acc_ref):
    e = pl.program_id(1)

    @pl.when(e == 0)
    def _():
        acc_ref[...] = jnp.zeros_like(acc_ref)

    h2 = h2_ref[...]
    comb = comb_ref[...]
    lane = lax.broadcasted_iota(jnp.int32, comb.shape, 1)
    c_e = jnp.sum(jnp.where(lane == e + EXPERT_LANE0, comb, 0.0), axis=-1, keepdims=True)
    hg = jnp.dot(h2, wg_ref[...], preferred_element_type=F32)
    hu = jnp.dot(h2, wu_ref[...], preferred_element_type=F32)
    a = (hg * jax.nn.sigmoid(hg)) * hu * c_e
    acc_ref[...] += jnp.dot(a.astype(BF16), wd_ref[...], preferred_element_type=F32)

    @pl.when(e == N_EXPERTS - 1)
    def _():
        out_ref[...] = _rms(x1_ref[...] + acc_ref[...], gfin_ref[...])


def _moe(h2, comb, x1, w_gate, w_up, w_down, g_final, *, tm):
    n_tok = h2.shape[0]
    row = lambda width: pl.BlockSpec((tm, width), lambda i, e: (i, 0))
    return pl.pallas_call(
        _moe_kernel,
        grid=(n_tok // tm, N_EXPERTS),
        in_specs=[row(D_MODEL), row(ROUTER_LANES), row(D_MODEL),
                  pl.BlockSpec((None, D_MODEL, D_EXPERT), lambda i, e: (e, 0, 0)),
                  pl.BlockSpec((None, D_MODEL, D_EXPERT), lambda i, e: (e, 0, 0)),
                  pl.BlockSpec((None, D_EXPERT, D_MODEL), lambda i, e: (e, 0, 0)),
                  pl.BlockSpec((1, D_MODEL), lambda i, e: (0, 0))],
        out_specs=row(D_MODEL),
        out_shape=jax.ShapeDtypeStruct((n_tok, D_MODEL), F32),
        scratch_shapes=[pltpu.VMEM((tm, D_MODEL), F32)],
        compiler_params=pltpu.CompilerParams(
            dimension_semantics=("arbitrary", "arbitrary"), vmem_limit_bytes=VMEM_LIMIT),
        name="moe",
    )(h2, comb, x1, w_gate, w_up, w_down, g_final)


def _rope_tables(seq, tm):
    pos = jnp.arange(seq, dtype=F32)
    inv_freq = ROPE_THETA ** (-jnp.arange(0, HEAD_DIM, 2, dtype=F32) / HEAD_DIM)
    ang = pos[:, None] * inv_freq[None, :]
    cos, sin = jnp.cos(ang), jnp.sin(ang)
    reps = CB // HEAD_DIM
    cos_t = jnp.tile(jnp.concatenate([cos, cos], axis=-1), (1, reps))
    sin_t = jnp.tile(jnp.concatenate([-sin, sin], axis=-1), (1, reps))

    def reorder(t, dil):
        return t.reshape(seq // tm, tm // dil, dil, CB).transpose(0, 2, 1, 3).reshape(seq, CB)

    dils = [dil for _, dil in DIL_GROUPS]
    return (jnp.stack([reorder(cos_t, dil) for dil in dils]),
            jnp.stack([reorder(sin_t, dil) for dil in dils]))


def kernel(x, w_in, b_in, sinks, w_proj_a, w_proj_b, w_out, g_mix, g_ffn, w_router_group, b_router_group,
           w_router_expert, b_router_expert, w_exp_gate, w_exp_up, w_exp_down, g_final):
    batch, seq, d = x.shape
    assert d == D_MODEL and w_in.shape[0] == 1, "single-layer kernel"
    n_tok = batch * seq
    x2 = x.reshape(n_tok, d)
    tm_in = 512
    cos_t, sin_t = _rope_tables(seq, tm_in)

    z_tok, z_d1, z_d2 = _in_proj(x2, g_mix[0][None, :], w_in[0].astype(BF16), b_in[0][None, :],
                                 cos_t, sin_t, seq=seq, tm=tm_in)

    a_bases = (0, A_BLOCKS, 2 * A_BLOCKS)
    tok_bases = tuple(ZB_A0 + b for b in a_bases)
    oa, lse = [], []
    for group, (z, bases) in enumerate(((z_tok, tok_bases), (z_d1, a_bases), (z_d2, a_bases))):
        o_g, l_g = _dilated_attention(z, bases, group, batch=batch, seq=seq, lq=512)
        oa.append(o_g)
        lse.append(l_g)
    ob = _swa_attention(z_tok, sinks[0], batch=batch, seq=seq, lq=512)

    pad = ROUTER_LANES - N_EXPERT_GROUPS - N_EXPERTS
    w_router = jnp.concatenate(
        [w_router_group[0], w_router_expert[0], jnp.zeros((d, pad), F32)], axis=1)
    b_router = jnp.concatenate(
        [b_router_group[0], b_router_expert[0], jnp.zeros((pad,), F32)])[None, :]
    x1, h2, comb = _post_attn(
        oa, lse, ob, z_tok, x2, w_proj_a[0].astype(BF16), w_proj_b[0].astype(BF16), w_out[0].astype(BF16),
        g_ffn[0][None, :], w_router, b_router, tm=512)

    out = _moe(h2, comb, x1, w_exp_gate[0].astype(BF16), w_exp_up[0].astype(BF16),
               w_exp_down[0].astype(BF16), g_final[None, :], tm=1024)
    return out.reshape(batch, seq, d)
```

```python
import functools

import jax
import jax.numpy as jnp
from jax import lax
from jax.experimental import pallas as pl
from jax.experimental.pallas import tpu as pltpu

F32 = jnp.float32
BF16 = jnp.bfloat16

D_MODEL = 1024
HEAD_DIM = 64
HALF = HEAD_DIM // 2
ROPE_THETA = 10000.0
RMS_EPS = 1e-6
BLOCK = 128
DIL_GROUPS = ((128, 1), (512, 4), (2048, 16))
N_DIL = len(DIL_GROUPS)
A_GROUP_WIDTH = 512
A_QKV_WIDTH = 3 * N_DIL * A_GROUP_WIDTH
B_Q_HEADS = 16
B_KV_HEADS = 2
B_Q_WIDTH = B_Q_HEADS * HEAD_DIM
B_WINDOW = 128
GATE_WIDTH = 2 * D_MODEL
IN_WIDTH = A_QKV_WIDTH + B_Q_WIDTH + 2 * B_KV_HEADS * HEAD_DIM + GATE_WIDTH
N_EXPERT_GROUPS = 4
EXPERTS_PER_GROUP = 8
N_EXPERTS = N_EXPERT_GROUPS * EXPERTS_PER_GROUP
D_EXPERT = D_MODEL // 4

CB = 256
PAIR = 128
N_IN_BLOCKS = IN_WIDTH // CB
A_BLOCKS = A_GROUP_WIDTH // CB
ZB_GATE = 0
ZB_QB = 8
ZB_KB = 12
ZB_VB = 13
ZB_A0 = 14
N_TOK_BLOCKS = ZB_A0 + 3 * A_BLOCKS
OUT_PERM_BLOCK = 256
LSE_PERM_BLOCK = 128
ROUTER_LANES = 128
EXPERT_LANE0 = N_EXPERT_GROUPS
MOE_TILE = 512
CHUNK = 16
EXP_TILE = 512
LOCAL_ROWS = -(-(2 * MOE_TILE + N_EXPERTS * (CHUNK - 1)) // CB) * CB
XS_WIDTH = D_MODEL + ROUTER_LANES

VMEM_LIMIT = 56 * 1024 * 1024


def _in_proj_plan():
    plan = []
    for c in range(N_IN_BLOCKS):
        col = c * CB
        if col < A_QKV_WIDTH:
            part, rem = divmod(col, N_DIL * A_GROUP_WIDTH)
            group, blk = divmod(rem // CB, A_BLOCKS)
            kind = ("q", "k", "v")[part]
            dil = DIL_GROUPS[group][1]
            if dil == 1:
                plan.append((0, ZB_A0 + part * A_BLOCKS + blk, kind, 1))
            else:
                plan.append((group, part * A_BLOCKS + blk, kind, dil))
        elif col < A_QKV_WIDTH + B_Q_WIDTH:
            plan.append((0, ZB_QB + (col - A_QKV_WIDTH) // CB, "q", 1))
        elif col < A_QKV_WIDTH + B_Q_WIDTH + CB:
            plan.append((0, -1, "kvb", 1))
        else:
            plan.append((0, ZB_GATE + (col - (A_QKV_WIDTH + B_Q_WIDTH + CB)) // CB, "v", 1))
    return tuple(plan)


def _rms(x, g):
    return x * lax.rsqrt(jnp.mean(x * x, axis=-1, keepdims=True) + RMS_EPS) * g


def _rope(acc, cos, sin_signed, first_half):
    partner = jnp.where(first_half, pltpu.roll(acc, CB - HALF, 1), pltpu.roll(acc, HALF, 1))
    return acc * cos + partner * sin_signed


def _in_proj_kernel(x_ref, g_ref, w_ref, b_ref, cos_ref, sin_ref, perm_ref, zt_ref, zd1_ref, zd2_ref,
                    *, plan, tm):
    out_refs = (zt_ref, zd1_ref, zd2_ref)
    lane = lax.broadcasted_iota(jnp.int32, (tm, CB), 1)
    first_half = (lane % HEAD_DIM) < HALF

    h = _rms(x_ref[...], g_ref[...]).astype(BF16)
    h_by_dil, tables = {}, {}
    for slot, (_, dil) in enumerate(DIL_GROUPS):
        tables[dil] = slot
        if dil == 1:
            h_by_dil[dil] = h
        else:
            h_by_dil[dil] = jnp.dot(perm_ref[slot - 1], h, preferred_element_type=F32).astype(BF16)

    for c, (arr, dst, kind, dil) in enumerate(plan):
        cols = slice(c * CB, (c + 1) * CB)
        acc = jnp.dot(h_by_dil[dil], w_ref[:, cols], preferred_element_type=F32) + b_ref[:, cols]
        if kind in ("q", "k", "kvb"):
            slot = tables[dil]
            rot = _rope(acc, cos_ref[slot], sin_ref[slot], first_half)
        if kind == "q":
            val = (rot * (HEAD_DIM ** -0.5)).astype(BF16)
        elif kind == "k":
            val = rot.astype(BF16)
        elif kind == "v":
            val = acc.astype(BF16)
        else:
            r64 = pltpu.roll(rot, HEAD_DIM, 1)
            r128 = pltpu.roll(rot, 2 * HEAD_DIM, 1)
            kdup = jnp.where(lane < HEAD_DIM, rot, jnp.where(lane < 3 * HEAD_DIM, r64, r128))
            a128 = pltpu.roll(acc, 2 * HEAD_DIM, 1)
            a192 = pltpu.roll(acc, 3 * HEAD_DIM, 1)
            vdup = jnp.where(lane < HEAD_DIM, a128, jnp.where(lane < 3 * HEAD_DIM, a192, acc))
            zt_ref[ZB_KB] = kdup.astype(BF16)
            zt_ref[ZB_VB] = vdup.astype(BF16)
            continue
        if dil == 1:
            out_refs[arr][dst] = val
        else:
            n = tm // dil
            for r in range(dil):
                out_refs[arr][dst, :, r * CB:(r + 1) * CB] = val[r * n:(r + 1) * n, :]


def _dilation_perm(block, dil, dtype):
    j = jnp.arange(block)
    src = (j % (block // dil)) * dil + j // (block // dil)
    return (src[:, None] == jnp.arange(block)[None, :]).astype(dtype)


def _in_proj(x2, g_mix, w_in, b_in, cos_t, sin_t, *, seq, tm):
    n_tok = x2.shape[0]
    tiles_per_seq = seq // tm
    const = lambda i: (0, 0)
    table = pl.BlockSpec((N_DIL, tm, CB), lambda i: (0, i % tiles_per_seq, 0))
    d1, d2 = DIL_GROUPS[1][1], DIL_GROUPS[2][1]
    perms = jnp.stack([_dilation_perm(tm, d1, BF16), _dilation_perm(tm, d2, BF16)])
    return pl.pallas_call(
        functools.partial(_in_proj_kernel, plan=_in_proj_plan(), tm=tm),
        grid=(n_tok // tm,),
        in_specs=[
            pl.BlockSpec((tm, D_MODEL), lambda i: (i, 0)),
            pl.BlockSpec((1, D_MODEL), const),
            pl.BlockSpec((D_MODEL, IN_WIDTH), const, pipeline_mode=pl.Buffered(1)),
            pl.BlockSpec((1, IN_WIDTH), const),
            table, table,
            pl.BlockSpec((2, tm, tm), lambda i: (0, 0, 0), pipeline_mode=pl.Buffered(1)),
        ],
        out_specs=[
            pl.BlockSpec((N_TOK_BLOCKS, tm, CB), lambda i: (0, i, 0)),
            pl.BlockSpec((3 * A_BLOCKS, tm // d1, d1 * CB), lambda i: (0, i, 0)),
            pl.BlockSpec((3 * A_BLOCKS, tm // d2, d2 * CB), lambda i: (0, i, 0)),
        ],
        out_shape=[
            jax.ShapeDtypeStruct((N_TOK_BLOCKS, n_tok, CB), BF16),
            jax.ShapeDtypeStruct((3 * A_BLOCKS, n_tok // d1, d1 * CB), BF16),
            jax.ShapeDtypeStruct((3 * A_BLOCKS, n_tok // d2, d2 * CB), BF16),
        ],
        compiler_params=pltpu.CompilerParams(
            dimension_semantics=("arbitrary",), vmem_limit_bytes=VMEM_LIMIT),
        name="in_proj",
    )(x2, g_mix, w_in, b_in, cos_t, sin_t, perms)


def _attn_kernel(*refs, lq, max_dist, kv_shared, has_sink, want_lse, heads_per_step, seq_axis):
    refs = list(refs)
    sink_ref = refs.pop(0) if has_sink else None
    q_ref, k_ref, v_ref, kp_ref, vp_ref = refs[:5]
    o_ref = refs[5]
    lse_ref = refs[6] if want_lse else None

    row = lax.broadcasted_iota(jnp.int32, (BLOCK, 2 * BLOCK), 0)
    col = lax.broadcasted_iota(jnp.int32, (BLOCK, 2 * BLOCK), 1)
    dist = row - col + BLOCK
    valid = (dist >= 0) & (dist <= max_dist)
    neg_inf = jnp.float32(-jnp.inf)
    bias = jnp.where(valid, 0.0, neg_inf)
    bias_first = jnp.where(valid & (col >= BLOCK), 0.0, neg_inf)
    bias0 = jnp.where(pl.program_id(seq_axis) == 0, bias_first, bias)
    lane_lo = lax.broadcasted_iota(jnp.int32, (BLOCK, PAIR), 1) < HEAD_DIM

    for ib in range(lq // BLOCK):
        rows = slice(ib * BLOCK, (ib + 1) * BLOCK)
        b_ib = bias0 if ib == 0 else bias
        for pp in range(2):
            qcols = slice(pp * PAIR, (pp + 1) * PAIR)
            kcols = slice(0, PAIR) if kv_shared else qcols
            if ib == 0:
                k_win = jnp.concatenate([kp_ref[:, kcols], k_ref[0:BLOCK, kcols]], axis=0)
                v_win = jnp.concatenate([vp_ref[:, kcols], v_ref[0:BLOCK, kcols]], axis=0)
            else:
                win = slice((ib - 1) * BLOCK, (ib + 1) * BLOCK)
                k_win = k_ref[win, kcols]
                v_win = v_ref[win, kcols]
            q_pair = q_ref[rows, qcols]
            outs, lses = [], []
            for hh in range(2):
                q_h = jnp.where(lane_lo == (hh == 0), q_pair, jnp.zeros_like(q_pair))
                s = lax.dot_general(q_h, k_win, (((1,), (1,)), ((), ())), preferred_element_type=F32)
                s = s + b_ib
                m = jnp.max(s, axis=-1, keepdims=True)
                if has_sink:
                    sink = sink_ref[pl.program_id(1) * heads_per_step + pp * 2 + hh]
                    m = jnp.maximum(m, sink)
                p = jnp.exp(s - m)
                den = jnp.sum(p, axis=-1, keepdims=True)
                if has_sink:
                    den = den + jnp.exp(sink - m)
                o_h = jnp.dot(p.astype(BF16), v_win, preferred_element_type=F32)
                outs.append(o_h * (1.0 / den))
                if want_lse:
                    lses.append(jnp.broadcast_to(m + jnp.log(den), (BLOCK, PAIR)))
            o_ref[rows, qcols] = jnp.where(lane_lo, outs[0], outs[1]).astype(BF16)
            if want_lse:
                lse_ref[rows, qcols] = jnp.where(lane_lo, lses[0], lses[1])


def _dilated_attention(z, bases, group, *, batch, seq, lq):
    window, dil = DIL_GROUPS[group]
    sub_len = seq // dil
    lq = min(lq, sub_len)
    z4 = z.reshape(z.shape[0], batch, sub_len, dil * CB)
    qb, kb, vb = bases
    bpq = lq // BLOCK
    cur = lambda base: pl.BlockSpec((None, None, lq, CB), lambda b, r, jj, i: (base + jj, b, i, r))
    prev = lambda base: pl.BlockSpec(
        (None, None, BLOCK, CB), lambda b, r, jj, i: (base + jj, b, jnp.maximum(i * bpq - 1, 0), r))
    out_spec = pl.BlockSpec((None, None, lq, CB), lambda b, r, jj, i: (jj, b, i, r))
    o, lse = pl.pallas_call(
        functools.partial(_attn_kernel, lq=lq, max_dist=window // dil, kv_shared=False,
                          has_sink=False, want_lse=True, heads_per_step=4, seq_axis=3),
        grid=(batch, dil, A_BLOCKS, sub_len // lq),
        in_specs=[cur(qb), cur(kb), cur(vb), prev(kb), prev(vb)],
        out_specs=[out_spec, out_spec],
        out_shape=[jax.ShapeDtypeStruct((A_BLOCKS, batch, sub_len, dil * CB), BF16),
                   jax.ShapeDtypeStruct((A_BLOCKS, batch, sub_len, dil * CB), F32)],
        compiler_params=pltpu.CompilerParams(
            dimension_semantics=("arbitrary",) * 4, vmem_limit_bytes=VMEM_LIMIT),
        name=f"dilated_attn_g{group}",
    )(z4, z4, z4, z4, z4)
    rows = batch * sub_len
    return o.reshape(A_BLOCKS, rows, dil * CB), lse.reshape(A_BLOCKS, rows, dil * CB)


def _swa_attention(z_tok, sinks, *, batch, seq, lq):
    z4 = z_tok.reshape(N_TOK_BLOCKS, batch, seq, CB)
    bpq = lq // BLOCK
    n_q_blocks = B_Q_WIDTH // CB
    q_per_kv = n_q_blocks // B_KV_HEADS
    q_spec = pl.BlockSpec((None, None, lq, CB), lambda b, jq, i, s: (ZB_QB + jq, b, i, 0))
    cur = lambda base: pl.BlockSpec((None, None, lq, PAIR), lambda b, jq, i, s: (base, b, i, jq // q_per_kv))
    prev = lambda base: pl.BlockSpec(
        (None, None, BLOCK, PAIR), lambda b, jq, i, s: (base, b, jnp.maximum(i * bpq - 1, 0), jq // q_per_kv))
    o = pl.pallas_call(
        functools.partial(_attn_kernel, lq=lq, max_dist=B_WINDOW - 1, kv_shared=True,
                          has_sink=True, want_lse=False, heads_per_step=4, seq_axis=2),
        grid_spec=pltpu.PrefetchScalarGridSpec(
            num_scalar_prefetch=1,
            grid=(batch, n_q_blocks, seq // lq),
            in_specs=[q_spec, cur(ZB_KB), cur(ZB_VB), prev(ZB_KB), prev(ZB_VB)],
            out_specs=pl.BlockSpec((None, None, lq, CB), lambda b, jq, i, s: (jq, b, i, 0)),
        ),
        out_shape=jax.ShapeDtypeStruct((n_q_blocks, batch, seq, CB), BF16),
        compiler_params=pltpu.CompilerParams(
            dimension_semantics=("arbitrary",) * 3, vmem_limit_bytes=VMEM_LIMIT),
        name="swa_attn",
    )(sinks, z4, z4, z4, z4, z4)
    return o.reshape(n_q_blocks, batch * seq, CB)


def _post_attn_kernel(o0_ref, o1_ref, o2_ref, l0_ref, l1_ref, l2_ref, ob_ref, gate_ref, x_ref,
                      wa_ref, wb_ref, wo_ref, gf_ref, wr_ref, br_ref,
                      po_ref, pl_ref, x1_ref, h2_ref, route_ref, cnt_ref, *, tm):

    def to_token_order(ref, cb, slot, perm_ref, blk, precision):
        dil = DIL_GROUPS[slot + 1][1]
        n = blk // dil
        parts = []
        for tb in range(tm // blk):
            stack = jnp.concatenate(
                [ref[cb, tb * n:(tb + 1) * n, r * CB:(r + 1) * CB] for r in range(dil)], axis=0)
            parts.append(jnp.dot(perm_ref[slot], stack, preferred_element_type=F32, precision=precision))
        return jnp.concatenate(parts, axis=0)

    pa = None
    for cb in range(A_BLOCKS):
        l0 = l0_ref[cb]
        l1 = to_token_order(l1_ref, cb, 0, pl_ref, LSE_PERM_BLOCK, lax.Precision.HIGHEST)
        l2 = to_token_order(l2_ref, cb, 1, pl_ref, LSE_PERM_BLOCK, lax.Precision.HIGHEST)
        o1 = to_token_order(o1_ref, cb, 0, po_ref, OUT_PERM_BLOCK, None)
        o2 = to_token_order(o2_ref, cb, 1, po_ref, OUT_PERM_BLOCK, None)
        mx = jnp.maximum(jnp.maximum(l0, l1), l2)
        e0, e1, e2 = jnp.exp(l0 - mx), jnp.exp(l1 - mx), jnp.exp(l2 - mx)
        inv = 1.0 / (e0 + e1 + e2)
        ya = (e0 * inv) * o0_ref[cb].astype(F32) + (e1 * inv) * o1 + (e2 * inv) * o2
        part = jnp.dot(ya.astype(BF16), wa_ref[cb * CB:(cb + 1) * CB, :], preferred_element_type=F32)
        pa = part if pa is None else pa + part
    pb = None
    for cb in range(B_Q_WIDTH // CB):
        part = jnp.dot(ob_ref[cb], wb_ref[cb * CB:(cb + 1) * CB, :], preferred_element_type=F32)
        pb = part if pb is None else pb + part
    n_gate = D_MODEL // CB
    merged = []
    for cb in range(n_gate):
        cols = slice(cb * CB, (cb + 1) * CB)
        ga = jax.nn.sigmoid(gate_ref[cb].astype(F32))
        gb = jax.nn.sigmoid(gate_ref[n_gate + cb].astype(F32))
        merged.append((ga * pa[:, cols] + gb * pb[:, cols]).astype(BF16))
    merged = jnp.concatenate(merged, axis=1)
    x1 = x_ref[...] + jnp.dot(merged, wo_ref[...], preferred_element_type=F32)
    x1_ref[...] = x1
    h2 = _rms(x1, gf_ref[...])
    h2_ref[...] = h2.astype(BF16)

    logits = jnp.dot(h2, wr_ref[...], preferred_element_type=F32,
                     precision=lax.Precision.HIGHEST) + br_ref[...]
    lane = lax.broadcasted_iota(jnp.int32, logits.shape, 1)
    neg_inf = jnp.float32(-jnp.inf)
    lg = jnp.where(lane < N_EXPERT_GROUPS, logits, neg_inf)
    mg = jnp.max(lg, axis=-1, keepdims=True)
    gsel = jnp.min(jnp.where(lg == mg, lane, ROUTER_LANES), axis=-1, keepdims=True)
    pg_sel = 1.0 / jnp.sum(jnp.exp(lg - mg), axis=-1, keepdims=True)
    expert = lane - EXPERT_LANE0
    in_group = (expert >= 0) & (expert < N_EXPERTS) & ((expert // EXPERTS_PER_GROUP) == gsel)
    le = jnp.where(in_group, logits, neg_inf)
    me = jnp.max(le, axis=-1, keepdims=True)
    ex = jnp.exp(le - me)
    pe = ex / jnp.sum(ex, axis=-1, keepdims=True)
    p1 = jnp.max(pe, axis=-1, keepdims=True)
    i1 = jnp.min(jnp.where(in_group & (pe == p1), lane, ROUTER_LANES), axis=-1, keepdims=True)
    rest = in_group & (lane != i1)
    p2 = jnp.max(jnp.where(rest, pe, -1.0), axis=-1, keepdims=True)
    i2 = jnp.min(jnp.where(rest & (pe == p2), lane, ROUTER_LANES), axis=-1, keepdims=True)
    norm = pg_sel / (p1 + p2)
    route_ref[...] = jnp.where(
        lane == 0, (i1 - EXPERT_LANE0).astype(F32),
        jnp.where(lane == 1, (i2 - EXPERT_LANE0).astype(F32),
                  jnp.where(lane == 2, p1 * norm, jnp.where(lane == 3, p2 * norm, 0.0))))
    hits = jnp.where((lane == i1) | (lane == i2), 1.0, 0.0)
    cnt_ref[0] = jnp.broadcast_to(jnp.sum(hits, axis=0, keepdims=True), cnt_ref.shape[1:])


def _post_attn(oa, lse, ob, z_tok, x2, w_proj_a, w_proj_b, w_out, g_ffn, w_router, b_router, *, tm):
    n_tok = x2.shape[0]
    const2 = lambda i: (0, 0)
    blk = lambda nb: pl.BlockSpec((nb, tm, CB), lambda i: (0, i, 0))
    dil_blk = lambda dil: pl.BlockSpec((A_BLOCKS, tm // dil, dil * CB), lambda i: (0, i, 0))
    row = lambda width: pl.BlockSpec((tm, width), lambda i: (i, 0))
    resident = lambda shape: pl.BlockSpec(shape, const2, pipeline_mode=pl.Buffered(1))
    d1, d2 = DIL_GROUPS[1][1], DIL_GROUPS[2][1]
    perm_o = jnp.stack([_dilation_perm(OUT_PERM_BLOCK, d, BF16).T for d in (d1, d2)])
    perm_l = jnp.stack([_dilation_perm(LSE_PERM_BLOCK, d, F32).T for d in (d1, d2)])
    return pl.pallas_call(
        functools.partial(_post_attn_kernel, tm=tm),
        grid=(n_tok // tm,),
        in_specs=[blk(A_BLOCKS), dil_blk(d1), dil_blk(d2), blk(A_BLOCKS), dil_blk(d1), dil_blk(d2),
                  blk(B_Q_WIDTH // CB), blk(GATE_WIDTH // CB), row(D_MODEL),
                  resident((A_GROUP_WIDTH, D_MODEL)), resident((B_Q_WIDTH, D_MODEL)),
                  resident((D_MODEL, D_MODEL)), resident((1, D_MODEL)),
                  resident((D_MODEL, ROUTER_LANES)), resident((1, ROUTER_LANES)),
                  pl.BlockSpec((2, OUT_PERM_BLOCK, OUT_PERM_BLOCK), lambda i: (0, 0, 0),
                               pipeline_mode=pl.Buffered(1)),
                  pl.BlockSpec((2, LSE_PERM_BLOCK, LSE_PERM_BLOCK), lambda i: (0, 0, 0),
                               pipeline_mode=pl.Buffered(1))],
        out_specs=[row(D_MODEL), row(D_MODEL), row(ROUTER_LANES),
                   pl.BlockSpec((1, 8, ROUTER_LANES), lambda i: (i, 0, 0))],
        out_shape=[jax.ShapeDtypeStruct((n_tok, D_MODEL), F32),
                   jax.ShapeDtypeStruct((n_tok, D_MODEL), BF16),
                   jax.ShapeDtypeStruct((n_tok, ROUTER_LANES), F32),
                   jax.ShapeDtypeStruct((n_tok // tm, 8, ROUTER_LANES), F32)],
        compiler_params=pltpu.CompilerParams(
            dimension_semantics=("arbitrary",), vmem_limit_bytes=VMEM_LIMIT),
        name="post_attn",
    )(oa[0], oa[1], oa[2], lse[0], lse[1], lse[2], ob, z_tok, x2,
      w_proj_a, w_proj_b, w_out, g_ffn, w_router, b_router, perm_o, perm_l)


def _local_slots(route, lbase_row, ltri):
    lane = lax.broadcasted_iota(jnp.int32, route.shape, 1)
    pick = lambda k: jnp.sum(jnp.where(lane == k, route, 0.0), axis=-1, keepdims=True)
    lanef = lane.astype(F32)
    oh1, oh2 = lanef == pick(0), lanef == pick(1)
    oh = jnp.where(oh1 | oh2, 1.0, 0.0).astype(BF16)
    table = jnp.dot(ltri, oh, preferred_element_type=F32) + lbase_row
    ls1 = jnp.sum(jnp.where(oh1, table, 0.0), axis=-1, keepdims=True)
    ls2 = jnp.sum(jnp.where(oh2, table, 0.0), axis=-1, keepdims=True)
    return ls1, ls2, pick


def _split3(w):
    hi = w.astype(BF16).astype(F32)
    mid = (w - hi).astype(BF16).astype(F32)
    lo = (w - hi - mid).astype(BF16).astype(F32)
    return hi, mid, lo


def _chunk_loop(count, body):
    lax.fori_loop(0, count, lambda c, carry: (body(c), carry)[1], 0)


def _dispatch_kernel(nch_ref, lb_ref, dst_ref, tot_ref, tail_start_ref, tail_n_ref,
                     h2_ref, route_ref, lbase_ref, ltri_ref, xs_ref, buf_ref, zero_ref, sem_ref, zsem_ref,
                     *, tm):
    i = pl.program_id(0)
    last = pl.num_programs(0) - 1
    slot = i % 2

    def chunk(slot_, src_row, dst_row):
        return pltpu.make_async_copy(buf_ref.at[slot_, pl.ds(src_row, CHUNK), :],
                                     xs_ref.at[pl.ds(dst_row, CHUNK), :], sem_ref.at[slot_])

    def wait_tile(slot_, tile):
        _chunk_loop(tot_ref[tile], lambda c: chunk(slot_, 0, 0).wait())

    @pl.when(i == 0)
    def _():
        zero_ref[...] = jnp.zeros_like(zero_ref)

        def zero_chunk(row):
            return pltpu.make_async_copy(zero_ref, xs_ref.at[pl.ds(row, CHUNK), :], zsem_ref)

        def per_expert(e, total):
            start = tail_start_ref[e] * CHUNK
            _chunk_loop(tail_n_ref[e], lambda c: zero_chunk(pl.multiple_of(start + c * CHUNK, CHUNK)).start())
            return total + tail_n_ref[e]

        total = lax.fori_loop(0, N_EXPERTS, per_expert, 0)
        _chunk_loop(total, lambda c: zero_chunk(0).wait())

    @pl.when(i >= 2)
    def _():
        wait_tile(slot, i - 2)

    route = route_ref[...]
    ls1, ls2, pick = _local_slots(route, lbase_ref[0, 0:1, :], ltri_ref[...])
    lane = lax.broadcasted_iota(jnp.int32, route.shape, 1)
    ls_t = jnp.where(lane == 0, ls1, jnp.where(lane == 1, ls2, 0.0)).T
    jrow = lax.broadcasted_iota(jnp.int32, (LOCAL_ROWS, tm), 0).astype(F32)
    sel = jnp.where((jrow == ls_t[0:1, :]) | (jrow == ls_t[1:2, :]), 1.0, 0.0).astype(BF16)
    pieces = _split3(pick(2)) + _split3(pick(3)) + (pick(0),)
    meta = jnp.zeros(route.shape, F32)
    for k, piece in enumerate(pieces):
        meta = jnp.where(lane == k, piece, meta)
    buf_ref[slot, :, :D_MODEL] = jnp.dot(sel, h2_ref[...], preferred_element_type=F32).astype(BF16)
    buf_ref[slot, :, D_MODEL:] = jnp.dot(sel, meta.astype(BF16), preferred_element_type=F32).astype(BF16)

    def per_expert(e, carry):
        idx = i * N_EXPERTS + e
        src0 = lb_ref[idx] * CHUNK
        dst0 = dst_ref[idx] * CHUNK
        _chunk_loop(nch_ref[idx], lambda c: chunk(slot, pl.multiple_of(src0 + c * CHUNK, CHUNK),
                                                  pl.multiple_of(dst0 + c * CHUNK, CHUNK)).start())
        return carry

    lax.fori_loop(0, N_EXPERTS, per_expert, 0)

    @pl.when(i == last)
    def _():
        @pl.when(i >= 1)
        def _():
            wait_tile(1 - slot, i - 1)
        wait_tile(slot, i)


def _expert_kernel(te_ref, nv_ref, xs_ref, wg_ref, wu_ref, wd_ref, ys_ref, wgb_ref, wub_ref, wdb_ref):
    g = pl.program_id(0)
    e = te_ref[g]

    @pl.when((g == 0) | (te_ref[jnp.maximum(g - 1, 0)] != e))
    def _():
        wgb_ref[...] = wg_ref[...].astype(BF16)
        wub_ref[...] = wu_ref[...].astype(BF16)
        wdb_ref[...] = wd_ref[...].astype(BF16)

    @pl.when(g < nv_ref[0])
    def _():
        x = xs_ref[:, :D_MODEL]
        meta = xs_ref[:, D_MODEL:].astype(F32)
        lane = lax.broadcasted_iota(jnp.int32, meta.shape, 1)
        pick = lambda k: jnp.sum(jnp.where(lane == k, meta, 0.0), axis=-1, keepdims=True)
        w1 = pick(0) + pick(1) + pick(2)
        w2 = pick(3) + pick(4) + pick(5)
        w = jnp.where(pick(6) == e.astype(F32), w1, w2)
        hg = jnp.dot(x, wgb_ref[...], preferred_element_type=F32)
        hu = jnp.dot(x, wub_ref[...], preferred_element_type=F32)
        a = (hg * jax.nn.sigmoid(hg)) * hu * w
        ys_ref[...] = jnp.dot(a.astype(BF16), wdb_ref[...], preferred_element_type=F32).astype(BF16)


def _combine_kernel(nch_ref, lb_ref, dst_ref, tot_ref,
                    x1_ref, route_ref, lbase_ref, ltri_ref, gfin_ref, ys_ref, out_ref, ybuf_ref, sem_ref,
                    *, tm):
    i = pl.program_id(0)
    n_tiles = pl.num_programs(0)
    slot = i % 2

    def chunk(slot_, src_row, dst_row):
        return pltpu.make_async_copy(ys_ref.at[pl.ds(src_row, CHUNK), :],
                                     ybuf_ref.at[slot_, pl.ds(dst_row, CHUNK), :], sem_ref.at[slot_])

    def fetch_tile(tile, slot_):
        def per_expert(e, carry):
            idx = tile * N_EXPERTS + e
            src0 = dst_ref[idx] * CHUNK
            dst0 = lb_ref[idx] * CHUNK
            _chunk_loop(nch_ref[idx], lambda c: chunk(slot_, pl.multiple_of(src0 + c * CHUNK, CHUNK),
                                                      pl.multiple_of(dst0 + c * CHUNK, CHUNK)).start())
            return carry
        lax.fori_loop(0, N_EXPERTS, per_expert, 0)

    @pl.when(i == 0)
    def _():
        ybuf_ref[...] = jnp.zeros_like(ybuf_ref)
        fetch_tile(0, 0)

    @pl.when(i + 1 < n_tiles)
    def _():
        fetch_tile(i + 1, 1 - slot)

    _chunk_loop(tot_ref[i], lambda c: chunk(slot, 0, 0).wait())

    ls1, ls2, _ = _local_slots(route_ref[...], lbase_ref[0, 0:1, :], ltri_ref[...])
    jlane = lax.broadcasted_iota(jnp.int32, (tm, LOCAL_ROWS), 1).astype(F32)
    sel = jnp.where((jlane == ls1) | (jlane == ls2), 1.0, 0.0).astype(BF16)
    y = jnp.dot(sel, ybuf_ref[slot], preferred_element_type=F32)
    out_ref[...] = _rms(x1_ref[...] + y, gfin_ref[...])


def _routing_tables(cnt, n_exp_tiles):
    c16 = (cnt + CHUNK - 1) // CHUNK
    lbase = jnp.cumsum(c16, axis=1) - c16
    tile_off = jnp.cumsum(c16, axis=0) - c16
    tot = jnp.sum(c16, axis=0)
    per = EXP_TILE // CHUNK
    region_tiles = (tot + per - 1) // per
    region = region_tiles * per
    base = jnp.cumsum(region) - region
    dst = base[None, :] + tile_off
    tile_end = jnp.cumsum(region_tiles)
    n_valid = tile_end[-1]
    g = jnp.arange(n_exp_tiles, dtype=jnp.int32)
    tile_expert = jnp.searchsorted(tile_end, jnp.minimum(g, n_valid - 1), side="right").astype(jnp.int32)
    i32 = lambda a: a.astype(jnp.int32).reshape(-1)
    return dict(nch=i32(c16), lb=i32(lbase), dst=i32(dst), tot=i32(jnp.sum(c16, axis=1)),
                tail_start=i32(base + tot), tail_n=i32(region - tot),
                tile_expert=tile_expert, n_valid=i32(n_valid),
                lbase_rows=(lbase * CHUNK).astype(F32))


def _moe(h2, route, cnt_tiles, x1, w_gate, w_up, w_down, g_final):
    n_tok = h2.shape[0]
    tm = MOE_TILE
    n_tiles = n_tok // tm
    assert cnt_tiles.shape[0] == n_tiles
    worst_rows = 2 * n_tok + n_tiles * N_EXPERTS * (CHUNK - 1) + N_EXPERTS * (EXP_TILE - CHUNK)
    n_exp_tiles = -(-worst_rows // EXP_TILE)
    n_slots = n_exp_tiles * EXP_TILE

    cnt = cnt_tiles[:, 0, EXPERT_LANE0:EXPERT_LANE0 + N_EXPERTS].astype(jnp.int32)
    tb = _routing_tables(cnt, n_exp_tiles)
    lbase_rows = jnp.zeros((n_tiles, 8, ROUTER_LANES), F32).at[:, :, :N_EXPERTS].set(
        tb["lbase_rows"][:, None, :])
    row_id = jnp.arange(tm)
    ltri = (row_id[:, None] > row_id[None, :]).astype(BF16)

    tile_row = lambda width: pl.BlockSpec((tm, width), lambda i, *_: (i, 0))
    lbase_spec = pl.BlockSpec((1, 8, ROUTER_LANES), lambda i, *_: (i, 0, 0))
    ltri_spec = pl.BlockSpec((tm, tm), lambda i, *_: (0, 0), pipeline_mode=pl.Buffered(1))
    hbm = pl.BlockSpec(memory_space=pl.ANY)

    xs = pl.pallas_call(
        functools.partial(_dispatch_kernel, tm=tm),
        grid_spec=pltpu.PrefetchScalarGridSpec(
            num_scalar_prefetch=6, grid=(n_tiles,),
            in_specs=[tile_row(D_MODEL), tile_row(ROUTER_LANES), lbase_spec, ltri_spec],
            out_specs=hbm,
            scratch_shapes=[pltpu.VMEM((2, LOCAL_ROWS, XS_WIDTH), BF16), pltpu.VMEM((CHUNK, XS_WIDTH), BF16),
                            pltpu.SemaphoreType.DMA((2,)), pltpu.SemaphoreType.DMA]),
        out_shape=jax.ShapeDtypeStruct((n_slots, XS_WIDTH), BF16),
        compiler_params=pltpu.CompilerParams(
            dimension_semantics=("arbitrary",), vmem_limit_bytes=VMEM_LIMIT),
        name="moe_dispatch",
    )(tb["nch"], tb["lb"], tb["dst"], tb["tot"], tb["tail_start"], tb["tail_n"],
      h2, route, lbase_rows, ltri)

    row_tile = lambda width: pl.BlockSpec(
        (EXP_TILE, width), lambda g, te, nv: (jnp.minimum(g, nv[0] - 1), 0))
    ys = pl.pallas_call(
        _expert_kernel,
        grid_spec=pltpu.PrefetchScalarGridSpec(
            num_scalar_prefetch=2, grid=(n_exp_tiles,),
            in_specs=[row_tile(XS_WIDTH),
                      pl.BlockSpec((None, D_MODEL, D_EXPERT), lambda g, te, nv: (te[g], 0, 0)),
                      pl.BlockSpec((None, D_MODEL, D_EXPERT), lambda g, te, nv: (te[g], 0, 0)),
                      pl.BlockSpec((None, D_EXPERT, D_MODEL), lambda g, te, nv: (te[g], 0, 0))],
            out_specs=row_tile(D_MODEL),
            scratch_shapes=[pltpu.VMEM((D_MODEL, D_EXPERT), BF16), pltpu.VMEM((D_MODEL, D_EXPERT), BF16),
                            pltpu.VMEM((D_EXPERT, D_MODEL), BF16)]),
        out_shape=jax.ShapeDtypeStruct((n_slots, D_MODEL), BF16),
        compiler_params=pltpu.CompilerParams(
            dimension_semantics=("arbitrary",), vmem_limit_bytes=VMEM_LIMIT),
        name="moe_expert",
    )(tb["tile_expert"], tb["n_valid"], xs, w_gate, w_up, w_down)

    return pl.pallas_call(
        functools.partial(_combine_kernel, tm=tm),
        grid_spec=pltpu.PrefetchScalarGridSpec(
            num_scalar_prefetch=4, grid=(n_tiles,),
            in_specs=[tile_row(D_MODEL), tile_row(ROUTER_LANES), lbase_spec, ltri_spec,
                      pl.BlockSpec((1, D_MODEL), lambda i, *_: (0, 0)), hbm],
            out_specs=tile_row(D_MODEL),
            scratch_shapes=[pltpu.VMEM((2, LOCAL_ROWS, D_MODEL), BF16), pltpu.SemaphoreType.DMA((2,))]),
        out_shape=jax.ShapeDtypeStruct((n_tok, D_MODEL), F32),
        compiler_params=pltpu.CompilerParams(
            dimension_semantics=("arbitrary",), vmem_limit_bytes=VMEM_LIMIT),
        name="moe_combine",
    )(tb["nch"], tb["lb"], tb["dst"], tb["tot"], x1, route, lbase_rows, ltri, g_final, ys)


def _rope_tables(seq, tm):
    pos = jnp.arange(seq, dtype=F32)
    inv_freq = ROPE_THETA ** (-jnp.arange(0, HEAD_DIM, 2, dtype=F32) / HEAD_DIM)
    ang = pos[:, None] * inv_freq[None, :]
    cos, sin = jnp.cos(ang), jnp.sin(ang)
    reps = CB // HEAD_DIM
    cos_t = jnp.tile(jnp.concatenate([cos, cos], axis=-1), (1, reps))
    sin_t = jnp.tile(jnp.concatenate([-sin, sin], axis=-1), (1, reps))

    def reorder(t, dil):
        return t.reshape(seq // tm, tm // dil, dil, CB).transpose(0, 2, 1, 3).reshape(seq, CB)

    dils = [dil for _, dil in DIL_GROUPS]
    return (jnp.stack([reorder(cos_t, dil) for dil in dils]),
            jnp.stack([reorder(sin_t, dil) for dil in dils]))


def kernel(x, w_in, b_in, sinks, w_proj_a, w_proj_b, w_out, g_mix, g_ffn, w_router_group, b_router_group,
           w_router_expert, b_router_expert, w_exp_gate, w_exp_up, w_exp_down, g_final):
    batch, seq, d = x.shape
    assert d == D_MODEL and w_in.shape[0] == 1, "single-layer kernel"
    n_tok = batch * seq
    x2 = x.reshape(n_tok, d)
    tm_in = 512
    cos_t, sin_t = _rope_tables(seq, tm_in)

    z_tok, z_d1, z_d2 = _in_proj(x2, g_mix[0][None, :], w_in[0].astype(BF16), b_in[0][None, :],
                                 cos_t, sin_t, seq=seq, tm=tm_in)

    a_bases = (0, A_BLOCKS, 2 * A_BLOCKS)
    tok_bases = tuple(ZB_A0 + b for b in a_bases)
    oa, lse = [], []
    for group, (z, bases) in enumerate(((z_tok, tok_bases), (z_d1, a_bases), (z_d2, a_bases))):
        o_g, l_g = _dilated_attention(z, bases, group, batch=batch, seq=seq, lq=512)
        oa.append(o_g)
        lse.append(l_g)
    ob = _swa_attention(z_tok, sinks[0], batch=batch, seq=seq, lq=512)

    pad = ROUTER_LANES - N_EXPERT_GROUPS - N_EXPERTS
    w_router = jnp.concatenate(
        [w_router_group[0], w_router_expert[0], jnp.zeros((d, pad), F32)], axis=1)
    b_router = jnp.concatenate(
        [b_router_group[0], b_router_expert[0], jnp.zeros((pad,), F32)])[None, :]
    x1, h2, route, cnt_tiles = _post_attn(
        oa, lse, ob, z_tok, x2, w_proj_a[0].astype(BF16), w_proj_b[0].astype(BF16), w_out[0].astype(BF16),
        g_ffn[0][None, :], w_router, b_router, tm=MOE_TILE)

    out = _moe(h2, route, cnt_tiles, x1, w_exp_gate[0], w_exp_up[0], w_exp_down[0], g_final[None, :])
    return out.reshape(batch, seq, d)
```

```python
import functools

import jax
import jax.numpy as jnp
from jax import lax
from jax.experimental import pallas as pl
from jax.experimental.pallas import tpu as pltpu

F32 = jnp.float32
BF16 = jnp.bfloat16

D_MODEL = 1024
HEAD_DIM = 64
HALF = HEAD_DIM // 2
ROPE_THETA = 10000.0
RMS_EPS = 1e-6
BLOCK = 128
DIL_GROUPS = ((128, 1), (512, 4), (2048, 16))
N_DIL = len(DIL_GROUPS)
A_GROUP_WIDTH = 512
A_QKV_WIDTH = 3 * N_DIL * A_GROUP_WIDTH
B_Q_HEADS = 16
B_KV_HEADS = 2
B_Q_WIDTH = B_Q_HEADS * HEAD_DIM
B_WINDOW = 128
GATE_WIDTH = 2 * D_MODEL
IN_WIDTH = A_QKV_WIDTH + B_Q_WIDTH + 2 * B_KV_HEADS * HEAD_DIM + GATE_WIDTH
N_EXPERT_GROUPS = 4
EXPERTS_PER_GROUP = 8
N_EXPERTS = N_EXPERT_GROUPS * EXPERTS_PER_GROUP
D_EXPERT = D_MODEL // 4

CB = 256
PAIR = 128
N_IN_BLOCKS = IN_WIDTH // CB
A_BLOCKS = A_GROUP_WIDTH // CB
ZB_GATE = 0
ZB_QB = 8
ZB_KB = 12
ZB_VB = 13
ZB_A0 = 14
N_TOK_BLOCKS = ZB_A0 + 3 * A_BLOCKS
OUT_PERM_BLOCK = 256
LSE_PERM_BLOCK = 128
ROUTER_LANES = 128
EXPERT_LANE0 = N_EXPERT_GROUPS
MOE_TILE = 512
CHUNK = 16
EXP_TILE = 512
LOCAL_ROWS = -(-(2 * MOE_TILE + N_EXPERTS * (CHUNK - 1)) // CB) * CB
XS_WIDTH = D_MODEL + ROUTER_LANES

VMEM_LIMIT = 56 * 1024 * 1024


def _in_proj_plan():
    plan = []
    for c in range(N_IN_BLOCKS):
        col = c * CB
        if col < A_QKV_WIDTH:
            part, rem = divmod(col, N_DIL * A_GROUP_WIDTH)
            group, blk = divmod(rem // CB, A_BLOCKS)
            kind = ("q", "k", "v")[part]
            dil = DIL_GROUPS[group][1]
            if dil == 1:
                plan.append((0, ZB_A0 + part * A_BLOCKS + blk, kind, 1))
            else:
                plan.append((group, part * A_BLOCKS + blk, kind, dil))
        elif col < A_QKV_WIDTH + B_Q_WIDTH:
            plan.append((0, ZB_QB + (col - A_QKV_WIDTH) // CB, "q", 1))
        elif col < A_QKV_WIDTH + B_Q_WIDTH + CB:
            plan.append((0, -1, "kvb", 1))
        else:
            plan.append((0, ZB_GATE + (col - (A_QKV_WIDTH + B_Q_WIDTH + CB)) // CB, "v", 1))
    return tuple(plan)


def _rms(x, g):
    return x * lax.rsqrt(jnp.mean(x * x, axis=-1, keepdims=True) + RMS_EPS) * g


def _split3(w):
    hi = w.astype(BF16).astype(F32)
    mid = (w - hi).astype(BF16).astype(F32)
    lo = (w - hi - mid).astype(BF16).astype(F32)
    return hi, mid, lo


def _rope(acc, cos, sin_signed, first_half):
    partner = jnp.where(first_half, pltpu.roll(acc, CB - HALF, 1), pltpu.roll(acc, HALF, 1))
    return acc * cos + partner * sin_signed


def _in_proj_kernel(x_ref, g_ref, w_ref, b_ref, cos_ref, sin_ref, perm_ref, zt_ref, zd1_ref, zd2_ref,
                    *, plan, tm):
    out_refs = (zt_ref, zd1_ref, zd2_ref)
    lane = lax.broadcasted_iota(jnp.int32, (tm, CB), 1)
    first_half = (lane % HEAD_DIM) < HALF

    h = _rms(x_ref[...], g_ref[...]).astype(BF16)
    h_by_dil, tables = {}, {}
    for slot, (_, dil) in enumerate(DIL_GROUPS):
        tables[dil] = slot
        if dil == 1:
            h_by_dil[dil] = h
        else:
            h_by_dil[dil] = jnp.dot(perm_ref[slot - 1], h, preferred_element_type=F32).astype(BF16)

    for c, (arr, dst, kind, dil) in enumerate(plan):
        cols = slice(c * CB, (c + 1) * CB)
        acc = jnp.dot(h_by_dil[dil], w_ref[:, cols], preferred_element_type=F32) + b_ref[:, cols]
        if kind in ("q", "k", "kvb"):
            slot = tables[dil]
            rot = _rope(acc, cos_ref[slot], sin_ref[slot], first_half)
        if kind == "q":
            val = (rot * (HEAD_DIM ** -0.5)).astype(BF16)
        elif kind == "k":
            val = rot.astype(BF16)
        elif kind == "v":
            val = acc.astype(BF16)
        else:
            r64 = pltpu.roll(rot, HEAD_DIM, 1)
            r128 = pltpu.roll(rot, 2 * HEAD_DIM, 1)
            kdup = jnp.where(lane < HEAD_DIM, rot, jnp.where(lane < 3 * HEAD_DIM, r64, r128))
            a128 = pltpu.roll(acc, 2 * HEAD_DIM, 1)
            a192 = pltpu.roll(acc, 3 * HEAD_DIM, 1)
            vdup = jnp.where(lane < HEAD_DIM, a128, jnp.where(lane < 3 * HEAD_DIM, a192, acc))
            zt_ref[ZB_KB] = kdup.astype(BF16)
            zt_ref[ZB_VB] = vdup.astype(BF16)
            continue
        if dil == 1:
            out_refs[arr][dst] = val
        else:
            n = tm // dil
            for r in range(dil):
                out_refs[arr][dst, :, r * CB:(r + 1) * CB] = val[r * n:(r + 1) * n, :]


def _dilation_perm(block, dil, dtype):
    j = jnp.arange(block)
    src = (j % (block // dil)) * dil + j // (block // dil)
    return (src[:, None] == jnp.arange(block)[None, :]).astype(dtype)


def _in_proj(x2, g_mix, w_in, b_in, cos_t, sin_t, *, seq, tm):
    n_tok = x2.shape[0]
    tiles_per_seq = seq // tm
    const = lambda i: (0, 0)
    table = pl.BlockSpec((N_DIL, tm, CB), lambda i: (0, i % tiles_per_seq, 0))
    d1, d2 = DIL_GROUPS[1][1], DIL_GROUPS[2][1]
    perms = jnp.stack([_dilation_perm(tm, d1, BF16), _dilation_perm(tm, d2, BF16)])
    return pl.pallas_call(
        functools.partial(_in_proj_kernel, plan=_in_proj_plan(), tm=tm),
        grid=(n_tok // tm,),
        in_specs=[
            pl.BlockSpec((tm, D_MODEL), lambda i: (i, 0)),
            pl.BlockSpec((1, D_MODEL), const),
            pl.BlockSpec((D_MODEL, IN_WIDTH), const, pipeline_mode=pl.Buffered(1)),
            pl.BlockSpec((1, IN_WIDTH), const),
            table, table,
            pl.BlockSpec((2, tm, tm), lambda i: (0, 0, 0), pipeline_mode=pl.Buffered(1)),
        ],
        out_specs=[
            pl.BlockSpec((N_TOK_BLOCKS, tm, CB), lambda i: (0, i, 0)),
            pl.BlockSpec((3 * A_BLOCKS, tm // d1, d1 * CB), lambda i: (0, i, 0)),
            pl.BlockSpec((3 * A_BLOCKS, tm // d2, d2 * CB), lambda i: (0, i, 0)),
        ],
        out_shape=[
            jax.ShapeDtypeStruct((N_TOK_BLOCKS, n_tok, CB), BF16),
            jax.ShapeDtypeStruct((3 * A_BLOCKS, n_tok // d1, d1 * CB), BF16),
            jax.ShapeDtypeStruct((3 * A_BLOCKS, n_tok // d2, d2 * CB), BF16),
        ],
        compiler_params=pltpu.CompilerParams(
            dimension_semantics=("arbitrary",), vmem_limit_bytes=VMEM_LIMIT),
        name="in_proj",
    )(x2, g_mix, w_in, b_in, cos_t, sin_t, perms)


def _attn_kernel(*refs, lq, max_dist, kv_shared, has_sink, want_lse, heads_per_step, seq_axis):
    refs = list(refs)
    sink_ref = refs.pop(0) if has_sink else None
    q_ref, k_ref, v_ref, kp_ref, vp_ref = refs[:5]
    o_ref = refs[5]
    lse_ref = refs[6] if want_lse else None

    row = lax.broadcasted_iota(jnp.int32, (BLOCK, 2 * BLOCK), 0)
    col = lax.broadcasted_iota(jnp.int32, (BLOCK, 2 * BLOCK), 1)
    dist = row - col + BLOCK
    valid = (dist >= 0) & (dist <= max_dist)
    neg_inf = jnp.float32(-jnp.inf)
    bias = jnp.where(valid, 0.0, neg_inf)
    bias_first = jnp.where(valid & (col >= BLOCK), 0.0, neg_inf)
    bias0 = jnp.where(pl.program_id(seq_axis) == 0, bias_first, bias)
    lane_lo = lax.broadcasted_iota(jnp.int32, (BLOCK, PAIR), 1) < HEAD_DIM

    for ib in range(lq // BLOCK):
        rows = slice(ib * BLOCK, (ib + 1) * BLOCK)
        b_ib = bias0 if ib == 0 else bias
        for pp in range(2):
            qcols = slice(pp * PAIR, (pp + 1) * PAIR)
            kcols = slice(0, PAIR) if kv_shared else qcols
            if ib == 0:
                k_win = jnp.concatenate([kp_ref[:, kcols], k_ref[0:BLOCK, kcols]], axis=0)
                v_win = jnp.concatenate([vp_ref[:, kcols], v_ref[0:BLOCK, kcols]], axis=0)
            else:
                win = slice((ib - 1) * BLOCK, (ib + 1) * BLOCK)
                k_win = k_ref[win, kcols]
                v_win = v_ref[win, kcols]
            q_pair = q_ref[rows, qcols]
            outs, lses = [], []
            for hh in range(2):
                q_h = jnp.where(lane_lo == (hh == 0), q_pair, jnp.zeros_like(q_pair))
                s = lax.dot_general(q_h, k_win, (((1,), (1,)), ((), ())), preferred_element_type=F32)
                s = s + b_ib
                m = jnp.max(s, axis=-1, keepdims=True)
                if has_sink:
                    sink = sink_ref[pl.program_id(1) * heads_per_step + pp * 2 + hh]
                    m = jnp.maximum(m, sink)
                p = jnp.exp(s - m)
                den = jnp.sum(p, axis=-1, keepdims=True)
                if has_sink:
                    den = den + jnp.exp(sink - m)
                o_h = jnp.dot(p.astype(BF16), v_win, preferred_element_type=F32)
                outs.append(o_h * (1.0 / den))
                if want_lse:
                    lses.append(jnp.broadcast_to(m + jnp.log(den), (BLOCK, PAIR)))
            o_ref[rows, qcols] = jnp.where(lane_lo, outs[0], outs[1]).astype(BF16)
            if want_lse:
                lse_ref[rows, qcols] = jnp.where(lane_lo, lses[0], lses[1])


def _dilated_attention(z, bases, group, *, batch, seq, lq):
    window, dil = DIL_GROUPS[group]
    sub_len = seq // dil
    lq = min(lq, sub_len)
    z4 = z.reshape(z.shape[0], batch, sub_len, dil * CB)
    qb, kb, vb = bases
    bpq = lq // BLOCK
    cur = lambda base: pl.BlockSpec((None, None, lq, CB), lambda b, r, jj, i: (base + jj, b, i, r))
    prev = lambda base: pl.BlockSpec(
        (None, None, BLOCK, CB), lambda b, r, jj, i: (base + jj, b, jnp.maximum(i * bpq - 1, 0), r))
    out_spec = pl.BlockSpec((None, None, lq, CB), lambda b, r, jj, i: (jj, b, i, r))
    o, lse = pl.pallas_call(
        functools.partial(_attn_kernel, lq=lq, max_dist=window // dil, kv_shared=False,
                          has_sink=False, want_lse=True, heads_per_step=4, seq_axis=3),
        grid=(batch, dil, A_BLOCKS, sub_len // lq),
        in_specs=[cur(qb), cur(kb), cur(vb), prev(kb), prev(vb)],
        out_specs=[out_spec, out_spec],
        out_shape=[jax.ShapeDtypeStruct((A_BLOCKS, batch, sub_len, dil * CB), BF16),
                   jax.ShapeDtypeStruct((A_BLOCKS, batch, sub_len, dil * CB), F32)],
        compiler_params=pltpu.CompilerParams(
            dimension_semantics=("arbitrary",) * 4, vmem_limit_bytes=VMEM_LIMIT),
        name=f"dilated_attn_g{group}",
    )(z4, z4, z4, z4, z4)
    rows = batch * sub_len
    return o.reshape(A_BLOCKS, rows, dil * CB), lse.reshape(A_BLOCKS, rows, dil * CB)


def _swa_attention(z_tok, sinks, *, batch, seq, lq):
    z4 = z_tok.reshape(N_TOK_BLOCKS, batch, seq, CB)
    bpq = lq // BLOCK
    n_q_blocks = B_Q_WIDTH // CB
    q_per_kv = n_q_blocks // B_KV_HEADS
    q_spec = pl.BlockSpec((None, None, lq, CB), lambda b, jq, i, s: (ZB_QB + jq, b, i, 0))
    cur = lambda base: pl.BlockSpec((None, None, lq, PAIR), lambda b, jq, i, s: (base, b, i, jq // q_per_kv))
    prev = lambda base: pl.BlockSpec(
        (None, None, BLOCK, PAIR), lambda b, jq, i, s: (base, b, jnp.maximum(i * bpq - 1, 0), jq // q_per_kv))
    o = pl.pallas_call(
        functools.partial(_attn_kernel, lq=lq, max_dist=B_WINDOW - 1, kv_shared=True,
                          has_sink=True, want_lse=False, heads_per_step=4, seq_axis=2),
        grid_spec=pltpu.PrefetchScalarGridSpec(
            num_scalar_prefetch=1,
            grid=(batch, n_q_blocks, seq // lq),
            in_specs=[q_spec, cur(ZB_KB), cur(ZB_VB), prev(ZB_KB), prev(ZB_VB)],
            out_specs=pl.BlockSpec((None, None, lq, CB), lambda b, jq, i, s: (jq, b, i, 0)),
        ),
        out_shape=jax.ShapeDtypeStruct((n_q_blocks, batch, seq, CB), BF16),
        compiler_params=pltpu.CompilerParams(
            dimension_semantics=("arbitrary",) * 3, vmem_limit_bytes=VMEM_LIMIT),
        name="swa_attn",
    )(sinks, z4, z4, z4, z4, z4)
    return o.reshape(n_q_blocks, batch * seq, CB)


def _post_attn_kernel(o0_ref, o1_ref, o2_ref, l0_ref, l1_ref, l2_ref, ob_ref, gate_ref, x_ref,
                      wa_ref, wb_ref, wo_ref, gf_ref, wr_ref, br_ref,
                      po_ref, pl_ref, x1_ref, h2_ref, route_ref, cnt_ref, *, tm):

    def to_token_order(ref, cb, slot, perm_ref, blk):
        dil = DIL_GROUPS[slot + 1][1]
        n = blk // dil
        parts = []
        for tb in range(tm // blk):
            stack = jnp.concatenate(
                [ref[cb, tb * n:(tb + 1) * n, r * CB:(r + 1) * CB] for r in range(dil)], axis=0)
            pieces = (stack,) if stack.dtype == BF16 else _split3(stack)
            moved = [jnp.dot(perm_ref[slot], p.astype(BF16), preferred_element_type=F32) for p in pieces]
            parts.append(functools.reduce(lambda a, b: a + b, moved))
        return jnp.concatenate(parts, axis=0)

    pa = None
    for cb in range(A_BLOCKS):
        l0 = l0_ref[cb]
        l1 = to_token_order(l1_ref, cb, 0, pl_ref, LSE_PERM_BLOCK)
        l2 = to_token_order(l2_ref, cb, 1, pl_ref, LSE_PERM_BLOCK)
        o1 = to_token_order(o1_ref, cb, 0, po_ref, OUT_PERM_BLOCK)
        o2 = to_token_order(o2_ref, cb, 1, po_ref, OUT_PERM_BLOCK)
        mx = jnp.maximum(jnp.maximum(l0, l1), l2)
        e0, e1, e2 = jnp.exp(l0 - mx), jnp.exp(l1 - mx), jnp.exp(l2 - mx)
        inv = 1.0 / (e0 + e1 + e2)
        ya = (e0 * inv) * o0_ref[cb].astype(F32) + (e1 * inv) * o1 + (e2 * inv) * o2
        part = jnp.dot(ya.astype(BF16), wa_ref[cb * CB:(cb + 1) * CB, :], preferred_element_type=F32)
        pa = part if pa is None else pa + part
    pb = None
    for cb in range(B_Q_WIDTH // CB):
        part = jnp.dot(ob_ref[cb], wb_ref[cb * CB:(cb + 1) * CB, :], preferred_element_type=F32)
        pb = part if pb is None else pb + part
    n_gate = D_MODEL // CB
    merged = []
    for cb in range(n_gate):
        cols = slice(cb * CB, (cb + 1) * CB)
        ga = jax.nn.sigmoid(gate_ref[cb].astype(F32))
        gb = jax.nn.sigmoid(gate_ref[n_gate + cb].astype(F32))
        merged.append((ga * pa[:, cols] + gb * pb[:, cols]).astype(BF16))
    merged = jnp.concatenate(merged, axis=1)
    x1 = x_ref[...] + jnp.dot(merged, wo_ref[...], preferred_element_type=F32)
    x1_ref[...] = x1
    h2 = _rms(x1, gf_ref[...])
    h2_ref[...] = h2.astype(BF16)

    h2_hi = h2.astype(BF16)
    h2_lo = (h2 - h2_hi.astype(F32)).astype(BF16)
    logits = (jnp.dot(h2_hi, wr_ref[0], preferred_element_type=F32)
              + (jnp.dot(h2_lo, wr_ref[0], preferred_element_type=F32)
                 + jnp.dot(h2_hi, wr_ref[1], preferred_element_type=F32))) + br_ref[...]
    lane = lax.broadcasted_iota(jnp.int32, logits.shape, 1)
    neg_inf = jnp.float32(-jnp.inf)
    lg = jnp.where(lane < N_EXPERT_GROUPS, logits, neg_inf)
    mg = jnp.max(lg, axis=-1, keepdims=True)
    gsel = jnp.min(jnp.where(lg == mg, lane, ROUTER_LANES), axis=-1, keepdims=True)
    pg_sel = 1.0 / jnp.sum(jnp.exp(lg - mg), axis=-1, keepdims=True)
    expert = lane - EXPERT_LANE0
    in_group = (expert >= 0) & (expert < N_EXPERTS) & ((expert // EXPERTS_PER_GROUP) == gsel)
    le = jnp.where(in_group, logits, neg_inf)
    me = jnp.max(le, axis=-1, keepdims=True)
    ex = jnp.exp(le - me)
    pe = ex / jnp.sum(ex, axis=-1, keepdims=True)
    p1 = jnp.max(pe, axis=-1, keepdims=True)
    i1 = jnp.min(jnp.where(in_group & (pe == p1), lane, ROUTER_LANES), axis=-1, keepdims=True)
    rest = in_group & (lane != i1)
    p2 = jnp.max(jnp.where(rest, pe, -1.0), axis=-1, keepdims=True)
    i2 = jnp.min(jnp.where(rest & (pe == p2), lane, ROUTER_LANES), axis=-1, keepdims=True)
    norm = pg_sel / (p1 + p2)
    route_ref[...] = jnp.where(
        lane == 0, (i1 - EXPERT_LANE0).astype(F32),
        jnp.where(lane == 1, (i2 - EXPERT_LANE0).astype(F32),
                  jnp.where(lane == 2, p1 * norm, jnp.where(lane == 3, p2 * norm, 0.0))))
    hits = jnp.where((lane == i1) | (lane == i2), 1.0, 0.0)
    cnt_ref[0] = jnp.broadcast_to(jnp.sum(hits, axis=0, keepdims=True), cnt_ref.shape[1:])


def _post_attn(oa, lse, ob, z_tok, x2, w_proj_a, w_proj_b, w_out, g_ffn, w_router, b_router, *, tm):
    n_tok = x2.shape[0]
    const2 = lambda i: (0, 0)
    blk = lambda nb: pl.BlockSpec((nb, tm, CB), lambda i: (0, i, 0))
    dil_blk = lambda dil: pl.BlockSpec((A_BLOCKS, tm // dil, dil * CB), lambda i: (0, i, 0))
    row = lambda width: pl.BlockSpec((tm, width), lambda i: (i, 0))
    resident = lambda shape: pl.BlockSpec(shape, const2, pipeline_mode=pl.Buffered(1))
    d1, d2 = DIL_GROUPS[1][1], DIL_GROUPS[2][1]
    perm_o = jnp.stack([_dilation_perm(OUT_PERM_BLOCK, d, BF16).T for d in (d1, d2)])
    perm_l = jnp.stack([_dilation_perm(LSE_PERM_BLOCK, d, BF16).T for d in (d1, d2)])
    w_router_hi = w_router.astype(BF16)
    w_router_lo = (w_router - w_router_hi.astype(F32)).astype(BF16)
    w_router = jnp.stack([w_router_hi, w_router_lo])
    return pl.pallas_call(
        functools.partial(_post_attn_kernel, tm=tm),
        grid=(n_tok // tm,),
        in_specs=[blk(A_BLOCKS), dil_blk(d1), dil_blk(d2), blk(A_BLOCKS), dil_blk(d1), dil_blk(d2),
                  blk(B_Q_WIDTH // CB), blk(GATE_WIDTH // CB), row(D_MODEL),
                  resident((A_GROUP_WIDTH, D_MODEL)), resident((B_Q_WIDTH, D_MODEL)),
                  resident((D_MODEL, D_MODEL)), resident((1, D_MODEL)),
                  pl.BlockSpec((2, D_MODEL, ROUTER_LANES), lambda i: (0, 0, 0), pipeline_mode=pl.Buffered(1)),
                  resident((1, ROUTER_LANES)),
                  pl.BlockSpec((2, OUT_PERM_BLOCK, OUT_PERM_BLOCK), lambda i: (0, 0, 0),
                               pipeline_mode=pl.Buffered(1)),
                  pl.BlockSpec((2, LSE_PERM_BLOCK, LSE_PERM_BLOCK), lambda i: (0, 0, 0),
                               pipeline_mode=pl.Buffered(1))],
        out_specs=[row(D_MODEL), row(D_MODEL), row(ROUTER_LANES),
                   pl.BlockSpec((1, 8, ROUTER_LANES), lambda i: (i, 0, 0))],
        out_shape=[jax.ShapeDtypeStruct((n_tok, D_MODEL), F32),
                   jax.ShapeDtypeStruct((n_tok, D_MODEL), BF16),
                   jax.ShapeDtypeStruct((n_tok, ROUTER_LANES), F32),
                   jax.ShapeDtypeStruct((n_tok // tm, 8, ROUTER_LANES), F32)],
        compiler_params=pltpu.CompilerParams(
            dimension_semantics=("arbitrary",), vmem_limit_bytes=VMEM_LIMIT),
        name="post_attn",
    )(oa[0], oa[1], oa[2], lse[0], lse[1], lse[2], ob, z_tok, x2,
      w_proj_a, w_proj_b, w_out, g_ffn, w_router, b_router, perm_o, perm_l)


def _local_slots(route, lbase_row, ltri):
    lane = lax.broadcasted_iota(jnp.int32, route.shape, 1)
    pick = lambda k: jnp.sum(jnp.where(lane == k, route, 0.0), axis=-1, keepdims=True)
    lanef = lane.astype(F32)
    oh1, oh2 = lanef == pick(0), lanef == pick(1)
    oh = jnp.where(oh1 | oh2, 1.0, 0.0).astype(BF16)
    table = jnp.dot(ltri, oh, preferred_element_type=F32) + lbase_row
    ls1 = jnp.sum(jnp.where(oh1, table, 0.0), axis=-1, keepdims=True)
    ls2 = jnp.sum(jnp.where(oh2, table, 0.0), axis=-1, keepdims=True)
    return ls1, ls2, pick


def _chunk_loop(count, body):
    lax.fori_loop(0, count, lambda c, carry: (body(c), carry)[1], 0)


def _dispatch_kernel(nch_ref, lb_ref, dst_ref, tot_ref, tail_start_ref, tail_n_ref,
                     h2_ref, route_ref, lbase_ref, ltri_ref, xs_ref, buf_ref, zero_ref, sem_ref, zsem_ref,
                     *, tm):
    i = pl.program_id(0)
    last = pl.num_programs(0) - 1
    slot = i % 2

    def chunk(slot_, src_row, dst_row):
        return pltpu.make_async_copy(buf_ref.at[slot_, pl.ds(src_row, CHUNK), :],
                                     xs_ref.at[pl.ds(dst_row, CHUNK), :], sem_ref.at[slot_])

    def wait_tile(slot_, tile):
        _chunk_loop(tot_ref[tile], lambda c: chunk(slot_, 0, 0).wait())

    @pl.when(i == 0)
    def _():
        zero_ref[...] = jnp.zeros_like(zero_ref)

        def zero_chunk(row):
            return pltpu.make_async_copy(zero_ref, xs_ref.at[pl.ds(row, CHUNK), :], zsem_ref)

        def per_expert(e, total):
            start = tail_start_ref[e] * CHUNK
            _chunk_loop(tail_n_ref[e], lambda c: zero_chunk(pl.multiple_of(start + c * CHUNK, CHUNK)).start())
            return total + tail_n_ref[e]

        total = lax.fori_loop(0, N_EXPERTS, per_expert, 0)
        _chunk_loop(total, lambda c: zero_chunk(0).wait())

    @pl.when(i >= 2)
    def _():
        wait_tile(slot, i - 2)

    route = route_ref[...]
    ls1, ls2, pick = _local_slots(route, lbase_ref[0, 0:1, :], ltri_ref[...])
    lane = lax.broadcasted_iota(jnp.int32, route.shape, 1)
    ls_t = jnp.where(lane == 0, ls1, jnp.where(lane == 1, ls2, 0.0)).T
    jrow = lax.broadcasted_iota(jnp.int32, (LOCAL_ROWS, tm), 0).astype(F32)
    sel = jnp.where((jrow == ls_t[0:1, :]) | (jrow == ls_t[1:2, :]), 1.0, 0.0).astype(BF16)
    pieces = _split3(pick(2)) + _split3(pick(3)) + (pick(0),)
    meta = jnp.zeros(route.shape, F32)
    for k, piece in enumerate(pieces):
        meta = jnp.where(lane == k, piece, meta)
    buf_ref[slot, :, :D_MODEL] = jnp.dot(sel, h2_ref[...], preferred_element_type=F32).astype(BF16)
    buf_ref[slot, :, D_MODEL:] = jnp.dot(sel, meta.astype(BF16), preferred_element_type=F32).astype(BF16)

    def per_expert(e, carry):
        idx = i * N_EXPERTS + e
        src0 = lb_ref[idx] * CHUNK
        dst0 = dst_ref[idx] * CHUNK
        _chunk_loop(nch_ref[idx], lambda c: chunk(slot, pl.multiple_of(src0 + c * CHUNK, CHUNK),
                                                  pl.multiple_of(dst0 + c * CHUNK, CHUNK)).start())
        return carry

    lax.fori_loop(0, N_EXPERTS, per_expert, 0)

    @pl.when(i == last)
    def _():
        @pl.when(i >= 1)
        def _():
            wait_tile(1 - slot, i - 1)
        wait_tile(slot, i)


def _expert_kernel(te_ref, nv_ref, xs_ref, wg_ref, wu_ref, wd_ref, ys_ref, wgb_ref, wub_ref, wdb_ref):
    g = pl.program_id(0)
    e = te_ref[g]

    @pl.when((g == 0) | (te_ref[jnp.maximum(g - 1, 0)] != e))
    def _():
        wgb_ref[...] = wg_ref[...].astype(BF16)
        wub_ref[...] = wu_ref[...].astype(BF16)
        wdb_ref[...] = wd_ref[...].astype(BF16)

    @pl.when(g < nv_ref[0])
    def _():
        x = xs_ref[:, :D_MODEL]
        meta = xs_ref[:, D_MODEL:].astype(F32)
        lane = lax.broadcasted_iota(jnp.int32, meta.shape, 1)
        pick = lambda k: jnp.sum(jnp.where(lane == k, meta, 0.0), axis=-1, keepdims=True)
        w1 = pick(0) + pick(1) + pick(2)
        w2 = pick(3) + pick(4) + pick(5)
        w = jnp.where(pick(6) == e.astype(F32), w1, w2)
        hg = jnp.dot(x, wgb_ref[...], preferred_element_type=F32)
        hu = jnp.dot(x, wub_ref[...], preferred_element_type=F32)
        a = (hg * jax.nn.sigmoid(hg)) * hu * w
        ys_ref[...] = jnp.dot(a.astype(BF16), wdb_ref[...], preferred_element_type=F32).astype(BF16)


def _combine_kernel(nch_ref, lb_ref, dst_ref, tot_ref,
                    x1_ref, route_ref, lbase_ref, ltri_ref, gfin_ref, ys_ref, out_ref, ybuf_ref, sem_ref,
                    *, tm):
    i = pl.program_id(0)
    n_tiles = pl.num_programs(0)
    slot = i % 2

    def chunk(slot_, src_row, dst_row):
        return pltpu.make_async_copy(ys_ref.at[pl.ds(src_row, CHUNK), :],
                                     ybuf_ref.at[slot_, pl.ds(dst_row, CHUNK), :], sem_ref.at[slot_])

    def fetch_tile(tile, slot_):
        def per_expert(e, carry):
            idx = tile * N_EXPERTS + e
            src0 = dst_ref[idx] * CHUNK
            dst0 = lb_ref[idx] * CHUNK
            _chunk_loop(nch_ref[idx], lambda c: chunk(slot_, pl.multiple_of(src0 + c * CHUNK, CHUNK),
                                                      pl.multiple_of(dst0 + c * CHUNK, CHUNK)).start())
            return carry
        lax.fori_loop(0, N_EXPERTS, per_expert, 0)

    @pl.when(i == 0)
    def _():
        ybuf_ref[...] = jnp.zeros_like(ybuf_ref)
        fetch_tile(0, 0)

    @pl.when(i + 1 < n_tiles)
    def _():
        fetch_tile(i + 1, 1 - slot)

    _chunk_loop(tot_ref[i], lambda c: chunk(slot, 0, 0).wait())

    ls1, ls2, _ = _local_slots(route_ref[...], lbase_ref[0, 0:1, :], ltri_ref[...])
    jlane = lax.broadcasted_iota(jnp.int32, (tm, LOCAL_ROWS), 1).astype(F32)
    sel = jnp.where((jlane == ls1) | (jlane == ls2), 1.0, 0.0).astype(BF16)
    y = jnp.dot(sel, ybuf_ref[slot], preferred_element_type=F32)
    out_ref[...] = _rms(x1_ref[...] + y, gfin_ref[...])


def _routing_tables(cnt, n_exp_tiles):
    c16 = (cnt + CHUNK - 1) // CHUNK
    lbase = jnp.cumsum(c16, axis=1) - c16
    tile_off = jnp.cumsum(c16, axis=0) - c16
    tot = jnp.sum(c16, axis=0)
    per = EXP_TILE // CHUNK
    region_tiles = (tot + per - 1) // per
    region = region_tiles * per
    base = jnp.cumsum(region) - region
    dst = base[None, :] + tile_off
    tile_end = jnp.cumsum(region_tiles)
    n_valid = tile_end[-1]
    g = jnp.arange(n_exp_tiles, dtype=jnp.int32)
    tile_expert = jnp.sum(tile_end[None, :] <= jnp.minimum(g, n_valid - 1)[:, None], axis=1).astype(jnp.int32)
    i32 = lambda a: a.astype(jnp.int32).reshape(-1)
    return dict(nch=i32(c16), lb=i32(lbase), dst=i32(dst), tot=i32(jnp.sum(c16, axis=1)),
                tail_start=i32(base + tot), tail_n=i32(region - tot),
                tile_expert=tile_expert, n_valid=i32(n_valid),
                lbase_rows=(lbase * CHUNK).astype(F32))


def _moe(h2, route, cnt_tiles, x1, w_gate, w_up, w_down, g_final):
    n_tok = h2.shape[0]
    tm = MOE_TILE
    n_tiles = n_tok // tm
    assert cnt_tiles.shape[0] == n_tiles
    worst_rows = 2 * n_tok + n_tiles * N_EXPERTS * (CHUNK - 1) + N_EXPERTS * (EXP_TILE - CHUNK)
    n_exp_tiles = -(-worst_rows // EXP_TILE)
    n_slots = n_exp_tiles * EXP_TILE

    cnt = cnt_tiles[:, 0, EXPERT_LANE0:EXPERT_LANE0 + N_EXPERTS].astype(jnp.int32)
    tb = _routing_tables(cnt, n_exp_tiles)
    lbase_rows = jnp.zeros((n_tiles, 8, ROUTER_LANES), F32).at[:, :, :N_EXPERTS].set(
        tb["lbase_rows"][:, None, :])
    row_id = jnp.arange(tm)
    ltri = (row_id[:, None] > row_id[None, :]).astype(BF16)

    tile_row = lambda width: pl.BlockSpec((tm, width), lambda i, *_: (i, 0))
    lbase_spec = pl.BlockSpec((1, 8, ROUTER_LANES), lambda i, *_: (i, 0, 0))
    ltri_spec = pl.BlockSpec((tm, tm), lambda i, *_: (0, 0), pipeline_mode=pl.Buffered(1))
    hbm = pl.BlockSpec(memory_space=pl.ANY)

    xs = pl.pallas_call(
        functools.partial(_dispatch_kernel, tm=tm),
        grid_spec=pltpu.PrefetchScalarGridSpec(
            num_scalar_prefetch=6, grid=(n_tiles,),
            in_specs=[tile_row(D_MODEL), tile_row(ROUTER_LANES), lbase_spec, ltri_spec],
            out_specs=hbm,
            scratch_shapes=[pltpu.VMEM((2, LOCAL_ROWS, XS_WIDTH), BF16), pltpu.VMEM((CHUNK, XS_WIDTH), BF16),
                            pltpu.SemaphoreType.DMA((2,)), pltpu.SemaphoreType.DMA]),
        out_shape=jax.ShapeDtypeStruct((n_slots, XS_WIDTH), BF16),
        compiler_params=pltpu.CompilerParams(
            dimension_semantics=("arbitrary",), vmem_limit_bytes=VMEM_LIMIT),
        name="moe_dispatch",
    )(tb["nch"], tb["lb"], tb["dst"], tb["tot"], tb["tail_start"], tb["tail_n"],
      h2, route, lbase_rows, ltri)

    row_tile = lambda width: pl.BlockSpec(
        (EXP_TILE, width), lambda g, te, nv: (jnp.minimum(g, nv[0] - 1), 0))
    ys = pl.pallas_call(
        _expert_kernel,
        grid_spec=pltpu.PrefetchScalarGridSpec(
            num_scalar_prefetch=2, grid=(n_exp_tiles,),
            in_specs=[row_tile(XS_WIDTH),
                      pl.BlockSpec((None, D_MODEL, D_EXPERT), lambda g, te, nv: (te[g], 0, 0)),
                      pl.BlockSpec((None, D_MODEL, D_EXPERT), lambda g, te, nv: (te[g], 0, 0)),
                      pl.BlockSpec((None, D_EXPERT, D_MODEL), lambda g, te, nv: (te[g], 0, 0))],
            out_specs=row_tile(D_MODEL),
            scratch_shapes=[pltpu.VMEM((D_MODEL, D_EXPERT), BF16), pltpu.VMEM((D_MODEL, D_EXPERT), BF16),
                            pltpu.VMEM((D_EXPERT, D_MODEL), BF16)]),
        out_shape=jax.ShapeDtypeStruct((n_slots, D_MODEL), BF16),
        compiler_params=pltpu.CompilerParams(
            dimension_semantics=("arbitrary",), vmem_limit_bytes=VMEM_LIMIT),
        name="moe_expert",
    )(tb["tile_expert"], tb["n_valid"], xs, w_gate, w_up, w_down)

    return pl.pallas_call(
        functools.partial(_combine_kernel, tm=tm),
        grid_spec=pltpu.PrefetchScalarGridSpec(
            num_scalar_prefetch=4, grid=(n_tiles,),
            in_specs=[tile_row(D_MODEL), tile_row(ROUTER_LANES), lbase_spec, ltri_spec,
                      pl.BlockSpec((1, D_MODEL), lambda i, *_: (0, 0)), hbm],
            out_specs=tile_row(D_MODEL),
            scratch_shapes=[pltpu.VMEM((2, LOCAL_ROWS, D_MODEL), BF16), pltpu.SemaphoreType.DMA((2,))]),
        out_shape=jax.ShapeDtypeStruct((n_tok, D_MODEL), F32),
        compiler_params=pltpu.CompilerParams(
            dimension_semantics=("arbitrary",), vmem_limit_bytes=VMEM_LIMIT),
        name="moe_combine",
    )(tb["nch"], tb["lb"], tb["dst"], tb["tot"], x1, route, lbase_rows, ltri, g_final, ys)


def _rope_tables(seq, tm):
    pos = jnp.arange(seq, dtype=F32)
    inv_freq = ROPE_THETA ** (-jnp.arange(0, HEAD_DIM, 2, dtype=F32) / HEAD_DIM)
    ang = pos[:, None] * inv_freq[None, :]
    cos, sin = jnp.cos(ang), jnp.sin(ang)
    reps = CB // HEAD_DIM
    cos_t = jnp.tile(jnp.concatenate([cos, cos], axis=-1), (1, reps))
    sin_t = jnp.tile(jnp.concatenate([-sin, sin], axis=-1), (1, reps))

    def reorder(t, dil):
        return t.reshape(seq // tm, tm // dil, dil, CB).transpose(0, 2, 1, 3).reshape(seq, CB)

    dils = [dil for _, dil in DIL_GROUPS]
    return (jnp.stack([reorder(cos_t, dil) for dil in dils]),
            jnp.stack([reorder(sin_t, dil) for dil in dils]))


def kernel(x, w_in, b_in, sinks, w_proj_a, w_proj_b, w_out, g_mix, g_ffn, w_router_group, b_router_group,
           w_router_expert, b_router_expert, w_exp_gate, w_exp_up, w_exp_down, g_final):
    batch, seq, d = x.shape
    assert d == D_MODEL and w_in.shape[0] == 1, "single-layer kernel"
    n_tok = batch * seq
    x2 = x.reshape(n_tok, d)
    tm_in = 512
    cos_t, sin_t = _rope_tables(seq, tm_in)

    z_tok, z_d1, z_d2 = _in_proj(x2, g_mix[0][None, :], w_in[0].astype(BF16), b_in[0][None, :],
                                 cos_t, sin_t, seq=seq, tm=tm_in)

    a_bases = (0, A_BLOCKS, 2 * A_BLOCKS)
    tok_bases = tuple(ZB_A0 + b for b in a_bases)
    oa, lse = [], []
    for group, (z, bases) in enumerate(((z_tok, tok_bases), (z_d1, a_bases), (z_d2, a_bases))):
        o_g, l_g = _dilated_attention(z, bases, group, batch=batch, seq=seq, lq=512)
        oa.append(o_g)
        lse.append(l_g)
    ob = _swa_attention(z_tok, sinks[0], batch=batch, seq=seq, lq=512)

    pad = ROUTER_LANES - N_EXPERT_GROUPS - N_EXPERTS
    w_router = jnp.concatenate(
        [w_router_group[0], w_router_expert[0], jnp.zeros((d, pad), F32)], axis=1)
    b_router = jnp.concatenate(
        [b_router_group[0], b_router_expert[0], jnp.zeros((pad,), F32)])[None, :]
    x1, h2, route, cnt_tiles = _post_attn(
        oa, lse, ob, z_tok, x2, w_proj_a[0].astype(BF16), w_proj_b[0].astype(BF16), w_out[0].astype(BF16),
        g_ffn[0][None, :], w_router, b_router, tm=MOE_TILE)

    out = _moe(h2, route, cnt_tiles, x1, w_exp_gate[0], w_exp_up[0], w_exp_down[0], g_final[None, :])
    return out.reshape(batch, seq, d)
```

```python
import functools
import math

import jax
import jax.numpy as jnp
from jax import lax
from jax.experimental import pallas as pl
from jax.experimental.pallas import tpu as pltpu

F32 = jnp.float32
BF16 = jnp.bfloat16

D_MODEL = 1024
HEAD_DIM = 64
HALF = HEAD_DIM // 2
ROPE_THETA = 10000.0
RMS_EPS = 1e-6
LOG2E = math.log2(math.e)
LN2 = math.log(2.0)
Q_SCALE = LOG2E * HEAD_DIM ** -0.5
BLOCK = 128
DIL_GROUPS = ((128, 1), (512, 4), (2048, 16))
N_DIL = len(DIL_GROUPS)
A_GROUP_WIDTH = 512
A_QKV_WIDTH = 3 * N_DIL * A_GROUP_WIDTH
B_Q_HEADS = 16
B_KV_HEADS = 2
B_Q_WIDTH = B_Q_HEADS * HEAD_DIM
B_WINDOW = 128
GATE_WIDTH = 2 * D_MODEL
IN_WIDTH = A_QKV_WIDTH + B_Q_WIDTH + 2 * B_KV_HEADS * HEAD_DIM + GATE_WIDTH
N_EXPERT_GROUPS = 4
EXPERTS_PER_GROUP = 8
N_EXPERTS = N_EXPERT_GROUPS * EXPERTS_PER_GROUP
D_EXPERT = D_MODEL // 4

CB = 256
PAIR = 128
N_IN_BLOCKS = IN_WIDTH // CB
A_BLOCKS = A_GROUP_WIDTH // CB
ZB_GATE = 0
ZB_QB = 8
ZB_KB = 12
ZB_VB = 13
ZB_A0 = 14
N_TOK_BLOCKS = ZB_A0 + 3 * A_BLOCKS
OUT_PERM_BLOCK = 256
LSE_PERM_BLOCK = 128
ROUTER_LANES = 128
EXPERT_LANE0 = N_EXPERT_GROUPS
MOE_TILE = 512
CHUNK = 16
EXP_TILE = 512
LOCAL_ROWS = -(-(2 * MOE_TILE + N_EXPERTS * (CHUNK - 1)) // CB) * CB
XS_WIDTH = D_MODEL + ROUTER_LANES

VMEM_LIMIT = 56 * 1024 * 1024


def _in_proj_plan():
    plan = []
    for c in range(N_IN_BLOCKS):
        col = c * CB
        if col < A_QKV_WIDTH:
            part, rem = divmod(col, N_DIL * A_GROUP_WIDTH)
            group, blk = divmod(rem // CB, A_BLOCKS)
            kind = ("q", "k", "v")[part]
            dil = DIL_GROUPS[group][1]
            if dil == 1:
                plan.append((0, ZB_A0 + part * A_BLOCKS + blk, kind, 1))
            else:
                plan.append((group, part * A_BLOCKS + blk, kind, dil))
        elif col < A_QKV_WIDTH + B_Q_WIDTH:
            plan.append((0, ZB_QB + (col - A_QKV_WIDTH) // CB, "q", 1))
        elif col < A_QKV_WIDTH + B_Q_WIDTH + CB:
            plan.append((0, -1, "kvb", 1))
        else:
            plan.append((0, ZB_GATE + (col - (A_QKV_WIDTH + B_Q_WIDTH + CB)) // CB, "v", 1))
    return tuple(plan)


def _rms(x, g):
    return x * lax.rsqrt(jnp.mean(x * x, axis=-1, keepdims=True) + RMS_EPS) * g


def _split3(w):
    hi = w.astype(BF16).astype(F32)
    mid = (w - hi).astype(BF16).astype(F32)
    lo = (w - hi - mid).astype(BF16).astype(F32)
    return hi, mid, lo


def _rope(acc, cos, sin_signed, first_half):
    partner = jnp.where(first_half, pltpu.roll(acc, CB - HALF, 1), pltpu.roll(acc, HALF, 1))
    return acc * cos + partner * sin_signed


def _in_perm_block(dil):
    return max(BLOCK, CHUNK * dil)


def _in_proj_kernel(x_ref, g_ref, w_ref, b_ref, cos_ref, sin_ref, perm1_ref, perm2_ref,
                    zt_ref, zd1_ref, zd2_ref, *, plan, tm):
    out_refs = (zt_ref, zd1_ref, zd2_ref)
    perm_refs = (None, perm1_ref, perm2_ref)
    lane = lax.broadcasted_iota(jnp.int32, (tm, CB), 1)
    first_half = (lane % HEAD_DIM) < HALF

    h = _rms(x_ref[...], g_ref[...]).astype(BF16)
    h_by_dil, tables = {}, {}
    for slot, (_, dil) in enumerate(DIL_GROUPS):
        tables[dil] = slot
        if dil == 1:
            h_by_dil[dil] = h
            continue
        blk = _in_perm_block(dil)
        n = blk // dil
        moved = [jnp.dot(perm_refs[slot][...], h[tb * blk:(tb + 1) * blk], preferred_element_type=F32
                         ).astype(BF16) for tb in range(tm // blk)]
        h_by_dil[dil] = jnp.concatenate(
            [part[r * n:(r + 1) * n] for r in range(dil) for part in moved], axis=0)

    for c, (arr, dst, kind, dil) in enumerate(plan):
        cols = slice(c * CB, (c + 1) * CB)
        acc = jnp.dot(h_by_dil[dil], w_ref[:, cols], preferred_element_type=F32) + b_ref[:, cols]
        if kind in ("q", "k", "kvb"):
            slot = tables[dil]
            rot = _rope(acc, cos_ref[slot], sin_ref[slot], first_half)
        if kind == "q":
            val = (rot * Q_SCALE).astype(BF16)
        elif kind == "k":
            val = rot.astype(BF16)
        elif kind == "v":
            val = acc.astype(BF16)
        else:
            r64 = pltpu.roll(rot, HEAD_DIM, 1)
            r128 = pltpu.roll(rot, 2 * HEAD_DIM, 1)
            kdup = jnp.where(lane < HEAD_DIM, rot, jnp.where(lane < 3 * HEAD_DIM, r64, r128))
            a128 = pltpu.roll(acc, 2 * HEAD_DIM, 1)
            a192 = pltpu.roll(acc, 3 * HEAD_DIM, 1)
            vdup = jnp.where(lane < HEAD_DIM, a128, jnp.where(lane < 3 * HEAD_DIM, a192, acc))
            zt_ref[ZB_KB] = kdup.astype(BF16)
            zt_ref[ZB_VB] = vdup.astype(BF16)
            continue
        if dil == 1:
            out_refs[arr][dst] = val
        else:
            n = tm // dil
            for r in range(dil):
                out_refs[arr][dst, :, r * CB:(r + 1) * CB] = val[r * n:(r + 1) * n, :]


def _dilation_perm(block, dil, dtype):
    j = jnp.arange(block)
    src = (j % (block // dil)) * dil + j // (block // dil)
    return (src[:, None] == jnp.arange(block)[None, :]).astype(dtype)


def _in_proj(x2, g_mix, w_in, b_in, cos_t, sin_t, *, seq, tm):
    n_tok = x2.shape[0]
    tiles_per_seq = seq // tm
    const = lambda i: (0, 0)
    table = pl.BlockSpec((N_DIL, tm, CB), lambda i: (0, i % tiles_per_seq, 0))
    d1, d2 = DIL_GROUPS[1][1], DIL_GROUPS[2][1]
    perms = [_dilation_perm(_in_perm_block(d), d, BF16) for d in (d1, d2)]
    perm_spec = lambda d: pl.BlockSpec((_in_perm_block(d),) * 2, const, pipeline_mode=pl.Buffered(1))
    return pl.pallas_call(
        functools.partial(_in_proj_kernel, plan=_in_proj_plan(), tm=tm),
        grid=(n_tok // tm,),
        in_specs=[
            pl.BlockSpec((tm, D_MODEL), lambda i: (i, 0)),
            pl.BlockSpec((1, D_MODEL), const),
            pl.BlockSpec((D_MODEL, IN_WIDTH), const, pipeline_mode=pl.Buffered(1)),
            pl.BlockSpec((1, IN_WIDTH), const),
            table, table,
            perm_spec(d1), perm_spec(d2),
        ],
        out_specs=[
            pl.BlockSpec((N_TOK_BLOCKS, tm, CB), lambda i: (0, i, 0)),
            pl.BlockSpec((3 * A_BLOCKS, tm // d1, d1 * CB), lambda i: (0, i, 0)),
            pl.BlockSpec((3 * A_BLOCKS, tm // d2, d2 * CB), lambda i: (0, i, 0)),
        ],
        out_shape=[
            jax.ShapeDtypeStruct((N_TOK_BLOCKS, n_tok, CB), BF16),
            jax.ShapeDtypeStruct((3 * A_BLOCKS, n_tok // d1, d1 * CB), BF16),
            jax.ShapeDtypeStruct((3 * A_BLOCKS, n_tok // d2, d2 * CB), BF16),
        ],
        compiler_params=pltpu.CompilerParams(
            dimension_semantics=("arbitrary",), vmem_limit_bytes=VMEM_LIMIT),
        name="in_proj",
    )(x2, g_mix, w_in, b_in, cos_t, sin_t, *perms)


def _attn_kernel(*refs, lq, max_dist, kv_shared, has_sink, want_lse, n_axes):
    refs = list(refs)
    sink_ref = refs.pop(0) if has_sink else None
    q_ref, k_ref, v_ref, kp_ref, vp_ref = refs[:5]
    o_ref = refs[5]
    lse_ref = refs[6] if want_lse else None
    vaug_ref = refs[-1]
    n_qblk = q_ref.shape[0]
    n_kv_pairs = vaug_ref.shape[0]
    rows_kv = lq + BLOCK

    row = lax.broadcasted_iota(jnp.int32, (BLOCK, 2 * BLOCK), 0)
    col = lax.broadcasted_iota(jnp.int32, (BLOCK, 2 * BLOCK), 1)
    dist = row - col + BLOCK
    valid = (dist >= 0) & (dist <= max_dist)
    neg_inf = jnp.float32(-jnp.inf)
    bias = jnp.where(valid, 0.0, neg_inf)
    bias_first = jnp.where(valid & (col >= BLOCK), 0.0, neg_inf)
    bias0 = jnp.where(pl.program_id(n_axes - 1) == 0, bias_first, bias)
    lane_lo = lax.broadcasted_iota(jnp.int32, (BLOCK, PAIR), 1) < HEAD_DIM

    def rd(ref, blk, rows, cols):
        return ref[rows, cols] if kv_shared else ref[blk, rows, cols]

    first_step = functools.reduce(lambda a, b: a & b, [pl.program_id(a) == 0 for a in range(n_axes)])

    @pl.when(first_step)
    def _():
        lane = lax.broadcasted_iota(jnp.int32, (rows_kv, PAIR), 1)
        for pair in range(n_kv_pairs):
            vaug_ref[pair, 0, :, PAIR:] = jnp.where(lane < HEAD_DIM, 1.0, 0.0).astype(BF16)
            vaug_ref[pair, 1, :, PAIR:] = jnp.where(lane < HEAD_DIM, 0.0, 1.0).astype(BF16)

    lane_kv = lax.broadcasted_iota(jnp.int32, (rows_kv, PAIR), 1) < HEAD_DIM
    for pair in range(n_kv_pairs):
        cols = slice((pair % 2) * PAIR, (pair % 2 + 1) * PAIR)
        everything = slice(None)
        v_all = jnp.concatenate([rd(vp_ref, pair // 2, everything, cols),
                                 rd(v_ref, pair // 2, everything, cols)], axis=0)
        zero = jnp.zeros_like(v_all)
        vaug_ref[pair, 0, :, :PAIR] = jnp.where(lane_kv, v_all, zero)
        vaug_ref[pair, 1, :, :PAIR] = jnp.where(lane_kv, zero, v_all)

    for ib in range(lq // BLOCK):
        rows = slice(ib * BLOCK, (ib + 1) * BLOCK)
        win = slice(ib * BLOCK, (ib + 2) * BLOCK)
        b_ib = bias0 if ib == 0 else bias
        for blk in range(n_qblk):
            for pp in range(2):
                qcols = slice(pp * PAIR, (pp + 1) * PAIR)
                pair = 0 if kv_shared else blk * 2 + pp
                kcols = slice(0, PAIR) if kv_shared else qcols
                if ib == 0:
                    k_win = jnp.concatenate([rd(kp_ref, blk, slice(None), kcols),
                                             rd(k_ref, blk, slice(0, BLOCK), kcols)], axis=0)
                else:
                    k_win = rd(k_ref, blk, slice((ib - 1) * BLOCK, (ib + 1) * BLOCK), kcols)
                q_pair = q_ref[blk, rows, qcols]
                ps, ms, sink_terms = [], [], []
                for hh in range(2):
                    q_h = jnp.where(lane_lo == (hh == 0), q_pair, jnp.zeros_like(q_pair))
                    s = lax.dot_general(q_h, k_win, (((1,), (1,)), ((), ())), preferred_element_type=F32)
                    s = s + b_ib
                    m = jnp.max(s, axis=-1, keepdims=True)
                    if has_sink:
                        head = (pl.program_id(1) * n_qblk + blk) * (CB // HEAD_DIM) + pp * 2 + hh
                        sink = sink_ref[head] * LOG2E
                        m = jnp.maximum(m, sink)
                        sink_terms.append(jnp.exp2(sink - m))
                    ps.append(jnp.exp2(s - m).astype(BF16))
                    ms.append(m)
                v_aug = jnp.concatenate([vaug_ref[pair, 0, win, :], vaug_ref[pair, 1, win, :]], axis=0)
                od = jnp.dot(jnp.concatenate(ps, axis=1), v_aug, preferred_element_type=F32)
                den = od[:, PAIR:]
                if has_sink:
                    den = den + jnp.where(lane_lo, sink_terms[0], sink_terms[1])
                o_ref[blk, rows, qcols] = (od[:, :PAIR] * (1.0 / den)).astype(BF16)
                if want_lse:
                    lse_ref[blk, rows, qcols] = (jnp.where(lane_lo, ms[0], ms[1]) + jnp.log2(den)) * LN2


def _dilated_attention(z, bases, group, *, batch, seq, lq):
    window, dil = DIL_GROUPS[group]
    sub_len = seq // dil
    lq = min(lq, sub_len)
    z4 = z.reshape(z.shape[0], batch, sub_len, dil * CB)
    qb, kb, vb = bases
    assert all(base % A_BLOCKS == 0 for base in bases)
    bpq = lq // BLOCK
    cur = lambda base: pl.BlockSpec(
        (A_BLOCKS, None, lq, CB), lambda b, r, i: (base // A_BLOCKS, b, i, r))
    prev = lambda base: pl.BlockSpec(
        (A_BLOCKS, None, BLOCK, CB), lambda b, r, i: (base // A_BLOCKS, b, jnp.maximum(i * bpq - 1, 0), r))
    out_spec = pl.BlockSpec((A_BLOCKS, None, lq, CB), lambda b, r, i: (0, b, i, r))
    o, lse = pl.pallas_call(
        functools.partial(_attn_kernel, lq=lq, max_dist=window // dil, kv_shared=False,
                          has_sink=False, want_lse=True, n_axes=3),
        grid=(batch, dil, sub_len // lq),
        in_specs=[cur(qb), cur(kb), cur(vb), prev(kb), prev(vb)],
        out_specs=[out_spec, out_spec],
        out_shape=[jax.ShapeDtypeStruct((A_BLOCKS, batch, sub_len, dil * CB), BF16),
                   jax.ShapeDtypeStruct((A_BLOCKS, batch, sub_len, dil * CB), F32)],
        scratch_shapes=[pltpu.VMEM((2 * A_BLOCKS, 2, lq + BLOCK, 2 * PAIR), BF16)],
        compiler_params=pltpu.CompilerParams(
            dimension_semantics=("arbitrary",) * 3, vmem_limit_bytes=VMEM_LIMIT),
        name=f"dilated_attn_g{group}",
    )(z4, z4, z4, z4, z4)
    rows = batch * sub_len
    return o.reshape(A_BLOCKS, rows, dil * CB), lse.reshape(A_BLOCKS, rows, dil * CB)


def _swa_attention(z_tok, sinks, *, batch, seq, lq):
    z4 = z_tok.reshape(N_TOK_BLOCKS, batch, seq, CB)
    bpq = lq // BLOCK
    n_q_blocks = B_Q_WIDTH // CB
    q_per_kv = n_q_blocks // B_KV_HEADS
    assert ZB_QB % q_per_kv == 0
    q_spec = pl.BlockSpec((q_per_kv, None, lq, CB), lambda b, kvh, i, s: (ZB_QB // q_per_kv + kvh, b, i, 0))
    cur = lambda base: pl.BlockSpec((None, None, lq, PAIR), lambda b, kvh, i, s: (base, b, i, kvh))
    prev = lambda base: pl.BlockSpec(
        (None, None, BLOCK, PAIR), lambda b, kvh, i, s: (base, b, jnp.maximum(i * bpq - 1, 0), kvh))
    o = pl.pallas_call(
        functools.partial(_attn_kernel, lq=lq, max_dist=B_WINDOW - 1, kv_shared=True,
                          has_sink=True, want_lse=False, n_axes=3),
        grid_spec=pltpu.PrefetchScalarGridSpec(
            num_scalar_prefetch=1,
            grid=(batch, B_KV_HEADS, seq // lq),
            in_specs=[q_spec, cur(ZB_KB), cur(ZB_VB), prev(ZB_KB), prev(ZB_VB)],
            out_specs=pl.BlockSpec((q_per_kv, None, lq, CB), lambda b, kvh, i, s: (kvh, b, i, 0)),
            scratch_shapes=[pltpu.VMEM((1, 2, lq + BLOCK, 2 * PAIR), BF16)],
        ),
        out_shape=jax.ShapeDtypeStruct((n_q_blocks, batch, seq, CB), BF16),
        compiler_params=pltpu.CompilerParams(
            dimension_semantics=("arbitrary",) * 3, vmem_limit_bytes=VMEM_LIMIT),
        name="swa_attn",
    )(sinks, z4, z4, z4, z4, z4)
    return o.reshape(n_q_blocks, batch * seq, CB)


def _post_attn_kernel(o0_ref, o1_ref, o2_ref, l0_ref, l1_ref, l2_ref, ob_ref, gate_ref, x_ref,
                      wa_ref, wb_ref, wo_ref, gf_ref, wr_ref, br_ref,
                      po_ref, pl_ref, x1_ref, h2_ref, route_ref, cnt_ref, *, tm):

    def to_token_order(ref, cb, slot, perm_ref, blk):
        dil = DIL_GROUPS[slot + 1][1]
        n = blk // dil
        parts = []
        for tb in range(tm // blk):
            stack = jnp.concatenate(
                [ref[cb, tb * n:(tb + 1) * n, r * CB:(r + 1) * CB] for r in range(dil)], axis=0)
            pieces = (stack,) if stack.dtype == BF16 else _split3(stack)
            moved = [jnp.dot(perm_ref[slot], p.astype(BF16), preferred_element_type=F32) for p in pieces]
            parts.append(functools.reduce(lambda a, b: a + b, moved))
        return jnp.concatenate(parts, axis=0)

    pa = None
    for cb in range(A_BLOCKS):
        l0 = l0_ref[cb]
        l1 = to_token_order(l1_ref, cb, 0, pl_ref, LSE_PERM_BLOCK)
        l2 = to_token_order(l2_ref, cb, 1, pl_ref, LSE_PERM_BLOCK)
        o1 = to_token_order(o1_ref, cb, 0, po_ref, OUT_PERM_BLOCK)
        o2 = to_token_order(o2_ref, cb, 1, po_ref, OUT_PERM_BLOCK)
        mx = jnp.maximum(jnp.maximum(l0, l1), l2)
        e0, e1, e2 = jnp.exp(l0 - mx), jnp.exp(l1 - mx), jnp.exp(l2 - mx)
        inv = 1.0 / (e0 + e1 + e2)
        ya = (e0 * inv) * o0_ref[cb].astype(F32) + (e1 * inv) * o1 + (e2 * inv) * o2
        part = jnp.dot(ya.astype(BF16), wa_ref[cb * CB:(cb + 1) * CB, :], preferred_element_type=F32)
        pa = part if pa is None else pa + part
    pb = None
    for cb in range(B_Q_WIDTH // CB):
        part = jnp.dot(ob_ref[cb], wb_ref[cb * CB:(cb + 1) * CB, :], preferred_element_type=F32)
        pb = part if pb is None else pb + part
    n_gate = D_MODEL // CB
    merged = []
    for cb in range(n_gate):
        cols = slice(cb * CB, (cb + 1) * CB)
        ga = jax.nn.sigmoid(gate_ref[cb].astype(F32))
        gb = jax.nn.sigmoid(gate_ref[n_gate + cb].astype(F32))
        merged.append((ga * pa[:, cols] + gb * pb[:, cols]).astype(BF16))
    merged = jnp.concatenate(merged, axis=1)
    x1 = x_ref[...] + jnp.dot(merged, wo_ref[...], preferred_element_type=F32)
    x1_ref[...] = x1
    h2 = _rms(x1, gf_ref[...])
    h2_ref[...] = h2.astype(BF16)

    h2_hi = h2.astype(BF16)
    h2_lo = (h2 - h2_hi.astype(F32)).astype(BF16)
    logits = (jnp.dot(h2_hi, wr_ref[0], preferred_element_type=F32)
              + (jnp.dot(h2_lo, wr_ref[0], preferred_element_type=F32)
                 + jnp.dot(h2_hi, wr_ref[1], preferred_element_type=F32))) + br_ref[...]
    lane = lax.broadcasted_iota(jnp.int32, logits.shape, 1)
    neg_inf = jnp.float32(-jnp.inf)
    lg = jnp.where(lane < N_EXPERT_GROUPS, logits, neg_inf)
    mg = jnp.max(lg, axis=-1, keepdims=True)
    gsel = jnp.min(jnp.where(lg == mg, lane, ROUTER_LANES), axis=-1, keepdims=True)
    pg_sel = 1.0 / jnp.sum(jnp.exp(lg - mg), axis=-1, keepdims=True)
    expert = lane - EXPERT_LANE0
    in_group = (expert >= 0) & (expert < N_EXPERTS) & ((expert // EXPERTS_PER_GROUP) == gsel)
    le = jnp.where(in_group, logits, neg_inf)
    me = jnp.max(le, axis=-1, keepdims=True)
    ex = jnp.exp(le - me)
    pe = ex / jnp.sum(ex, axis=-1, keepdims=True)
    p1 = jnp.max(pe, axis=-1, keepdims=True)
    i1 = jnp.min(jnp.where(in_group & (pe == p1), lane, ROUTER_LANES), axis=-1, keepdims=True)
    rest = in_group & (lane != i1)
    p2 = jnp.max(jnp.where(rest, pe, -1.0), axis=-1, keepdims=True)
    i2 = jnp.min(jnp.where(rest & (pe == p2), lane, ROUTER_LANES), axis=-1, keepdims=True)
    norm = pg_sel / (p1 + p2)
    route_ref[...] = jnp.where(
        lane == 0, (i1 - EXPERT_LANE0).astype(F32),
        jnp.where(lane == 1, (i2 - EXPERT_LANE0).astype(F32),
                  jnp.where(lane == 2, p1 * norm, jnp.where(lane == 3, p2 * norm, 0.0))))
    hits = jnp.where((lane == i1) | (lane == i2), 1.0, 0.0)
    cnt_ref[0] = jnp.broadcast_to(jnp.sum(hits, axis=0, keepdims=True), cnt_ref.shape[1:])


def _post_attn(oa, lse, ob, z_tok, x2, w_proj_a, w_proj_b, w_out, g_ffn, w_router, b_router, *, tm):
    n_tok = x2.shape[0]
    const2 = lambda i: (0, 0)
    blk = lambda nb: pl.BlockSpec((nb, tm, CB), lambda i: (0, i, 0))
    dil_blk = lambda dil: pl.BlockSpec((A_BLOCKS, tm // dil, dil * CB), lambda i: (0, i, 0))
    row = lambda width: pl.BlockSpec((tm, width), lambda i: (i, 0))
    resident = lambda shape: pl.BlockSpec(shape, const2, pipeline_mode=pl.Buffered(1))
    d1, d2 = DIL_GROUPS[1][1], DIL_GROUPS[2][1]
    perm_o = jnp.stack([_dilation_perm(OUT_PERM_BLOCK, d, BF16).T for d in (d1, d2)])
    perm_l = jnp.stack([_dilation_perm(LSE_PERM_BLOCK, d, BF16).T for d in (d1, d2)])
    w_router_hi = w_router.astype(BF16)
    w_router_lo = (w_router - w_router_hi.astype(F32)).astype(BF16)
    w_router = jnp.stack([w_router_hi, w_router_lo])
    return pl.pallas_call(
        functools.partial(_post_attn_kernel, tm=tm),
        grid=(n_tok // tm,),
        in_specs=[blk(A_BLOCKS), dil_blk(d1), dil_blk(d2), blk(A_BLOCKS), dil_blk(d1), dil_blk(d2),
                  blk(B_Q_WIDTH // CB), blk(GATE_WIDTH // CB), row(D_MODEL),
                  resident((A_GROUP_WIDTH, D_MODEL)), resident((B_Q_WIDTH, D_MODEL)),
                  resident((D_MODEL, D_MODEL)), resident((1, D_MODEL)),
                  pl.BlockSpec((2, D_MODEL, ROUTER_LANES), lambda i: (0, 0, 0), pipeline_mode=pl.Buffered(1)),
                  resident((1, ROUTER_LANES)),
                  pl.BlockSpec((2, OUT_PERM_BLOCK, OUT_PERM_BLOCK), lambda i: (0, 0, 0),
                               pipeline_mode=pl.Buffered(1)),
                  pl.BlockSpec((2, LSE_PERM_BLOCK, LSE_PERM_BLOCK), lambda i: (0, 0, 0),
                               pipeline_mode=pl.Buffered(1))],
        out_specs=[row(D_MODEL), row(D_MODEL), row(ROUTER_LANES),
                   pl.BlockSpec((1, 8, ROUTER_LANES), lambda i: (i, 0, 0))],
        out_shape=[jax.ShapeDtypeStruct((n_tok, D_MODEL), F32),
                   jax.ShapeDtypeStruct((n_tok, D_MODEL), BF16),
                   jax.ShapeDtypeStruct((n_tok, ROUTER_LANES), F32),
                   jax.ShapeDtypeStruct((n_tok // tm, 8, ROUTER_LANES), F32)],
        compiler_params=pltpu.CompilerParams(
            dimension_semantics=("arbitrary",), vmem_limit_bytes=VMEM_LIMIT),
        name="post_attn",
    )(oa[0], oa[1], oa[2], lse[0], lse[1], lse[2], ob, z_tok, x2,
      w_proj_a, w_proj_b, w_out, g_ffn, w_router, b_router, perm_o, perm_l)


def _local_slots(route, lbase_row, ltri):
    lane = lax.broadcasted_iota(jnp.int32, route.shape, 1)
    pick = lambda k: jnp.sum(jnp.where(lane == k, route, 0.0), axis=-1, keepdims=True)
    lanef = lane.astype(F32)
    oh1, oh2 = lanef == pick(0), lanef == pick(1)
    oh = jnp.where(oh1 | oh2, 1.0, 0.0).astype(BF16)
    table = jnp.dot(ltri, oh, preferred_element_type=F32) + lbase_row
    ls1 = jnp.sum(jnp.where(oh1, table, 0.0), axis=-1, keepdims=True)
    ls2 = jnp.sum(jnp.where(oh2, table, 0.0), axis=-1, keepdims=True)
    return ls1, ls2, pick


def _chunk_loop(count, body):
    lax.fori_loop(0, count, lambda c, carry: (body(c), carry)[1], 0)


def _dispatch_kernel(nch_ref, lb_ref, dst_ref, tot_ref, tail_start_ref, tail_n_ref,
                     h2_ref, route_ref, lbase_ref, ltri_ref, xs_ref, buf_ref, zero_ref, sem_ref, zsem_ref,
                     *, tm):
    i = pl.program_id(0)
    last = pl.num_programs(0) - 1
    slot = i % 2

    def chunk(slot_, src_row, dst_row):
        return pltpu.make_async_copy(buf_ref.at[slot_, pl.ds(src_row, CHUNK), :],
                                     xs_ref.at[pl.ds(dst_row, CHUNK), :], sem_ref.at[slot_])

    def wait_tile(slot_, tile):
        _chunk_loop(tot_ref[tile], lambda c: chunk(slot_, 0, 0).wait())

    @pl.when(i == 0)
    def _():
        zero_ref[...] = jnp.zeros_like(zero_ref)

        def zero_chunk(row):
            return pltpu.make_async_copy(zero_ref, xs_ref.at[pl.ds(row, CHUNK), :], zsem_ref)

        def per_expert(e, total):
            start = tail_start_ref[e] * CHUNK
            _chunk_loop(tail_n_ref[e], lambda c: zero_chunk(pl.multiple_of(start + c * CHUNK, CHUNK)).start())
            return total + tail_n_ref[e]

        total = lax.fori_loop(0, N_EXPERTS, per_expert, 0)
        _chunk_loop(total, lambda c: zero_chunk(0).wait())

    @pl.when(i >= 2)
    def _():
        wait_tile(slot, i - 2)

    route = route_ref[...]
    ls1, ls2, pick = _local_slots(route, lbase_ref[0, 0:1, :], ltri_ref[...])
    lane = lax.broadcasted_iota(jnp.int32, route.shape, 1)
    ls_t = jnp.where(lane == 0, ls1, jnp.where(lane == 1, ls2, 0.0)).T.astype(jnp.int32)
    jrow = lax.broadcasted_iota(jnp.int32, (LOCAL_ROWS, tm), 0)
    sel = jnp.where(jrow == ls_t[0:1, :], 1.0, jnp.where(jrow == ls_t[1:2, :], 1.0, 0.0)).astype(BF16)
    pieces = _split3(pick(2)) + _split3(pick(3)) + (pick(0),)
    meta = jnp.zeros(route.shape, F32)
    for k, piece in enumerate(pieces):
        meta = jnp.where(lane == k, piece, meta)
    buf_ref[slot, :, :D_MODEL] = jnp.dot(sel, h2_ref[...], preferred_element_type=F32).astype(BF16)
    buf_ref[slot, :, D_MODEL:] = jnp.dot(sel, meta.astype(BF16), preferred_element_type=F32).astype(BF16)

    def per_expert(e, carry):
        idx = i * N_EXPERTS + e
        src0 = lb_ref[idx] * CHUNK
        dst0 = dst_ref[idx] * CHUNK
        _chunk_loop(nch_ref[idx], lambda c: chunk(slot, pl.multiple_of(src0 + c * CHUNK, CHUNK),
                                                  pl.multiple_of(dst0 + c * CHUNK, CHUNK)).start())
        return carry

    lax.fori_loop(0, N_EXPERTS, per_expert, 0)

    @pl.when(i == last)
    def _():
        @pl.when(i >= 1)
        def _():
            wait_tile(1 - slot, i - 1)
        wait_tile(slot, i)


def _expert_kernel(te_ref, nv_ref, xs_ref, wg_ref, wu_ref, wd_ref, ys_ref, wgb_ref, wub_ref, wdb_ref):
    g = pl.program_id(0)
    e = te_ref[g]

    @pl.when((g == 0) | (te_ref[jnp.maximum(g - 1, 0)] != e))
    def _():
        wgb_ref[...] = wg_ref[...].astype(BF16)
        wub_ref[...] = wu_ref[...].astype(BF16)
        wdb_ref[...] = wd_ref[...].astype(BF16)

    @pl.when(g < nv_ref[0])
    def _():
        x = xs_ref[:, :D_MODEL]
        meta = xs_ref[:, D_MODEL:].astype(F32)
        lane = lax.broadcasted_iota(jnp.int32, meta.shape, 1)
        pick = lambda k: jnp.sum(jnp.where(lane == k, meta, 0.0), axis=-1, keepdims=True)
        w1 = pick(0) + pick(1) + pick(2)
        w2 = pick(3) + pick(4) + pick(5)
        w = jnp.where(pick(6) == e.astype(F32), w1, w2)
        hg = jnp.dot(x, wgb_ref[...], preferred_element_type=F32)
        hu = jnp.dot(x, wub_ref[...], preferred_element_type=F32)
        a = (hg * jax.nn.sigmoid(hg)) * hu * w
        ys_ref[...] = jnp.dot(a.astype(BF16), wdb_ref[...], preferred_element_type=F32).astype(BF16)


def _combine_kernel(nch_ref, lb_ref, dst_ref, tot_ref,
                    x1_ref, route_ref, lbase_ref, ltri_ref, gfin_ref, ys_ref, out_ref, ybuf_ref, sem_ref,
                    *, tm):
    i = pl.program_id(0)
    n_tiles = pl.num_programs(0)
    slot = i % 2

    def chunk(slot_, src_row, dst_row):
        return pltpu.make_async_copy(ys_ref.at[pl.ds(src_row, CHUNK), :],
                                     ybuf_ref.at[slot_, pl.ds(dst_row, CHUNK), :], sem_ref.at[slot_])

    def fetch_tile(tile, slot_):
        def per_expert(e, carry):
            idx = tile * N_EXPERTS + e
            src0 = dst_ref[idx] * CHUNK
            dst0 = lb_ref[idx] * CHUNK
            _chunk_loop(nch_ref[idx], lambda c: chunk(slot_, pl.multiple_of(src0 + c * CHUNK, CHUNK),
                                                      pl.multiple_of(dst0 + c * CHUNK, CHUNK)).start())
            return carry
        lax.fori_loop(0, N_EXPERTS, per_expert, 0)

    @pl.when(i == 0)
    def _():
        ybuf_ref[...] = jnp.zeros_like(ybuf_ref)
        fetch_tile(0, 0)

    @pl.when(i + 1 < n_tiles)
    def _():
        fetch_tile(i + 1, 1 - slot)

    _chunk_loop(tot_ref[i], lambda c: chunk(slot, 0, 0).wait())

    ls1, ls2, _ = _local_slots(route_ref[...], lbase_ref[0, 0:1, :], ltri_ref[...])
    jlane = lax.broadcasted_iota(jnp.int32, (tm, LOCAL_ROWS), 1)
    sel = jnp.where(jlane == ls1.astype(jnp.int32), 1.0,
                    jnp.where(jlane == ls2.astype(jnp.int32), 1.0, 0.0)).astype(BF16)
    y = jnp.dot(sel, ybuf_ref[slot], preferred_element_type=F32)
    out_ref[...] = _rms(x1_ref[...] + y, gfin_ref[...])


def _routing_tables(cnt, n_exp_tiles):
    c16 = (cnt + CHUNK - 1) // CHUNK
    lbase = jnp.cumsum(c16, axis=1) - c16
    tile_off = jnp.cumsum(c16, axis=0) - c16
    tot = jnp.sum(c16, axis=0)
    per = EXP_TILE // CHUNK
    region_tiles = (tot + per - 1) // per
    region = region_tiles * per
    base = jnp.cumsum(region) - region
    dst = base[None, :] + tile_off
    tile_end = jnp.cumsum(region_tiles)
    n_valid = tile_end[-1]
    g = jnp.arange(n_exp_tiles, dtype=jnp.int32)
    tile_expert = jnp.sum(tile_end[None, :] <= jnp.minimum(g, n_valid - 1)[:, None], axis=1).astype(jnp.int32)
    i32 = lambda a: a.astype(jnp.int32).reshape(-1)
    return dict(nch=i32(c16), lb=i32(lbase), dst=i32(dst), tot=i32(jnp.sum(c16, axis=1)),
                tail_start=i32(base + tot), tail_n=i32(region - tot),
                tile_expert=tile_expert, n_valid=i32(n_valid),
                lbase_rows=(lbase * CHUNK).astype(F32))


def _moe(h2, route, cnt_tiles, x1, w_gate, w_up, w_down, g_final):
    n_tok = h2.shape[0]
    tm = MOE_TILE
    n_tiles = n_tok // tm
    assert cnt_tiles.shape[0] == n_tiles
    worst_rows = 2 * n_tok + n_tiles * N_EXPERTS * (CHUNK - 1) + N_EXPERTS * (EXP_TILE - CHUNK)
    n_exp_tiles = -(-worst_rows // EXP_TILE)
    n_slots = n_exp_tiles * EXP_TILE

    cnt = cnt_tiles[:, 0, EXPERT_LANE0:EXPERT_LANE0 + N_EXPERTS].astype(jnp.int32)
    tb = _routing_tables(cnt, n_exp_tiles)
    lbase_rows = jnp.zeros((n_tiles, 8, ROUTER_LANES), F32).at[:, :, :N_EXPERTS].set(
        tb["lbase_rows"][:, None, :])
    row_id = jnp.arange(tm)
    ltri = (row_id[:, None] > row_id[None, :]).astype(BF16)

    tile_row = lambda width: pl.BlockSpec((tm, width), lambda i, *_: (i, 0))
    lbase_spec = pl.BlockSpec((1, 8, ROUTER_LANES), lambda i, *_: (i, 0, 0))
    ltri_spec = pl.BlockSpec((tm, tm), lambda i, *_: (0, 0), pipeline_mode=pl.Buffered(1))
    hbm = pl.BlockSpec(memory_space=pl.ANY)

    xs = pl.pallas_call(
        functools.partial(_dispatch_kernel, tm=tm),
        grid_spec=pltpu.PrefetchScalarGridSpec(
            num_scalar_prefetch=6, grid=(n_tiles,),
            in_specs=[tile_row(D_MODEL), tile_row(ROUTER_LANES), lbase_spec, ltri_spec],
            out_specs=hbm,
            scratch_shapes=[pltpu.VMEM((2, LOCAL_ROWS, XS_WIDTH), BF16), pltpu.VMEM((CHUNK, XS_WIDTH), BF16),
                            pltpu.SemaphoreType.DMA((2,)), pltpu.SemaphoreType.DMA]),
        out_shape=jax.ShapeDtypeStruct((n_slots, XS_WIDTH), BF16),
        compiler_params=pltpu.CompilerParams(
            dimension_semantics=("arbitrary",), vmem_limit_bytes=VMEM_LIMIT),
        name="moe_dispatch",
    )(tb["nch"], tb["lb"], tb["dst"], tb["tot"], tb["tail_start"], tb["tail_n"],
      h2, route, lbase_rows, ltri)

    row_tile = lambda width: pl.BlockSpec(
        (EXP_TILE, width), lambda g, te, nv: (jnp.maximum(jnp.minimum(g, nv[0] - 1), 0), 0))
    ys = pl.pallas_call(
        _expert_kernel,
        grid_spec=pltpu.PrefetchScalarGridSpec(
            num_scalar_prefetch=2, grid=(n_exp_tiles,),
            in_specs=[row_tile(XS_WIDTH),
                      pl.BlockSpec((None, D_MODEL, D_EXPERT), lambda g, te, nv: (te[g], 0, 0)),
                      pl.BlockSpec((None, D_MODEL, D_EXPERT), lambda g, te, nv: (te[g], 0, 0)),
                      pl.BlockSpec((None, D_EXPERT, D_MODEL), lambda g, te, nv: (te[g], 0, 0))],
            out_specs=row_tile(D_MODEL),
            scratch_shapes=[pltpu.VMEM((D_MODEL, D_EXPERT), BF16), pltpu.VMEM((D_MODEL, D_EXPERT), BF16),
                            pltpu.VMEM((D_EXPERT, D_MODEL), BF16)]),
        out_shape=jax.ShapeDtypeStruct((n_slots, D_MODEL), BF16),
        compiler_params=pltpu.CompilerParams(
            dimension_semantics=("arbitrary",), vmem_limit_bytes=VMEM_LIMIT),
        name="moe_expert",
    )(tb["tile_expert"], tb["n_valid"], xs, w_gate, w_up, w_down)

    return pl.pallas_call(
        functools.partial(_combine_kernel, tm=tm),
        grid_spec=pltpu.PrefetchScalarGridSpec(
            num_scalar_prefetch=4, grid=(n_tiles,),
            in_specs=[tile_row(D_MODEL), tile_row(ROUTER_LANES), lbase_spec, ltri_spec,
                      pl.BlockSpec((1, D_MODEL), lambda i, *_: (0, 0)), hbm],
            out_specs=tile_row(D_MODEL),
            scratch_shapes=[pltpu.VMEM((2, LOCAL_ROWS, D_MODEL), BF16), pltpu.SemaphoreType.DMA((2,))]),
        out_shape=jax.ShapeDtypeStruct((n_tok, D_MODEL), F32),
        compiler_params=pltpu.CompilerParams(
            dimension_semantics=("arbitrary",), vmem_limit_bytes=VMEM_LIMIT),
        name="moe_combine",
    )(tb["nch"], tb["lb"], tb["dst"], tb["tot"], x1, route, lbase_rows, ltri, g_final, ys)


def _rope_tables(seq, tm):
    pos = jnp.arange(seq, dtype=F32)
    inv_freq = ROPE_THETA ** (-jnp.arange(0, HEAD_DIM, 2, dtype=F32) / HEAD_DIM)
    ang = pos[:, None] * inv_freq[None, :]
    cos, sin = jnp.cos(ang), jnp.sin(ang)
    reps = CB // HEAD_DIM
    cos_t = jnp.tile(jnp.concatenate([cos, cos], axis=-1), (1, reps))
    sin_t = jnp.tile(jnp.concatenate([-sin, sin], axis=-1), (1, reps))

    def reorder(t, dil):
        return t.reshape(seq // tm, tm // dil, dil, CB).transpose(0, 2, 1, 3).reshape(seq, CB)

    dils = [dil for _, dil in DIL_GROUPS]
    return (jnp.stack([reorder(cos_t, dil) for dil in dils]),
            jnp.stack([reorder(sin_t, dil) for dil in dils]))


def kernel(x, w_in, b_in, sinks, w_proj_a, w_proj_b, w_out, g_mix, g_ffn, w_router_group, b_router_group,
           w_router_expert, b_router_expert, w_exp_gate, w_exp_up, w_exp_down, g_final):
    batch, seq, d = x.shape
    assert d == D_MODEL and w_in.shape[0] == 1, "single-layer kernel"
    n_tok = batch * seq
    x2 = x.reshape(n_tok, d)
    tm_in = 512
    cos_t, sin_t = _rope_tables(seq, tm_in)

    z_tok, z_d1, z_d2 = _in_proj(x2, g_mix[0][None, :], w_in[0].astype(BF16), b_in[0][None, :],
                                 cos_t, sin_t, seq=seq, tm=tm_in)

    a_bases = (0, A_BLOCKS, 2 * A_BLOCKS)
    tok_bases = tuple(ZB_A0 + b for b in a_bases)
    oa, lse = [], []
    for group, (z, bases) in enumerate(((z_tok, tok_bases), (z_d1, a_bases), (z_d2, a_bases))):
        o_g, l_g = _dilated_attention(z, bases, group, batch=batch, seq=seq, lq=512)
        oa.append(o_g)
        lse.append(l_g)
    ob = _swa_attention(z_tok, sinks[0], batch=batch, seq=seq, lq=512)

    pad = ROUTER_LANES - N_EXPERT_GROUPS - N_EXPERTS
    w_router = jnp.concatenate(
        [w_router_group[0], w_router_expert[0], jnp.zeros((d, pad), F32)], axis=1)
    b_router = jnp.concatenate(
        [b_router_group[0], b_router_expert[0], jnp.zeros((pad,), F32)])[None, :]
    x1, h2, route, cnt_tiles = _post_attn(
        oa, lse, ob, z_tok, x2, w_proj_a[0].astype(BF16), w_proj_b[0].astype(BF16), w_out[0].astype(BF16),
        g_ffn[0][None, :], w_router, b_router, tm=MOE_TILE)

    out = _moe(h2, route, cnt_tiles, x1, w_exp_gate[0], w_exp_up[0], w_exp_down[0], g_final[None, :])
    return out.reshape(batch, seq, d)
```

```python
import functools
import math

import jax
import jax.numpy as jnp
from jax import lax
from jax.experimental import pallas as pl
from jax.experimental.pallas import tpu as pltpu

F32 = jnp.float32
BF16 = jnp.bfloat16

D_MODEL = 1024
HEAD_DIM = 64
HALF = HEAD_DIM // 2
ROPE_THETA = 10000.0
RMS_EPS = 1e-6
LOG2E = math.log2(math.e)
LN2 = math.log(2.0)
Q_SCALE = LOG2E * HEAD_DIM ** -0.5
BLOCK = 128
DIL_GROUPS = ((128, 1), (512, 4), (2048, 16))
N_DIL = len(DIL_GROUPS)
A_GROUP_WIDTH = 512
A_QKV_WIDTH = 3 * N_DIL * A_GROUP_WIDTH
B_Q_HEADS = 16
B_KV_HEADS = 2
B_Q_WIDTH = B_Q_HEADS * HEAD_DIM
B_WINDOW = 128
GATE_WIDTH = 2 * D_MODEL
IN_WIDTH = A_QKV_WIDTH + B_Q_WIDTH + 2 * B_KV_HEADS * HEAD_DIM + GATE_WIDTH
N_EXPERT_GROUPS = 4
EXPERTS_PER_GROUP = 8
N_EXPERTS = N_EXPERT_GROUPS * EXPERTS_PER_GROUP
D_EXPERT = D_MODEL // 4

CB = 256
PAIR = 128
N_IN_BLOCKS = IN_WIDTH // CB
A_BLOCKS = A_GROUP_WIDTH // CB
ZB_GATE = 0
ZB_QB = 8
ZB_KB = 12
ZB_VB = 13
ZB_A0 = 14
N_TOK_BLOCKS = ZB_A0 + 3 * A_BLOCKS
OUT_PERM_BLOCK = 256
LSE_PERM_BLOCK = 128
ROUTER_LANES = 128
EXPERT_LANE0 = N_EXPERT_GROUPS
MOE_TILE = 512
CHUNK = 16
EXP_TILE = 512
LOCAL_ROWS = -(-(2 * MOE_TILE + N_EXPERTS * (CHUNK - 1)) // CB) * CB
LOCAL_CHUNKS = LOCAL_ROWS // CHUNK
XS_WIDTH = D_MODEL + ROUTER_LANES
META_PIECES = 3
assert META_PIECES * N_EXPERTS <= ROUTER_LANES

VMEM_LIMIT = 56 * 1024 * 1024


def _in_proj_plan():
    plan = []
    for c in range(N_IN_BLOCKS):
        col = c * CB
        if col < A_QKV_WIDTH:
            part, rem = divmod(col, N_DIL * A_GROUP_WIDTH)
            group, blk = divmod(rem // CB, A_BLOCKS)
            kind = ("q", "k", "v")[part]
            dil = DIL_GROUPS[group][1]
            if dil == 1:
                plan.append((0, ZB_A0 + part * A_BLOCKS + blk, kind, 1))
            else:
                plan.append((group, part * A_BLOCKS + blk, kind, dil))
        elif col < A_QKV_WIDTH + B_Q_WIDTH:
            plan.append((0, ZB_QB + (col - A_QKV_WIDTH) // CB, "q", 1))
        elif col < A_QKV_WIDTH + B_Q_WIDTH + CB:
            plan.append((0, -1, "kvb", 1))
        else:
            plan.append((0, ZB_GATE + (col - (A_QKV_WIDTH + B_Q_WIDTH + CB)) // CB, "v", 1))
    return tuple(plan)


def _rms(x, g):
    return x * lax.rsqrt(jnp.mean(x * x, axis=-1, keepdims=True) + RMS_EPS) * g


def _split3(w):
    hi = w.astype(BF16).astype(F32)
    mid = (w - hi).astype(BF16).astype(F32)
    lo = (w - hi - mid).astype(BF16).astype(F32)
    return hi, mid, lo


def _rope(acc, cos, sin_signed, first_half):
    partner = jnp.where(first_half, pltpu.roll(acc, CB - HALF, 1), pltpu.roll(acc, HALF, 1))
    return acc * cos + partner * sin_signed


def _in_perm_block(dil):
    return max(BLOCK, CHUNK * dil)


def _in_proj_kernel(x_ref, g_ref, w_ref, b_ref, cos_ref, sin_ref, perm1_ref, perm2_ref,
                    zt_ref, zd1_ref, zd2_ref, *, plan, tm):
    out_refs = (zt_ref, zd1_ref, zd2_ref)
    perm_refs = (None, perm1_ref, perm2_ref)
    lane = lax.broadcasted_iota(jnp.int32, (tm, CB), 1)
    first_half = (lane % HEAD_DIM) < HALF

    h = _rms(x_ref[...], g_ref[...]).astype(BF16)
    h_by_dil, tables = {}, {}
    for slot, (_, dil) in enumerate(DIL_GROUPS):
        tables[dil] = slot
        if dil == 1:
            h_by_dil[dil] = h
            continue
        blk = _in_perm_block(dil)
        n = blk // dil
        moved = [jnp.dot(perm_refs[slot][...], h[tb * blk:(tb + 1) * blk], preferred_element_type=F32
                         ).astype(BF16) for tb in range(tm // blk)]
        h_by_dil[dil] = jnp.concatenate(
            [part[r * n:(r + 1) * n] for r in range(dil) for part in moved], axis=0)

    for c, (arr, dst, kind, dil) in enumerate(plan):
        cols = slice(c * CB, (c + 1) * CB)
        acc = jnp.dot(h_by_dil[dil], w_ref[:, cols], preferred_element_type=F32) + b_ref[:, cols]
        if kind in ("q", "k", "kvb"):
            slot = tables[dil]
            rot = _rope(acc, cos_ref[slot], sin_ref[slot], first_half)
        if kind == "q":
            val = (rot * Q_SCALE).astype(BF16)
        elif kind == "k":
            val = rot.astype(BF16)
        elif kind == "v":
            val = acc.astype(BF16)
        else:
            r64 = pltpu.roll(rot, HEAD_DIM, 1)
            r128 = pltpu.roll(rot, 2 * HEAD_DIM, 1)
            kdup = jnp.where(lane < HEAD_DIM, rot, jnp.where(lane < 3 * HEAD_DIM, r64, r128))
            a128 = pltpu.roll(acc, 2 * HEAD_DIM, 1)
            a192 = pltpu.roll(acc, 3 * HEAD_DIM, 1)
            vdup = jnp.where(lane < HEAD_DIM, a128, jnp.where(lane < 3 * HEAD_DIM, a192, acc))
            zt_ref[ZB_KB] = kdup.astype(BF16)
            zt_ref[ZB_VB] = vdup.astype(BF16)
            continue
        if dil == 1:
            out_refs[arr][dst] = val
        else:
            n = tm // dil
            for r in range(dil):
                out_refs[arr][dst, :, r * CB:(r + 1) * CB] = val[r * n:(r + 1) * n, :]


def _dilation_perm(block, dil, dtype):
    j = jnp.arange(block)
    src = (j % (block // dil)) * dil + j // (block // dil)
    return (src[:, None] == jnp.arange(block)[None, :]).astype(dtype)


def _in_proj(x2, g_mix, w_in, b_in, cos_t, sin_t, *, seq, tm):
    n_tok = x2.shape[0]
    tiles_per_seq = seq // tm
    const = lambda i: (0, 0)
    table = pl.BlockSpec((N_DIL, tm, CB), lambda i: (0, i % tiles_per_seq, 0))
    d1, d2 = DIL_GROUPS[1][1], DIL_GROUPS[2][1]
    perms = [_dilation_perm(_in_perm_block(d), d, BF16) for d in (d1, d2)]
    perm_spec = lambda d: pl.BlockSpec((_in_perm_block(d),) * 2, const, pipeline_mode=pl.Buffered(1))
    return pl.pallas_call(
        functools.partial(_in_proj_kernel, plan=_in_proj_plan(), tm=tm),
        grid=(n_tok // tm,),
        in_specs=[
            pl.BlockSpec((tm, D_MODEL), lambda i: (i, 0)),
            pl.BlockSpec((1, D_MODEL), const),
            pl.BlockSpec((D_MODEL, IN_WIDTH), const, pipeline_mode=pl.Buffered(1)),
            pl.BlockSpec((1, IN_WIDTH), const),
            table, table,
            perm_spec(d1), perm_spec(d2),
        ],
        out_specs=[
            pl.BlockSpec((N_TOK_BLOCKS, tm, CB), lambda i: (0, i, 0)),
            pl.BlockSpec((3 * A_BLOCKS, tm // d1, d1 * CB), lambda i: (0, i, 0)),
            pl.BlockSpec((3 * A_BLOCKS, tm // d2, d2 * CB), lambda i: (0, i, 0)),
        ],
        out_shape=[
            jax.ShapeDtypeStruct((N_TOK_BLOCKS, n_tok, CB), BF16),
            jax.ShapeDtypeStruct((3 * A_BLOCKS, n_tok // d1, d1 * CB), BF16),
            jax.ShapeDtypeStruct((3 * A_BLOCKS, n_tok // d2, d2 * CB), BF16),
        ],
        compiler_params=pltpu.CompilerParams(
            dimension_semantics=("arbitrary",), vmem_limit_bytes=VMEM_LIMIT),
        name="in_proj",
    )(x2, g_mix, w_in, b_in, cos_t, sin_t, *perms)


def _attn_kernel(*refs, lq, max_dist, kv_shared, has_sink, want_lse, n_axes):
    refs = list(refs)
    sink_ref = refs.pop(0) if has_sink else None
    q_ref, k_ref, v_ref, kp_ref, vp_ref = refs[:5]
    o_ref = refs[5]
    lse_ref = refs[6] if want_lse else None
    vaug_ref = refs[-1]
    n_qblk = q_ref.shape[0]
    n_kv_pairs = vaug_ref.shape[0]
    rows_kv = lq + BLOCK

    row = lax.broadcasted_iota(jnp.int32, (BLOCK, 2 * BLOCK), 0)
    col = lax.broadcasted_iota(jnp.int32, (BLOCK, 2 * BLOCK), 1)
    dist = row - col + BLOCK
    valid = (dist >= 0) & (dist <= max_dist)
    neg_inf = jnp.float32(-jnp.inf)
    bias = jnp.where(valid, 0.0, neg_inf)
    bias_first = jnp.where(valid & (col >= BLOCK), 0.0, neg_inf)
    bias0 = jnp.where(pl.program_id(n_axes - 1) == 0, bias_first, bias)
    lane_lo = lax.broadcasted_iota(jnp.int32, (BLOCK, PAIR), 1) < HEAD_DIM

    def rd(ref, blk, rows, cols):
        return ref[rows, cols] if kv_shared else ref[blk, rows, cols]

    first_step = functools.reduce(lambda a, b: a & b, [pl.program_id(a) == 0 for a in range(n_axes)])

    @pl.when(first_step)
    def _():
        lane = lax.broadcasted_iota(jnp.int32, (rows_kv, PAIR), 1)
        for pair in range(n_kv_pairs):
            vaug_ref[pair, 0, :, PAIR:] = jnp.where(lane < HEAD_DIM, 1.0, 0.0).astype(BF16)
            vaug_ref[pair, 1, :, PAIR:] = jnp.where(lane < HEAD_DIM, 0.0, 1.0).astype(BF16)

    lane_kv = lax.broadcasted_iota(jnp.int32, (rows_kv, PAIR), 1) < HEAD_DIM
    for pair in range(n_kv_pairs):
        cols = slice((pair % 2) * PAIR, (pair % 2 + 1) * PAIR)
        everything = slice(None)
        v_all = jnp.concatenate([rd(vp_ref, pair // 2, everything, cols),
                                 rd(v_ref, pair // 2, everything, cols)], axis=0)
        zero = jnp.zeros_like(v_all)
        vaug_ref[pair, 0, :, :PAIR] = jnp.where(lane_kv, v_all, zero)
        vaug_ref[pair, 1, :, :PAIR] = jnp.where(lane_kv, zero, v_all)

    for ib in range(lq // BLOCK):
        rows = slice(ib * BLOCK, (ib + 1) * BLOCK)
        win = slice(ib * BLOCK, (ib + 2) * BLOCK)
        b_ib = bias0 if ib == 0 else bias
        for blk in range(n_qblk):
            for pp in range(2):
                qcols = slice(pp * PAIR, (pp + 1) * PAIR)
                pair = 0 if kv_shared else blk * 2 + pp
                kcols = slice(0, PAIR) if kv_shared else qcols
                if ib == 0:
                    k_win = jnp.concatenate([rd(kp_ref, blk, slice(None), kcols),
                                             rd(k_ref, blk, slice(0, BLOCK), kcols)], axis=0)
                else:
                    k_win = rd(k_ref, blk, slice((ib - 1) * BLOCK, (ib + 1) * BLOCK), kcols)
                q_pair = q_ref[blk, rows, qcols]
                ps, ms, sink_terms = [], [], []
                for hh in range(2):
                    q_h = jnp.where(lane_lo == (hh == 0), q_pair, jnp.zeros_like(q_pair))
                    s = lax.dot_general(q_h, k_win, (((1,), (1,)), ((), ())), preferred_element_type=F32)
                    s = s + b_ib
                    m = jnp.max(s, axis=-1, keepdims=True)
                    if has_sink:
                        head = (pl.program_id(1) * n_qblk + blk) * (CB // HEAD_DIM) + pp * 2 + hh
                        sink = sink_ref[head] * LOG2E
                        m = jnp.maximum(m, sink)
                        sink_terms.append(jnp.exp2(sink - m))
                    ps.append(jnp.exp2(s - m).astype(BF16))
                    ms.append(m)
                v_aug = jnp.concatenate([vaug_ref[pair, 0, win, :], vaug_ref[pair, 1, win, :]], axis=0)
                od = jnp.dot(jnp.concatenate(ps, axis=1), v_aug, preferred_element_type=F32)
                den = od[:, PAIR:]
                if has_sink:
                    den = den + jnp.where(lane_lo, sink_terms[0], sink_terms[1])
                o_ref[blk, rows, qcols] = (od[:, :PAIR] * (1.0 / den)).astype(BF16)
                if want_lse:
                    lse_ref[blk, rows, qcols] = (jnp.where(lane_lo, ms[0], ms[1]) + jnp.log2(den)) * LN2


def _dilated_attention(z, bases, group, *, batch, seq, lq):
    window, dil = DIL_GROUPS[group]
    sub_len = seq // dil
    lq = min(lq, sub_len)
    z4 = z.reshape(z.shape[0], batch, sub_len, dil * CB)
    qb, kb, vb = bases
    assert all(base % A_BLOCKS == 0 for base in bases)
    bpq = lq // BLOCK
    cur = lambda base: pl.BlockSpec(
        (A_BLOCKS, None, lq, CB), lambda b, r, i: (base // A_BLOCKS, b, i, r))
    prev = lambda base: pl.BlockSpec(
        (A_BLOCKS, None, BLOCK, CB), lambda b, r, i: (base // A_BLOCKS, b, jnp.maximum(i * bpq - 1, 0), r))
    out_spec = pl.BlockSpec((A_BLOCKS, None, lq, CB), lambda b, r, i: (0, b, i, r))
    o, lse = pl.pallas_call(
        functools.partial(_attn_kernel, lq=lq, max_dist=window // dil, kv_shared=False,
                          has_sink=False, want_lse=True, n_axes=3),
        grid=(batch, dil, sub_len // lq),
        in_specs=[cur(qb), cur(kb), cur(vb), prev(kb), prev(vb)],
        out_specs=[out_spec, out_spec],
        out_shape=[jax.ShapeDtypeStruct((A_BLOCKS, batch, sub_len, dil * CB), BF16),
                   jax.ShapeDtypeStruct((A_BLOCKS, batch, sub_len, dil * CB), F32)],
        scratch_shapes=[pltpu.VMEM((2 * A_BLOCKS, 2, lq + BLOCK, 2 * PAIR), BF16)],
        compiler_params=pltpu.CompilerParams(
            dimension_semantics=("arbitrary",) * 3, vmem_limit_bytes=VMEM_LIMIT),
        name=f"dilated_attn_g{group}",
    )(z4, z4, z4, z4, z4)
    rows = batch * sub_len
    return o.reshape(A_BLOCKS, rows, dil * CB), lse.reshape(A_BLOCKS, rows, dil * CB)


def _swa_attention(z_tok, sinks, *, batch, seq, lq):
    z4 = z_tok.reshape(N_TOK_BLOCKS, batch, seq, CB)
    bpq = lq // BLOCK
    n_q_blocks = B_Q_WIDTH // CB
    q_per_kv = n_q_blocks // B_KV_HEADS
    assert ZB_QB % q_per_kv == 0
    q_spec = pl.BlockSpec((q_per_kv, None, lq, CB), lambda b, kvh, i, s: (ZB_QB // q_per_kv + kvh, b, i, 0))
    cur = lambda base: pl.BlockSpec((None, None, lq, PAIR), lambda b, kvh, i, s: (base, b, i, kvh))
    prev = lambda base: pl.BlockSpec(
        (None, None, BLOCK, PAIR), lambda b, kvh, i, s: (base, b, jnp.maximum(i * bpq - 1, 0), kvh))
    o = pl.pallas_call(
        functools.partial(_attn_kernel, lq=lq, max_dist=B_WINDOW - 1, kv_shared=True,
                          has_sink=True, want_lse=False, n_axes=3),
        grid_spec=pltpu.PrefetchScalarGridSpec(
            num_scalar_prefetch=1,
            grid=(batch, B_KV_HEADS, seq // lq),
            in_specs=[q_spec, cur(ZB_KB), cur(ZB_VB), prev(ZB_KB), prev(ZB_VB)],
            out_specs=pl.BlockSpec((q_per_kv, None, lq, CB), lambda b, kvh, i, s: (kvh, b, i, 0)),
            scratch_shapes=[pltpu.VMEM((1, 2, lq + BLOCK, 2 * PAIR), BF16)],
        ),
        out_shape=jax.ShapeDtypeStruct((n_q_blocks, batch, seq, CB), BF16),
        compiler_params=pltpu.CompilerParams(
            dimension_semantics=("arbitrary",) * 3, vmem_limit_bytes=VMEM_LIMIT),
        name="swa_attn",
    )(sinks, z4, z4, z4, z4, z4)
    return o.reshape(n_q_blocks, batch * seq, CB)


def _post_attn_kernel(o0_ref, o1_ref, o2_ref, l0_ref, l1_ref, l2_ref, ob_ref, gate_ref, x_ref,
                      wa_ref, wb_ref, wo_ref, gf_ref, wr_ref, br_ref,
                      po_ref, pl_ref, x1_ref, h2_ref, route_ref, cnt_ref, *, tm):

    def to_token_order(ref, cb, slot, perm_ref, blk):
        dil = DIL_GROUPS[slot + 1][1]
        n = blk // dil
        parts = []
        for tb in range(tm // blk):
            stack = jnp.concatenate(
                [ref[cb, tb * n:(tb + 1) * n, r * CB:(r + 1) * CB] for r in range(dil)], axis=0)
            pieces = (stack,) if stack.dtype == BF16 else _split3(stack)
            moved = [jnp.dot(perm_ref[slot], p.astype(BF16), preferred_element_type=F32) for p in pieces]
            parts.append(functools.reduce(lambda a, b: a + b, moved))
        return jnp.concatenate(parts, axis=0)

    pa = None
    for cb in range(A_BLOCKS):
        l0 = l0_ref[cb]
        l1 = to_token_order(l1_ref, cb, 0, pl_ref, LSE_PERM_BLOCK)
        l2 = to_token_order(l2_ref, cb, 1, pl_ref, LSE_PERM_BLOCK)
        o1 = to_token_order(o1_ref, cb, 0, po_ref, OUT_PERM_BLOCK)
        o2 = to_token_order(o2_ref, cb, 1, po_ref, OUT_PERM_BLOCK)
        mx = jnp.maximum(jnp.maximum(l0, l1), l2)
        e0, e1, e2 = jnp.exp(l0 - mx), jnp.exp(l1 - mx), jnp.exp(l2 - mx)
        inv = 1.0 / (e0 + e1 + e2)
        ya = (e0 * inv) * o0_ref[cb].astype(F32) + (e1 * inv) * o1 + (e2 * inv) * o2
        part = jnp.dot(ya.astype(BF16), wa_ref[cb * CB:(cb + 1) * CB, :], preferred_element_type=F32)
        pa = part if pa is None else pa + part
    pb = None
    for cb in range(B_Q_WIDTH // CB):
        part = jnp.dot(ob_ref[cb], wb_ref[cb * CB:(cb + 1) * CB, :], preferred_element_type=F32)
        pb = part if pb is None else pb + part
    n_gate = D_MODEL // CB
    merged = []
    for cb in range(n_gate):
        cols = slice(cb * CB, (cb + 1) * CB)
        ga = jax.nn.sigmoid(gate_ref[cb].astype(F32))
        gb = jax.nn.sigmoid(gate_ref[n_gate + cb].astype(F32))
        merged.append((ga * pa[:, cols] + gb * pb[:, cols]).astype(BF16))
    merged = jnp.concatenate(merged, axis=1)
    x1 = x_ref[...] + jnp.dot(merged, wo_ref[...], preferred_element_type=F32)
    x1_ref[...] = x1
    h2 = _rms(x1, gf_ref[...])
    h2_ref[...] = h2.astype(BF16)

    h2_hi = h2.astype(BF16)
    h2_lo = (h2 - h2_hi.astype(F32)).astype(BF16)
    wr = wr_ref[...]
    wr_hi = wr.astype(BF16)
    wr_lo = (wr - wr_hi.astype(F32)).astype(BF16)
    logits = (jnp.dot(h2_hi, wr_hi, preferred_element_type=F32)
              + (jnp.dot(h2_lo, wr_hi, preferred_element_type=F32)
                 + jnp.dot(h2_hi, wr_lo, preferred_element_type=F32))) + br_ref[...]
    lane = lax.broadcasted_iota(jnp.int32, logits.shape, 1)
    neg_inf = jnp.float32(-jnp.inf)
    lg = jnp.where(lane < N_EXPERT_GROUPS, logits, neg_inf)
    mg = jnp.max(lg, axis=-1, keepdims=True)
    gsel = jnp.min(jnp.where(lg == mg, lane, ROUTER_LANES), axis=-1, keepdims=True)
    pg_sel = 1.0 / jnp.sum(jnp.exp(lg - mg), axis=-1, keepdims=True)
    expert = lane - EXPERT_LANE0
    in_group = (expert >= 0) & (expert < N_EXPERTS) & ((expert // EXPERTS_PER_GROUP) == gsel)
    le = jnp.where(in_group, logits, neg_inf)
    me = jnp.max(le, axis=-1, keepdims=True)
    ex = jnp.exp(le - me)
    pe = ex / jnp.sum(ex, axis=-1, keepdims=True)
    p1 = jnp.max(pe, axis=-1, keepdims=True)
    i1 = jnp.min(jnp.where(in_group & (pe == p1), lane, ROUTER_LANES), axis=-1, keepdims=True)
    rest = in_group & (lane != i1)
    p2 = jnp.max(jnp.where(rest, pe, -1.0), axis=-1, keepdims=True)
    i2 = jnp.min(jnp.where(rest & (pe == p2), lane, ROUTER_LANES), axis=-1, keepdims=True)
    norm = pg_sel / (p1 + p2)
    route_ref[...] = jnp.where(
        lane == 0, (i1 - EXPERT_LANE0).astype(F32),
        jnp.where(lane == 1, (i2 - EXPERT_LANE0).astype(F32),
                  jnp.where(lane == 2, p1 * norm, jnp.where(lane == 3, p2 * norm, 0.0))))
    hits = jnp.where((lane == i1) | (lane == i2), 1.0, 0.0)
    cnt_ref[0] = jnp.broadcast_to(jnp.sum(hits, axis=0, keepdims=True), cnt_ref.shape[1:])


def _post_attn(oa, lse, ob, z_tok, x2, w_proj_a, w_proj_b, w_out, g_ffn, w_router, b_router, *, tm):
    n_tok = x2.shape[0]
    const2 = lambda i: (0, 0)
    blk = lambda nb: pl.BlockSpec((nb, tm, CB), lambda i: (0, i, 0))
    dil_blk = lambda dil: pl.BlockSpec((A_BLOCKS, tm // dil, dil * CB), lambda i: (0, i, 0))
    row = lambda width: pl.BlockSpec((tm, width), lambda i: (i, 0))
    resident = lambda shape: pl.BlockSpec(shape, const2, pipeline_mode=pl.Buffered(1))
    d1, d2 = DIL_GROUPS[1][1], DIL_GROUPS[2][1]
    perm_o = jnp.stack([_dilation_perm(OUT_PERM_BLOCK, d, BF16).T for d in (d1, d2)])
    perm_l = jnp.stack([_dilation_perm(LSE_PERM_BLOCK, d, BF16).T for d in (d1, d2)])
    return pl.pallas_call(
        functools.partial(_post_attn_kernel, tm=tm),
        grid=(n_tok // tm,),
        in_specs=[blk(A_BLOCKS), dil_blk(d1), dil_blk(d2), blk(A_BLOCKS), dil_blk(d1), dil_blk(d2),
                  blk(B_Q_WIDTH // CB), blk(GATE_WIDTH // CB), row(D_MODEL),
                  resident((A_GROUP_WIDTH, D_MODEL)), resident((B_Q_WIDTH, D_MODEL)),
                  resident((D_MODEL, D_MODEL)), resident((1, D_MODEL)),
                  resident((D_MODEL, ROUTER_LANES)), resident((1, ROUTER_LANES)),
                  pl.BlockSpec((2, OUT_PERM_BLOCK, OUT_PERM_BLOCK), lambda i: (0, 0, 0),
                               pipeline_mode=pl.Buffered(1)),
                  pl.BlockSpec((2, LSE_PERM_BLOCK, LSE_PERM_BLOCK), lambda i: (0, 0, 0),
                               pipeline_mode=pl.Buffered(1))],
        out_specs=[row(D_MODEL), row(D_MODEL), row(ROUTER_LANES),
                   pl.BlockSpec((1, 8, ROUTER_LANES), lambda i: (i, 0, 0))],
        out_shape=[jax.ShapeDtypeStruct((n_tok, D_MODEL), F32),
                   jax.ShapeDtypeStruct((n_tok, D_MODEL), BF16),
                   jax.ShapeDtypeStruct((n_tok, ROUTER_LANES), F32),
                   jax.ShapeDtypeStruct((n_tok // tm, 8, ROUTER_LANES), F32)],
        compiler_params=pltpu.CompilerParams(
            dimension_semantics=("arbitrary",), vmem_limit_bytes=VMEM_LIMIT),
        name="post_attn",
    )(oa[0], oa[1], oa[2], lse[0], lse[1], lse[2], ob, z_tok, x2,
      w_proj_a, w_proj_b, w_out, g_ffn, w_router, b_router, perm_o, perm_l)


def _local_slots(route, lbase_row, ltri):
    lane = lax.broadcasted_iota(jnp.int32, route.shape, 1)
    pick = lambda k: jnp.sum(jnp.where(lane == k, route, 0.0), axis=-1, keepdims=True)
    lanef = lane.astype(F32)
    oh1, oh2 = lanef == pick(0), lanef == pick(1)
    oh = jnp.where(oh1 | oh2, 1.0, 0.0).astype(BF16)
    table = jnp.dot(ltri, oh, preferred_element_type=F32) + lbase_row
    ls1 = jnp.sum(jnp.where(oh1, table, 0.0), axis=-1, keepdims=True)
    ls2 = jnp.sum(jnp.where(oh2, table, 0.0), axis=-1, keepdims=True)
    return ls1, ls2, pick


def _chunk_loop(count, body):
    lax.fori_loop(0, count, lambda c, carry: (body(c), carry)[1], 0)


def _selection_matrix(slots_a, slots_b, axis, shape):
    span = CB
    local = lax.broadcasted_iota(jnp.int32, tuple(span if d == axis else s for d, s in enumerate(shape)),
                                 axis).astype(F32).astype(BF16)
    one, zero = jnp.ones_like(local), jnp.zeros_like(local)
    blocks = []
    for a in range(shape[axis] // span):
        rel_a = (slots_a - float(a * span)).astype(BF16)
        rel_b = (slots_b - float(a * span)).astype(BF16)
        blocks.append(jnp.where(local == rel_a, one, jnp.where(local == rel_b, one, zero)))
    return jnp.concatenate(blocks, axis=axis)


def _dispatch_kernel(cdst_ref, tot_ref, tail_start_ref, tail_n_ref,
                     h2_ref, route_ref, lbase_ref, ltri_ref, xs_ref, buf_ref, zero_ref, sem_ref, zsem_ref,
                     *, tm):
    i = pl.program_id(0)
    last = pl.num_programs(0) - 1
    slot = i % 2

    def chunk(slot_, src_row, dst_row):
        return pltpu.make_async_copy(buf_ref.at[slot_, pl.ds(src_row, CHUNK), :],
                                     xs_ref.at[pl.ds(dst_row, CHUNK), :], sem_ref.at[slot_])

    def wait_tile(slot_, tile):
        _chunk_loop(tot_ref[tile], lambda c: chunk(slot_, 0, 0).wait())

    @pl.when(i == 0)
    def _():
        zero_ref[...] = jnp.zeros_like(zero_ref)

        def zero_chunk(row):
            return pltpu.make_async_copy(zero_ref, xs_ref.at[pl.ds(row, CHUNK), :], zsem_ref)

        def per_expert(e, total):
            start = tail_start_ref[e] * CHUNK
            _chunk_loop(tail_n_ref[e], lambda c: zero_chunk(pl.multiple_of(start + c * CHUNK, CHUNK)).start())
            return total + tail_n_ref[e]

        total = lax.fori_loop(0, N_EXPERTS, per_expert, 0)
        _chunk_loop(total, lambda c: zero_chunk(0).wait())

    @pl.when(i >= 2)
    def _():
        wait_tile(slot, i - 2)

    route = route_ref[...]
    ls1, ls2, pick = _local_slots(route, lbase_ref[0, 0:1, :], ltri_ref[...])
    lane = lax.broadcasted_iota(jnp.int32, route.shape, 1)
    ls_t = jnp.where(lane == 0, ls1, jnp.where(lane == 1, ls2, 0.0)).T
    sel = _selection_matrix(ls_t[0:1, :], ls_t[1:2, :], 0, (LOCAL_ROWS, tm))
    lane_expert = (lane // META_PIECES).astype(F32)
    lane_piece = lane % META_PIECES
    by_piece = lambda pieces: jnp.where(lane_piece == 0, pieces[0],
                                        jnp.where(lane_piece == 1, pieces[1], pieces[2]))
    meta = jnp.where(lane_expert == pick(0), by_piece(_split3(pick(2))),
                     jnp.where(lane_expert == pick(1), by_piece(_split3(pick(3))), 0.0))
    buf_ref[slot, :, :D_MODEL] = jnp.dot(sel, h2_ref[...], preferred_element_type=F32).astype(BF16)
    buf_ref[slot, :, D_MODEL:] = jnp.dot(sel, meta.astype(BF16), preferred_element_type=F32).astype(BF16)

    _chunk_loop(tot_ref[i], lambda c: chunk(
        slot, pl.multiple_of(c * CHUNK, CHUNK),
        pl.multiple_of(cdst_ref[i * LOCAL_CHUNKS + c] * CHUNK, CHUNK)).start())

    @pl.when(i == last)
    def _():
        @pl.when(i >= 1)
        def _():
            wait_tile(1 - slot, i - 1)
        wait_tile(slot, i)


def _expert_kernel(te_ref, nv_ref, xs_ref, wg_ref, wu_ref, wd_ref, ys_ref, wgb_ref, wub_ref, wdb_ref):
    g = pl.program_id(0)
    e = te_ref[g]

    @pl.when((g == 0) | (te_ref[jnp.maximum(g - 1, 0)] != e))
    def _():
        wgb_ref[...] = wg_ref[...].astype(BF16)
        wub_ref[...] = wu_ref[...].astype(BF16)
        wdb_ref[...] = wd_ref[...].astype(BF16)

    @pl.when(g < nv_ref[0])
    def _():
        x = xs_ref[:, :D_MODEL]
        meta = xs_ref[:, D_MODEL:].astype(F32)
        lane = lax.broadcasted_iota(jnp.int32, meta.shape, 1)
        mine = (lane >= e * META_PIECES) & (lane < (e + 1) * META_PIECES)
        w = jnp.sum(jnp.where(mine, meta, 0.0), axis=-1, keepdims=True)
        hg = jnp.dot(x, wgb_ref[...], preferred_element_type=F32)
        hu = jnp.dot(x, wub_ref[...], preferred_element_type=F32)
        a = (hg * jax.nn.sigmoid(hg)) * hu * w
        ys_ref[...] = jnp.dot(a.astype(BF16), wdb_ref[...], preferred_element_type=F32).astype(BF16)


def _combine_kernel(cdst_ref, tot_ref,
                    x1_ref, route_ref, lbase_ref, ltri_ref, gfin_ref, ys_ref, out_ref, ybuf_ref, sem_ref,
                    *, tm):
    i = pl.program_id(0)
    n_tiles = pl.num_programs(0)
    slot = i % 2

    def chunk(slot_, src_row, dst_row):
        return pltpu.make_async_copy(ys_ref.at[pl.ds(src_row, CHUNK), :],
                                     ybuf_ref.at[slot_, pl.ds(dst_row, CHUNK), :], sem_ref.at[slot_])

    def fetch_tile(tile, slot_):
        _chunk_loop(tot_ref[tile], lambda c: chunk(
            slot_, pl.multiple_of(cdst_ref[tile * LOCAL_CHUNKS + c] * CHUNK, CHUNK),
            pl.multiple_of(c * CHUNK, CHUNK)).start())

    @pl.when(i == 0)
    def _():
        ybuf_ref[...] = jnp.zeros_like(ybuf_ref)
        fetch_tile(0, 0)

    @pl.when(i + 1 < n_tiles)
    def _():
        fetch_tile(i + 1, 1 - slot)

    _chunk_loop(tot_ref[i], lambda c: chunk(slot, 0, 0).wait())

    ls1, ls2, _ = _local_slots(route_ref[...], lbase_ref[0, 0:1, :], ltri_ref[...])
    sel = _selection_matrix(ls1, ls2, 1, (tm, LOCAL_ROWS))
    y = jnp.dot(sel, ybuf_ref[slot], preferred_element_type=F32)
    out_ref[...] = _rms(x1_ref[...] + y, gfin_ref[...])


def _routing_tables(cnt, n_exp_tiles):
    c16 = (cnt + CHUNK - 1) // CHUNK
    lbase = jnp.cumsum(c16, axis=1) - c16
    tile_off = jnp.cumsum(c16, axis=0) - c16
    tot = jnp.sum(c16, axis=0)
    per = EXP_TILE // CHUNK
    region_tiles = (tot + per - 1) // per
    region = region_tiles * per
    base = jnp.cumsum(region) - region
    dst = base[None, :] + tile_off
    tile_end = jnp.cumsum(region_tiles)
    n_valid = tile_end[-1]
    g = jnp.arange(n_exp_tiles, dtype=jnp.int32)
    tile_expert = jnp.sum(tile_end[None, :] <= jnp.minimum(g, n_valid - 1)[:, None], axis=1).astype(jnp.int32)
    c = jnp.arange(LOCAL_CHUNKS, dtype=jnp.int32)
    lend = lbase + c16
    owner = jnp.minimum(jnp.sum(lend[:, None, :] <= c[None, :, None], axis=2), N_EXPERTS - 1)
    chunk_dst = (jnp.take_along_axis(dst, owner, axis=1) + c[None, :]
                 - jnp.take_along_axis(lbase, owner, axis=1))
    i32 = lambda a: a.astype(jnp.int32).reshape(-1)
    return dict(chunk_dst=i32(chunk_dst), tot=i32(jnp.sum(c16, axis=1)),
                tail_start=i32(base + tot), tail_n=i32(region - tot),
                tile_expert=tile_expert, n_valid=i32(n_valid),
                lbase_rows=(lbase * CHUNK).astype(F32))


def _moe(h2, route, cnt_tiles, x1, w_gate, w_up, w_down, g_final):
    n_tok = h2.shape[0]
    tm = MOE_TILE
    n_tiles = n_tok // tm
    assert cnt_tiles.shape[0] == n_tiles
    worst_rows = 2 * n_tok + n_tiles * N_EXPERTS * (CHUNK - 1) + N_EXPERTS * (EXP_TILE - CHUNK)
    n_exp_tiles = -(-worst_rows // EXP_TILE)
    n_slots = n_exp_tiles * EXP_TILE

    cnt = cnt_tiles[:, 0, EXPERT_LANE0:EXPERT_LANE0 + N_EXPERTS].astype(jnp.int32)
    tb = _routing_tables(cnt, n_exp_tiles)
    lbase_rows = jnp.zeros((n_tiles, 8, ROUTER_LANES), F32).at[:, :, :N_EXPERTS].set(
        tb["lbase_rows"][:, None, :])
    row_id = jnp.arange(tm)
    ltri = (row_id[:, None] > row_id[None, :]).astype(BF16)

    tile_row = lambda width: pl.BlockSpec((tm, width), lambda i, *_: (i, 0))
    lbase_spec = pl.BlockSpec((1, 8, ROUTER_LANES), lambda i, *_: (i, 0, 0))
    ltri_spec = pl.BlockSpec((tm, tm), lambda i, *_: (0, 0), pipeline_mode=pl.Buffered(1))
    hbm = pl.BlockSpec(memory_space=pl.ANY)

    xs = pl.pallas_call(
        functools.partial(_dispatch_kernel, tm=tm),
        grid_spec=pltpu.PrefetchScalarGridSpec(
            num_scalar_prefetch=4, grid=(n_tiles,),
            in_specs=[tile_row(D_MODEL), tile_row(ROUTER_LANES), lbase_spec, ltri_spec],
            out_specs=hbm,
            scratch_shapes=[pltpu.VMEM((2, LOCAL_ROWS, XS_WIDTH), BF16), pltpu.VMEM((CHUNK, XS_WIDTH), BF16),
                            pltpu.SemaphoreType.DMA((2,)), pltpu.SemaphoreType.DMA]),
        out_shape=jax.ShapeDtypeStruct((n_slots, XS_WIDTH), BF16),
        compiler_params=pltpu.CompilerParams(
            dimension_semantics=("arbitrary",), vmem_limit_bytes=VMEM_LIMIT),
        name="moe_dispatch",
    )(tb["chunk_dst"], tb["tot"], tb["tail_start"], tb["tail_n"], h2, route, lbase_rows, ltri)

    row_tile = lambda width: pl.BlockSpec(
        (EXP_TILE, width), lambda g, te, nv: (jnp.maximum(jnp.minimum(g, nv[0] - 1), 0), 0))
    ys = pl.pallas_call(
        _expert_kernel,
        grid_spec=pltpu.PrefetchScalarGridSpec(
            num_scalar_prefetch=2, grid=(n_exp_tiles,),
            in_specs=[row_tile(XS_WIDTH),
                      pl.BlockSpec((None, D_MODEL, D_EXPERT), lambda g, te, nv: (te[g], 0, 0)),
                      pl.BlockSpec((None, D_MODEL, D_EXPERT), lambda g, te, nv: (te[g], 0, 0)),
                      pl.BlockSpec((None, D_EXPERT, D_MODEL), lambda g, te, nv: (te[g], 0, 0))],
            out_specs=row_tile(D_MODEL),
            scratch_shapes=[pltpu.VMEM((D_MODEL, D_EXPERT), BF16), pltpu.VMEM((D_MODEL, D_EXPERT), BF16),
                            pltpu.VMEM((D_EXPERT, D_MODEL), BF16)]),
        out_shape=jax.ShapeDtypeStruct((n_slots, D_MODEL), BF16),
        compiler_params=pltpu.CompilerParams(
            dimension_semantics=("arbitrary",), vmem_limit_bytes=VMEM_LIMIT),
        name="moe_expert",
    )(tb["tile_expert"], tb["n_valid"], xs, w_gate, w_up, w_down)

    return pl.pallas_call(
        functools.partial(_combine_kernel, tm=tm),
        grid_spec=pltpu.PrefetchScalarGridSpec(
            num_scalar_prefetch=2, grid=(n_tiles,),
            in_specs=[tile_row(D_MODEL), tile_row(ROUTER_LANES), lbase_spec, ltri_spec,
                      pl.BlockSpec((1, D_MODEL), lambda i, *_: (0, 0)), hbm],
            out_specs=tile_row(D_MODEL),
            scratch_shapes=[pltpu.VMEM((2, LOCAL_ROWS, D_MODEL), BF16), pltpu.SemaphoreType.DMA((2,))]),
        out_shape=jax.ShapeDtypeStruct((n_tok, D_MODEL), F32),
        compiler_params=pltpu.CompilerParams(
            dimension_semantics=("arbitrary",), vmem_limit_bytes=VMEM_LIMIT),
        name="moe_combine",
    )(tb["chunk_dst"], tb["tot"], x1, route, lbase_rows, ltri, g_final, ys)


def _rope_tables(seq, tm):
    pos = jnp.arange(seq, dtype=F32)
    inv_freq = ROPE_THETA ** (-jnp.arange(0, HEAD_DIM, 2, dtype=F32) / HEAD_DIM)
    ang = pos[:, None] * inv_freq[None, :]
    cos, sin = jnp.cos(ang), jnp.sin(ang)
    reps = CB // HEAD_DIM
    cos_t = jnp.tile(jnp.concatenate([cos, cos], axis=-1), (1, reps))
    sin_t = jnp.tile(jnp.concatenate([-sin, sin], axis=-1), (1, reps))

    def reorder(t, dil):
        return t.reshape(seq // tm, tm // dil, dil, CB).transpose(0, 2, 1, 3).reshape(seq, CB)

    dils = [dil for _, dil in DIL_GROUPS]
    return (jnp.stack([reorder(cos_t, dil) for dil in dils]),
            jnp.stack([reorder(sin_t, dil) for dil in dils]))


def kernel(x, w_in, b_in, sinks, w_proj_a, w_proj_b, w_out, g_mix, g_ffn, w_router_group, b_router_group,
           w_router_expert, b_router_expert, w_exp_gate, w_exp_up, w_exp_down, g_final):
    batch, seq, d = x.shape
    assert d == D_MODEL and w_in.shape[0] == 1, "single-layer kernel"
    n_tok = batch * seq
    x2 = x.reshape(n_tok, d)
    tm_in = 512
    cos_t, sin_t = _rope_tables(seq, tm_in)

    z_tok, z_d1, z_d2 = _in_proj(x2, g_mix[0][None, :], w_in[0].astype(BF16), b_in[0][None, :],
                                 cos_t, sin_t, seq=seq, tm=tm_in)

    a_bases = (0, A_BLOCKS, 2 * A_BLOCKS)
    tok_bases = tuple(ZB_A0 + b for b in a_bases)
    oa, lse = [], []
    for group, (z, bases) in enumerate(((z_tok, tok_bases), (z_d1, a_bases), (z_d2, a_bases))):
        o_g, l_g = _dilated_attention(z, bases, group, batch=batch, seq=seq, lq=512)
        oa.append(o_g)
        lse.append(l_g)
    ob = _swa_attention(z_tok, sinks[0], batch=batch, seq=seq, lq=512)

    pad = ROUTER_LANES - N_EXPERT_GROUPS - N_EXPERTS
    w_router = jnp.concatenate(
        [w_router_group[0], w_router_expert[0], jnp.zeros((d, pad), F32)], axis=1)
    b_router = jnp.concatenate(
        [b_router_group[0], b_router_expert[0], jnp.zeros((pad,), F32)])[None, :]
    x1, h2, route, cnt_tiles = _post_attn(
        oa, lse, ob, z_tok, x2, w_proj_a[0].astype(BF16), w_proj_b[0].astype(BF16), w_out[0].astype(BF16),
        g_ffn[0][None, :], w_router, b_router, tm=MOE_TILE)

    out = _moe(h2, route, cnt_tiles, x1, w_exp_gate[0], w_exp_up[0], w_exp_down[0], g_final[None, :])
    return out.reshape(batch, seq, d)
```

```python
import functools
import math

import jax
import jax.numpy as jnp
import numpy as np
from jax import lax
from jax.experimental import pallas as pl
from jax.experimental.pallas import tpu as pltpu

F32 = jnp.float32
BF16 = jnp.bfloat16

D_MODEL = 1024
HEAD_DIM = 64
HALF = HEAD_DIM // 2
ROPE_THETA = 10000.0
RMS_EPS = 1e-6
LOG2E = math.log2(math.e)
LN2 = math.log(2.0)
Q_SCALE = LOG2E * HEAD_DIM ** -0.5
BLOCK = 128
ATTN_ROWS_PER_STEP = 512
DIL_GROUPS = ((128, 1), (512, 4), (2048, 16))
N_DIL = len(DIL_GROUPS)
A_GROUP_WIDTH = 512
A_QKV_WIDTH = 3 * N_DIL * A_GROUP_WIDTH
B_Q_HEADS = 16
B_KV_HEADS = 2
B_Q_WIDTH = B_Q_HEADS * HEAD_DIM
B_WINDOW = 128
GATE_WIDTH = 2 * D_MODEL
IN_WIDTH = A_QKV_WIDTH + B_Q_WIDTH + 2 * B_KV_HEADS * HEAD_DIM + GATE_WIDTH
N_EXPERT_GROUPS = 4
EXPERTS_PER_GROUP = 8
N_EXPERTS = N_EXPERT_GROUPS * EXPERTS_PER_GROUP
D_EXPERT = D_MODEL // 4

CB = 256
PAIR = 128
N_IN_BLOCKS = IN_WIDTH // CB
A_BLOCKS = A_GROUP_WIDTH // CB
ZB_GATE = 0
ZB_QB = 8
ZB_KB = 12
ZB_VB = 13
ZB_A0 = 14
N_TOK_BLOCKS = ZB_A0 + 3 * A_BLOCKS
OUT_PERM_BLOCK = 256
LSE_PERM_BLOCK = 128
ROUTER_LANES = 128
EXPERT_LANE0 = N_EXPERT_GROUPS
MOE_TILE = 512
CHUNK = 16
EXP_TILE = 512
LOCAL_ROWS = -(-(2 * MOE_TILE + N_EXPERTS * (CHUNK - 1)) // CB) * CB
LOCAL_CHUNKS = LOCAL_ROWS // CHUNK
XS_WIDTH = D_MODEL + ROUTER_LANES
META_PIECES = 3
assert META_PIECES * N_EXPERTS <= ROUTER_LANES

VMEM_LIMIT = 56 * 1024 * 1024


def _in_proj_plan():
    plan = []
    for c in range(N_IN_BLOCKS):
        col = c * CB
        if col < A_QKV_WIDTH:
            part, rem = divmod(col, N_DIL * A_GROUP_WIDTH)
            group, blk = divmod(rem // CB, A_BLOCKS)
            kind = ("q", "k", "v")[part]
            dil = DIL_GROUPS[group][1]
            if dil == 1:
                plan.append((0, ZB_A0 + part * A_BLOCKS + blk, kind, 1))
            else:
                plan.append((group, part * A_BLOCKS + blk, kind, dil))
        elif col < A_QKV_WIDTH + B_Q_WIDTH:
            plan.append((0, ZB_QB + (col - A_QKV_WIDTH) // CB, "q", 1))
        elif col < A_QKV_WIDTH + B_Q_WIDTH + CB:
            plan.append((0, -1, "kvb", 1))
        else:
            plan.append((0, ZB_GATE + (col - (A_QKV_WIDTH + B_Q_WIDTH + CB)) // CB, "v", 1))
    return tuple(plan)


def _rms(x, g):
    return x * lax.rsqrt(jnp.mean(x * x, axis=-1, keepdims=True) + RMS_EPS) * g


def _split3(w):
    hi = w.astype(BF16).astype(F32)
    mid = (w - hi).astype(BF16).astype(F32)
    lo = (w - hi - mid).astype(BF16).astype(F32)
    return hi, mid, lo


def _rope(acc, cos, sin_signed, first_half):
    partner = jnp.where(first_half, pltpu.roll(acc, CB - HALF, 1), pltpu.roll(acc, HALF, 1))
    return acc * cos + partner * sin_signed


def _in_perm_block(dil):
    return max(BLOCK, CHUNK * dil)


def _in_proj_kernel(x_ref, g_ref, w_ref, b_ref, cos_ref, sin_ref, perm1_ref, perm2_ref,
                    zt_ref, zd1_ref, zd2_ref, *, plan, tm):
    out_refs = (zt_ref, zd1_ref, zd2_ref)
    perm_refs = (None, perm1_ref, perm2_ref)
    lane = lax.broadcasted_iota(jnp.int32, (tm, CB), 1)
    first_half = (lane % HEAD_DIM) < HALF

    h = _rms(x_ref[...], g_ref[...]).astype(BF16)
    h_by_dil, tables = {}, {}
    for slot, (_, dil) in enumerate(DIL_GROUPS):
        tables[dil] = slot
        if dil == 1:
            h_by_dil[dil] = h
            continue
        blk = _in_perm_block(dil)
        n = blk // dil
        moved = [jnp.dot(perm_refs[slot][...], h[tb * blk:(tb + 1) * blk], preferred_element_type=F32
                         ).astype(BF16) for tb in range(tm // blk)]
        h_by_dil[dil] = jnp.concatenate(
            [part[r * n:(r + 1) * n] for r in range(dil) for part in moved], axis=0)

    for c, (arr, dst, kind, dil) in enumerate(plan):
        cols = slice(c * CB, (c + 1) * CB)
        acc = jnp.dot(h_by_dil[dil], w_ref[:, cols], preferred_element_type=F32) + b_ref[:, cols]
        if kind in ("q", "k", "kvb"):
            slot = tables[dil]
            rot = _rope(acc, cos_ref[slot], sin_ref[slot], first_half)
        if kind == "q":
            val = (rot * Q_SCALE).astype(BF16)
        elif kind == "k":
            val = rot.astype(BF16)
        elif kind == "v":
            val = acc.astype(BF16)
        else:
            r64 = pltpu.roll(rot, HEAD_DIM, 1)
            r128 = pltpu.roll(rot, 2 * HEAD_DIM, 1)
            kdup = jnp.where(lane < HEAD_DIM, rot, jnp.where(lane < 3 * HEAD_DIM, r64, r128))
            a128 = pltpu.roll(acc, 2 * HEAD_DIM, 1)
            a192 = pltpu.roll(acc, 3 * HEAD_DIM, 1)
            vdup = jnp.where(lane < HEAD_DIM, a128, jnp.where(lane < 3 * HEAD_DIM, a192, acc))
            zt_ref[ZB_KB] = kdup.astype(BF16)
            zt_ref[ZB_VB] = vdup.astype(BF16)
            continue
        if dil == 1:
            out_refs[arr][dst] = val
        else:
            n = tm // dil
            for r in range(dil):
                out_refs[arr][dst, :, r * CB:(r + 1) * CB] = val[r * n:(r + 1) * n, :]


def _dilation_perm(block, dil, dtype):
    j = np.arange(block)
    src = (j % (block // dil)) * dil + j // (block // dil)
    return jnp.asarray((src[:, None] == j[None, :]).astype(np.float32), dtype=dtype)


def _in_proj(x2, g_mix, w_in, b_in, cos_t, sin_t, *, seq, tm):
    n_tok = x2.shape[0]
    tiles_per_seq = seq // tm
    const = lambda i: (0, 0)
    table = pl.BlockSpec((N_DIL, tm, CB), lambda i: (0, i % tiles_per_seq, 0))
    d1, d2 = DIL_GROUPS[1][1], DIL_GROUPS[2][1]
    perms = [_dilation_perm(_in_perm_block(d), d, BF16) for d in (d1, d2)]
    perm_spec = lambda d: pl.BlockSpec((_in_perm_block(d),) * 2, const, pipeline_mode=pl.Buffered(1))
    return pl.pallas_call(
        functools.partial(_in_proj_kernel, plan=_in_proj_plan(), tm=tm),
        grid=(n_tok // tm,),
        in_specs=[
            pl.BlockSpec((tm, D_MODEL), lambda i: (i, 0)),
            pl.BlockSpec((1, D_MODEL), const),
            pl.BlockSpec((D_MODEL, IN_WIDTH), const, pipeline_mode=pl.Buffered(1)),
            pl.BlockSpec((1, IN_WIDTH), const),
            table, table,
            perm_spec(d1), perm_spec(d2),
        ],
        out_specs=[
            pl.BlockSpec((N_TOK_BLOCKS, tm, CB), lambda i: (0, i, 0)),
            pl.BlockSpec((3 * A_BLOCKS, tm // d1, d1 * CB), lambda i: (0, i, 0)),
            pl.BlockSpec((3 * A_BLOCKS, tm // d2, d2 * CB), lambda i: (0, i, 0)),
        ],
        out_shape=[
            jax.ShapeDtypeStruct((N_TOK_BLOCKS, n_tok, CB), BF16),
            jax.ShapeDtypeStruct((3 * A_BLOCKS, n_tok // d1, d1 * CB), BF16),
            jax.ShapeDtypeStruct((3 * A_BLOCKS, n_tok // d2, d2 * CB), BF16),
        ],
        compiler_params=pltpu.CompilerParams(
            dimension_semantics=("arbitrary",), vmem_limit_bytes=VMEM_LIMIT),
        name="in_proj",
    )(x2, g_mix, w_in, b_in, cos_t, sin_t, *perms)


def _attn_kernel(*refs, lq, max_dist, kv_shared, has_sink, want_lse, n_axes):
    refs = list(refs)
    sink_ref = refs.pop(0) if has_sink else None
    q_ref, k_ref, v_ref, kp_ref, vp_ref = refs[:5]
    o_ref = refs[5]
    lse_ref = refs[6] if want_lse else None
    vaug_ref = refs[-1]
    n_qblk = q_ref.shape[0]
    n_kv_pairs = vaug_ref.shape[0]
    rows_kv = lq + BLOCK

    row = lax.broadcasted_iota(jnp.int32, (BLOCK, 2 * BLOCK), 0)
    col = lax.broadcasted_iota(jnp.int32, (BLOCK, 2 * BLOCK), 1)
    dist = row - col + BLOCK
    valid = (dist >= 0) & (dist <= max_dist)
    neg_inf = jnp.float32(-jnp.inf)
    bias = jnp.where(valid, 0.0, neg_inf)
    bias_first = jnp.where(valid & (col >= BLOCK), 0.0, neg_inf)
    bias0 = jnp.where(pl.program_id(n_axes - 1) == 0, bias_first, bias)
    lane_lo = lax.broadcasted_iota(jnp.int32, (BLOCK, PAIR), 1) < HEAD_DIM

    def rd(ref, blk, rows, cols):
        return ref[rows, cols] if kv_shared else ref[blk, rows, cols]

    first_step = functools.reduce(lambda a, b: a & b, [pl.program_id(a) == 0 for a in range(n_axes)])

    @pl.when(first_step)
    def _():
        lane = lax.broadcasted_iota(jnp.int32, (rows_kv, PAIR), 1)
        for pair in range(n_kv_pairs):
            vaug_ref[pair, 0, :, PAIR:] = jnp.where(lane < HEAD_DIM, 1.0, 0.0).astype(BF16)
            vaug_ref[pair, 1, :, PAIR:] = jnp.where(lane < HEAD_DIM, 0.0, 1.0).astype(BF16)

    lane_kv = lax.broadcasted_iota(jnp.int32, (rows_kv, PAIR), 1) < HEAD_DIM
    n_col_pairs = q_ref.shape[-1] // PAIR
    for pair in range(n_kv_pairs):
        cols = slice((pair % n_col_pairs) * PAIR, (pair % n_col_pairs + 1) * PAIR)
        everything = slice(None)
        v_all = jnp.concatenate([rd(vp_ref, pair // n_col_pairs, everything, cols),
                                 rd(v_ref, pair // n_col_pairs, everything, cols)], axis=0)
        zero = jnp.zeros_like(v_all)
        vaug_ref[pair, 0, :, :PAIR] = jnp.where(lane_kv, v_all, zero)
        vaug_ref[pair, 1, :, :PAIR] = jnp.where(lane_kv, zero, v_all)

    for ib in range(lq // BLOCK):
        rows = slice(ib * BLOCK, (ib + 1) * BLOCK)
        win = slice(ib * BLOCK, (ib + 2) * BLOCK)
        b_ib = bias0 if ib == 0 else bias
        for blk in range(n_qblk):
            for pp in range(n_col_pairs):
                qcols = slice(pp * PAIR, (pp + 1) * PAIR)
                pair = 0 if kv_shared else blk * n_col_pairs + pp
                kcols = slice(0, PAIR) if kv_shared else qcols
                if ib == 0:
                    k_win = jnp.concatenate([rd(kp_ref, blk, slice(None), kcols),
                                             rd(k_ref, blk, slice(0, BLOCK), kcols)], axis=0)
                else:
                    k_win = rd(k_ref, blk, slice((ib - 1) * BLOCK, (ib + 1) * BLOCK), kcols)
                q_pair = q_ref[blk, rows, qcols]
                ps, ms, sink_terms = [], [], []
                for hh in range(2):
                    q_h = jnp.where(lane_lo == (hh == 0), q_pair, jnp.zeros_like(q_pair))
                    s = lax.dot_general(q_h, k_win, (((1,), (1,)), ((), ())), preferred_element_type=F32)
                    s = s + b_ib
                    m = jnp.max(s, axis=-1, keepdims=True)
                    if has_sink:
                        head = (pl.program_id(1) * n_qblk + blk) * (CB // HEAD_DIM) + pp * 2 + hh
                        sink = sink_ref[head] * LOG2E
                        m = jnp.maximum(m, sink)
                        sink_terms.append(jnp.exp2(sink - m))
                    ps.append(jnp.exp2(s - m).astype(BF16))
                    ms.append(m)
                v_aug = jnp.concatenate([vaug_ref[pair, 0, win, :], vaug_ref[pair, 1, win, :]], axis=0)
                od = jnp.dot(jnp.concatenate(ps, axis=1), v_aug, preferred_element_type=F32)
                den = od[:, PAIR:]
                if has_sink:
                    den = den + jnp.where(lane_lo, sink_terms[0], sink_terms[1])
                o_ref[blk, rows, qcols] = (od[:, :PAIR] * (1.0 / den)).astype(BF16)
                if want_lse:
                    lse_ref[blk, rows, qcols] = (jnp.where(lane_lo, ms[0], ms[1]) + jnp.log2(den)) * LN2


def _dilated_attention(z, bases, group, *, batch, seq, lq):
    window, dil = DIL_GROUPS[group]
    sub_len = seq // dil
    lq = min(lq, sub_len)
    z4 = z.reshape(z.shape[0], batch, sub_len, dil * CB)
    qb, kb, vb = bases
    assert all(base % A_BLOCKS == 0 for base in bases)
    bpq = lq // BLOCK
    slabs = max(1, min(dil, ATTN_ROWS_PER_STEP // lq))
    width = slabs * CB
    cur = lambda base: pl.BlockSpec(
        (A_BLOCKS, None, lq, width), lambda b, r, i: (base // A_BLOCKS, b, i, r))
    prev = lambda base: pl.BlockSpec(
        (A_BLOCKS, None, BLOCK, width), lambda b, r, i: (base // A_BLOCKS, b, jnp.maximum(i * bpq - 1, 0), r))
    out_spec = pl.BlockSpec((A_BLOCKS, None, lq, width), lambda b, r, i: (0, b, i, r))
    o, lse = pl.pallas_call(
        functools.partial(_attn_kernel, lq=lq, max_dist=window // dil, kv_shared=False,
                          has_sink=False, want_lse=True, n_axes=3),
        grid=(batch, dil // slabs, sub_len // lq),
        in_specs=[cur(qb), cur(kb), cur(vb), prev(kb), prev(vb)],
        out_specs=[out_spec, out_spec],
        out_shape=[jax.ShapeDtypeStruct((A_BLOCKS, batch, sub_len, dil * CB), BF16),
                   jax.ShapeDtypeStruct((A_BLOCKS, batch, sub_len, dil * CB), F32)],
        scratch_shapes=[pltpu.VMEM((2 * A_BLOCKS * slabs, 2, lq + BLOCK, 2 * PAIR), BF16)],
        compiler_params=pltpu.CompilerParams(
            dimension_semantics=("arbitrary",) * 3, vmem_limit_bytes=VMEM_LIMIT),
        name=f"dilated_attn_g{group}",
    )(z4, z4, z4, z4, z4)
    rows = batch * sub_len
    return o.reshape(A_BLOCKS, rows, dil * CB), lse.reshape(A_BLOCKS, rows, dil * CB)


def _swa_attention(z_tok, sinks, *, batch, seq, lq):
    z4 = z_tok.reshape(N_TOK_BLOCKS, batch, seq, CB)
    bpq = lq // BLOCK
    n_q_blocks = B_Q_WIDTH // CB
    q_per_kv = n_q_blocks // B_KV_HEADS
    assert ZB_QB % q_per_kv == 0
    q_spec = pl.BlockSpec((q_per_kv, None, lq, CB), lambda b, kvh, i, s: (ZB_QB // q_per_kv + kvh, b, i, 0))
    cur = lambda base: pl.BlockSpec((None, None, lq, PAIR), lambda b, kvh, i, s: (base, b, i, kvh))
    prev = lambda base: pl.BlockSpec(
        (None, None, BLOCK, PAIR), lambda b, kvh, i, s: (base, b, jnp.maximum(i * bpq - 1, 0), kvh))
    o = pl.pallas_call(
        functools.partial(_attn_kernel, lq=lq, max_dist=B_WINDOW - 1, kv_shared=True,
                          has_sink=True, want_lse=False, n_axes=3),
        grid_spec=pltpu.PrefetchScalarGridSpec(
            num_scalar_prefetch=1,
            grid=(batch, B_KV_HEADS, seq // lq),
            in_specs=[q_spec, cur(ZB_KB), cur(ZB_VB), prev(ZB_KB), prev(ZB_VB)],
            out_specs=pl.BlockSpec((q_per_kv, None, lq, CB), lambda b, kvh, i, s: (kvh, b, i, 0)),
            scratch_shapes=[pltpu.VMEM((1, 2, lq + BLOCK, 2 * PAIR), BF16)],
        ),
        out_shape=jax.ShapeDtypeStruct((n_q_blocks, batch, seq, CB), BF16),
        compiler_params=pltpu.CompilerParams(
            dimension_semantics=("arbitrary",) * 3, vmem_limit_bytes=VMEM_LIMIT),
        name="swa_attn",
    )(sinks, z4, z4, z4, z4, z4)
    return o.reshape(n_q_blocks, batch * seq, CB)


def _post_attn_kernel(o0_ref, o1_ref, o2_ref, l0_ref, l1_ref, l2_ref, ob_ref, gate_ref, x_ref,
                      wa_ref, wb_ref, wo_ref, gf_ref, wr_ref, br_ref,
                      po_ref, pl_ref, x1_ref, h2_ref, route_ref, cnt_ref, *, tm):

    def to_token_order(ref, cb, slot, perm_ref, blk):
        dil = DIL_GROUPS[slot + 1][1]
        n = blk // dil
        parts = []
        for tb in range(tm // blk):
            stack = jnp.concatenate(
                [ref[cb, tb * n:(tb + 1) * n, r * CB:(r + 1) * CB] for r in range(dil)], axis=0)
            pieces = (stack,) if stack.dtype == BF16 else _split3(stack)
            moved = [jnp.dot(perm_ref[slot], p.astype(BF16), preferred_element_type=F32) for p in pieces]
            parts.append(functools.reduce(lambda a, b: a + b, moved))
        return jnp.concatenate(parts, axis=0)

    pa = None
    for cb in range(A_BLOCKS):
        l0 = l0_ref[cb]
        l1 = to_token_order(l1_ref, cb, 0, pl_ref, LSE_PERM_BLOCK)
        l2 = to_token_order(l2_ref, cb, 1, pl_ref, LSE_PERM_BLOCK)
        o1 = to_token_order(o1_ref, cb, 0, po_ref, OUT_PERM_BLOCK)
        o2 = to_token_order(o2_ref, cb, 1, po_ref, OUT_PERM_BLOCK)
        mx = jnp.maximum(jnp.maximum(l0, l1), l2)
        e0, e1, e2 = jnp.exp(l0 - mx), jnp.exp(l1 - mx), jnp.exp(l2 - mx)
        inv = 1.0 / (e0 + e1 + e2)
        ya = (e0 * inv) * o0_ref[cb].astype(F32) + (e1 * inv) * o1 + (e2 * inv) * o2
        part = jnp.dot(ya.astype(BF16), wa_ref[cb * CB:(cb + 1) * CB, :], preferred_element_type=F32)
        pa = part if pa is None else pa + part
    pb = None
    for cb in range(B_Q_WIDTH // CB):
        part = jnp.dot(ob_ref[cb], wb_ref[cb * CB:(cb + 1) * CB, :], preferred_element_type=F32)
        pb = part if pb is None else pb + part
    n_gate = D_MODEL // CB
    merged = []
    for cb in range(n_gate):
        cols = slice(cb * CB, (cb + 1) * CB)
        ga = jax.nn.sigmoid(gate_ref[cb].astype(F32))
        gb = jax.nn.sigmoid(gate_ref[n_gate + cb].astype(F32))
        merged.append((ga * pa[:, cols] + gb * pb[:, cols]).astype(BF16))
    merged = jnp.concatenate(merged, axis=1)
    x1 = x_ref[...] + jnp.dot(merged, wo_ref[...], preferred_element_type=F32)
    x1_ref[...] = x1
    h2 = _rms(x1, gf_ref[...])
    h2_ref[...] = h2.astype(BF16)

    h2_hi = h2.astype(BF16)
    h2_lo = (h2 - h2_hi.astype(F32)).astype(BF16)
    wr = wr_ref[...]
    wr_hi = wr.astype(BF16)
    wr_lo = (wr - wr_hi.astype(F32)).astype(BF16)
    hi_terms = jnp.dot(h2_hi, jnp.concatenate([wr_hi, wr_lo], axis=1), preferred_element_type=F32)
    logits = (hi_terms[:, :ROUTER_LANES]
              + (jnp.dot(h2_lo, wr_hi, preferred_element_type=F32) + hi_terms[:, ROUTER_LANES:])) + br_ref[...]
    lane = lax.broadcasted_iota(jnp.int32, logits.shape, 1)
    neg_inf = jnp.float32(-jnp.inf)
    lg = jnp.where(lane < N_EXPERT_GROUPS, logits, neg_inf)
    mg = jnp.max(lg, axis=-1, keepdims=True)
    gsel = jnp.min(jnp.where(lg == mg, lane, ROUTER_LANES), axis=-1, keepdims=True)
    pg_sel = 1.0 / jnp.sum(jnp.exp(lg - mg), axis=-1, keepdims=True)
    expert = lane - EXPERT_LANE0
    in_group = (expert >= 0) & (expert < N_EXPERTS) & ((expert // EXPERTS_PER_GROUP) == gsel)
    le = jnp.where(in_group, logits, neg_inf)
    me = jnp.max(le, axis=-1, keepdims=True)
    ex = jnp.exp(le - me)
    pe = ex / jnp.sum(ex, axis=-1, keepdims=True)
    p1 = jnp.max(pe, axis=-1, keepdims=True)
    i1 = jnp.min(jnp.where(in_group & (pe == p1), lane, ROUTER_LANES), axis=-1, keepdims=True)
    rest = in_group & (lane != i1)
    p2 = jnp.max(jnp.where(rest, pe, -1.0), axis=-1, keepdims=True)
    i2 = jnp.min(jnp.where(rest & (pe == p2), lane, ROUTER_LANES), axis=-1, keepdims=True)
    norm = pg_sel / (p1 + p2)
    route_ref[...] = jnp.where(
        lane == 0, (i1 - EXPERT_LANE0).astype(F32),
        jnp.where(lane == 1, (i2 - EXPERT_LANE0).astype(F32),
                  jnp.where(lane == 2, p1 * norm, jnp.where(lane == 3, p2 * norm, 0.0))))
    hits = jnp.where((lane == i1) | (lane == i2), 1.0, 0.0)
    cnt_ref[0] = jnp.broadcast_to(jnp.sum(hits, axis=0, keepdims=True), cnt_ref.shape[1:])


def _post_attn(oa, lse, ob, z_tok, x2, w_proj_a, w_proj_b, w_out, g_ffn, w_router, b_router, *, tm):
    n_tok = x2.shape[0]
    const2 = lambda i: (0, 0)
    blk = lambda nb: pl.BlockSpec((nb, tm, CB), lambda i: (0, i, 0))
    dil_blk = lambda dil: pl.BlockSpec((A_BLOCKS, tm // dil, dil * CB), lambda i: (0, i, 0))
    row = lambda width: pl.BlockSpec((tm, width), lambda i: (i, 0))
    resident = lambda shape: pl.BlockSpec(shape, const2, pipeline_mode=pl.Buffered(1))
    d1, d2 = DIL_GROUPS[1][1], DIL_GROUPS[2][1]
    perm_o = jnp.stack([_dilation_perm(OUT_PERM_BLOCK, d, BF16).T for d in (d1, d2)])
    perm_l = jnp.stack([_dilation_perm(LSE_PERM_BLOCK, d, BF16).T for d in (d1, d2)])
    return pl.pallas_call(
        functools.partial(_post_attn_kernel, tm=tm),
        grid=(n_tok // tm,),
        in_specs=[blk(A_BLOCKS), dil_blk(d1), dil_blk(d2), blk(A_BLOCKS), dil_blk(d1), dil_blk(d2),
                  blk(B_Q_WIDTH // CB), blk(GATE_WIDTH // CB), row(D_MODEL),
                  resident((A_GROUP_WIDTH, D_MODEL)), resident((B_Q_WIDTH, D_MODEL)),
                  resident((D_MODEL, D_MODEL)), resident((1, D_MODEL)),
                  resident((D_MODEL, ROUTER_LANES)), resident((1, ROUTER_LANES)),
                  pl.BlockSpec((2, OUT_PERM_BLOCK, OUT_PERM_BLOCK), lambda i: (0, 0, 0),
                               pipeline_mode=pl.Buffered(1)),
                  pl.BlockSpec((2, LSE_PERM_BLOCK, LSE_PERM_BLOCK), lambda i: (0, 0, 0),
                               pipeline_mode=pl.Buffered(1))],
        out_specs=[row(D_MODEL), row(D_MODEL), row(ROUTER_LANES),
                   pl.BlockSpec((1, 8, ROUTER_LANES), lambda i: (i, 0, 0))],
        out_shape=[jax.ShapeDtypeStruct((n_tok, D_MODEL), F32),
                   jax.ShapeDtypeStruct((n_tok, D_MODEL), BF16),
                   jax.ShapeDtypeStruct((n_tok, ROUTER_LANES), F32),
                   jax.ShapeDtypeStruct((n_tok // tm, 8, ROUTER_LANES), F32)],
        compiler_params=pltpu.CompilerParams(
            dimension_semantics=("arbitrary",), vmem_limit_bytes=VMEM_LIMIT),
        name="post_attn",
    )(oa[0], oa[1], oa[2], lse[0], lse[1], lse[2], ob, z_tok, x2,
      w_proj_a, w_proj_b, w_out, g_ffn, w_router, b_router, perm_o, perm_l)


def _local_slots(route, lbase_row, ltri):
    lane = lax.broadcasted_iota(jnp.int32, route.shape, 1)
    pick = lambda k: jnp.sum(jnp.where(lane == k, route, 0.0), axis=-1, keepdims=True)
    lanef = lane.astype(F32)
    oh1, oh2 = lanef == pick(0), lanef == pick(1)
    oh = jnp.where(oh1 | oh2, 1.0, 0.0).astype(BF16)
    table = jnp.dot(ltri, oh, preferred_element_type=F32) + lbase_row
    ls1 = jnp.sum(jnp.where(oh1, table, 0.0), axis=-1, keepdims=True)
    ls2 = jnp.sum(jnp.where(oh2, table, 0.0), axis=-1, keepdims=True)
    return ls1, ls2, pick


def _chunk_loop(count, body):
    lax.fori_loop(0, count, lambda c, carry: (body(c), carry)[1], 0)


def _selection_matrix(slots_a, slots_b, axis, shape):
    span = CB
    local = lax.broadcasted_iota(jnp.int32, tuple(span if d == axis else s for d, s in enumerate(shape)),
                                 axis).astype(F32).astype(BF16)
    one, zero = jnp.ones_like(local), jnp.zeros_like(local)
    blocks = []
    for a in range(shape[axis] // span):
        rel_a = (slots_a - float(a * span)).astype(BF16)
        rel_b = (slots_b - float(a * span)).astype(BF16)
        blocks.append(jnp.where(local == rel_a, one, jnp.where(local == rel_b, one, zero)))
    return jnp.concatenate(blocks, axis=axis)


def _dispatch_kernel(cdst_ref, tot_ref, tail_start_ref, tail_n_ref,
                     h2_ref, route_ref, lbase_ref, ltri_ref, xs_ref, buf_ref, zero_ref, sem_ref, zsem_ref,
                     *, tm):
    i = pl.program_id(0)
    last = pl.num_programs(0) - 1
    slot = i % 2

    def chunk(slot_, src_row, dst_row):
        return pltpu.make_async_copy(buf_ref.at[slot_, pl.ds(src_row, CHUNK), :],
                                     xs_ref.at[pl.ds(dst_row, CHUNK), :], sem_ref.at[slot_])

    def wait_tile(slot_, tile):
        _chunk_loop(tot_ref[tile], lambda c: chunk(slot_, 0, 0).wait())

    @pl.when(i == 0)
    def _():
        zero_ref[...] = jnp.zeros_like(zero_ref)

        def zero_chunk(row):
            return pltpu.make_async_copy(zero_ref, xs_ref.at[pl.ds(row, CHUNK), :], zsem_ref)

        def per_expert(e, total):
            start = tail_start_ref[e] * CHUNK
            _chunk_loop(tail_n_ref[e], lambda c: zero_chunk(pl.multiple_of(start + c * CHUNK, CHUNK)).start())
            return total + tail_n_ref[e]

        total = lax.fori_loop(0, N_EXPERTS, per_expert, 0)
        _chunk_loop(total, lambda c: zero_chunk(0).wait())

    @pl.when(i >= 2)
    def _():
        wait_tile(slot, i - 2)

    route = route_ref[...]
    ls1, ls2, pick = _local_slots(route, lbase_ref[0, 0:1, :], ltri_ref[...])
    lane = lax.broadcasted_iota(jnp.int32, route.shape, 1)
    ls_t = jnp.where(lane == 0, ls1, jnp.where(lane == 1, ls2, 0.0)).T
    sel = _selection_matrix(ls_t[0:1, :], ls_t[1:2, :], 0, (LOCAL_ROWS, tm))
    lane_expert = (lane // META_PIECES).astype(F32)
    lane_piece = lane % META_PIECES
    by_piece = lambda pieces: jnp.where(lane_piece == 0, pieces[0],
                                        jnp.where(lane_piece == 1, pieces[1], pieces[2]))
    meta = jnp.where(lane_expert == pick(0), by_piece(_split3(pick(2))),
                     jnp.where(lane_expert == pick(1), by_piece(_split3(pick(3))), 0.0))
    buf_ref[slot, :, :D_MODEL] = jnp.dot(sel, h2_ref[...], preferred_element_type=F32).astype(BF16)
    buf_ref[slot, :, D_MODEL:] = jnp.dot(sel, meta.astype(BF16), preferred_element_type=F32).astype(BF16)

    _chunk_loop(tot_ref[i], lambda c: chunk(
        slot, pl.multiple_of(c * CHUNK, CHUNK),
        pl.multiple_of(cdst_ref[i * LOCAL_CHUNKS + c] * CHUNK, CHUNK)).start())

    @pl.when(i == last)
    def _():
        @pl.when(i >= 1)
        def _():
            wait_tile(1 - slot, i - 1)
        wait_tile(slot, i)


def _expert_kernel(te_ref, nv_ref, xs_ref, wg_ref, wu_ref, wd_ref, ys_ref, wgb_ref, wub_ref, wdb_ref):
    g = pl.program_id(0)
    e = te_ref[g]

    @pl.when((g == 0) | (te_ref[jnp.maximum(g - 1, 0)] != e))
    def _():
        wgb_ref[...] = wg_ref[...].astype(BF16)
        wub_ref[...] = wu_ref[...].astype(BF16)
        wdb_ref[...] = wd_ref[...].astype(BF16)

    @pl.when(g < nv_ref[0])
    def _():
        x = xs_ref[:, :D_MODEL]
        meta = xs_ref[:, D_MODEL:].astype(F32)
        lane = lax.broadcasted_iota(jnp.int32, meta.shape, 1)
        mine = (lane >= e * META_PIECES) & (lane < (e + 1) * META_PIECES)
        w = jnp.sum(jnp.where(mine, meta, 0.0), axis=-1, keepdims=True)
        hg = jnp.dot(x, wgb_ref[...], preferred_element_type=F32)
        hu = jnp.dot(x, wub_ref[...], preferred_element_type=F32)
        a = (hg * jax.nn.sigmoid(hg)) * hu * w
        ys_ref[...] = jnp.dot(a.astype(BF16), wdb_ref[...], preferred_element_type=F32).astype(BF16)


def _combine_kernel(cdst_ref, tot_ref,
                    x1_ref, route_ref, lbase_ref, ltri_ref, gfin_ref, ys_ref, out_ref, ybuf_ref, sem_ref,
                    *, tm):
    i = pl.program_id(0)
    n_tiles = pl.num_programs(0)
    slot = i % 2

    def chunk(slot_, src_row, dst_row):
        return pltpu.make_async_copy(ys_ref.at[pl.ds(src_row, CHUNK), :],
                                     ybuf_ref.at[slot_, pl.ds(dst_row, CHUNK), :], sem_ref.at[slot_])

    def fetch_tile(tile, slot_):
        _chunk_loop(tot_ref[tile], lambda c: chunk(
            slot_, pl.multiple_of(cdst_ref[tile * LOCAL_CHUNKS + c] * CHUNK, CHUNK),
            pl.multiple_of(c * CHUNK, CHUNK)).start())

    @pl.when(i == 0)
    def _():
        ybuf_ref[...] = jnp.zeros_like(ybuf_ref)
        fetch_tile(0, 0)

    @pl.when(i + 1 < n_tiles)
    def _():
        fetch_tile(i + 1, 1 - slot)

    _chunk_loop(tot_ref[i], lambda c: chunk(slot, 0, 0).wait())

    ls1, ls2, _ = _local_slots(route_ref[...], lbase_ref[0, 0:1, :], ltri_ref[...])
    sel = _selection_matrix(ls1, ls2, 1, (tm, LOCAL_ROWS))
    y = jnp.dot(sel, ybuf_ref[slot], preferred_element_type=F32)
    out_ref[...] = _rms(x1_ref[...] + y, gfin_ref[...])


def _routing_tables(cnt, n_exp_tiles):
    c16 = (cnt + CHUNK - 1) // CHUNK
    lbase = jnp.cumsum(c16, axis=1) - c16
    tile_off = jnp.cumsum(c16, axis=0) - c16
    tot = jnp.sum(c16, axis=0)
    per = EXP_TILE // CHUNK
    region_tiles = (tot + per - 1) // per
    region = region_tiles * per
    base = jnp.cumsum(region) - region
    dst = base[None, :] + tile_off
    tile_end = jnp.cumsum(region_tiles)
    n_valid = tile_end[-1]
    g = jnp.arange(n_exp_tiles, dtype=jnp.int32)
    tile_expert = jnp.sum(tile_end[None, :] <= jnp.minimum(g, n_valid - 1)[:, None], axis=1).astype(jnp.int32)
    c = jnp.arange(LOCAL_CHUNKS, dtype=jnp.int32)
    lend = lbase + c16
    owns = (lbase[:, None, :] <= c[None, :, None]) & (c[None, :, None] < lend[:, None, :])
    chunk_dst = jnp.sum(jnp.where(owns, (dst - lbase)[:, None, :], 0), axis=2) + c[None, :]
    i32 = lambda a: a.astype(jnp.int32).reshape(-1)
    return dict(chunk_dst=i32(chunk_dst), tot=i32(jnp.sum(c16, axis=1)),
                tail_start=i32(base + tot), tail_n=i32(region - tot),
                tile_expert=tile_expert, n_valid=i32(n_valid),
                lbase_rows=(lbase * CHUNK).astype(F32))


def _moe(h2, route, cnt_tiles, x1, w_gate, w_up, w_down, g_final):
    n_tok = h2.shape[0]
    tm = MOE_TILE
    n_tiles = n_tok // tm
    assert cnt_tiles.shape[0] == n_tiles
    worst_rows = 2 * n_tok + n_tiles * N_EXPERTS * (CHUNK - 1) + N_EXPERTS * (EXP_TILE - CHUNK)
    n_exp_tiles = -(-worst_rows // EXP_TILE)
    n_slots = n_exp_tiles * EXP_TILE

    cnt = cnt_tiles[:, 0, EXPERT_LANE0:EXPERT_LANE0 + N_EXPERTS].astype(jnp.int32)
    tb = _routing_tables(cnt, n_exp_tiles)
    lbase_rows = jnp.zeros((n_tiles, 8, ROUTER_LANES), F32).at[:, :, :N_EXPERTS].set(
        tb["lbase_rows"][:, None, :])
    row_id = np.arange(tm)
    ltri = jnp.asarray((row_id[:, None] > row_id[None, :]).astype(np.float32), dtype=BF16)

    tile_row = lambda width: pl.BlockSpec((tm, width), lambda i, *_: (i, 0))
    lbase_spec = pl.BlockSpec((1, 8, ROUTER_LANES), lambda i, *_: (i, 0, 0))
    ltri_spec = pl.BlockSpec((tm, tm), lambda i, *_: (0, 0), pipeline_mode=pl.Buffered(1))
    hbm = pl.BlockSpec(memory_space=pl.ANY)

    xs = pl.pallas_call(
        functools.partial(_dispatch_kernel, tm=tm),
        grid_spec=pltpu.PrefetchScalarGridSpec(
            num_scalar_prefetch=4, grid=(n_tiles,),
            in_specs=[tile_row(D_MODEL), tile_row(ROUTER_LANES), lbase_spec, ltri_spec],
            out_specs=hbm,
            scratch_shapes=[pltpu.VMEM((2, LOCAL_ROWS, XS_WIDTH), BF16), pltpu.VMEM((CHUNK, XS_WIDTH), BF16),
                            pltpu.SemaphoreType.DMA((2,)), pltpu.SemaphoreType.DMA]),
        out_shape=jax.ShapeDtypeStruct((n_slots, XS_WIDTH), BF16),
        compiler_params=pltpu.CompilerParams(
            dimension_semantics=("arbitrary",), vmem_limit_bytes=VMEM_LIMIT),
        name="moe_dispatch",
    )(tb["chunk_dst"], tb["tot"], tb["tail_start"], tb["tail_n"], h2, route, lbase_rows, ltri)

    row_tile = lambda width: pl.BlockSpec(
        (EXP_TILE, width), lambda g, te, nv: (jnp.maximum(jnp.minimum(g, nv[0] - 1), 0), 0))
    ys = pl.pallas_call(
        _expert_kernel,
        grid_spec=pltpu.PrefetchScalarGridSpec(
            num_scalar_prefetch=2, grid=(n_exp_tiles,),
            in_specs=[row_tile(XS_WIDTH),
                      pl.BlockSpec((None, D_MODEL, D_EXPERT), lambda g, te, nv: (te[g], 0, 0)),
                      pl.BlockSpec((None, D_MODEL, D_EXPERT), lambda g, te, nv: (te[g], 0, 0)),
                      pl.BlockSpec((None, D_EXPERT, D_MODEL), lambda g, te, nv: (te[g], 0, 0))],
            out_specs=row_tile(D_MODEL),
            scratch_shapes=[pltpu.VMEM((D_MODEL, D_EXPERT), BF16), pltpu.VMEM((D_MODEL, D_EXPERT), BF16),
                            pltpu.VMEM((D_EXPERT, D_MODEL), BF16)]),
        out_shape=jax.ShapeDtypeStruct((n_slots, D_MODEL), BF16),
        compiler_params=pltpu.CompilerParams(
            dimension_semantics=("arbitrary",), vmem_limit_bytes=VMEM_LIMIT),
        name="moe_expert",
    )(tb["tile_expert"], tb["n_valid"], xs, w_gate, w_up, w_down)

    return pl.pallas_call(
        functools.partial(_combine_kernel, tm=tm),
        grid_spec=pltpu.PrefetchScalarGridSpec(
            num_scalar_prefetch=2, grid=(n_tiles,),
            in_specs=[tile_row(D_MODEL), tile_row(ROUTER_LANES), lbase_spec, ltri_spec,
                      pl.BlockSpec((1, D_MODEL), lambda i, *_: (0, 0)), hbm],
            out_specs=tile_row(D_MODEL),
            scratch_shapes=[pltpu.VMEM((2, LOCAL_ROWS, D_MODEL), BF16), pltpu.SemaphoreType.DMA((2,))]),
        out_shape=jax.ShapeDtypeStruct((n_tok, D_MODEL), F32),
        compiler_params=pltpu.CompilerParams(
            dimension_semantics=("arbitrary",), vmem_limit_bytes=VMEM_LIMIT),
        name="moe_combine",
    )(tb["chunk_dst"], tb["tot"], x1, route, lbase_rows, ltri, g_final, ys)


def _rope_tables(seq, tm):
    pos = np.arange(seq, dtype=np.float64)
    inv_freq = ROPE_THETA ** (-np.arange(0, HEAD_DIM, 2, dtype=np.float64) / HEAD_DIM)
    ang = pos[:, None] * inv_freq[None, :]
    cos, sin = np.cos(ang), np.sin(ang)
    reps = CB // HEAD_DIM
    cos_t = np.tile(np.concatenate([cos, cos], axis=-1), (1, reps))
    sin_t = np.tile(np.concatenate([-sin, sin], axis=-1), (1, reps))

    def reorder(t, dil):
        return t.reshape(seq // tm, tm // dil, dil, CB).transpose(0, 2, 1, 3).reshape(seq, CB)

    dils = [dil for _, dil in DIL_GROUPS]
    return (jnp.asarray(np.stack([reorder(cos_t, dil) for dil in dils]).astype(np.float32)),
            jnp.asarray(np.stack([reorder(sin_t, dil) for dil in dils]).astype(np.float32)))


def kernel(x, w_in, b_in, sinks, w_proj_a, w_proj_b, w_out, g_mix, g_ffn, w_router_group, b_router_group,
           w_router_expert, b_router_expert, w_exp_gate, w_exp_up, w_exp_down, g_final):
    batch, seq, d = x.shape
    assert d == D_MODEL and w_in.shape[0] == 1, "single-layer kernel"
    n_tok = batch * seq
    x2 = x.reshape(n_tok, d)
    tm_in = 512
    cos_t, sin_t = _rope_tables(seq, tm_in)

    z_tok, z_d1, z_d2 = _in_proj(x2, g_mix[0][None, :], w_in[0].astype(BF16), b_in[0][None, :],
                                 cos_t, sin_t, seq=seq, tm=tm_in)

    a_bases = (0, A_BLOCKS, 2 * A_BLOCKS)
    tok_bases = tuple(ZB_A0 + b for b in a_bases)
    oa, lse = [], []
    for group, (z, bases) in enumerate(((z_tok, tok_bases), (z_d1, a_bases), (z_d2, a_bases))):
        o_g, l_g = _dilated_attention(z, bases, group, batch=batch, seq=seq, lq=512)
        oa.append(o_g)
        lse.append(l_g)
    ob = _swa_attention(z_tok, sinks[0], batch=batch, seq=seq, lq=512)

    pad = ROUTER_LANES - N_EXPERT_GROUPS - N_EXPERTS
    w_router = jnp.concatenate(
        [w_router_group[0], w_router_expert[0], jnp.zeros((d, pad), F32)], axis=1)
    b_router = jnp.concatenate(
        [b_router_group[0], b_router_expert[0], jnp.zeros((pad,), F32)])[None, :]
    x1, h2, route, cnt_tiles = _post_attn(
        oa, lse, ob, z_tok, x2, w_proj_a[0].astype(BF16), w_proj_b[0].astype(BF16), w_out[0].astype(BF16),
        g_ffn[0][None, :], w_router, b_router, tm=MOE_TILE)

    out = _moe(h2, route, cnt_tiles, x1, w_exp_gate[0], w_exp_up[0], w_exp_down[0], g_final[None, :])
    return out.reshape(batch, seq, d)
```

```python
import functools
import math

import jax
import jax.numpy as jnp
import numpy as np
from jax import lax
from jax.experimental import pallas as pl
from jax.experimental.pallas import tpu as pltpu

F32 = jnp.float32
BF16 = jnp.bfloat16

D_MODEL = 1024
HEAD_DIM = 64
HALF = HEAD_DIM // 2
ROPE_THETA = 10000.0
RMS_EPS = 1e-6
LOG2E = math.log2(math.e)
LN2 = math.log(2.0)
Q_SCALE = LOG2E * HEAD_DIM ** -0.5
BLOCK = 128
ATTN_ROWS_PER_STEP = 1024
DIL_GROUPS = ((128, 1), (512, 4), (2048, 16))
N_DIL = len(DIL_GROUPS)
A_GROUP_WIDTH = 512
A_QKV_WIDTH = 3 * N_DIL * A_GROUP_WIDTH
B_Q_HEADS = 16
B_KV_HEADS = 2
B_Q_WIDTH = B_Q_HEADS * HEAD_DIM
B_WINDOW = 128
GATE_WIDTH = 2 * D_MODEL
IN_WIDTH = A_QKV_WIDTH + B_Q_WIDTH + 2 * B_KV_HEADS * HEAD_DIM + GATE_WIDTH
N_EXPERT_GROUPS = 4
EXPERTS_PER_GROUP = 8
N_EXPERTS = N_EXPERT_GROUPS * EXPERTS_PER_GROUP
D_EXPERT = D_MODEL // 4

CB = 256
PAIR = 128
N_IN_BLOCKS = IN_WIDTH // CB
A_BLOCKS = A_GROUP_WIDTH // CB
ZB_GATE = 0
ZB_QB = 8
ZB_KB = 12
ZB_VB = 13
ZB_A0 = 14
N_TOK_BLOCKS = ZB_A0 + 3 * A_BLOCKS
OUT_PERM_BLOCK = 256
LSE_PERM_BLOCK = 128
ROUTER_LANES = 128
EXPERT_LANE0 = N_EXPERT_GROUPS
MOE_TILE = 512
CHUNK = 16
EXP_TILE = 512
LOCAL_ROWS = -(-(2 * MOE_TILE + N_EXPERTS * (CHUNK - 1)) // CB) * CB
LOCAL_CHUNKS = LOCAL_ROWS // CHUNK
ISSUE_UNROLL = 4
WAIT_GROUP = 8
XS_WIDTH = D_MODEL + ROUTER_LANES
META_PIECES = 3
assert META_PIECES * N_EXPERTS <= ROUTER_LANES

VMEM_LIMIT = 56 * 1024 * 1024


def _in_proj_plan():
    plan = []
    for c in range(N_IN_BLOCKS):
        col = c * CB
        if col < A_QKV_WIDTH:
            part, rem = divmod(col, N_DIL * A_GROUP_WIDTH)
            group, blk = divmod(rem // CB, A_BLOCKS)
            kind = ("q", "k", "v")[part]
            dil = DIL_GROUPS[group][1]
            if dil == 1:
                plan.append((0, ZB_A0 + part * A_BLOCKS + blk, kind, 1))
            else:
                plan.append((group, part * A_BLOCKS + blk, kind, dil))
        elif col < A_QKV_WIDTH + B_Q_WIDTH:
            plan.append((0, ZB_QB + (col - A_QKV_WIDTH) // CB, "q", 1))
        elif col < A_QKV_WIDTH + B_Q_WIDTH + CB:
            plan.append((0, -1, "kvb", 1))
        else:
            plan.append((0, ZB_GATE + (col - (A_QKV_WIDTH + B_Q_WIDTH + CB)) // CB, "v", 1))
    return tuple(plan)


def _rms(x, g):
    return x * lax.rsqrt(jnp.mean(x * x, axis=-1, keepdims=True) + RMS_EPS) * g


def _split3(w):
    hi = w.astype(BF16).astype(F32)
    mid = (w - hi).astype(BF16).astype(F32)
    lo = (w - hi - mid).astype(BF16).astype(F32)
    return hi, mid, lo


def _rope(acc, cos, sin_signed, first_half):
    partner = jnp.where(first_half, pltpu.roll(acc, CB - HALF, 1), pltpu.roll(acc, HALF, 1))
    return acc * cos + partner * sin_signed


def _in_perm_block(dil):
    return max(BLOCK, CHUNK * dil)


def _in_proj_kernel(x_ref, g_ref, w_ref, b_ref, cos_ref, sin_ref, perm1_ref, perm2_ref,
                    zt_ref, zd1_ref, zd2_ref, *, plan, tm):
    out_refs = (zt_ref, zd1_ref, zd2_ref)
    perm_refs = (None, perm1_ref, perm2_ref)
    lane = lax.broadcasted_iota(jnp.int32, (tm, CB), 1)
    first_half = (lane % HEAD_DIM) < HALF

    h = _rms(x_ref[...], g_ref[...]).astype(BF16)
    h_by_dil, tables = {}, {}
    for slot, (_, dil) in enumerate(DIL_GROUPS):
        tables[dil] = slot
        if dil == 1:
            h_by_dil[dil] = h
            continue
        blk = _in_perm_block(dil)
        n = blk // dil
        moved = [jnp.dot(perm_refs[slot][...], h[tb * blk:(tb + 1) * blk], preferred_element_type=F32
                         ).astype(BF16) for tb in range(tm // blk)]
        h_by_dil[dil] = jnp.concatenate(
            [part[r * n:(r + 1) * n] for r in range(dil) for part in moved], axis=0)

    for c, (arr, dst, kind, dil) in enumerate(plan):
        cols = slice(c * CB, (c + 1) * CB)
        acc = jnp.dot(h_by_dil[dil], w_ref[:, cols], preferred_element_type=F32) + b_ref[:, cols]
        if kind in ("q", "k", "kvb"):
            slot = tables[dil]
            rot = _rope(acc, cos_ref[slot], sin_ref[slot], first_half)
        if kind == "q":
            val = (rot * Q_SCALE).astype(BF16)
        elif kind == "k":
            val = rot.astype(BF16)
        elif kind == "v":
            val = acc.astype(BF16)
        else:
            r64 = pltpu.roll(rot, HEAD_DIM, 1)
            r128 = pltpu.roll(rot, 2 * HEAD_DIM, 1)
            kdup = jnp.where(lane < HEAD_DIM, rot, jnp.where(lane < 3 * HEAD_DIM, r64, r128))
            a128 = pltpu.roll(acc, 2 * HEAD_DIM, 1)
            a192 = pltpu.roll(acc, 3 * HEAD_DIM, 1)
            vdup = jnp.where(lane < HEAD_DIM, a128, jnp.where(lane < 3 * HEAD_DIM, a192, acc))
            zt_ref[ZB_KB] = kdup.astype(BF16)
            zt_ref[ZB_VB] = vdup.astype(BF16)
            continue
        if dil == 1:
            out_refs[arr][dst] = val
        else:
            n = tm // dil
            for r in range(dil):
                out_refs[arr][dst, :, r * CB:(r + 1) * CB] = val[r * n:(r + 1) * n, :]


def _dilation_perm(block, dil, dtype):
    j = np.arange(block)
    src = (j % (block // dil)) * dil + j // (block // dil)
    return jnp.asarray((src[:, None] == j[None, :]).astype(np.float32), dtype=dtype)


def _in_proj(x2, g_mix, w_in, b_in, cos_t, sin_t, *, seq, tm):
    n_tok = x2.shape[0]
    tiles_per_seq = seq // tm
    const = lambda i: (0, 0)
    table = pl.BlockSpec((N_DIL, tm, CB), lambda i: (0, i % tiles_per_seq, 0))
    d1, d2 = DIL_GROUPS[1][1], DIL_GROUPS[2][1]
    perms = [_dilation_perm(_in_perm_block(d), d, BF16) for d in (d1, d2)]
    perm_spec = lambda d: pl.BlockSpec((_in_perm_block(d),) * 2, const, pipeline_mode=pl.Buffered(1))
    return pl.pallas_call(
        functools.partial(_in_proj_kernel, plan=_in_proj_plan(), tm=tm),
        grid=(n_tok // tm,),
        in_specs=[
            pl.BlockSpec((tm, D_MODEL), lambda i: (i, 0)),
            pl.BlockSpec((1, D_MODEL), const),
            pl.BlockSpec((D_MODEL, IN_WIDTH), const, pipeline_mode=pl.Buffered(1)),
            pl.BlockSpec((1, IN_WIDTH), const),
            table, table,
            perm_spec(d1), perm_spec(d2),
        ],
        out_specs=[
            pl.BlockSpec((N_TOK_BLOCKS, tm, CB), lambda i: (0, i, 0)),
            pl.BlockSpec((3 * A_BLOCKS, tm // d1, d1 * CB), lambda i: (0, i, 0)),
            pl.BlockSpec((3 * A_BLOCKS, tm // d2, d2 * CB), lambda i: (0, i, 0)),
        ],
        out_shape=[
            jax.ShapeDtypeStruct((N_TOK_BLOCKS, n_tok, CB), BF16),
            jax.ShapeDtypeStruct((3 * A_BLOCKS, n_tok // d1, d1 * CB), BF16),
            jax.ShapeDtypeStruct((3 * A_BLOCKS, n_tok // d2, d2 * CB), BF16),
        ],
        compiler_params=pltpu.CompilerParams(
            dimension_semantics=("arbitrary",), vmem_limit_bytes=VMEM_LIMIT),
        name="in_proj",
    )(x2, g_mix, w_in, b_in, cos_t, sin_t, *perms)


def _attn_kernel(*refs, lq, max_dist, kv_shared, has_sink, want_lse, n_axes):
    refs = list(refs)
    sink_ref = refs.pop(0) if has_sink else None
    q_ref, k_ref, v_ref, kp_ref, vp_ref = refs[:5]
    o_ref = refs[5]
    lse_ref = refs[6] if want_lse else None
    vaug_ref = refs[-1]
    n_qblk = q_ref.shape[0]
    n_kv_pairs = vaug_ref.shape[0]
    rows_kv = lq + BLOCK

    row = lax.broadcasted_iota(jnp.int32, (BLOCK, 2 * BLOCK), 0)
    col = lax.broadcasted_iota(jnp.int32, (BLOCK, 2 * BLOCK), 1)
    dist = row - col + BLOCK
    valid = (dist >= 0) & (dist <= max_dist)
    neg_inf = jnp.float32(-jnp.inf)
    bias = jnp.where(valid, 0.0, neg_inf)
    bias_first = jnp.where(valid & (col >= BLOCK), 0.0, neg_inf)
    bias0 = jnp.where(pl.program_id(n_axes - 1) == 0, bias_first, bias)
    lane_lo = lax.broadcasted_iota(jnp.int32, (BLOCK, PAIR), 1) < HEAD_DIM

    def rd(ref, blk, rows, cols):
        return ref[rows, cols] if kv_shared else ref[blk, rows, cols]

    first_step = functools.reduce(lambda a, b: a & b, [pl.program_id(a) == 0 for a in range(n_axes)])

    @pl.when(first_step)
    def _():
        lane = lax.broadcasted_iota(jnp.int32, (rows_kv, PAIR), 1)
        for pair in range(n_kv_pairs):
            vaug_ref[pair, 0, :, PAIR:] = jnp.where(lane < HEAD_DIM, 1.0, 0.0).astype(BF16)
            vaug_ref[pair, 1, :, PAIR:] = jnp.where(lane < HEAD_DIM, 0.0, 1.0).astype(BF16)

    lane_kv = lax.broadcasted_iota(jnp.int32, (rows_kv, PAIR), 1) < HEAD_DIM
    n_col_pairs = q_ref.shape[-1] // PAIR
    for pair in range(n_kv_pairs):
        cols = slice((pair % n_col_pairs) * PAIR, (pair % n_col_pairs + 1) * PAIR)
        everything = slice(None)
        v_all = jnp.concatenate([rd(vp_ref, pair // n_col_pairs, everything, cols),
                                 rd(v_ref, pair // n_col_pairs, everything, cols)], axis=0)
        zero = jnp.zeros_like(v_all)
        vaug_ref[pair, 0, :, :PAIR] = jnp.where(lane_kv, v_all, zero)
        vaug_ref[pair, 1, :, :PAIR] = jnp.where(lane_kv, zero, v_all)

    for ib in range(lq // BLOCK):
        rows = slice(ib * BLOCK, (ib + 1) * BLOCK)
        win = slice(ib * BLOCK, (ib + 2) * BLOCK)
        b_ib = bias0 if ib == 0 else bias
        for blk in range(n_qblk):
            for pp in range(n_col_pairs):
                qcols = slice(pp * PAIR, (pp + 1) * PAIR)
                pair = 0 if kv_shared else blk * n_col_pairs + pp
                kcols = slice(0, PAIR) if kv_shared else qcols
                if ib == 0:
                    k_win = jnp.concatenate([rd(kp_ref, blk, slice(None), kcols),
                                             rd(k_ref, blk, slice(0, BLOCK), kcols)], axis=0)
                else:
                    k_win = rd(k_ref, blk, slice((ib - 1) * BLOCK, (ib + 1) * BLOCK), kcols)
                q_pair = q_ref[blk, rows, qcols]
                ps, ms, sink_terms = [], [], []
                for hh in range(2):
                    q_h = jnp.where(lane_lo == (hh == 0), q_pair, jnp.zeros_like(q_pair))
                    s = lax.dot_general(q_h, k_win, (((1,), (1,)), ((), ())), preferred_element_type=F32)
                    s = s + b_ib
                    m = jnp.max(s, axis=-1, keepdims=True)
                    if has_sink:
                        head = (pl.program_id(1) * n_qblk + blk) * (CB // HEAD_DIM) + pp * 2 + hh
                        sink = sink_ref[head] * LOG2E
                        m = jnp.maximum(m, sink)
                        sink_terms.append(jnp.exp2(sink - m))
                    ps.append(jnp.exp2(s - m).astype(BF16))
                    ms.append(m)
                v_aug = jnp.concatenate([vaug_ref[pair, 0, win, :], vaug_ref[pair, 1, win, :]], axis=0)
                od = jnp.dot(jnp.concatenate(ps, axis=1), v_aug, preferred_element_type=F32)
                den = od[:, PAIR:]
                if has_sink:
                    den = den + jnp.where(lane_lo, sink_terms[0], sink_terms[1])
                o_ref[blk, rows, qcols] = (od[:, :PAIR] * (1.0 / den)).astype(BF16)
                if want_lse:
                    lse_ref[blk, rows, qcols] = (jnp.where(lane_lo, ms[0], ms[1]) + jnp.log2(den)) * LN2


def _dilated_attention(z, bases, group, *, batch, seq, lq):
    window, dil = DIL_GROUPS[group]
    sub_len = seq // dil
    lq = min(lq, sub_len)
    z4 = z.reshape(z.shape[0], batch, sub_len, dil * CB)
    qb, kb, vb = bases
    assert all(base % A_BLOCKS == 0 for base in bases)
    bpq = lq // BLOCK
    slabs = max(1, min(dil, ATTN_ROWS_PER_STEP // lq))
    width = slabs * CB
    cur = lambda base: pl.BlockSpec(
        (A_BLOCKS, None, lq, width), lambda b, r, i: (base // A_BLOCKS, b, i, r))
    prev = lambda base: pl.BlockSpec(
        (A_BLOCKS, None, BLOCK, width), lambda b, r, i: (base // A_BLOCKS, b, jnp.maximum(i * bpq - 1, 0), r))
    out_spec = pl.BlockSpec((A_BLOCKS, None, lq, width), lambda b, r, i: (0, b, i, r))
    o, lse = pl.pallas_call(
        functools.partial(_attn_kernel, lq=lq, max_dist=window // dil, kv_shared=False,
                          has_sink=False, want_lse=True, n_axes=3),
        grid=(batch, dil // slabs, sub_len // lq),
        in_specs=[cur(qb), cur(kb), cur(vb), prev(kb), prev(vb)],
        out_specs=[out_spec, out_spec],
        out_shape=[jax.ShapeDtypeStruct((A_BLOCKS, batch, sub_len, dil * CB), BF16),
                   jax.ShapeDtypeStruct((A_BLOCKS, batch, sub_len, dil * CB), F32)],
        scratch_shapes=[pltpu.VMEM((2 * A_BLOCKS * slabs, 2, lq + BLOCK, 2 * PAIR), BF16)],
        compiler_params=pltpu.CompilerParams(
            dimension_semantics=("arbitrary",) * 3, vmem_limit_bytes=VMEM_LIMIT),
        name=f"dilated_attn_g{group}",
    )(z4, z4, z4, z4, z4)
    rows = batch * sub_len
    return o.reshape(A_BLOCKS, rows, dil * CB), lse.reshape(A_BLOCKS, rows, dil * CB)


def _swa_attention(z_tok, sinks, *, batch, seq, lq):
    z4 = z_tok.reshape(N_TOK_BLOCKS, batch, seq, CB)
    bpq = lq // BLOCK
    n_q_blocks = B_Q_WIDTH // CB
    q_per_kv = n_q_blocks // B_KV_HEADS
    assert ZB_QB % q_per_kv == 0
    q_spec = pl.BlockSpec((q_per_kv, None, lq, CB), lambda b, kvh, i, s: (ZB_QB // q_per_kv + kvh, b, i, 0))
    cur = lambda base: pl.BlockSpec((None, None, lq, PAIR), lambda b, kvh, i, s: (base, b, i, kvh))
    prev = lambda base: pl.BlockSpec(
        (None, None, BLOCK, PAIR), lambda b, kvh, i, s: (base, b, jnp.maximum(i * bpq - 1, 0), kvh))
    o = pl.pallas_call(
        functools.partial(_attn_kernel, lq=lq, max_dist=B_WINDOW - 1, kv_shared=True,
                          has_sink=True, want_lse=False, n_axes=3),
        grid_spec=pltpu.PrefetchScalarGridSpec(
            num_scalar_prefetch=1,
            grid=(batch, B_KV_HEADS, seq // lq),
            in_specs=[q_spec, cur(ZB_KB), cur(ZB_VB), prev(ZB_KB), prev(ZB_VB)],
            out_specs=pl.BlockSpec((q_per_kv, None, lq, CB), lambda b, kvh, i, s: (kvh, b, i, 0)),
            scratch_shapes=[pltpu.VMEM((1, 2, lq + BLOCK, 2 * PAIR), BF16)],
        ),
        out_shape=jax.ShapeDtypeStruct((n_q_blocks, batch, seq, CB), BF16),
        compiler_params=pltpu.CompilerParams(
            dimension_semantics=("arbitrary",) * 3, vmem_limit_bytes=VMEM_LIMIT),
        name="swa_attn",
    )(sinks, z4, z4, z4, z4, z4)
    return o.reshape(n_q_blocks, batch * seq, CB)


def _post_attn_kernel(o0_ref, o1_ref, o2_ref, l0_ref, l1_ref, l2_ref, ob_ref, gate_ref, x_ref,
                      wa_ref, wb_ref, wo_ref, gf_ref, wr_ref, br_ref,
                      po_ref, pl_ref, x1_ref, h2_ref, route_ref, cnt_ref, *, tm):

    def to_token_order(ref, cb, slot, perm_ref, blk):
        dil = DIL_GROUPS[slot + 1][1]
        n = blk // dil
        parts = []
        for tb in range(tm // blk):
            stack = jnp.concatenate(
                [ref[cb, tb * n:(tb + 1) * n, r * CB:(r + 1) * CB] for r in range(dil)], axis=0)
            pieces = (stack,) if stack.dtype == BF16 else _split3(stack)
            moved = [jnp.dot(perm_ref[slot], p.astype(BF16), preferred_element_type=F32) for p in pieces]
            parts.append(functools.reduce(lambda a, b: a + b, moved))
        return jnp.concatenate(parts, axis=0)

    pa = None
    for cb in range(A_BLOCKS):
        l0 = l0_ref[cb]
        l1 = to_token_order(l1_ref, cb, 0, pl_ref, LSE_PERM_BLOCK)
        l2 = to_token_order(l2_ref, cb, 1, pl_ref, LSE_PERM_BLOCK)
        o1 = to_token_order(o1_ref, cb, 0, po_ref, OUT_PERM_BLOCK)
        o2 = to_token_order(o2_ref, cb, 1, po_ref, OUT_PERM_BLOCK)
        mx = jnp.maximum(jnp.maximum(l0, l1), l2)
        e0, e1, e2 = jnp.exp(l0 - mx), jnp.exp(l1 - mx), jnp.exp(l2 - mx)
        inv = 1.0 / (e0 + e1 + e2)
        ya = (e0 * inv) * o0_ref[cb].astype(F32) + (e1 * inv) * o1 + (e2 * inv) * o2
        part = jnp.dot(ya.astype(BF16), wa_ref[cb * CB:(cb + 1) * CB, :], preferred_element_type=F32)
        pa = part if pa is None else pa + part
    pb = None
    for cb in range(B_Q_WIDTH // CB):
        part = jnp.dot(ob_ref[cb], wb_ref[cb * CB:(cb + 1) * CB, :], preferred_element_type=F32)
        pb = part if pb is None else pb + part
    n_gate = D_MODEL // CB
    merged = []
    for cb in range(n_gate):
        cols = slice(cb * CB, (cb + 1) * CB)
        ga = jax.nn.sigmoid(gate_ref[cb].astype(F32))
        gb = jax.nn.sigmoid(gate_ref[n_gate + cb].astype(F32))
        merged.append((ga * pa[:, cols] + gb * pb[:, cols]).astype(BF16))
    merged = jnp.concatenate(merged, axis=1)
    x1 = x_ref[...] + jnp.dot(merged, wo_ref[...], preferred_element_type=F32)
    x1_ref[...] = x1
    h2 = _rms(x1, gf_ref[...])
    h2_ref[...] = h2.astype(BF16)

    h2_hi = h2.astype(BF16)
    h2_lo = (h2 - h2_hi.astype(F32)).astype(BF16)
    wr = wr_ref[...]
    wr_hi = wr.astype(BF16)
    wr_lo = (wr - wr_hi.astype(F32)).astype(BF16)
    hi_terms = jnp.dot(h2_hi, jnp.concatenate([wr_hi, wr_lo], axis=1), preferred_element_type=F32)
    logits = (hi_terms[:, :ROUTER_LANES]
              + (jnp.dot(h2_lo, wr_hi, preferred_element_type=F32) + hi_terms[:, ROUTER_LANES:])) + br_ref[...]
    lane = lax.broadcasted_iota(jnp.int32, logits.shape, 1)
    neg_inf = jnp.float32(-jnp.inf)
    lg = jnp.where(lane < N_EXPERT_GROUPS, logits, neg_inf)
    mg = jnp.max(lg, axis=-1, keepdims=True)
    gsel = jnp.min(jnp.where(lg == mg, lane, ROUTER_LANES), axis=-1, keepdims=True)
    pg_sel = 1.0 / jnp.sum(jnp.exp(lg - mg), axis=-1, keepdims=True)
    expert = lane - EXPERT_LANE0
    in_group = (expert >= 0) & (expert < N_EXPERTS) & ((expert // EXPERTS_PER_GROUP) == gsel)
    le = jnp.where(in_group, logits, neg_inf)
    me = jnp.max(le, axis=-1, keepdims=True)
    ex = jnp.exp(le - me)
    pe = ex / jnp.sum(ex, axis=-1, keepdims=True)
    p1 = jnp.max(pe, axis=-1, keepdims=True)
    i1 = jnp.min(jnp.where(in_group & (pe == p1), lane, ROUTER_LANES), axis=-1, keepdims=True)
    rest = in_group & (lane != i1)
    p2 = jnp.max(jnp.where(rest, pe, -1.0), axis=-1, keepdims=True)
    i2 = jnp.min(jnp.where(rest & (pe == p2), lane, ROUTER_LANES), axis=-1, keepdims=True)
    norm = pg_sel / (p1 + p2)
    route_ref[...] = jnp.where(
        lane == 0, (i1 - EXPERT_LANE0).astype(F32),
        jnp.where(lane == 1, (i2 - EXPERT_LANE0).astype(F32),
                  jnp.where(lane == 2, p1 * norm, jnp.where(lane == 3, p2 * norm, 0.0))))
    hits = jnp.where((lane == i1) | (lane == i2), 1.0, 0.0)
    cnt_ref[0] = jnp.broadcast_to(jnp.sum(hits, axis=0, keepdims=True), cnt_ref.shape[1:])


def _post_attn(oa, lse, ob, z_tok, x2, w_proj_a, w_proj_b, w_out, g_ffn, w_router, b_router, *, tm):
    n_tok = x2.shape[0]
    const2 = lambda i: (0, 0)
    blk = lambda nb: pl.BlockSpec((nb, tm, CB), lambda i: (0, i, 0))
    dil_blk = lambda dil: pl.BlockSpec((A_BLOCKS, tm // dil, dil * CB), lambda i: (0, i, 0))
    row = lambda width: pl.BlockSpec((tm, width), lambda i: (i, 0))
    resident = lambda shape: pl.BlockSpec(shape, const2, pipeline_mode=pl.Buffered(1))
    d1, d2 = DIL_GROUPS[1][1], DIL_GROUPS[2][1]
    perm_o = jnp.stack([_dilation_perm(OUT_PERM_BLOCK, d, BF16).T for d in (d1, d2)])
    perm_l = jnp.stack([_dilation_perm(LSE_PERM_BLOCK, d, BF16).T for d in (d1, d2)])
    return pl.pallas_call(
        functools.partial(_post_attn_kernel, tm=tm),
        grid=(n_tok // tm,),
        in_specs=[blk(A_BLOCKS), dil_blk(d1), dil_blk(d2), blk(A_BLOCKS), dil_blk(d1), dil_blk(d2),
                  blk(B_Q_WIDTH // CB), blk(GATE_WIDTH // CB), row(D_MODEL),
                  resident((A_GROUP_WIDTH, D_MODEL)), resident((B_Q_WIDTH, D_MODEL)),
                  resident((D_MODEL, D_MODEL)), resident((1, D_MODEL)),
                  resident((D_MODEL, ROUTER_LANES)), resident((1, ROUTER_LANES)),
                  pl.BlockSpec((2, OUT_PERM_BLOCK, OUT_PERM_BLOCK), lambda i: (0, 0, 0),
                               pipeline_mode=pl.Buffered(1)),
                  pl.BlockSpec((2, LSE_PERM_BLOCK, LSE_PERM_BLOCK), lambda i: (0, 0, 0),
                               pipeline_mode=pl.Buffered(1))],
        out_specs=[row(D_MODEL), row(D_MODEL), row(ROUTER_LANES),
                   pl.BlockSpec((1, 8, ROUTER_LANES), lambda i: (i, 0, 0))],
        out_shape=[jax.ShapeDtypeStruct((n_tok, D_MODEL), F32),
                   jax.ShapeDtypeStruct((n_tok, D_MODEL), BF16),
                   jax.ShapeDtypeStruct((n_tok, ROUTER_LANES), F32),
                   jax.ShapeDtypeStruct((n_tok // tm, 8, ROUTER_LANES), F32)],
        compiler_params=pltpu.CompilerParams(
            dimension_semantics=("arbitrary",), vmem_limit_bytes=VMEM_LIMIT),
        name="post_attn",
    )(oa[0], oa[1], oa[2], lse[0], lse[1], lse[2], ob, z_tok, x2,
      w_proj_a, w_proj_b, w_out, g_ffn, w_router, b_router, perm_o, perm_l)


def _local_slots(route, lbase_row, ltri):
    lane = lax.broadcasted_iota(jnp.int32, route.shape, 1)
    pick = lambda k: jnp.sum(jnp.where(lane == k, route, 0.0), axis=-1, keepdims=True)
    lanef = lane.astype(F32)
    oh1, oh2 = lanef == pick(0), lanef == pick(1)
    oh = jnp.where(oh1 | oh2, 1.0, 0.0).astype(BF16)
    table = jnp.dot(ltri, oh, preferred_element_type=F32) + lbase_row
    ls1 = jnp.sum(jnp.where(oh1, table, 0.0), axis=-1, keepdims=True)
    ls2 = jnp.sum(jnp.where(oh2, table, 0.0), axis=-1, keepdims=True)
    return ls1, ls2, pick


def _chunk_loop(count, body, unroll=1):
    one = lambda c, carry: (body(c), carry)[1]
    full = 0
    if unroll > 1:
        full = (count // unroll) * unroll

        def group(g, carry):
            for u in range(unroll):
                body(g * unroll + u)
            return carry

        lax.fori_loop(0, count // unroll, group, 0)
    lax.fori_loop(full, count, one, 0)


def _wait_chunks(count, wait_rows):
    _chunk_loop(count // WAIT_GROUP, lambda c: wait_rows(WAIT_GROUP * CHUNK))
    _chunk_loop(count % WAIT_GROUP, lambda c: wait_rows(CHUNK))


def _selection_matrix(slots_a, slots_b, axis, shape):
    span = CB
    local = lax.broadcasted_iota(jnp.int32, tuple(span if d == axis else s for d, s in enumerate(shape)),
                                 axis).astype(F32).astype(BF16)
    one, zero = jnp.ones_like(local), jnp.zeros_like(local)
    blocks = []
    for a in range(shape[axis] // span):
        rel_a = (slots_a - float(a * span)).astype(BF16)
        rel_b = (slots_b - float(a * span)).astype(BF16)
        blocks.append(jnp.where(local == rel_a, one, jnp.where(local == rel_b, one, zero)))
    return jnp.concatenate(blocks, axis=axis)


def _dispatch_kernel(cdst_ref, tot_ref, tail_start_ref, tail_n_ref,
                     h2_ref, route_ref, lbase_ref, ltri_ref, xs_ref, buf_ref, zero_ref, sem_ref, zsem_ref,
                     *, tm):
    i = pl.program_id(0)
    last = pl.num_programs(0) - 1
    slot = i % 2

    def chunk(slot_, src_row, dst_row, rows=CHUNK):
        return pltpu.make_async_copy(buf_ref.at[slot_, pl.ds(src_row, rows), :],
                                     xs_ref.at[pl.ds(dst_row, rows), :], sem_ref.at[slot_])

    def wait_tile(slot_, tile):
        _wait_chunks(tot_ref[tile], lambda rows: chunk(slot_, 0, 0, rows).wait())

    @pl.when(i == 0)
    def _():
        zero_ref[...] = jnp.zeros_like(zero_ref)

        def zero_chunk(row):
            return pltpu.make_async_copy(zero_ref, xs_ref.at[pl.ds(row, CHUNK), :], zsem_ref)

        def per_expert(e, total):
            start = tail_start_ref[e] * CHUNK
            _chunk_loop(tail_n_ref[e], lambda c: zero_chunk(pl.multiple_of(start + c * CHUNK, CHUNK)).start())
            return total + tail_n_ref[e]

        total = lax.fori_loop(0, N_EXPERTS, per_expert, 0)
        _chunk_loop(total, lambda c: zero_chunk(0).wait())

    @pl.when(i >= 2)
    def _():
        wait_tile(slot, i - 2)

    route = route_ref[...]
    ls1, ls2, pick = _local_slots(route, lbase_ref[0, 0:1, :], ltri_ref[...])
    lane = lax.broadcasted_iota(jnp.int32, route.shape, 1)
    ls_t = jnp.where(lane == 0, ls1, jnp.where(lane == 1, ls2, 0.0)).T
    sel = _selection_matrix(ls_t[0:1, :], ls_t[1:2, :], 0, (LOCAL_ROWS, tm))
    lane_expert = (lane // META_PIECES).astype(F32)
    lane_piece = lane % META_PIECES
    by_piece = lambda pieces: jnp.where(lane_piece == 0, pieces[0],
                                        jnp.where(lane_piece == 1, pieces[1], pieces[2]))
    meta = jnp.where(lane_expert == pick(0), by_piece(_split3(pick(2))),
                     jnp.where(lane_expert == pick(1), by_piece(_split3(pick(3))), 0.0))
    buf_ref[slot, :, :D_MODEL] = jnp.dot(sel, h2_ref[...], preferred_element_type=F32).astype(BF16)
    buf_ref[slot, :, D_MODEL:] = jnp.dot(sel, meta.astype(BF16), preferred_element_type=F32).astype(BF16)

    _chunk_loop(tot_ref[i], lambda c: chunk(
        slot, pl.multiple_of(c * CHUNK, CHUNK),
        pl.multiple_of(cdst_ref[i * LOCAL_CHUNKS + c] * CHUNK, CHUNK)).start(), unroll=ISSUE_UNROLL)

    @pl.when(i == last)
    def _():
        @pl.when(i >= 1)
        def _():
            wait_tile(1 - slot, i - 1)
        wait_tile(slot, i)


def _expert_kernel(te_ref, nv_ref, xs_ref, wg_ref, wu_ref, wd_ref, ys_ref, wgb_ref, wub_ref, wdb_ref):
    g = pl.program_id(0)
    e = te_ref[g]

    @pl.when((g == 0) | (te_ref[jnp.maximum(g - 1, 0)] != e))
    def _():
        wgb_ref[...] = wg_ref[...].astype(BF16)
        wub_ref[...] = wu_ref[...].astype(BF16)
        wdb_ref[...] = wd_ref[...].astype(BF16)

    @pl.when(g < nv_ref[0])
    def _():
        x = xs_ref[:, :D_MODEL]
        meta = xs_ref[:, D_MODEL:].astype(F32)
        lane = lax.broadcasted_iota(jnp.int32, meta.shape, 1)
        mine = (lane >= e * META_PIECES) & (lane < (e + 1) * META_PIECES)
        w = jnp.sum(jnp.where(mine, meta, 0.0), axis=-1, keepdims=True)
        hg = jnp.dot(x, wgb_ref[...], preferred_element_type=F32)
        hu = jnp.dot(x, wub_ref[...], preferred_element_type=F32)
        a = (hg * jax.nn.sigmoid(hg)) * hu * w
        ys_ref[...] = jnp.dot(a.astype(BF16), wdb_ref[...], preferred_element_type=F32).astype(BF16)


def _combine_kernel(cdst_ref, tot_ref,
                    x1_ref, route_ref, lbase_ref, ltri_ref, gfin_ref, ys_ref, out_ref, ybuf_ref, sem_ref,
                    *, tm):
    i = pl.program_id(0)
    n_tiles = pl.num_programs(0)
    slot = i % 2

    def chunk(slot_, src_row, dst_row, rows=CHUNK):
        return pltpu.make_async_copy(ys_ref.at[pl.ds(src_row, rows), :],
                                     ybuf_ref.at[slot_, pl.ds(dst_row, rows), :], sem_ref.at[slot_])

    def fetch_tile(tile, slot_):
        _chunk_loop(tot_ref[tile], lambda c: chunk(
            slot_, pl.multiple_of(cdst_ref[tile * LOCAL_CHUNKS + c] * CHUNK, CHUNK),
            pl.multiple_of(c * CHUNK, CHUNK)).start(), unroll=ISSUE_UNROLL)

    @pl.when(i == 0)
    def _():
        ybuf_ref[...] = jnp.zeros_like(ybuf_ref)
        fetch_tile(0, 0)

    @pl.when(i + 1 < n_tiles)
    def _():
        fetch_tile(i + 1, 1 - slot)

    _wait_chunks(tot_ref[i], lambda rows: chunk(slot, 0, 0, rows).wait())

    ls1, ls2, _ = _local_slots(route_ref[...], lbase_ref[0, 0:1, :], ltri_ref[...])
    sel = _selection_matrix(ls1, ls2, 1, (tm, LOCAL_ROWS))
    y = jnp.dot(sel, ybuf_ref[slot], preferred_element_type=F32)
    out_ref[...] = _rms(x1_ref[...] + y, gfin_ref[...])


def _routing_tables(cnt, n_exp_tiles):
    c16 = (cnt + CHUNK - 1) // CHUNK
    lbase = jnp.cumsum(c16, axis=1) - c16
    tile_off = jnp.cumsum(c16, axis=0) - c16
    tot = jnp.sum(c16, axis=0)
    per = EXP_TILE // CHUNK
    region_tiles = (tot + per - 1) // per
    region = region_tiles * per
    base = jnp.cumsum(region) - region
    dst = base[None, :] + tile_off
    tile_end = jnp.cumsum(region_tiles)
    n_valid = tile_end[-1]
    g = jnp.arange(n_exp_tiles, dtype=jnp.int32)
    tile_expert = jnp.sum(tile_end[None, :] <= jnp.minimum(g, n_valid - 1)[:, None], axis=1).astype(jnp.int32)
    c = jnp.arange(LOCAL_CHUNKS, dtype=jnp.int32)
    lend = lbase + c16
    owns = (lbase[:, None, :] <= c[None, :, None]) & (c[None, :, None] < lend[:, None, :])
    chunk_dst = jnp.sum(jnp.where(owns, (dst - lbase)[:, None, :], 0), axis=2) + c[None, :]
    i32 = lambda a: a.astype(jnp.int32).reshape(-1)
    return dict(chunk_dst=i32(chunk_dst), tot=i32(jnp.sum(c16, axis=1)),
                tail_start=i32(base + tot), tail_n=i32(region - tot),
                tile_expert=tile_expert, n_valid=i32(n_valid),
                lbase_rows=(lbase * CHUNK).astype(F32))


def _moe(h2, route, cnt_tiles, x1, w_gate, w_up, w_down, g_final):
    n_tok = h2.shape[0]
    tm = MOE_TILE
    n_tiles = n_tok // tm
    assert cnt_tiles.shape[0] == n_tiles
    worst_rows = 2 * n_tok + n_tiles * N_EXPERTS * (CHUNK - 1) + N_EXPERTS * (EXP_TILE - CHUNK)
    n_exp_tiles = -(-worst_rows // EXP_TILE)
    n_slots = n_exp_tiles * EXP_TILE

    cnt = cnt_tiles[:, 0, EXPERT_LANE0:EXPERT_LANE0 + N_EXPERTS].astype(jnp.int32)
    tb = _routing_tables(cnt, n_exp_tiles)
    lbase_rows = jnp.zeros((n_tiles, 8, ROUTER_LANES), F32).at[:, :, :N_EXPERTS].set(
        tb["lbase_rows"][:, None, :])
    row_id = np.arange(tm)
    ltri = jnp.asarray((row_id[:, None] > row_id[None, :]).astype(np.float32), dtype=BF16)

    tile_row = lambda width: pl.BlockSpec((tm, width), lambda i, *_: (i, 0))
    lbase_spec = pl.BlockSpec((1, 8, ROUTER_LANES), lambda i, *_: (i, 0, 0))
    ltri_spec = pl.BlockSpec((tm, tm), lambda i, *_: (0, 0), pipeline_mode=pl.Buffered(1))
    hbm = pl.BlockSpec(memory_space=pl.ANY)

    xs = pl.pallas_call(
        functools.partial(_dispatch_kernel, tm=tm),
        grid_spec=pltpu.PrefetchScalarGridSpec(
            num_scalar_prefetch=4, grid=(n_tiles,),
            in_specs=[tile_row(D_MODEL), tile_row(ROUTER_LANES), lbase_spec, ltri_spec],
            out_specs=hbm,
            scratch_shapes=[pltpu.VMEM((2, LOCAL_ROWS, XS_WIDTH), BF16), pltpu.VMEM((CHUNK, XS_WIDTH), BF16),
                            pltpu.SemaphoreType.DMA((2,)), pltpu.SemaphoreType.DMA]),
        out_shape=jax.ShapeDtypeStruct((n_slots, XS_WIDTH), BF16),
        compiler_params=pltpu.CompilerParams(
            dimension_semantics=("arbitrary",), vmem_limit_bytes=VMEM_LIMIT),
        name="moe_dispatch",
    )(tb["chunk_dst"], tb["tot"], tb["tail_start"], tb["tail_n"], h2, route, lbase_rows, ltri)

    row_tile = lambda width: pl.BlockSpec(
        (EXP_TILE, width), lambda g, te, nv: (jnp.maximum(jnp.minimum(g, nv[0] - 1), 0), 0))
    ys = pl.pallas_call(
        _expert_kernel,
        grid_spec=pltpu.PrefetchScalarGridSpec(
            num_scalar_prefetch=2, grid=(n_exp_tiles,),
            in_specs=[row_tile(XS_WIDTH),
                      pl.BlockSpec((None, D_MODEL, D_EXPERT), lambda g, te, nv: (te[g], 0, 0)),
                      pl.BlockSpec((None, D_MODEL, D_EXPERT), lambda g, te, nv: (te[g], 0, 0)),
                      pl.BlockSpec((None, D_EXPERT, D_MODEL), lambda g, te, nv: (te[g], 0, 0))],
            out_specs=row_tile(D_MODEL),
            scratch_shapes=[pltpu.VMEM((D_MODEL, D_EXPERT), BF16), pltpu.VMEM((D_MODEL, D_EXPERT), BF16),
                            pltpu.VMEM((D_EXPERT, D_MODEL), BF16)]),
        out_shape=jax.ShapeDtypeStruct((n_slots, D_MODEL), BF16),
        compiler_params=pltpu.CompilerParams(
            dimension_semantics=("arbitrary",), vmem_limit_bytes=VMEM_LIMIT),
        name="moe_expert",
    )(tb["tile_expert"], tb["n_valid"], xs, w_gate, w_up, w_down)

    return pl.pallas_call(
        functools.partial(_combine_kernel, tm=tm),
        grid_spec=pltpu.PrefetchScalarGridSpec(
            num_scalar_prefetch=2, grid=(n_tiles,),
            in_specs=[tile_row(D_MODEL), tile_row(ROUTER_LANES), lbase_spec, ltri_spec,
                      pl.BlockSpec((1, D_MODEL), lambda i, *_: (0, 0)), hbm],
            out_specs=tile_row(D_MODEL),
            scratch_shapes=[pltpu.VMEM((2, LOCAL_ROWS, D_MODEL), BF16), pltpu.SemaphoreType.DMA((2,))]),
        out_shape=jax.ShapeDtypeStruct((n_tok, D_MODEL), F32),
        compiler_params=pltpu.CompilerParams(
            dimension_semantics=("arbitrary",), vmem_limit_bytes=VMEM_LIMIT),
        name="moe_combine",
    )(tb["chunk_dst"], tb["tot"], x1, route, lbase_rows, ltri, g_final, ys)


def _rope_tables(seq, tm):
    pos = np.arange(seq, dtype=np.float64)
    inv_freq = ROPE_THETA ** (-np.arange(0, HEAD_DIM, 2, dtype=np.float64) / HEAD_DIM)
    ang = pos[:, None] * inv_freq[None, :]
    cos, sin = np.cos(ang), np.sin(ang)
    reps = CB // HEAD_DIM
    cos_t = np.tile(np.concatenate([cos, cos], axis=-1), (1, reps))
    sin_t = np.tile(np.concatenate([-sin, sin], axis=-1), (1, reps))

    def reorder(t, dil):
        return t.reshape(seq // tm, tm // dil, dil, CB).transpose(0, 2, 1, 3).reshape(seq, CB)

    dils = [dil for _, dil in DIL_GROUPS]
    return (jnp.asarray(np.stack([reorder(cos_t, dil) for dil in dils]).astype(np.float32)),
            jnp.asarray(np.stack([reorder(sin_t, dil) for dil in dils]).astype(np.float32)))


def kernel(x, w_in, b_in, sinks, w_proj_a, w_proj_b, w_out, g_mix, g_ffn, w_router_group, b_router_group,
           w_router_expert, b_router_expert, w_exp_gate, w_exp_up, w_exp_down, g_final):
    batch, seq, d = x.shape
    assert d == D_MODEL and w_in.shape[0] == 1, "single-layer kernel"
    n_tok = batch * seq
    x2 = x.reshape(n_tok, d)
    tm_in = 512
    cos_t, sin_t = _rope_tables(seq, tm_in)

    z_tok, z_d1, z_d2 = _in_proj(x2, g_mix[0][None, :], w_in[0].astype(BF16), b_in[0][None, :],
                                 cos_t, sin_t, seq=seq, tm=tm_in)

    a_bases = (0, A_BLOCKS, 2 * A_BLOCKS)
    tok_bases = tuple(ZB_A0 + b for b in a_bases)
    oa, lse = [], []
    for group, (z, bases) in enumerate(((z_tok, tok_bases), (z_d1, a_bases), (z_d2, a_bases))):
        o_g, l_g = _dilated_attention(z, bases, group, batch=batch, seq=seq, lq=ATTN_ROWS_PER_STEP)
        oa.append(o_g)
        lse.append(l_g)
    ob = _swa_attention(z_tok, sinks[0], batch=batch, seq=seq, lq=ATTN_ROWS_PER_STEP)

    pad = ROUTER_LANES - N_EXPERT_GROUPS - N_EXPERTS
    w_router = jnp.concatenate(
        [w_router_group[0], w_router_expert[0], jnp.zeros((d, pad), F32)], axis=1)
    b_router = jnp.concatenate(
        [b_router_group[0], b_router_expert[0], jnp.zeros((pad,), F32)])[None, :]
    x1, h2, route, cnt_tiles = _post_attn(
        oa, lse, ob, z_tok, x2, w_proj_a[0].astype(BF16), w_proj_b[0].astype(BF16), w_out[0].astype(BF16),
        g_ffn[0][None, :], w_router, b_router, tm=MOE_TILE)

    out = _moe(h2, route, cnt_tiles, x1, w_exp_gate[0], w_exp_up[0], w_exp_down[0], g_final[None, :])
    return out.reshape(batch, seq, d)
```

```python
import functools
import math

import jax
import jax.numpy as jnp
import numpy as np
from jax import lax
from jax.experimental import pallas as pl
from jax.experimental.pallas import tpu as pltpu

F32 = jnp.float32
BF16 = jnp.bfloat16

D_MODEL = 1024
HEAD_DIM = 64
HALF = HEAD_DIM // 2
ROPE_THETA = 10000.0
RMS_EPS = 1e-6
LOG2E = math.log2(math.e)
LN2 = math.log(2.0)
Q_SCALE = LOG2E * HEAD_DIM ** -0.5
BLOCK = 128
ATTN_ROWS_PER_STEP = 2048
DIL_GROUPS = ((128, 1), (512, 4), (2048, 16))
N_DIL = len(DIL_GROUPS)
A_GROUP_WIDTH = 512
A_QKV_WIDTH = 3 * N_DIL * A_GROUP_WIDTH
B_Q_HEADS = 16
B_KV_HEADS = 2
B_Q_WIDTH = B_Q_HEADS * HEAD_DIM
B_WINDOW = 128
GATE_WIDTH = 2 * D_MODEL
IN_WIDTH = A_QKV_WIDTH + B_Q_WIDTH + 2 * B_KV_HEADS * HEAD_DIM + GATE_WIDTH
N_EXPERT_GROUPS = 4
EXPERTS_PER_GROUP = 8
N_EXPERTS = N_EXPERT_GROUPS * EXPERTS_PER_GROUP
D_EXPERT = D_MODEL // 4

CB = 256
PAIR = 128
N_IN_BLOCKS = IN_WIDTH // CB
A_BLOCKS = A_GROUP_WIDTH // CB
ZB_GATE = 0
ZB_QB = 8
ZB_KB = 12
ZB_VB = 13
ZB_A0 = 14
N_TOK_BLOCKS = ZB_A0 + 3 * A_BLOCKS
OUT_PERM_BLOCK = 256
LSE_PERM_BLOCK = 128
ROUTER_LANES = 128
EXPERT_LANE0 = N_EXPERT_GROUPS
MOE_TILE = 512
CHUNK = 16
EXP_TILE = 512
LOCAL_ROWS = -(-(2 * MOE_TILE + N_EXPERTS * (CHUNK - 1)) // CB) * CB
LOCAL_CHUNKS = LOCAL_ROWS // CHUNK
SEL_BLOCK = CB
ISSUE_UNROLL = 4
WAIT_GROUP = 8
XS_WIDTH = D_MODEL + ROUTER_LANES
META_PIECES = 3
assert META_PIECES * N_EXPERTS <= ROUTER_LANES

VMEM_LIMIT = 56 * 1024 * 1024


def _in_proj_plan():
    plan = []
    for c in range(N_IN_BLOCKS):
        col = c * CB
        if col < A_QKV_WIDTH:
            part, rem = divmod(col, N_DIL * A_GROUP_WIDTH)
            group, blk = divmod(rem // CB, A_BLOCKS)
            kind = ("q", "k", "v")[part]
            dil = DIL_GROUPS[group][1]
            if dil == 1:
                plan.append((0, ZB_A0 + part * A_BLOCKS + blk, kind, 1))
            else:
                plan.append((group, part * A_BLOCKS + blk, kind, dil))
        elif col < A_QKV_WIDTH + B_Q_WIDTH:
            plan.append((0, ZB_QB + (col - A_QKV_WIDTH) // CB, "q", 1))
        elif col < A_QKV_WIDTH + B_Q_WIDTH + CB:
            plan.append((0, -1, "kvb", 1))
        else:
            plan.append((0, ZB_GATE + (col - (A_QKV_WIDTH + B_Q_WIDTH + CB)) // CB, "v", 1))
    return tuple(plan)


def _rms(x, g):
    return x * lax.rsqrt(jnp.mean(x * x, axis=-1, keepdims=True) + RMS_EPS) * g


def _split3(w):
    hi = w.astype(BF16).astype(F32)
    mid = (w - hi).astype(BF16).astype(F32)
    lo = (w - hi - mid).astype(BF16).astype(F32)
    return hi, mid, lo


def _rope(acc, cos, sin_signed, first_half):
    partner = jnp.where(first_half, pltpu.roll(acc, CB - HALF, 1), pltpu.roll(acc, HALF, 1))
    return acc * cos + partner * sin_signed


def _in_perm_block(dil):
    return max(BLOCK, CHUNK * dil)


def _in_proj_kernel(x_ref, g_ref, w_ref, b_ref, cos_ref, sin_ref, perm1_ref, perm2_ref,
                    zt_ref, zd1_ref, zd2_ref, *, plan, tm):
    out_refs = (zt_ref, zd1_ref, zd2_ref)
    perm_refs = (None, perm1_ref, perm2_ref)
    lane = lax.broadcasted_iota(jnp.int32, (tm, CB), 1)
    first_half = (lane % HEAD_DIM) < HALF

    h = _rms(x_ref[...], g_ref[...]).astype(BF16)
    h_by_dil, tables = {}, {}
    for slot, (_, dil) in enumerate(DIL_GROUPS):
        tables[dil] = slot
        if dil == 1:
            h_by_dil[dil] = h
            continue
        blk = _in_perm_block(dil)
        n = blk // dil
        moved = [jnp.dot(perm_refs[slot][...], h[tb * blk:(tb + 1) * blk], preferred_element_type=F32
                         ).astype(BF16) for tb in range(tm // blk)]
        h_by_dil[dil] = jnp.concatenate(
            [part[r * n:(r + 1) * n] for r in range(dil) for part in moved], axis=0)

    for c, (arr, dst, kind, dil) in enumerate(plan):
        cols = slice(c * CB, (c + 1) * CB)
        acc = jnp.dot(h_by_dil[dil], w_ref[:, cols], preferred_element_type=F32) + b_ref[:, cols]
        if kind in ("q", "k", "kvb"):
            slot = tables[dil]
            rot = _rope(acc, cos_ref[slot], sin_ref[slot], first_half)
        if kind == "q":
            val = (rot * Q_SCALE).astype(BF16)
        elif kind == "k":
            val = rot.astype(BF16)
        elif kind == "v":
            val = acc.astype(BF16)
        else:
            r64 = pltpu.roll(rot, HEAD_DIM, 1)
            r128 = pltpu.roll(rot, 2 * HEAD_DIM, 1)
            kdup = jnp.where(lane < HEAD_DIM, rot, jnp.where(lane < 3 * HEAD_DIM, r64, r128))
            a128 = pltpu.roll(acc, 2 * HEAD_DIM, 1)
            a192 = pltpu.roll(acc, 3 * HEAD_DIM, 1)
            vdup = jnp.where(lane < HEAD_DIM, a128, jnp.where(lane < 3 * HEAD_DIM, a192, acc))
            zt_ref[ZB_KB] = kdup.astype(BF16)
            zt_ref[ZB_VB] = vdup.astype(BF16)
            continue
        if dil == 1:
            out_refs[arr][dst] = val
        else:
            n = tm // dil
            for r in range(dil):
                out_refs[arr][dst, :, r * CB:(r + 1) * CB] = val[r * n:(r + 1) * n, :]


def _dilation_perm(block, dil, dtype):
    j = np.arange(block)
    src = (j % (block // dil)) * dil + j // (block // dil)
    return jnp.asarray((src[:, None] == j[None, :]).astype(np.float32), dtype=dtype)


def _in_proj(x2, g_mix, w_in, b_in, cos_t, sin_t, *, seq, tm):
    n_tok = x2.shape[0]
    tiles_per_seq = seq // tm
    const = lambda i: (0, 0)
    table = pl.BlockSpec((N_DIL, tm, CB), lambda i: (0, i % tiles_per_seq, 0))
    d1, d2 = DIL_GROUPS[1][1], DIL_GROUPS[2][1]
    perms = [_dilation_perm(_in_perm_block(d), d, BF16) for d in (d1, d2)]
    perm_spec = lambda d: pl.BlockSpec((_in_perm_block(d),) * 2, const, pipeline_mode=pl.Buffered(1))
    return pl.pallas_call(
        functools.partial(_in_proj_kernel, plan=_in_proj_plan(), tm=tm),
        grid=(n_tok // tm,),
        in_specs=[
            pl.BlockSpec((tm, D_MODEL), lambda i: (i, 0)),
            pl.BlockSpec((1, D_MODEL), const),
            pl.BlockSpec((D_MODEL, IN_WIDTH), const, pipeline_mode=pl.Buffered(1)),
            pl.BlockSpec((1, IN_WIDTH), const),
            table, table,
            perm_spec(d1), perm_spec(d2),
        ],
        out_specs=[
            pl.BlockSpec((N_TOK_BLOCKS, tm, CB), lambda i: (0, i, 0)),
            pl.BlockSpec((3 * A_BLOCKS, tm // d1, d1 * CB), lambda i: (0, i, 0)),
            pl.BlockSpec((3 * A_BLOCKS, tm // d2, d2 * CB), lambda i: (0, i, 0)),
        ],
        out_shape=[
            jax.ShapeDtypeStruct((N_TOK_BLOCKS, n_tok, CB), BF16),
            jax.ShapeDtypeStruct((3 * A_BLOCKS, n_tok // d1, d1 * CB), BF16),
            jax.ShapeDtypeStruct((3 * A_BLOCKS, n_tok // d2, d2 * CB), BF16),
        ],
        compiler_params=pltpu.CompilerParams(
            dimension_semantics=("arbitrary",), vmem_limit_bytes=VMEM_LIMIT),
        name="in_proj",
    )(x2, g_mix, w_in, b_in, cos_t, sin_t, *perms)


def _attn_kernel(*refs, lq, max_dist, kv_shared, has_sink, want_lse, n_axes):
    refs = list(refs)
    sink_ref = refs.pop(0) if has_sink else None
    q_ref, k_ref, v_ref, kp_ref, vp_ref = refs[:5]
    o_ref = refs[5]
    lse_ref = refs[6] if want_lse else None
    vaug_ref = refs[-1]
    n_qblk = q_ref.shape[0]
    n_kv_pairs = vaug_ref.shape[0]
    rows_kv = lq + BLOCK

    row = lax.broadcasted_iota(jnp.int32, (BLOCK, 2 * BLOCK), 0)
    col = lax.broadcasted_iota(jnp.int32, (BLOCK, 2 * BLOCK), 1)
    dist = row - col + BLOCK
    valid = (dist >= 0) & (dist <= max_dist)
    neg_inf = jnp.float32(-jnp.inf)
    bias = jnp.where(valid, 0.0, neg_inf)
    bias_first = jnp.where(valid & (col >= BLOCK), 0.0, neg_inf)
    bias0 = jnp.where(pl.program_id(n_axes - 1) == 0, bias_first, bias)
    lane_lo = lax.broadcasted_iota(jnp.int32, (BLOCK, PAIR), 1) < HEAD_DIM

    def rd(ref, blk, rows, cols):
        return ref[rows, cols] if kv_shared else ref[blk, rows, cols]

    first_step = functools.reduce(lambda a, b: a & b, [pl.program_id(a) == 0 for a in range(n_axes)])

    @pl.when(first_step)
    def _():
        lane = lax.broadcasted_iota(jnp.int32, (rows_kv, PAIR), 1)
        for pair in range(n_kv_pairs):
            vaug_ref[pair, 0, :, PAIR:] = jnp.where(lane < HEAD_DIM, 1.0, 0.0).astype(BF16)
            vaug_ref[pair, 1, :, PAIR:] = jnp.where(lane < HEAD_DIM, 0.0, 1.0).astype(BF16)

    lane_kv = lax.broadcasted_iota(jnp.int32, (rows_kv, PAIR), 1) < HEAD_DIM
    n_col_pairs = q_ref.shape[-1] // PAIR
    for pair in range(n_kv_pairs):
        cols = slice((pair % n_col_pairs) * PAIR, (pair % n_col_pairs + 1) * PAIR)
        everything = slice(None)
        v_all = jnp.concatenate([rd(vp_ref, pair // n_col_pairs, everything, cols),
                                 rd(v_ref, pair // n_col_pairs, everything, cols)], axis=0)
        zero = jnp.zeros_like(v_all)
        vaug_ref[pair, 0, :, :PAIR] = jnp.where(lane_kv, v_all, zero)
        vaug_ref[pair, 1, :, :PAIR] = jnp.where(lane_kv, zero, v_all)

    for ib in range(lq // BLOCK):
        rows = slice(ib * BLOCK, (ib + 1) * BLOCK)
        win = slice(ib * BLOCK, (ib + 2) * BLOCK)
        b_ib = bias0 if ib == 0 else bias
        for blk in range(n_qblk):
            for pp in range(n_col_pairs):
                qcols = slice(pp * PAIR, (pp + 1) * PAIR)
                pair = 0 if kv_shared else blk * n_col_pairs + pp
                kcols = slice(0, PAIR) if kv_shared else qcols
                if ib == 0:
                    k_win = jnp.concatenate([rd(kp_ref, blk, slice(None), kcols),
                                             rd(k_ref, blk, slice(0, BLOCK), kcols)], axis=0)
                else:
                    k_win = rd(k_ref, blk, slice((ib - 1) * BLOCK, (ib + 1) * BLOCK), kcols)
                q_pair = q_ref[blk, rows, qcols]
                ps, ms, sink_terms = [], [], []
                for hh in range(2):
                    q_h = jnp.where(lane_lo == (hh == 0), q_pair, jnp.zeros_like(q_pair))
                    s = lax.dot_general(q_h, k_win, (((1,), (1,)), ((), ())), preferred_element_type=F32)
                    s = s + b_ib
                    m = jnp.max(s, axis=-1, keepdims=True)
                    if has_sink:
                        head = (pl.program_id(1) * n_qblk + blk) * (CB // HEAD_DIM) + pp * 2 + hh
                        sink = sink_ref[head] * LOG2E
                        m = jnp.maximum(m, sink)
                        sink_terms.append(jnp.exp2(sink - m))
                    ps.append(jnp.exp2(s - m).astype(BF16))
                    ms.append(m)
                v_aug = jnp.concatenate([vaug_ref[pair, 0, win, :], vaug_ref[pair, 1, win, :]], axis=0)
                od = jnp.dot(jnp.concatenate(ps, axis=1), v_aug, preferred_element_type=F32)
                den = od[:, PAIR:]
                if has_sink:
                    den = den + jnp.where(lane_lo, sink_terms[0], sink_terms[1])
                o_ref[blk, rows, qcols] = (od[:, :PAIR] * (1.0 / den)).astype(BF16)
                if want_lse:
                    lse_ref[blk, rows, qcols] = (jnp.where(lane_lo, ms[0], ms[1]) + jnp.log2(den)) * LN2


def _dilated_attention(z, bases, group, *, batch, seq, lq):
    window, dil = DIL_GROUPS[group]
    sub_len = seq // dil
    lq = min(lq, sub_len)
    z4 = z.reshape(z.shape[0], batch, sub_len, dil * CB)
    qb, kb, vb = bases
    assert all(base % A_BLOCKS == 0 for base in bases)
    bpq = lq // BLOCK
    slabs = max(1, min(dil, ATTN_ROWS_PER_STEP // lq))
    width = slabs * CB
    cur = lambda base: pl.BlockSpec(
        (A_BLOCKS, None, lq, width), lambda b, r, i: (base // A_BLOCKS, b, i, r))
    prev = lambda base: pl.BlockSpec(
        (A_BLOCKS, None, BLOCK, width), lambda b, r, i: (base // A_BLOCKS, b, jnp.maximum(i * bpq - 1, 0), r))
    out_spec = pl.BlockSpec((A_BLOCKS, None, lq, width), lambda b, r, i: (0, b, i, r))
    o, lse = pl.pallas_call(
        functools.partial(_attn_kernel, lq=lq, max_dist=window // dil, kv_shared=False,
                          has_sink=False, want_lse=True, n_axes=3),
        grid=(batch, dil // slabs, sub_len // lq),
        in_specs=[cur(qb), cur(kb), cur(vb), prev(kb), prev(vb)],
        out_specs=[out_spec, out_spec],
        out_shape=[jax.ShapeDtypeStruct((A_BLOCKS, batch, sub_len, dil * CB), BF16),
                   jax.ShapeDtypeStruct((A_BLOCKS, batch, sub_len, dil * CB), F32)],
        scratch_shapes=[pltpu.VMEM((2 * A_BLOCKS * slabs, 2, lq + BLOCK, 2 * PAIR), BF16)],
        compiler_params=pltpu.CompilerParams(
            dimension_semantics=("arbitrary",) * 3, vmem_limit_bytes=VMEM_LIMIT),
        name=f"dilated_attn_g{group}",
    )(z4, z4, z4, z4, z4)
    rows = batch * sub_len
    return o.reshape(A_BLOCKS, rows, dil * CB), lse.reshape(A_BLOCKS, rows, dil * CB)


def _swa_attention(z_tok, sinks, *, batch, seq, lq):
    z4 = z_tok.reshape(N_TOK_BLOCKS, batch, seq, CB)
    bpq = lq // BLOCK
    n_q_blocks = B_Q_WIDTH // CB
    q_per_kv = n_q_blocks // B_KV_HEADS
    assert ZB_QB % q_per_kv == 0
    q_spec = pl.BlockSpec((q_per_kv, None, lq, CB), lambda b, kvh, i, s: (ZB_QB // q_per_kv + kvh, b, i, 0))
    cur = lambda base: pl.BlockSpec((None, None, lq, PAIR), lambda b, kvh, i, s: (base, b, i, kvh))
    prev = lambda base: pl.BlockSpec(
        (None, None, BLOCK, PAIR), lambda b, kvh, i, s: (base, b, jnp.maximum(i * bpq - 1, 0), kvh))
    o = pl.pallas_call(
        functools.partial(_attn_kernel, lq=lq, max_dist=B_WINDOW - 1, kv_shared=True,
                          has_sink=True, want_lse=False, n_axes=3),
        grid_spec=pltpu.PrefetchScalarGridSpec(
            num_scalar_prefetch=1,
            grid=(batch, B_KV_HEADS, seq // lq),
            in_specs=[q_spec, cur(ZB_KB), cur(ZB_VB), prev(ZB_KB), prev(ZB_VB)],
            out_specs=pl.BlockSpec((q_per_kv, None, lq, CB), lambda b, kvh, i, s: (kvh, b, i, 0)),
            scratch_shapes=[pltpu.VMEM((1, 2, lq + BLOCK, 2 * PAIR), BF16)],
        ),
        out_shape=jax.ShapeDtypeStruct((n_q_blocks, batch, seq, CB), BF16),
        compiler_params=pltpu.CompilerParams(
            dimension_semantics=("arbitrary",) * 3, vmem_limit_bytes=VMEM_LIMIT),
        name="swa_attn",
    )(sinks, z4, z4, z4, z4, z4)
    return o.reshape(n_q_blocks, batch * seq, CB)


def _post_attn_kernel(o0_ref, o1_ref, o2_ref, l0_ref, l1_ref, l2_ref, ob_ref, gate_ref, x_ref,
                      wa_ref, wb_ref, wo_ref, gf_ref, wr_ref, br_ref,
                      po_ref, pl_ref, x1_ref, h2_ref, route_ref, cnt_ref, *, tm):

    def to_token_order(ref, cb, slot, perm_ref, blk):
        dil = DIL_GROUPS[slot + 1][1]
        n = blk // dil
        parts = []
        for tb in range(tm // blk):
            stack = jnp.concatenate(
                [ref[cb, tb * n:(tb + 1) * n, r * CB:(r + 1) * CB] for r in range(dil)], axis=0)
            pieces = (stack,) if stack.dtype == BF16 else _split3(stack)
            moved = [jnp.dot(perm_ref[slot], p.astype(BF16), preferred_element_type=F32) for p in pieces]
            parts.append(functools.reduce(lambda a, b: a + b, moved))
        return jnp.concatenate(parts, axis=0)

    pa = None
    for cb in range(A_BLOCKS):
        l0 = l0_ref[cb]
        l1 = to_token_order(l1_ref, cb, 0, pl_ref, LSE_PERM_BLOCK)
        l2 = to_token_order(l2_ref, cb, 1, pl_ref, LSE_PERM_BLOCK)
        o1 = to_token_order(o1_ref, cb, 0, po_ref, OUT_PERM_BLOCK)
        o2 = to_token_order(o2_ref, cb, 1, po_ref, OUT_PERM_BLOCK)
        mx = jnp.maximum(jnp.maximum(l0, l1), l2)
        e0, e1, e2 = jnp.exp(l0 - mx), jnp.exp(l1 - mx), jnp.exp(l2 - mx)
        inv = 1.0 / (e0 + e1 + e2)
        ya = (e0 * inv) * o0_ref[cb].astype(F32) + (e1 * inv) * o1 + (e2 * inv) * o2
        part = jnp.dot(ya.astype(BF16), wa_ref[cb * CB:(cb + 1) * CB, :], preferred_element_type=F32)
        pa = part if pa is None else pa + part
    pb = None
    for cb in range(B_Q_WIDTH // CB):
        part = jnp.dot(ob_ref[cb], wb_ref[cb * CB:(cb + 1) * CB, :], preferred_element_type=F32)
        pb = part if pb is None else pb + part
    n_gate = D_MODEL // CB
    merged = []
    for cb in range(n_gate):
        cols = slice(cb * CB, (cb + 1) * CB)
        ga = jax.nn.sigmoid(gate_ref[cb].astype(F32))
        gb = jax.nn.sigmoid(gate_ref[n_gate + cb].astype(F32))
        merged.append((ga * pa[:, cols] + gb * pb[:, cols]).astype(BF16))
    merged = jnp.concatenate(merged, axis=1)
    x1 = x_ref[...] + jnp.dot(merged, wo_ref[...], preferred_element_type=F32)
    x1_ref[...] = x1
    h2 = _rms(x1, gf_ref[...])
    h2_ref[...] = h2.astype(BF16)

    h2_hi = h2.astype(BF16)
    h2_lo = (h2 - h2_hi.astype(F32)).astype(BF16)
    wr = wr_ref[...]
    wr_hi = wr.astype(BF16)
    wr_lo = (wr - wr_hi.astype(F32)).astype(BF16)
    hi_terms = jnp.dot(h2_hi, jnp.concatenate([wr_hi, wr_lo], axis=1), preferred_element_type=F32)
    logits = (hi_terms[:, :ROUTER_LANES]
              + (jnp.dot(h2_lo, wr_hi, preferred_element_type=F32) + hi_terms[:, ROUTER_LANES:])) + br_ref[...]
    lane = lax.broadcasted_iota(jnp.int32, logits.shape, 1)
    neg_inf = jnp.float32(-jnp.inf)
    lg = jnp.where(lane < N_EXPERT_GROUPS, logits, neg_inf)
    mg = jnp.max(lg, axis=-1, keepdims=True)
    gsel = jnp.min(jnp.where(lg == mg, lane, ROUTER_LANES), axis=-1, keepdims=True)
    pg_sel = 1.0 / jnp.sum(jnp.exp(lg - mg), axis=-1, keepdims=True)
    expert = lane - EXPERT_LANE0
    in_group = (expert >= 0) & (expert < N_EXPERTS) & ((expert // EXPERTS_PER_GROUP) == gsel)
    le = jnp.where(in_group, logits, neg_inf)
    me = jnp.max(le, axis=-1, keepdims=True)
    ex = jnp.exp(le - me)
    pe = ex / jnp.sum(ex, axis=-1, keepdims=True)
    p1 = jnp.max(pe, axis=-1, keepdims=True)
    i1 = jnp.min(jnp.where(in_group & (pe == p1), lane, ROUTER_LANES), axis=-1, keepdims=True)
    rest = in_group & (lane != i1)
    p2 = jnp.max(jnp.where(rest, pe, -1.0), axis=-1, keepdims=True)
    i2 = jnp.min(jnp.where(rest & (pe == p2), lane, ROUTER_LANES), axis=-1, keepdims=True)
    norm = pg_sel / (p1 + p2)
    route_ref[...] = jnp.where(
        lane == 0, (i1 - EXPERT_LANE0).astype(F32),
        jnp.where(lane == 1, (i2 - EXPERT_LANE0).astype(F32),
                  jnp.where(lane == 2, p1 * norm, jnp.where(lane == 3, p2 * norm, 0.0))))
    hits = jnp.where((lane == i1) | (lane == i2), 1.0, 0.0)
    cnt_ref[0] = jnp.broadcast_to(jnp.sum(hits, axis=0, keepdims=True), cnt_ref.shape[1:])


def _post_attn(oa, lse, ob, z_tok, x2, w_proj_a, w_proj_b, w_out, g_ffn, w_router, b_router, *, tm):
    n_tok = x2.shape[0]
    const2 = lambda i: (0, 0)
    blk = lambda nb: pl.BlockSpec((nb, tm, CB), lambda i: (0, i, 0))
    dil_blk = lambda dil: pl.BlockSpec((A_BLOCKS, tm // dil, dil * CB), lambda i: (0, i, 0))
    row = lambda width: pl.BlockSpec((tm, width), lambda i: (i, 0))
    resident = lambda shape: pl.BlockSpec(shape, const2, pipeline_mode=pl.Buffered(1))
    d1, d2 = DIL_GROUPS[1][1], DIL_GROUPS[2][1]
    perm_o = jnp.stack([_dilation_perm(OUT_PERM_BLOCK, d, BF16).T for d in (d1, d2)])
    perm_l = jnp.stack([_dilation_perm(LSE_PERM_BLOCK, d, BF16).T for d in (d1, d2)])
    return pl.pallas_call(
        functools.partial(_post_attn_kernel, tm=tm),
        grid=(n_tok // tm,),
        in_specs=[blk(A_BLOCKS), dil_blk(d1), dil_blk(d2), blk(A_BLOCKS), dil_blk(d1), dil_blk(d2),
                  blk(B_Q_WIDTH // CB), blk(GATE_WIDTH // CB), row(D_MODEL),
                  resident((A_GROUP_WIDTH, D_MODEL)), resident((B_Q_WIDTH, D_MODEL)),
                  resident((D_MODEL, D_MODEL)), resident((1, D_MODEL)),
                  resident((D_MODEL, ROUTER_LANES)), resident((1, ROUTER_LANES)),
                  pl.BlockSpec((2, OUT_PERM_BLOCK, OUT_PERM_BLOCK), lambda i: (0, 0, 0),
                               pipeline_mode=pl.Buffered(1)),
                  pl.BlockSpec((2, LSE_PERM_BLOCK, LSE_PERM_BLOCK), lambda i: (0, 0, 0),
                               pipeline_mode=pl.Buffered(1))],
        out_specs=[row(D_MODEL), row(D_MODEL), row(ROUTER_LANES),
                   pl.BlockSpec((1, 8, ROUTER_LANES), lambda i: (i, 0, 0))],
        out_shape=[jax.ShapeDtypeStruct((n_tok, D_MODEL), F32),
                   jax.ShapeDtypeStruct((n_tok, D_MODEL), BF16),
                   jax.ShapeDtypeStruct((n_tok, ROUTER_LANES), F32),
                   jax.ShapeDtypeStruct((n_tok // tm, 8, ROUTER_LANES), F32)],
        compiler_params=pltpu.CompilerParams(
            dimension_semantics=("arbitrary",), vmem_limit_bytes=VMEM_LIMIT),
        name="post_attn",
    )(oa[0], oa[1], oa[2], lse[0], lse[1], lse[2], ob, z_tok, x2,
      w_proj_a, w_proj_b, w_out, g_ffn, w_router, b_router, perm_o, perm_l)


def _local_slots(route, lbase_row, ltri):
    lane = lax.broadcasted_iota(jnp.int32, route.shape, 1)
    pick = lambda k: jnp.sum(jnp.where(lane == k, route, 0.0), axis=-1, keepdims=True)
    lanef = lane.astype(F32)
    oh1, oh2 = lanef == pick(0), lanef == pick(1)
    oh = jnp.where(oh1 | oh2, 1.0, 0.0).astype(BF16)
    table = jnp.dot(ltri, oh, preferred_element_type=F32) + lbase_row
    ls1 = jnp.sum(jnp.where(oh1, table, 0.0), axis=-1, keepdims=True)
    ls2 = jnp.sum(jnp.where(oh2, table, 0.0), axis=-1, keepdims=True)
    return ls1, ls2, pick


def _chunk_loop(count, body, unroll=1):
    one = lambda c, carry: (body(c), carry)[1]
    full = 0
    if unroll > 1:
        full = (count // unroll) * unroll

        def group(g, carry):
            for u in range(unroll):
                body(g * unroll + u)
            return carry

        lax.fori_loop(0, count // unroll, group, 0)
    lax.fori_loop(full, count, one, 0)


def _wait_chunks(count, wait_rows):
    _chunk_loop(count // WAIT_GROUP, lambda c: wait_rows(WAIT_GROUP * CHUNK))
    _chunk_loop(count % WAIT_GROUP, lambda c: wait_rows(CHUNK))


def _selection_blocks(slots_a, slots_b, axis, other):
    shape = (SEL_BLOCK, other) if axis == 0 else (other, SEL_BLOCK)
    local = lax.broadcasted_iota(jnp.int32, shape, axis).astype(F32).astype(BF16)
    one, zero = jnp.ones_like(local), jnp.zeros_like(local)

    def block(a):
        rel_a = (slots_a - float(a * SEL_BLOCK)).astype(BF16)
        rel_b = (slots_b - float(a * SEL_BLOCK)).astype(BF16)
        return jnp.where(local == rel_a, one, jnp.where(local == rel_b, one, zero))

    return block


def _for_used_blocks(used_rows, body):
    always = 2 * MOE_TILE // SEL_BLOCK
    body(0, always)
    for a in range(always, LOCAL_ROWS // SEL_BLOCK):
        pl.when(used_rows > a * SEL_BLOCK)(functools.partial(body, a, a + 1))


def _dispatch_kernel(cdst_ref, tot_ref, tail_start_ref, tail_n_ref,
                     h2_ref, route_ref, lbase_ref, ltri_ref, xs_ref, buf_ref, zero_ref, sem_ref, zsem_ref,
                     *, tm):
    i = pl.program_id(0)
    last = pl.num_programs(0) - 1
    slot = i % 2

    def chunk(slot_, src_row, dst_row, rows=CHUNK):
        return pltpu.make_async_copy(buf_ref.at[slot_, pl.ds(src_row, rows), :],
                                     xs_ref.at[pl.ds(dst_row, rows), :], sem_ref.at[slot_])

    def wait_tile(slot_, tile):
        _wait_chunks(tot_ref[tile], lambda rows: chunk(slot_, 0, 0, rows).wait())

    @pl.when(i == 0)
    def _():
        zero_ref[...] = jnp.zeros_like(zero_ref)

        def zero_chunk(row):
            return pltpu.make_async_copy(zero_ref, xs_ref.at[pl.ds(row, CHUNK), :], zsem_ref)

        def per_expert(e, total):
            start = tail_start_ref[e] * CHUNK
            _chunk_loop(tail_n_ref[e], lambda c: zero_chunk(pl.multiple_of(start + c * CHUNK, CHUNK)).start())
            return total + tail_n_ref[e]

        total = lax.fori_loop(0, N_EXPERTS, per_expert, 0)
        _chunk_loop(total, lambda c: zero_chunk(0).wait())

    @pl.when(i >= 2)
    def _():
        wait_tile(slot, i - 2)

    route = route_ref[...]
    ls1, ls2, pick = _local_slots(route, lbase_ref[0, 0:1, :], ltri_ref[...])
    lane = lax.broadcasted_iota(jnp.int32, route.shape, 1)
    ls_t = jnp.where(lane == 0, ls1, jnp.where(lane == 1, ls2, 0.0)).T
    sel_block = _selection_blocks(ls_t[0:1, :], ls_t[1:2, :], 0, tm)
    lane_expert = (lane // META_PIECES).astype(F32)
    lane_piece = lane % META_PIECES
    by_piece = lambda pieces: jnp.where(lane_piece == 0, pieces[0],
                                        jnp.where(lane_piece == 1, pieces[1], pieces[2]))
    meta = jnp.where(lane_expert == pick(0), by_piece(_split3(pick(2))),
                     jnp.where(lane_expert == pick(1), by_piece(_split3(pick(3))), 0.0))
    meta = meta.astype(BF16)

    def sort_block(a0, a1):
        rows = slice(a0 * SEL_BLOCK, a1 * SEL_BLOCK)
        sel = jnp.concatenate([sel_block(a) for a in range(a0, a1)], axis=0)
        buf_ref[slot, rows, :D_MODEL] = jnp.dot(sel, h2_ref[...], preferred_element_type=F32).astype(BF16)
        buf_ref[slot, rows, D_MODEL:] = jnp.dot(sel, meta, preferred_element_type=F32).astype(BF16)

    _for_used_blocks(tot_ref[i] * CHUNK, sort_block)

    _chunk_loop(tot_ref[i], lambda c: chunk(
        slot, pl.multiple_of(c * CHUNK, CHUNK),
        pl.multiple_of(cdst_ref[i * LOCAL_CHUNKS + c] * CHUNK, CHUNK)).start(), unroll=ISSUE_UNROLL)

    @pl.when(i == last)
    def _():
        @pl.when(i >= 1)
        def _():
            wait_tile(1 - slot, i - 1)
        wait_tile(slot, i)


def _expert_kernel(te_ref, nv_ref, xs_ref, wg_ref, wu_ref, wd_ref, ys_ref, wgb_ref, wub_ref, wdb_ref):
    g = pl.program_id(0)
    e = te_ref[g]

    @pl.when((g == 0) | (te_ref[jnp.maximum(g - 1, 0)] != e))
    def _():
        wgb_ref[...] = wg_ref[...].astype(BF16)
        wub_ref[...] = wu_ref[...].astype(BF16)
        wdb_ref[...] = wd_ref[...].astype(BF16)

    @pl.when(g < nv_ref[0])
    def _():
        x = xs_ref[:, :D_MODEL]
        meta = xs_ref[:, D_MODEL:].astype(F32)
        lane = lax.broadcasted_iota(jnp.int32, meta.shape, 1)
        mine = (lane >= e * META_PIECES) & (lane < (e + 1) * META_PIECES)
        w = jnp.sum(jnp.where(mine, meta, 0.0), axis=-1, keepdims=True)
        hg = jnp.dot(x, wgb_ref[...], preferred_element_type=F32)
        hu = jnp.dot(x, wub_ref[...], preferred_element_type=F32)
        a = (hg * jax.nn.sigmoid(hg)) * hu * w
        ys_ref[...] = jnp.dot(a.astype(BF16), wdb_ref[...], preferred_element_type=F32).astype(BF16)


def _combine_kernel(cdst_ref, tot_ref,
                    x1_ref, route_ref, lbase_ref, ltri_ref, gfin_ref, ys_ref, out_ref, ybuf_ref, sem_ref,
                    *, tm):
    i = pl.program_id(0)
    n_tiles = pl.num_programs(0)
    slot = i % 2

    def chunk(slot_, src_row, dst_row, rows=CHUNK):
        return pltpu.make_async_copy(ys_ref.at[pl.ds(src_row, rows), :],
                                     ybuf_ref.at[slot_, pl.ds(dst_row, rows), :], sem_ref.at[slot_])

    def fetch_tile(tile, slot_):
        _chunk_loop(tot_ref[tile], lambda c: chunk(
            slot_, pl.multiple_of(cdst_ref[tile * LOCAL_CHUNKS + c] * CHUNK, CHUNK),
            pl.multiple_of(c * CHUNK, CHUNK)).start(), unroll=ISSUE_UNROLL)

    @pl.when(i == 0)
    def _():
        ybuf_ref[...] = jnp.zeros_like(ybuf_ref)
        fetch_tile(0, 0)

    @pl.when(i + 1 < n_tiles)
    def _():
        fetch_tile(i + 1, 1 - slot)

    _wait_chunks(tot_ref[i], lambda rows: chunk(slot, 0, 0, rows).wait())

    ls1, ls2, _ = _local_slots(route_ref[...], lbase_ref[0, 0:1, :], ltri_ref[...])
    sel_block = _selection_blocks(ls1, ls2, 1, tm)
    out_ref[...] = x1_ref[...]

    def add_block(a0, a1):
        rows = slice(a0 * SEL_BLOCK, a1 * SEL_BLOCK)
        sel = jnp.concatenate([sel_block(a) for a in range(a0, a1)], axis=1)
        out_ref[...] += jnp.dot(sel, ybuf_ref[slot, rows, :], preferred_element_type=F32)

    _for_used_blocks(tot_ref[i] * CHUNK, add_block)
    out_ref[...] = _rms(out_ref[...], gfin_ref[...])


def _routing_tables(cnt, n_exp_tiles):
    c16 = (cnt + CHUNK - 1) // CHUNK
    lbase = jnp.cumsum(c16, axis=1) - c16
    tile_off = jnp.cumsum(c16, axis=0) - c16
    tot = jnp.sum(c16, axis=0)
    per = EXP_TILE // CHUNK
    region_tiles = (tot + per - 1) // per
    region = region_tiles * per
    base = jnp.cumsum(region) - region
    dst = base[None, :] + tile_off
    tile_end = jnp.cumsum(region_tiles)
    n_valid = tile_end[-1]
    g = jnp.arange(n_exp_tiles, dtype=jnp.int32)
    tile_expert = jnp.sum(tile_end[None, :] <= jnp.minimum(g, n_valid - 1)[:, None], axis=1).astype(jnp.int32)
    c = jnp.arange(LOCAL_CHUNKS, dtype=jnp.int32)
    lend = lbase + c16
    owns = (lbase[:, None, :] <= c[None, :, None]) & (c[None, :, None] < lend[:, None, :])
    chunk_dst = jnp.sum(jnp.where(owns, (dst - lbase)[:, None, :], 0), axis=2) + c[None, :]
    i32 = lambda a: a.astype(jnp.int32).reshape(-1)
    return dict(chunk_dst=i32(chunk_dst), tot=i32(jnp.sum(c16, axis=1)),
                tail_start=i32(base + tot), tail_n=i32(region - tot),
                tile_expert=tile_expert, n_valid=i32(n_valid),
                lbase_rows=(lbase * CHUNK).astype(F32))


def _moe(h2, route, cnt_tiles, x1, w_gate, w_up, w_down, g_final):
    n_tok = h2.shape[0]
    tm = MOE_TILE
    n_tiles = n_tok // tm
    assert cnt_tiles.shape[0] == n_tiles
    worst_rows = 2 * n_tok + n_tiles * N_EXPERTS * (CHUNK - 1) + N_EXPERTS * (EXP_TILE - CHUNK)
    n_exp_tiles = -(-worst_rows // EXP_TILE)
    n_slots = n_exp_tiles * EXP_TILE

    cnt = cnt_tiles[:, 0, EXPERT_LANE0:EXPERT_LANE0 + N_EXPERTS].astype(jnp.int32)
    tb = _routing_tables(cnt, n_exp_tiles)
    lbase_rows = jnp.zeros((n_tiles, 8, ROUTER_LANES), F32).at[:, :, :N_EXPERTS].set(
        tb["lbase_rows"][:, None, :])
    row_id = np.arange(tm)
    ltri = jnp.asarray((row_id[:, None] > row_id[None, :]).astype(np.float32), dtype=BF16)

    tile_row = lambda width: pl.BlockSpec((tm, width), lambda i, *_: (i, 0))
    lbase_spec = pl.BlockSpec((1, 8, ROUTER_LANES), lambda i, *_: (i, 0, 0))
    ltri_spec = pl.BlockSpec((tm, tm), lambda i, *_: (0, 0), pipeline_mode=pl.Buffered(1))
    hbm = pl.BlockSpec(memory_space=pl.ANY)

    xs = pl.pallas_call(
        functools.partial(_dispatch_kernel, tm=tm),
        grid_spec=pltpu.PrefetchScalarGridSpec(
            num_scalar_prefetch=4, grid=(n_tiles,),
            in_specs=[tile_row(D_MODEL), tile_row(ROUTER_LANES), lbase_spec, ltri_spec],
            out_specs=hbm,
            scratch_shapes=[pltpu.VMEM((2, LOCAL_ROWS, XS_WIDTH), BF16), pltpu.VMEM((CHUNK, XS_WIDTH), BF16),
                            pltpu.SemaphoreType.DMA((2,)), pltpu.SemaphoreType.DMA]),
        out_shape=jax.ShapeDtypeStruct((n_slots, XS_WIDTH), BF16),
        compiler_params=pltpu.CompilerParams(
            dimension_semantics=("arbitrary",), vmem_limit_bytes=VMEM_LIMIT),
        name="moe_dispatch",
    )(tb["chunk_dst"], tb["tot"], tb["tail_start"], tb["tail_n"], h2, route, lbase_rows, ltri)

    row_tile = lambda width: pl.BlockSpec(
        (EXP_TILE, width), lambda g, te, nv: (jnp.maximum(jnp.minimum(g, nv[0] - 1), 0), 0))
    ys = pl.pallas_call(
        _expert_kernel,
        grid_spec=pltpu.PrefetchScalarGridSpec(
            num_scalar_prefetch=2, grid=(n_exp_tiles,),
            in_specs=[row_tile(XS_WIDTH),
                      pl.BlockSpec((None, D_MODEL, D_EXPERT), lambda g, te, nv: (te[g], 0, 0)),
                      pl.BlockSpec((None, D_MODEL, D_EXPERT), lambda g, te, nv: (te[g], 0, 0)),
                      pl.BlockSpec((None, D_EXPERT, D_MODEL), lambda g, te, nv: (te[g], 0, 0))],
            out_specs=row_tile(D_MODEL),
            scratch_shapes=[pltpu.VMEM((D_MODEL, D_EXPERT), BF16), pltpu.VMEM((D_MODEL, D_EXPERT), BF16),
                            pltpu.VMEM((D_EXPERT, D_MODEL), BF16)]),
        out_shape=jax.ShapeDtypeStruct((n_slots, D_MODEL), BF16),
        compiler_params=pltpu.CompilerParams(
            dimension_semantics=("arbitrary",), vmem_limit_bytes=VMEM_LIMIT),
        name="moe_expert",
    )(tb["tile_expert"], tb["n_valid"], xs, w_gate, w_up, w_down)

    return pl.pallas_call(
        functools.partial(_combine_kernel, tm=tm),
        grid_spec=pltpu.PrefetchScalarGridSpec(
            num_scalar_prefetch=2, grid=(n_tiles,),
            in_specs=[tile_row(D_MODEL), tile_row(ROUTER_LANES), lbase_spec, ltri_spec,
                      pl.BlockSpec((1, D_MODEL), lambda i, *_: (0, 0)), hbm],
            out_specs=tile_row(D_MODEL),
            scratch_shapes=[pltpu.VMEM((2, LOCAL_ROWS, D_MODEL), BF16), pltpu.SemaphoreType.DMA((2,))]),
        out_shape=jax.ShapeDtypeStruct((n_tok, D_MODEL), F32),
        compiler_params=pltpu.CompilerParams(
            dimension_semantics=("arbitrary",), vmem_limit_bytes=VMEM_LIMIT),
        name="moe_combine",
    )(tb["chunk_dst"], tb["tot"], x1, route, lbase_rows, ltri, g_final, ys)


def _rope_tables(seq, tm):
    pos = np.arange(seq, dtype=np.float64)
    inv_freq = ROPE_THETA ** (-np.arange(0, HEAD_DIM, 2, dtype=np.float64) / HEAD_DIM)
    ang = pos[:, None] * inv_freq[None, :]
    cos, sin = np.cos(ang), np.sin(ang)
    reps = CB // HEAD_DIM
    cos_t = np.tile(np.concatenate([cos, cos], axis=-1), (1, reps))
    sin_t = np.tile(np.concatenate([-sin, sin], axis=-1), (1, reps))

    def reorder(t, dil):
        return t.reshape(seq // tm, tm // dil, dil, CB).transpose(0, 2, 1, 3).reshape(seq, CB)

    dils = [dil for _, dil in DIL_GROUPS]
    return (jnp.asarray(np.stack([reorder(cos_t, dil) for dil in dils]).astype(np.float32)),
            jnp.asarray(np.stack([reorder(sin_t, dil) for dil in dils]).astype(np.float32)))


def kernel(x, w_in, b_in, sinks, w_proj_a, w_proj_b, w_out, g_mix, g_ffn, w_router_group, b_router_group,
           w_router_expert, b_router_expert, w_exp_gate, w_exp_up, w_exp_down, g_final):
    batch, seq, d = x.shape
    assert d == D_MODEL and w_in.shape[0] == 1, "single-layer kernel"
    n_tok = batch * seq
    x2 = x.reshape(n_tok, d)
    tm_in = 512
    cos_t, sin_t = _rope_tables(seq, tm_in)

    z_tok, z_d1, z_d2 = _in_proj(x2, g_mix[0][None, :], w_in[0].astype(BF16), b_in[0][None, :],
                                 cos_t, sin_t, seq=seq, tm=tm_in)

    a_bases = (0, A_BLOCKS, 2 * A_BLOCKS)
    tok_bases = tuple(ZB_A0 + b for b in a_bases)
    oa, lse = [], []
    for group, (z, bases) in enumerate(((z_tok, tok_bases), (z_d1, a_bases), (z_d2, a_bases))):
        o_g, l_g = _dilated_attention(z, bases, group, batch=batch, seq=seq, lq=ATTN_ROWS_PER_STEP)
        oa.append(o_g)
        lse.append(l_g)
    ob = _swa_attention(z_tok, sinks[0], batch=batch, seq=seq, lq=ATTN_ROWS_PER_STEP)

    pad = ROUTER_LANES - N_EXPERT_GROUPS - N_EXPERTS
    w_router = jnp.concatenate(
        [w_router_group[0], w_router_expert[0], jnp.zeros((d, pad), F32)], axis=1)
    b_router = jnp.concatenate(
        [b_router_group[0], b_router_expert[0], jnp.zeros((pad,), F32)])[None, :]
    x1, h2, route, cnt_tiles = _post_attn(
        oa, lse, ob, z_tok, x2, w_proj_a[0].astype(BF16), w_proj_b[0].astype(BF16), w_out[0].astype(BF16),
        g_ffn[0][None, :], w_router, b_router, tm=MOE_TILE)

    out = _moe(h2, route, cnt_tiles, x1, w_exp_gate[0], w_exp_up[0], w_exp_down[0], g_final[None, :])
    return out.reshape(batch, seq, d)
```

```python
import functools
import math

import jax
import jax.numpy as jnp
import numpy as np
from jax import lax
from jax.experimental import pallas as pl
from jax.experimental.pallas import tpu as pltpu

F32 = jnp.float32
BF16 = jnp.bfloat16

D_MODEL = 1024
HEAD_DIM = 64
HALF = HEAD_DIM // 2
ROPE_THETA = 10000.0
RMS_EPS = 1e-6
LOG2E = math.log2(math.e)
LN2 = math.log(2.0)
Q_SCALE = LOG2E * HEAD_DIM ** -0.5
BLOCK = 128
ATTN_ROWS_PER_STEP = 2048
DIL_GROUPS = ((128, 1), (512, 4), (2048, 16))
N_DIL = len(DIL_GROUPS)
A_GROUP_WIDTH = 512
A_QKV_WIDTH = 3 * N_DIL * A_GROUP_WIDTH
B_Q_HEADS = 16
B_KV_HEADS = 2
B_Q_WIDTH = B_Q_HEADS * HEAD_DIM
B_WINDOW = 128
GATE_WIDTH = 2 * D_MODEL
IN_WIDTH = A_QKV_WIDTH + B_Q_WIDTH + 2 * B_KV_HEADS * HEAD_DIM + GATE_WIDTH
N_EXPERT_GROUPS = 4
EXPERTS_PER_GROUP = 8
N_EXPERTS = N_EXPERT_GROUPS * EXPERTS_PER_GROUP
D_EXPERT = D_MODEL // 4

CB = 256
PAIR = 128
N_IN_BLOCKS = IN_WIDTH // CB
A_BLOCKS = A_GROUP_WIDTH // CB
ZB_GATE = 0
ZB_QB = 8
ZB_KB = 12
ZB_VB = 13
ZB_A0 = 14
N_TOK_BLOCKS = ZB_A0 + 3 * A_BLOCKS
OUT_PERM_BLOCK = 256
LSE_PERM_BLOCK = 128
ROUTER_LANES = 128
EXPERT_LANE0 = EXPERTS_PER_GROUP
assert N_EXPERT_GROUPS <= EXPERTS_PER_GROUP
MOE_TILE = 512
CHUNK = 16
EXP_TILE = 512
LOCAL_ROWS = -(-(2 * MOE_TILE + N_EXPERTS * (CHUNK - 1)) // CB) * CB
LOCAL_CHUNKS = LOCAL_ROWS // CHUNK
SEL_BLOCK = CB
ISSUE_UNROLL = 4
WAIT_GROUP = 8
XS_WIDTH = D_MODEL + ROUTER_LANES
META_PIECES = 3
assert META_PIECES * N_EXPERTS <= ROUTER_LANES

VMEM_LIMIT = 56 * 1024 * 1024


def _in_proj_plan():
    plan = []
    for c in range(N_IN_BLOCKS):
        col = c * CB
        if col < A_QKV_WIDTH:
            part, rem = divmod(col, N_DIL * A_GROUP_WIDTH)
            group, blk = divmod(rem // CB, A_BLOCKS)
            kind = ("q", "k", "v")[part]
            dil = DIL_GROUPS[group][1]
            if dil == 1:
                plan.append((0, ZB_A0 + part * A_BLOCKS + blk, kind, 1))
            else:
                plan.append((group, part * A_BLOCKS + blk, kind, dil))
        elif col < A_QKV_WIDTH + B_Q_WIDTH:
            plan.append((0, ZB_QB + (col - A_QKV_WIDTH) // CB, "q", 1))
        elif col < A_QKV_WIDTH + B_Q_WIDTH + CB:
            plan.append((0, -1, "kvb", 1))
        else:
            plan.append((0, ZB_GATE + (col - (A_QKV_WIDTH + B_Q_WIDTH + CB)) // CB, "v", 1))
    return tuple(plan)


def _rms(x, g):
    return x * lax.rsqrt(jnp.mean(x * x, axis=-1, keepdims=True) + RMS_EPS) * g


def _split3(w):
    hi = w.astype(BF16).astype(F32)
    mid = (w - hi).astype(BF16).astype(F32)
    lo = (w - hi - mid).astype(BF16).astype(F32)
    return hi, mid, lo


def _rope(acc, cos, sin_signed, first_half):
    partner = jnp.where(first_half, pltpu.roll(acc, CB - HALF, 1), pltpu.roll(acc, HALF, 1))
    return acc * cos + partner * sin_signed


def _in_perm_block(dil):
    return max(BLOCK, CHUNK * dil)


def _in_proj_kernel(x_ref, g_ref, w_ref, b_ref, cos_ref, sin_ref, perm1_ref, perm2_ref,
                    zt_ref, zd1_ref, zd2_ref, *, plan, tm):
    out_refs = (zt_ref, zd1_ref, zd2_ref)
    perm_refs = (None, perm1_ref, perm2_ref)
    lane = lax.broadcasted_iota(jnp.int32, (tm, CB), 1)
    first_half = (lane % HEAD_DIM) < HALF

    h = _rms(x_ref[...], g_ref[...]).astype(BF16)
    h_by_dil, tables = {}, {}
    for slot, (_, dil) in enumerate(DIL_GROUPS):
        tables[dil] = slot
        if dil == 1:
            h_by_dil[dil] = h
            continue
        blk = _in_perm_block(dil)
        n = blk // dil
        moved = [jnp.dot(perm_refs[slot][...], h[tb * blk:(tb + 1) * blk], preferred_element_type=F32
                         ).astype(BF16) for tb in range(tm // blk)]
        h_by_dil[dil] = jnp.concatenate(
            [part[r * n:(r + 1) * n] for r in range(dil) for part in moved], axis=0)

    for c, (arr, dst, kind, dil) in enumerate(plan):
        cols = slice(c * CB, (c + 1) * CB)
        acc = jnp.dot(h_by_dil[dil], w_ref[:, cols], preferred_element_type=F32) + b_ref[:, cols]
        if kind in ("q", "k", "kvb"):
            slot = tables[dil]
            rot = _rope(acc, cos_ref[slot], sin_ref[slot], first_half)
        if kind == "q":
            val = (rot * Q_SCALE).astype(BF16)
        elif kind == "k":
            val = rot.astype(BF16)
        elif kind == "v":
            val = acc.astype(BF16)
        else:
            r64 = pltpu.roll(rot, HEAD_DIM, 1)
            r128 = pltpu.roll(rot, 2 * HEAD_DIM, 1)
            kdup = jnp.where(lane < HEAD_DIM, rot, jnp.where(lane < 3 * HEAD_DIM, r64, r128))
            a128 = pltpu.roll(acc, 2 * HEAD_DIM, 1)
            a192 = pltpu.roll(acc, 3 * HEAD_DIM, 1)
            vdup = jnp.where(lane < HEAD_DIM, a128, jnp.where(lane < 3 * HEAD_DIM, a192, acc))
            zt_ref[ZB_KB] = kdup.astype(BF16)
            zt_ref[ZB_VB] = vdup.astype(BF16)
            continue
        if dil == 1:
            out_refs[arr][dst] = val
        else:
            n = tm // dil
            for r in range(dil):
                out_refs[arr][dst, :, r * CB:(r + 1) * CB] = val[r * n:(r + 1) * n, :]


def _dilation_perm(block, dil, dtype):
    j = np.arange(block)
    src = (j % (block // dil)) * dil + j // (block // dil)
    return jnp.asarray((src[:, None] == j[None, :]).astype(np.float32), dtype=dtype)


def _in_proj(x2, g_mix, w_in, b_in, cos_t, sin_t, *, seq, tm):
    n_tok = x2.shape[0]
    tiles_per_seq = seq // tm
    const = lambda i: (0, 0)
    table = pl.BlockSpec((N_DIL, tm, CB), lambda i: (0, i % tiles_per_seq, 0))
    d1, d2 = DIL_GROUPS[1][1], DIL_GROUPS[2][1]
    perms = [_dilation_perm(_in_perm_block(d), d, BF16) for d in (d1, d2)]
    perm_spec = lambda d: pl.BlockSpec((_in_perm_block(d),) * 2, const, pipeline_mode=pl.Buffered(1))
    return pl.pallas_call(
        functools.partial(_in_proj_kernel, plan=_in_proj_plan(), tm=tm),
        grid=(n_tok // tm,),
        in_specs=[
            pl.BlockSpec((tm, D_MODEL), lambda i: (i, 0)),
            pl.BlockSpec((1, D_MODEL), const),
            pl.BlockSpec((D_MODEL, IN_WIDTH), const, pipeline_mode=pl.Buffered(1)),
            pl.BlockSpec((1, IN_WIDTH), const),
            table, table,
            perm_spec(d1), perm_spec(d2),
        ],
        out_specs=[
            pl.BlockSpec((N_TOK_BLOCKS, tm, CB), lambda i: (0, i, 0)),
            pl.BlockSpec((3 * A_BLOCKS, tm // d1, d1 * CB), lambda i: (0, i, 0)),
            pl.BlockSpec((3 * A_BLOCKS, tm // d2, d2 * CB), lambda i: (0, i, 0)),
        ],
        out_shape=[
            jax.ShapeDtypeStruct((N_TOK_BLOCKS, n_tok, CB), BF16),
            jax.ShapeDtypeStruct((3 * A_BLOCKS, n_tok // d1, d1 * CB), BF16),
            jax.ShapeDtypeStruct((3 * A_BLOCKS, n_tok // d2, d2 * CB), BF16),
        ],
        compiler_params=pltpu.CompilerParams(
            dimension_semantics=("arbitrary",), vmem_limit_bytes=VMEM_LIMIT),
        name="in_proj",
    )(x2, g_mix, w_in, b_in, cos_t, sin_t, *perms)


def _attn_kernel(*refs, lq, max_dist, kv_shared, has_sink, want_lse, n_axes):
    refs = list(refs)
    sink_ref = refs.pop(0) if has_sink else None
    q_ref, k_ref, v_ref, kp_ref, vp_ref = refs[:5]
    o_ref = refs[5]
    lse_ref = refs[6] if want_lse else None
    vaug_ref = refs[-1]
    n_qblk = q_ref.shape[0]
    n_kv_pairs = vaug_ref.shape[0]
    rows_kv = lq + BLOCK

    row = lax.broadcasted_iota(jnp.int32, (BLOCK, 2 * BLOCK), 0)
    col = lax.broadcasted_iota(jnp.int32, (BLOCK, 2 * BLOCK), 1)
    dist = row - col + BLOCK
    valid = (dist >= 0) & (dist <= max_dist)
    neg_inf = jnp.float32(-jnp.inf)
    bias = jnp.where(valid, 0.0, neg_inf)
    bias_first = jnp.where(valid & (col >= BLOCK), 0.0, neg_inf)
    bias0 = jnp.where(pl.program_id(n_axes - 1) == 0, bias_first, bias)
    lane_lo = lax.broadcasted_iota(jnp.int32, (BLOCK, PAIR), 1) < HEAD_DIM

    def rd(ref, blk, rows, cols):
        return ref[rows, cols] if kv_shared else ref[blk, rows, cols]

    first_step = functools.reduce(lambda a, b: a & b, [pl.program_id(a) == 0 for a in range(n_axes)])

    @pl.when(first_step)
    def _():
        lane = lax.broadcasted_iota(jnp.int32, (rows_kv, PAIR), 1)
        for pair in range(n_kv_pairs):
            vaug_ref[pair, 0, :, PAIR:] = jnp.where(lane < HEAD_DIM, 1.0, 0.0).astype(BF16)
            vaug_ref[pair, 1, :, PAIR:] = jnp.where(lane < HEAD_DIM, 0.0, 1.0).astype(BF16)

    lane_kv = lax.broadcasted_iota(jnp.int32, (rows_kv, PAIR), 1) < HEAD_DIM
    n_col_pairs = q_ref.shape[-1] // PAIR
    for pair in range(n_kv_pairs):
        cols = slice((pair % n_col_pairs) * PAIR, (pair % n_col_pairs + 1) * PAIR)
        everything = slice(None)
        v_all = jnp.concatenate([rd(vp_ref, pair // n_col_pairs, everything, cols),
                                 rd(v_ref, pair // n_col_pairs, everything, cols)], axis=0)
        zero = jnp.zeros_like(v_all)
        vaug_ref[pair, 0, :, :PAIR] = jnp.where(lane_kv, v_all, zero)
        vaug_ref[pair, 1, :, :PAIR] = jnp.where(lane_kv, zero, v_all)

    for ib in range(lq // BLOCK):
        rows = slice(ib * BLOCK, (ib + 1) * BLOCK)
        win = slice(ib * BLOCK, (ib + 2) * BLOCK)
        b_ib = bias0 if ib == 0 else bias
        for blk in range(n_qblk):
            for pp in range(n_col_pairs):
                qcols = slice(pp * PAIR, (pp + 1) * PAIR)
                pair = 0 if kv_shared else blk * n_col_pairs + pp
                kcols = slice(0, PAIR) if kv_shared else qcols
                if ib == 0:
                    k_win = jnp.concatenate([rd(kp_ref, blk, slice(None), kcols),
                                             rd(k_ref, blk, slice(0, BLOCK), kcols)], axis=0)
                else:
                    k_win = rd(k_ref, blk, slice((ib - 1) * BLOCK, (ib + 1) * BLOCK), kcols)
                q_pair = q_ref[blk, rows, qcols]
                ps, ms, sink_terms = [], [], []
                for hh in range(2):
                    q_h = jnp.where(lane_lo == (hh == 0), q_pair, jnp.zeros_like(q_pair))
                    s = lax.dot_general(q_h, k_win, (((1,), (1,)), ((), ())), preferred_element_type=F32)
                    s = s + b_ib
                    m = jnp.max(s, axis=-1, keepdims=True)
                    if has_sink:
                        head = (pl.program_id(1) * n_qblk + blk) * (CB // HEAD_DIM) + pp * 2 + hh
                        sink = sink_ref[head] * LOG2E
                        m = jnp.maximum(m, sink)
                        sink_terms.append(jnp.exp2(sink - m))
                    ps.append(jnp.exp2(s - m).astype(BF16))
                    ms.append(m)
                v_aug = jnp.concatenate([vaug_ref[pair, 0, win, :], vaug_ref[pair, 1, win, :]], axis=0)
                od = jnp.dot(jnp.concatenate(ps, axis=1), v_aug, preferred_element_type=F32)
                den = od[:, PAIR:]
                if has_sink:
                    den = den + jnp.where(lane_lo, sink_terms[0], sink_terms[1])
                o_ref[blk, rows, qcols] = (od[:, :PAIR] * (1.0 / den)).astype(BF16)
                if want_lse:
                    lse_ref[blk, rows, qcols] = (jnp.where(lane_lo, ms[0], ms[1]) + jnp.log2(den)) * LN2


def _dilated_attention(z, bases, group, *, batch, seq, lq):
    window, dil = DIL_GROUPS[group]
    sub_len = seq // dil
    lq = min(lq, sub_len)
    z4 = z.reshape(z.shape[0], batch, sub_len, dil * CB)
    qb, kb, vb = bases
    assert all(base % A_BLOCKS == 0 for base in bases)
    bpq = lq // BLOCK
    slabs = max(1, min(dil, ATTN_ROWS_PER_STEP // lq))
    width = slabs * CB
    cur = lambda base: pl.BlockSpec(
        (A_BLOCKS, None, lq, width), lambda b, r, i: (base // A_BLOCKS, b, i, r))
    prev = lambda base: pl.BlockSpec(
        (A_BLOCKS, None, BLOCK, width), lambda b, r, i: (base // A_BLOCKS, b, jnp.maximum(i * bpq - 1, 0), r))
    out_spec = pl.BlockSpec((A_BLOCKS, None, lq, width), lambda b, r, i: (0, b, i, r))
    o, lse = pl.pallas_call(
        functools.partial(_attn_kernel, lq=lq, max_dist=window // dil, kv_shared=False,
                          has_sink=False, want_lse=True, n_axes=3),
        grid=(batch, dil // slabs, sub_len // lq),
        in_specs=[cur(qb), cur(kb), cur(vb), prev(kb), prev(vb)],
        out_specs=[out_spec, out_spec],
        out_shape=[jax.ShapeDtypeStruct((A_BLOCKS, batch, sub_len, dil * CB), BF16),
                   jax.ShapeDtypeStruct((A_BLOCKS, batch, sub_len, dil * CB), F32)],
        scratch_shapes=[pltpu.VMEM((2 * A_BLOCKS * slabs, 2, lq + BLOCK, 2 * PAIR), BF16)],
        compiler_params=pltpu.CompilerParams(
            dimension_semantics=("arbitrary",) * 3, vmem_limit_bytes=VMEM_LIMIT),
        name=f"dilated_attn_g{group}",
    )(z4, z4, z4, z4, z4)
    rows = batch * sub_len
    return o.reshape(A_BLOCKS, rows, dil * CB), lse.reshape(A_BLOCKS, rows, dil * CB)


def _swa_attention(z_tok, sinks, *, batch, seq, lq):
    z4 = z_tok.reshape(N_TOK_BLOCKS, batch, seq, CB)
    bpq = lq // BLOCK
    n_q_blocks = B_Q_WIDTH // CB
    q_per_kv = n_q_blocks // B_KV_HEADS
    assert ZB_QB % q_per_kv == 0
    q_spec = pl.BlockSpec((q_per_kv, None, lq, CB), lambda b, kvh, i, s: (ZB_QB // q_per_kv + kvh, b, i, 0))
    cur = lambda base: pl.BlockSpec((None, None, lq, PAIR), lambda b, kvh, i, s: (base, b, i, kvh))
    prev = lambda base: pl.BlockSpec(
        (None, None, BLOCK, PAIR), lambda b, kvh, i, s: (base, b, jnp.maximum(i * bpq - 1, 0), kvh))
    o = pl.pallas_call(
        functools.partial(_attn_kernel, lq=lq, max_dist=B_WINDOW - 1, kv_shared=True,
                          has_sink=True, want_lse=False, n_axes=3),
        grid_spec=pltpu.PrefetchScalarGridSpec(
            num_scalar_prefetch=1,
            grid=(batch, B_KV_HEADS, seq // lq),
            in_specs=[q_spec, cur(ZB_KB), cur(ZB_VB), prev(ZB_KB), prev(ZB_VB)],
            out_specs=pl.BlockSpec((q_per_kv, None, lq, CB), lambda b, kvh, i, s: (kvh, b, i, 0)),
            scratch_shapes=[pltpu.VMEM((1, 2, lq + BLOCK, 2 * PAIR), BF16)],
        ),
        out_shape=jax.ShapeDtypeStruct((n_q_blocks, batch, seq, CB), BF16),
        compiler_params=pltpu.CompilerParams(
            dimension_semantics=("arbitrary",) * 3, vmem_limit_bytes=VMEM_LIMIT),
        name="swa_attn",
    )(sinks, z4, z4, z4, z4, z4)
    return o.reshape(n_q_blocks, batch * seq, CB)


def _post_attn_kernel(o0_ref, o1_ref, o2_ref, l0_ref, l1_ref, l2_ref, ob_ref, gate_ref, x_ref,
                      wa_ref, wb_ref, wo_ref, gf_ref, wr_ref, br_ref,
                      po_ref, pl_ref, x1_ref, h2_ref, route_ref, cnt_ref, *, tm):

    def to_token_order(ref, cb, slot, perm_ref, blk):
        dil = DIL_GROUPS[slot + 1][1]
        n = blk // dil
        parts = []
        for tb in range(tm // blk):
            stack = jnp.concatenate(
                [ref[cb, tb * n:(tb + 1) * n, r * CB:(r + 1) * CB] for r in range(dil)], axis=0)
            pieces = (stack,) if stack.dtype == BF16 else _split3(stack)
            moved = [jnp.dot(perm_ref[slot], p.astype(BF16), preferred_element_type=F32) for p in pieces]
            parts.append(functools.reduce(lambda a, b: a + b, moved))
        return jnp.concatenate(parts, axis=0)

    pa = None
    for cb in range(A_BLOCKS):
        l0 = l0_ref[cb]
        l1 = to_token_order(l1_ref, cb, 0, pl_ref, LSE_PERM_BLOCK)
        l2 = to_token_order(l2_ref, cb, 1, pl_ref, LSE_PERM_BLOCK)
        o1 = to_token_order(o1_ref, cb, 0, po_ref, OUT_PERM_BLOCK)
        o2 = to_token_order(o2_ref, cb, 1, po_ref, OUT_PERM_BLOCK)
        mx = jnp.maximum(jnp.maximum(l0, l1), l2)
        e0, e1, e2 = jnp.exp(l0 - mx), jnp.exp(l1 - mx), jnp.exp(l2 - mx)
        inv = 1.0 / (e0 + e1 + e2)
        ya = (e0 * inv) * o0_ref[cb].astype(F32) + (e1 * inv) * o1 + (e2 * inv) * o2
        part = jnp.dot(ya.astype(BF16), wa_ref[cb * CB:(cb + 1) * CB, :], preferred_element_type=F32)
        pa = part if pa is None else pa + part
    pb = None
    for cb in range(B_Q_WIDTH // CB):
        part = jnp.dot(ob_ref[cb], wb_ref[cb * CB:(cb + 1) * CB, :], preferred_element_type=F32)
        pb = part if pb is None else pb + part
    n_gate = D_MODEL // CB
    merged = []
    for cb in range(n_gate):
        cols = slice(cb * CB, (cb + 1) * CB)
        ga = jax.nn.sigmoid(gate_ref[cb].astype(F32))
        gb = jax.nn.sigmoid(gate_ref[n_gate + cb].astype(F32))
        merged.append((ga * pa[:, cols] + gb * pb[:, cols]).astype(BF16))
    merged = jnp.concatenate(merged, axis=1)
    x1 = x_ref[...] + jnp.dot(merged, wo_ref[...], preferred_element_type=F32)
    x1_ref[...] = x1
    h2 = _rms(x1, gf_ref[...])
    h2_ref[...] = h2.astype(BF16)

    h2_hi = h2.astype(BF16)
    h2_lo = (h2 - h2_hi.astype(F32)).astype(BF16)
    wr = wr_ref[...]
    wr_hi = wr.astype(BF16)
    wr_lo = (wr - wr_hi.astype(F32)).astype(BF16)
    nt = (((1,), (1,)), ((), ()))
    hi_terms = lax.dot_general(jnp.concatenate([wr_hi, wr_lo], axis=0), h2_hi, nt, preferred_element_type=F32)
    logits = (hi_terms[:ROUTER_LANES]
              + (lax.dot_general(wr_hi, h2_lo, nt, preferred_element_type=F32) + hi_terms[ROUTER_LANES:])
              ) + br_ref[...]
    sub = lax.broadcasted_iota(jnp.int32, (EXPERTS_PER_GROUP, tm), 0)
    neg_inf = jnp.float32(-jnp.inf)
    top = lambda v: jnp.max(v, axis=0, keepdims=True)
    first = lambda hit: jnp.min(jnp.where(hit, sub, EXPERTS_PER_GROUP), axis=0, keepdims=True)
    lg = jnp.where(sub < N_EXPERT_GROUPS, logits[:EXPERTS_PER_GROUP], neg_inf)
    mg = top(lg)
    gsel = first(lg == mg)
    pg_sel = 1.0 / jnp.sum(jnp.exp(lg - mg), axis=0, keepdims=True)
    le = logits[EXPERT_LANE0:EXPERT_LANE0 + EXPERTS_PER_GROUP]
    for g in range(1, N_EXPERT_GROUPS):
        r0 = EXPERT_LANE0 + g * EXPERTS_PER_GROUP
        le = jnp.where(gsel == g, logits[r0:r0 + EXPERTS_PER_GROUP], le)
    ex = jnp.exp(le - top(le))
    pe = ex / jnp.sum(ex, axis=0, keepdims=True)
    p1 = top(pe)
    i1 = first(pe == p1)
    rest = sub != i1
    p2 = top(jnp.where(rest, pe, -1.0))
    i2 = first(rest & (pe == p2))
    norm = pg_sel / (p1 + p2)
    e1 = gsel * EXPERTS_PER_GROUP + i1
    e2 = gsel * EXPERTS_PER_GROUP + i2
    row = lax.broadcasted_iota(jnp.int32, (ROUTER_LANES, tm), 0)
    route_t = jnp.where(
        row == 0, e1.astype(F32),
        jnp.where(row == 1, e2.astype(F32),
                  jnp.where(row == 2, p1 * norm, jnp.where(row == 3, p2 * norm, 0.0))))
    route_ref[...] = route_t.T
    hits_t = jnp.where((row == e1 + EXPERT_LANE0) | (row == e2 + EXPERT_LANE0), 1.0, 0.0).astype(BF16)
    cnt_ref[0] = lax.dot_general(jnp.ones((cnt_ref.shape[1], tm), BF16), hits_t, nt,
                                 preferred_element_type=F32)


def _post_attn(oa, lse, ob, z_tok, x2, w_proj_a, w_proj_b, w_out, g_ffn, w_router, b_router, *, tm):
    n_tok = x2.shape[0]
    const2 = lambda i: (0, 0)
    blk = lambda nb: pl.BlockSpec((nb, tm, CB), lambda i: (0, i, 0))
    dil_blk = lambda dil: pl.BlockSpec((A_BLOCKS, tm // dil, dil * CB), lambda i: (0, i, 0))
    row = lambda width: pl.BlockSpec((tm, width), lambda i: (i, 0))
    resident = lambda shape: pl.BlockSpec(shape, const2, pipeline_mode=pl.Buffered(1))
    d1, d2 = DIL_GROUPS[1][1], DIL_GROUPS[2][1]
    perm_o = jnp.stack([_dilation_perm(OUT_PERM_BLOCK, d, BF16).T for d in (d1, d2)])
    perm_l = jnp.stack([_dilation_perm(LSE_PERM_BLOCK, d, BF16).T for d in (d1, d2)])
    return pl.pallas_call(
        functools.partial(_post_attn_kernel, tm=tm),
        grid=(n_tok // tm,),
        in_specs=[blk(A_BLOCKS), dil_blk(d1), dil_blk(d2), blk(A_BLOCKS), dil_blk(d1), dil_blk(d2),
                  blk(B_Q_WIDTH // CB), blk(GATE_WIDTH // CB), row(D_MODEL),
                  resident((A_GROUP_WIDTH, D_MODEL)), resident((B_Q_WIDTH, D_MODEL)),
                  resident((D_MODEL, D_MODEL)), resident((1, D_MODEL)),
                  resident((ROUTER_LANES, D_MODEL)), resident((ROUTER_LANES, 1)),
                  pl.BlockSpec((2, OUT_PERM_BLOCK, OUT_PERM_BLOCK), lambda i: (0, 0, 0),
                               pipeline_mode=pl.Buffered(1)),
                  pl.BlockSpec((2, LSE_PERM_BLOCK, LSE_PERM_BLOCK), lambda i: (0, 0, 0),
                               pipeline_mode=pl.Buffered(1))],
        out_specs=[row(D_MODEL), row(D_MODEL), row(ROUTER_LANES),
                   pl.BlockSpec((1, 8, ROUTER_LANES), lambda i: (i, 0, 0))],
        out_shape=[jax.ShapeDtypeStruct((n_tok, D_MODEL), F32),
                   jax.ShapeDtypeStruct((n_tok, D_MODEL), BF16),
                   jax.ShapeDtypeStruct((n_tok, ROUTER_LANES), F32),
                   jax.ShapeDtypeStruct((n_tok // tm, 8, ROUTER_LANES), F32)],
        compiler_params=pltpu.CompilerParams(
            dimension_semantics=("arbitrary",), vmem_limit_bytes=VMEM_LIMIT),
        name="post_attn",
    )(oa[0], oa[1], oa[2], lse[0], lse[1], lse[2], ob, z_tok, x2,
      w_proj_a, w_proj_b, w_out, g_ffn, w_router, b_router, perm_o, perm_l)


def _local_slots(route, lbase_row, ltri):
    lane = lax.broadcasted_iota(jnp.int32, route.shape, 1)
    pick = lambda k: jnp.sum(jnp.where(lane == k, route, 0.0), axis=-1, keepdims=True)
    lanef = lane.astype(F32)
    oh1, oh2 = lanef == pick(0), lanef == pick(1)
    oh = jnp.where(oh1 | oh2, 1.0, 0.0).astype(BF16)
    table = jnp.dot(ltri, oh, preferred_element_type=F32) + lbase_row
    ls1 = jnp.sum(jnp.where(oh1, table, 0.0), axis=-1, keepdims=True)
    ls2 = jnp.sum(jnp.where(oh2, table, 0.0), axis=-1, keepdims=True)
    return ls1, ls2, pick


def _chunk_loop(count, body, unroll=1):
    one = lambda c, carry: (body(c), carry)[1]
    full = 0
    if unroll > 1:
        full = (count // unroll) * unroll

        def group(g, carry):
            for u in range(unroll):
                body(g * unroll + u)
            return carry

        lax.fori_loop(0, count // unroll, group, 0)
    lax.fori_loop(full, count, one, 0)


def _wait_chunks(count, wait_rows):
    _chunk_loop(count // WAIT_GROUP, lambda c: wait_rows(WAIT_GROUP * CHUNK))
    _chunk_loop(count % WAIT_GROUP, lambda c: wait_rows(CHUNK))


def _selection_blocks(slots_a, slots_b, axis, other):
    shape = (SEL_BLOCK, other) if axis == 0 else (other, SEL_BLOCK)
    local = lax.broadcasted_iota(jnp.int32, shape, axis).astype(F32).astype(BF16)
    one, zero = jnp.ones_like(local), jnp.zeros_like(local)

    def block(a):
        rel_a = (slots_a - float(a * SEL_BLOCK)).astype(BF16)
        rel_b = (slots_b - float(a * SEL_BLOCK)).astype(BF16)
        return jnp.where(local == rel_a, one, jnp.where(local == rel_b, one, zero))

    return block


def _for_used_blocks(used_rows, body):
    always = 2 * MOE_TILE // SEL_BLOCK
    body(0, always)
    for a in range(always, LOCAL_ROWS // SEL_BLOCK):
        pl.when(used_rows > a * SEL_BLOCK)(functools.partial(body, a, a + 1))


def _dispatch_kernel(cdst_ref, tot_ref, tail_start_ref, tail_n_ref,
                     h2_ref, route_ref, lbase_ref, ltri_ref, xs_ref, buf_ref, zero_ref, sem_ref, zsem_ref,
                     *, tm):
    i = pl.program_id(0)
    last = pl.num_programs(0) - 1
    slot = i % 2

    def chunk(slot_, src_row, dst_row, rows=CHUNK):
        return pltpu.make_async_copy(buf_ref.at[slot_, pl.ds(src_row, rows), :],
                                     xs_ref.at[pl.ds(dst_row, rows), :], sem_ref.at[slot_])

    def wait_tile(slot_, tile):
        _wait_chunks(tot_ref[tile], lambda rows: chunk(slot_, 0, 0, rows).wait())

    @pl.when(i == 0)
    def _():
        zero_ref[...] = jnp.zeros_like(zero_ref)

        def zero_chunk(row):
            return pltpu.make_async_copy(zero_ref, xs_ref.at[pl.ds(row, CHUNK), :], zsem_ref)

        def per_expert(e, total):
            start = tail_start_ref[e] * CHUNK
            _chunk_loop(tail_n_ref[e], lambda c: zero_chunk(pl.multiple_of(start + c * CHUNK, CHUNK)).start())
            return total + tail_n_ref[e]

        total = lax.fori_loop(0, N_EXPERTS, per_expert, 0)
        _chunk_loop(total, lambda c: zero_chunk(0).wait())

    @pl.when(i >= 2)
    def _():
        wait_tile(slot, i - 2)

    route = route_ref[...]
    ls1, ls2, pick = _local_slots(route, lbase_ref[0, 0:1, :], ltri_ref[...])
    lane = lax.broadcasted_iota(jnp.int32, route.shape, 1)
    ls_t = jnp.where(lane == 0, ls1, jnp.where(lane == 1, ls2, 0.0)).T
    sel_block = _selection_blocks(ls_t[0:1, :], ls_t[1:2, :], 0, tm)
    lane_expert = (lane // META_PIECES).astype(F32)
    lane_piece = lane % META_PIECES
    by_piece = lambda pieces: jnp.where(lane_piece == 0, pieces[0],
                                        jnp.where(lane_piece == 1, pieces[1], pieces[2]))
    meta = jnp.where(lane_expert == pick(0), by_piece(_split3(pick(2))),
                     jnp.where(lane_expert == pick(1), by_piece(_split3(pick(3))), 0.0))
    meta = meta.astype(BF16)

    def sort_block(a0, a1):
        rows = slice(a0 * SEL_BLOCK, a1 * SEL_BLOCK)
        sel = jnp.concatenate([sel_block(a) for a in range(a0, a1)], axis=0)
        buf_ref[slot, rows, :D_MODEL] = jnp.dot(sel, h2_ref[...], preferred_element_type=F32).astype(BF16)
        buf_ref[slot, rows, D_MODEL:] = jnp.dot(sel, meta, preferred_element_type=F32).astype(BF16)

    _for_used_blocks(tot_ref[i] * CHUNK, sort_block)

    _chunk_loop(tot_ref[i], lambda c: chunk(
        slot, pl.multiple_of(c * CHUNK, CHUNK),
        pl.multiple_of(cdst_ref[i * LOCAL_CHUNKS + c] * CHUNK, CHUNK)).start(), unroll=ISSUE_UNROLL)

    @pl.when(i == last)
    def _():
        @pl.when(i >= 1)
        def _():
            wait_tile(1 - slot, i - 1)
        wait_tile(slot, i)


def _expert_kernel(te_ref, nv_ref, xs_ref, wg_ref, wu_ref, wd_ref, ys_ref, wgb_ref, wub_ref, wdb_ref):
    g = pl.program_id(0)
    e = te_ref[g]

    @pl.when((g == 0) | (te_ref[jnp.maximum(g - 1, 0)] != e))
    def _():
        wgb_ref[...] = wg_ref[...].astype(BF16)
        wub_ref[...] = wu_ref[...].astype(BF16)
        wdb_ref[...] = wd_ref[...].astype(BF16)

    @pl.when(g < nv_ref[0])
    def _():
        x = xs_ref[:, :D_MODEL]
        meta = xs_ref[:, D_MODEL:].astype(F32)
        lane = lax.broadcasted_iota(jnp.int32, meta.shape, 1)
        mine = (lane >= e * META_PIECES) & (lane < (e + 1) * META_PIECES)
        w = jnp.sum(jnp.where(mine, meta, 0.0), axis=-1, keepdims=True)
        hg = jnp.dot(x, wgb_ref[...], preferred_element_type=F32)
        hu = jnp.dot(x, wub_ref[...], preferred_element_type=F32)
        a = (hg * jax.nn.sigmoid(hg)) * hu * w
        ys_ref[...] = jnp.dot(a.astype(BF16), wdb_ref[...], preferred_element_type=F32).astype(BF16)


def _combine_kernel(cdst_ref, tot_ref,
                    x1_ref, route_ref, lbase_ref, ltri_ref, gfin_ref, ys_ref, out_ref, ybuf_ref, sem_ref,
                    *, tm):
    i = pl.program_id(0)
    n_tiles = pl.num_programs(0)
    slot = i % 2

    def chunk(slot_, src_row, dst_row, rows=CHUNK):
        return pltpu.make_async_copy(ys_ref.at[pl.ds(src_row, rows), :],
                                     ybuf_ref.at[slot_, pl.ds(dst_row, rows), :], sem_ref.at[slot_])

    def fetch_tile(tile, slot_):
        _chunk_loop(tot_ref[tile], lambda c: chunk(
            slot_, pl.multiple_of(cdst_ref[tile * LOCAL_CHUNKS + c] * CHUNK, CHUNK),
            pl.multiple_of(c * CHUNK, CHUNK)).start(), unroll=ISSUE_UNROLL)

    @pl.when(i == 0)
    def _():
        ybuf_ref[...] = jnp.zeros_like(ybuf_ref)
        fetch_tile(0, 0)

    @pl.when(i + 1 < n_tiles)
    def _():
        fetch_tile(i + 1, 1 - slot)

    _wait_chunks(tot_ref[i], lambda rows: chunk(slot, 0, 0, rows).wait())

    ls1, ls2, _ = _local_slots(route_ref[...], lbase_ref[0, 0:1, :], ltri_ref[...])
    sel_block = _selection_blocks(ls1, ls2, 1, tm)
    out_ref[...] = x1_ref[...]

    def add_block(a0, a1):
        rows = slice(a0 * SEL_BLOCK, a1 * SEL_BLOCK)
        sel = jnp.concatenate([sel_block(a) for a in range(a0, a1)], axis=1)
        out_ref[...] += jnp.dot(sel, ybuf_ref[slot, rows, :], preferred_element_type=F32)

    _for_used_blocks(tot_ref[i] * CHUNK, add_block)
    out_ref[...] = _rms(out_ref[...], gfin_ref[...])


def _routing_tables(cnt, n_exp_tiles):
    c16 = (cnt + CHUNK - 1) // CHUNK
    lbase = jnp.cumsum(c16, axis=1) - c16
    tile_off = jnp.cumsum(c16, axis=0) - c16
    tot = jnp.sum(c16, axis=0)
    per = EXP_TILE // CHUNK
    region_tiles = (tot + per - 1) // per
    region = region_tiles * per
    base = jnp.cumsum(region) - region
    dst = base[None, :] + tile_off
    tile_end = jnp.cumsum(region_tiles)
    n_valid = tile_end[-1]
    g = jnp.arange(n_exp_tiles, dtype=jnp.int32)
    tile_expert = jnp.sum(tile_end[None, :] <= jnp.minimum(g, n_valid - 1)[:, None], axis=1).astype(jnp.int32)
    c = jnp.arange(LOCAL_CHUNKS, dtype=jnp.int32)
    lend = lbase + c16
    owns = (lbase[:, None, :] <= c[None, :, None]) & (c[None, :, None] < lend[:, None, :])
    chunk_dst = jnp.sum(jnp.where(owns, (dst - lbase)[:, None, :], 0), axis=2) + c[None, :]
    i32 = lambda a: a.astype(jnp.int32).reshape(-1)
    return dict(chunk_dst=i32(chunk_dst), tot=i32(jnp.sum(c16, axis=1)),
                tail_start=i32(base + tot), tail_n=i32(region - tot),
                tile_expert=tile_expert, n_valid=i32(n_valid),
                lbase_rows=(lbase * CHUNK).astype(F32))


def _moe(h2, route, cnt_tiles, x1, w_gate, w_up, w_down, g_final):
    n_tok = h2.shape[0]
    tm = MOE_TILE
    n_tiles = n_tok // tm
    assert cnt_tiles.shape[0] == n_tiles
    worst_rows = 2 * n_tok + n_tiles * N_EXPERTS * (CHUNK - 1) + N_EXPERTS * (EXP_TILE - CHUNK)
    n_exp_tiles = -(-worst_rows // EXP_TILE)
    n_slots = n_exp_tiles * EXP_TILE

    cnt = cnt_tiles[:, 0, EXPERT_LANE0:EXPERT_LANE0 + N_EXPERTS].astype(jnp.int32)
    tb = _routing_tables(cnt, n_exp_tiles)
    lbase_rows = jnp.zeros((n_tiles, 8, ROUTER_LANES), F32).at[:, :, :N_EXPERTS].set(
        tb["lbase_rows"][:, None, :])
    row_id = np.arange(tm)
    ltri = jnp.asarray((row_id[:, None] > row_id[None, :]).astype(np.float32), dtype=BF16)

    tile_row = lambda width: pl.BlockSpec((tm, width), lambda i, *_: (i, 0))
    lbase_spec = pl.BlockSpec((1, 8, ROUTER_LANES), lambda i, *_: (i, 0, 0))
    ltri_spec = pl.BlockSpec((tm, tm), lambda i, *_: (0, 0), pipeline_mode=pl.Buffered(1))
    hbm = pl.BlockSpec(memory_space=pl.ANY)

    xs = pl.pallas_call(
        functools.partial(_dispatch_kernel, tm=tm),
        grid_spec=pltpu.PrefetchScalarGridSpec(
            num_scalar_prefetch=4, grid=(n_tiles,),
            in_specs=[tile_row(D_MODEL), tile_row(ROUTER_LANES), lbase_spec, ltri_spec],
            out_specs=hbm,
            scratch_shapes=[pltpu.VMEM((2, LOCAL_ROWS, XS_WIDTH), BF16), pltpu.VMEM((CHUNK, XS_WIDTH), BF16),
                            pltpu.SemaphoreType.DMA((2,)), pltpu.SemaphoreType.DMA]),
        out_shape=jax.ShapeDtypeStruct((n_slots, XS_WIDTH), BF16),
        compiler_params=pltpu.CompilerParams(
            dimension_semantics=("arbitrary",), vmem_limit_bytes=VMEM_LIMIT),
        name="moe_dispatch",
    )(tb["chunk_dst"], tb["tot"], tb["tail_start"], tb["tail_n"], h2, route, lbase_rows, ltri)

    row_tile = lambda width: pl.BlockSpec(
        (EXP_TILE, width), lambda g, te, nv: (jnp.maximum(jnp.minimum(g, nv[0] - 1), 0), 0))
    ys = pl.pallas_call(
        _expert_kernel,
        grid_spec=pltpu.PrefetchScalarGridSpec(
            num_scalar_prefetch=2, grid=(n_exp_tiles,),
            in_specs=[row_tile(XS_WIDTH),
                      pl.BlockSpec((None, D_MODEL, D_EXPERT), lambda g, te, nv: (te[g], 0, 0)),
                      pl.BlockSpec((None, D_MODEL, D_EXPERT), lambda g, te, nv: (te[g], 0, 0)),
                      pl.BlockSpec((None, D_EXPERT, D_MODEL), lambda g, te, nv: (te[g], 0, 0))],
            out_specs=row_tile(D_MODEL),
            scratch_shapes=[pltpu.VMEM((D_MODEL, D_EXPERT), BF16), pltpu.VMEM((D_MODEL, D_EXPERT), BF16),
                            pltpu.VMEM((D_EXPERT, D_MODEL), BF16)]),
        out_shape=jax.ShapeDtypeStruct((n_slots, D_MODEL), BF16),
        compiler_params=pltpu.CompilerParams(
            dimension_semantics=("arbitrary",), vmem_limit_bytes=VMEM_LIMIT),
        name="moe_expert",
    )(tb["tile_expert"], tb["n_valid"], xs, w_gate, w_up, w_down)

    return pl.pallas_call(
        functools.partial(_combine_kernel, tm=tm),
        grid_spec=pltpu.PrefetchScalarGridSpec(
            num_scalar_prefetch=2, grid=(n_tiles,),
            in_specs=[tile_row(D_MODEL), tile_row(ROUTER_LANES), lbase_spec, ltri_spec,
                      pl.BlockSpec((1, D_MODEL), lambda i, *_: (0, 0)), hbm],
            out_specs=tile_row(D_MODEL),
            scratch_shapes=[pltpu.VMEM((2, LOCAL_ROWS, D_MODEL), BF16), pltpu.SemaphoreType.DMA((2,))]),
        out_shape=jax.ShapeDtypeStruct((n_tok, D_MODEL), F32),
        compiler_params=pltpu.CompilerParams(
            dimension_semantics=("arbitrary",), vmem_limit_bytes=VMEM_LIMIT),
        name="moe_combine",
    )(tb["chunk_dst"], tb["tot"], x1, route, lbase_rows, ltri, g_final, ys)


def _rope_tables(seq, tm):
    pos = np.arange(seq, dtype=np.float64)
    inv_freq = ROPE_THETA ** (-np.arange(0, HEAD_DIM, 2, dtype=np.float64) / HEAD_DIM)
    ang = pos[:, None] * inv_freq[None, :]
    cos, sin = np.cos(ang), np.sin(ang)
    reps = CB // HEAD_DIM
    cos_t = np.tile(np.concatenate([cos, cos], axis=-1), (1, reps))
    sin_t = np.tile(np.concatenate([-sin, sin], axis=-1), (1, reps))

    def reorder(t, dil):
        return t.reshape(seq // tm, tm // dil, dil, CB).transpose(0, 2, 1, 3).reshape(seq, CB)

    dils = [dil for _, dil in DIL_GROUPS]
    return (jnp.asarray(np.stack([reorder(cos_t, dil) for dil in dils]).astype(np.float32)),
            jnp.asarray(np.stack([reorder(sin_t, dil) for dil in dils]).astype(np.float32)))


def kernel(x, w_in, b_in, sinks, w_proj_a, w_proj_b, w_out, g_mix, g_ffn, w_router_group, b_router_group,
           w_router_expert, b_router_expert, w_exp_gate, w_exp_up, w_exp_down, g_final):
    batch, seq, d = x.shape
    assert d == D_MODEL and w_in.shape[0] == 1, "single-layer kernel"
    n_tok = batch * seq
    x2 = x.reshape(n_tok, d)
    tm_in = 512
    cos_t, sin_t = _rope_tables(seq, tm_in)

    z_tok, z_d1, z_d2 = _in_proj(x2, g_mix[0][None, :], w_in[0].astype(BF16), b_in[0][None, :],
                                 cos_t, sin_t, seq=seq, tm=tm_in)

    a_bases = (0, A_BLOCKS, 2 * A_BLOCKS)
    tok_bases = tuple(ZB_A0 + b for b in a_bases)
    oa, lse = [], []
    for group, (z, bases) in enumerate(((z_tok, tok_bases), (z_d1, a_bases), (z_d2, a_bases))):
        o_g, l_g = _dilated_attention(z, bases, group, batch=batch, seq=seq, lq=ATTN_ROWS_PER_STEP)
        oa.append(o_g)
        lse.append(l_g)
    ob = _swa_attention(z_tok, sinks[0], batch=batch, seq=seq, lq=ATTN_ROWS_PER_STEP)

    gap = EXPERT_LANE0 - N_EXPERT_GROUPS
    tail = ROUTER_LANES - EXPERT_LANE0 - N_EXPERTS
    w_router = jnp.concatenate(
        [w_router_group[0].T, jnp.zeros((gap, d), F32), w_router_expert[0].T, jnp.zeros((tail, d), F32)], axis=0)
    b_router = jnp.concatenate(
        [b_router_group[0], jnp.zeros((gap,), F32), b_router_expert[0], jnp.zeros((tail,), F32)])[:, None]
    x1, h2, route, cnt_tiles = _post_attn(
        oa, lse, ob, z_tok, x2, w_proj_a[0].astype(BF16), w_proj_b[0].astype(BF16), w_out[0].astype(BF16),
        g_ffn[0][None, :], w_router, b_router, tm=MOE_TILE)

    out = _moe(h2, route, cnt_tiles, x1, w_exp_gate[0], w_exp_up[0], w_exp_down[0], g_final[None, :])
    return out.reshape(batch, seq, d)
```

```python
import functools
import math

import jax
import jax.numpy as jnp
import numpy as np
from jax import lax
from jax.experimental import pallas as pl
from jax.experimental.pallas import tpu as pltpu

F32 = jnp.float32
BF16 = jnp.bfloat16

D_MODEL = 1024
HEAD_DIM = 64
HALF = HEAD_DIM // 2
ROPE_THETA = 10000.0
RMS_EPS = 1e-6
LOG2E = math.log2(math.e)
LN2 = math.log(2.0)
Q_SCALE = LOG2E * HEAD_DIM ** -0.5
BLOCK = 128
ATTN_ROWS_PER_STEP = 2048
DIL_GROUPS = ((128, 1), (512, 4), (2048, 16))
N_DIL = len(DIL_GROUPS)
A_GROUP_WIDTH = 512
A_QKV_WIDTH = 3 * N_DIL * A_GROUP_WIDTH
B_Q_HEADS = 16
B_KV_HEADS = 2
B_Q_WIDTH = B_Q_HEADS * HEAD_DIM
B_WINDOW = 128
GATE_WIDTH = 2 * D_MODEL
IN_WIDTH = A_QKV_WIDTH + B_Q_WIDTH + 2 * B_KV_HEADS * HEAD_DIM + GATE_WIDTH
N_EXPERT_GROUPS = 4
EXPERTS_PER_GROUP = 8
N_EXPERTS = N_EXPERT_GROUPS * EXPERTS_PER_GROUP
D_EXPERT = D_MODEL // 4

CB = 256
PAIR = 128
N_IN_BLOCKS = IN_WIDTH // CB
A_BLOCKS = A_GROUP_WIDTH // CB
ZB_GATE = 0
ZB_QB = 8
ZB_KB = 12
ZB_VB = 13
ZB_A0 = 14
N_TOK_BLOCKS = ZB_A0 + 3 * A_BLOCKS
OUT_PERM_BLOCK = 256
LSE_PERM_BLOCK = 128
ROUTER_LANES = 128
EXPERT_LANE0 = EXPERTS_PER_GROUP
assert N_EXPERT_GROUPS <= EXPERTS_PER_GROUP
MOE_TILE = 512
CHUNK = 16
EXP_TILE = 512
LOCAL_ROWS = -(-(2 * MOE_TILE + N_EXPERTS * (CHUNK - 1)) // CB) * CB
LOCAL_CHUNKS = LOCAL_ROWS // CHUNK
SEL_BLOCK = CB
ISSUE_UNROLL = 4
WAIT_GROUP = 8
XS_WIDTH = D_MODEL + ROUTER_LANES
META_PIECES = 3
assert META_PIECES * N_EXPERTS <= ROUTER_LANES

VMEM_LIMIT = 56 * 1024 * 1024


def _in_proj_plan():
    plan = []
    for c in range(N_IN_BLOCKS):
        col = c * CB
        if col < A_QKV_WIDTH:
            part, rem = divmod(col, N_DIL * A_GROUP_WIDTH)
            group, blk = divmod(rem // CB, A_BLOCKS)
            kind = ("q", "k", "v")[part]
            dil = DIL_GROUPS[group][1]
            if dil == 1:
                plan.append((0, ZB_A0 + part * A_BLOCKS + blk, kind, 1))
            else:
                plan.append((group, part * A_BLOCKS + blk, kind, dil))
        elif col < A_QKV_WIDTH + B_Q_WIDTH:
            plan.append((0, ZB_QB + (col - A_QKV_WIDTH) // CB, "q", 1))
        elif col < A_QKV_WIDTH + B_Q_WIDTH + CB:
            plan.append((0, -1, "kvb", 1))
        else:
            plan.append((0, ZB_GATE + (col - (A_QKV_WIDTH + B_Q_WIDTH + CB)) // CB, "v", 1))
    return tuple(plan)


def _rms(x, g):
    return x * lax.rsqrt(jnp.mean(x * x, axis=-1, keepdims=True) + RMS_EPS) * g


def _split3(w):
    hi = w.astype(BF16).astype(F32)
    mid = (w - hi).astype(BF16).astype(F32)
    lo = (w - hi - mid).astype(BF16).astype(F32)
    return hi, mid, lo


def _rope(acc, cos, sin_signed, first_half):
    partner = jnp.where(first_half, pltpu.roll(acc, CB - HALF, 1), pltpu.roll(acc, HALF, 1))
    return acc * cos + partner * sin_signed


def _in_perm_block(dil):
    return max(BLOCK, CHUNK * dil)


def _in_proj_kernel(x_ref, g_ref, w_ref, b_ref, cos_ref, sin_ref, perm1_ref, perm2_ref,
                    zt_ref, zd1_ref, zd2_ref, *, plan, tm):
    out_refs = (zt_ref, zd1_ref, zd2_ref)
    perm_refs = (None, perm1_ref, perm2_ref)
    lane = lax.broadcasted_iota(jnp.int32, (tm, CB), 1)
    first_half = (lane % HEAD_DIM) < HALF

    h = _rms(x_ref[...], g_ref[...]).astype(BF16)
    h_by_dil, tables = {}, {}
    for slot, (_, dil) in enumerate(DIL_GROUPS):
        tables[dil] = slot
        if dil == 1:
            h_by_dil[dil] = h
            continue
        blk = _in_perm_block(dil)
        n = blk // dil
        moved = [jnp.dot(perm_refs[slot][...], h[tb * blk:(tb + 1) * blk], preferred_element_type=F32
                         ).astype(BF16) for tb in range(tm // blk)]
        h_by_dil[dil] = jnp.concatenate(
            [part[r * n:(r + 1) * n] for r in range(dil) for part in moved], axis=0)

    for c, (arr, dst, kind, dil) in enumerate(plan):
        cols = slice(c * CB, (c + 1) * CB)
        acc = jnp.dot(h_by_dil[dil], w_ref[:, cols], preferred_element_type=F32) + b_ref[:, cols]
        if kind in ("q", "k", "kvb"):
            slot = tables[dil]
            rot = _rope(acc, cos_ref[slot], sin_ref[slot], first_half)
        if kind == "q":
            val = (rot * Q_SCALE).astype(BF16)
        elif kind == "k":
            val = rot.astype(BF16)
        elif kind == "v":
            val = acc.astype(BF16)
        else:
            r64 = pltpu.roll(rot, HEAD_DIM, 1)
            r128 = pltpu.roll(rot, 2 * HEAD_DIM, 1)
            kdup = jnp.where(lane < HEAD_DIM, rot, jnp.where(lane < 3 * HEAD_DIM, r64, r128))
            a128 = pltpu.roll(acc, 2 * HEAD_DIM, 1)
            a192 = pltpu.roll(acc, 3 * HEAD_DIM, 1)
            vdup = jnp.where(lane < HEAD_DIM, a128, jnp.where(lane < 3 * HEAD_DIM, a192, acc))
            zt_ref[ZB_KB] = kdup.astype(BF16)
            zt_ref[ZB_VB] = vdup.astype(BF16)
            continue
        if dil == 1:
            out_refs[arr][dst] = val
        else:
            n = tm // dil
            for r in range(dil):
                out_refs[arr][dst, :, r * CB:(r + 1) * CB] = val[r * n:(r + 1) * n, :]


def _dilation_perm(block, dil, dtype):
    j = np.arange(block)
    src = (j % (block // dil)) * dil + j // (block // dil)
    return jnp.asarray((src[:, None] == j[None, :]).astype(np.float32), dtype=dtype)


def _in_proj(x2, g_mix, w_in, b_in, cos_t, sin_t, *, seq, tm):
    n_tok = x2.shape[0]
    tiles_per_seq = seq // tm
    const = lambda i: (0, 0)
    table = pl.BlockSpec((N_DIL, tm, CB), lambda i: (0, i % tiles_per_seq, 0))
    d1, d2 = DIL_GROUPS[1][1], DIL_GROUPS[2][1]
    perms = [_dilation_perm(_in_perm_block(d), d, BF16) for d in (d1, d2)]
    perm_spec = lambda d: pl.BlockSpec((_in_perm_block(d),) * 2, const, pipeline_mode=pl.Buffered(1))
    return pl.pallas_call(
        functools.partial(_in_proj_kernel, plan=_in_proj_plan(), tm=tm),
        grid=(n_tok // tm,),
        in_specs=[
            pl.BlockSpec((tm, D_MODEL), lambda i: (i, 0)),
            pl.BlockSpec((1, D_MODEL), const),
            pl.BlockSpec((D_MODEL, IN_WIDTH), const, pipeline_mode=pl.Buffered(1)),
            pl.BlockSpec((1, IN_WIDTH), const),
            table, table,
            perm_spec(d1), perm_spec(d2),
        ],
        out_specs=[
            pl.BlockSpec((N_TOK_BLOCKS, tm, CB), lambda i: (0, i, 0)),
            pl.BlockSpec((3 * A_BLOCKS, tm // d1, d1 * CB), lambda i: (0, i, 0)),
            pl.BlockSpec((3 * A_BLOCKS, tm // d2, d2 * CB), lambda i: (0, i, 0)),
        ],
        out_shape=[
            jax.ShapeDtypeStruct((N_TOK_BLOCKS, n_tok, CB), BF16),
            jax.ShapeDtypeStruct((3 * A_BLOCKS, n_tok // d1, d1 * CB), BF16),
            jax.ShapeDtypeStruct((3 * A_BLOCKS, n_tok // d2, d2 * CB), BF16),
        ],
        compiler_params=pltpu.CompilerParams(
            dimension_semantics=("arbitrary",), vmem_limit_bytes=VMEM_LIMIT),
        name="in_proj",
    )(x2, g_mix, w_in, b_in, cos_t, sin_t, *perms)


def _attn_kernel(*refs, lq, max_dist, kv_shared, has_sink, want_lse, n_axes):
    refs = list(refs)
    sink_ref = refs.pop(0) if has_sink else None
    q_ref, k_ref, v_ref, kp_ref, vp_ref = refs[:5]
    o_ref = refs[5]
    lse_ref = refs[6] if want_lse else None
    vaug_ref = refs[-1]
    n_qblk = q_ref.shape[0]
    n_kv_pairs = vaug_ref.shape[0]
    rows_kv = lq + BLOCK

    row = lax.broadcasted_iota(jnp.int32, (BLOCK, 2 * BLOCK), 0)
    col = lax.broadcasted_iota(jnp.int32, (BLOCK, 2 * BLOCK), 1)
    dist = row - col + BLOCK
    valid = (dist >= 0) & (dist <= max_dist)
    neg_inf = jnp.float32(-jnp.inf)
    bias = jnp.where(valid, 0.0, neg_inf)
    bias_first = jnp.where(valid & (col >= BLOCK), 0.0, neg_inf)
    bias0 = jnp.where(pl.program_id(n_axes - 1) == 0, bias_first, bias)
    lane_lo = lax.broadcasted_iota(jnp.int32, (BLOCK, PAIR), 1) < HEAD_DIM

    def rd(ref, blk, rows, cols):
        return ref[rows, cols] if kv_shared else ref[blk, rows, cols]

    first_step = functools.reduce(lambda a, b: a & b, [pl.program_id(a) == 0 for a in range(n_axes)])

    @pl.when(first_step)
    def _():
        lane = lax.broadcasted_iota(jnp.int32, (rows_kv, PAIR), 1)
        for pair in range(n_kv_pairs):
            vaug_ref[pair, 0, :, PAIR:] = jnp.where(lane < HEAD_DIM, 1.0, 0.0).astype(BF16)
            vaug_ref[pair, 1, :, PAIR:] = jnp.where(lane < HEAD_DIM, 0.0, 1.0).astype(BF16)

    lane_kv = lax.broadcasted_iota(jnp.int32, (rows_kv, PAIR), 1) < HEAD_DIM
    n_col_pairs = q_ref.shape[-1] // PAIR
    for pair in range(n_kv_pairs):
        cols = slice((pair % n_col_pairs) * PAIR, (pair % n_col_pairs + 1) * PAIR)
        everything = slice(None)
        v_all = jnp.concatenate([rd(vp_ref, pair // n_col_pairs, everything, cols),
                                 rd(v_ref, pair // n_col_pairs, everything, cols)], axis=0)
        zero = jnp.zeros_like(v_all)
        vaug_ref[pair, 0, :, :PAIR] = jnp.where(lane_kv, v_all, zero)
        vaug_ref[pair, 1, :, :PAIR] = jnp.where(lane_kv, zero, v_all)

    for ib in range(lq // BLOCK):
        rows = slice(ib * BLOCK, (ib + 1) * BLOCK)
        win = slice(ib * BLOCK, (ib + 2) * BLOCK)
        b_ib = bias0 if ib == 0 else bias
        for blk in range(n_qblk):
            for pp in range(n_col_pairs):
                qcols = slice(pp * PAIR, (pp + 1) * PAIR)
                pair = 0 if kv_shared else blk * n_col_pairs + pp
                kcols = slice(0, PAIR) if kv_shared else qcols
                if ib == 0:
                    k_win = jnp.concatenate([rd(kp_ref, blk, slice(None), kcols),
                                             rd(k_ref, blk, slice(0, BLOCK), kcols)], axis=0)
                else:
                    k_win = rd(k_ref, blk, slice((ib - 1) * BLOCK, (ib + 1) * BLOCK), kcols)
                q_pair = q_ref[blk, rows, qcols]
                ps, ms, sink_terms = [], [], []
                for hh in range(2):
                    q_h = jnp.where(lane_lo == (hh == 0), q_pair, jnp.zeros_like(q_pair))
                    s = lax.dot_general(q_h, k_win, (((1,), (1,)), ((), ())), preferred_element_type=F32)
                    s = s + b_ib
                    m = jnp.max(s, axis=-1, keepdims=True)
                    if has_sink:
                        head = (pl.program_id(1) * n_qblk + blk) * (CB // HEAD_DIM) + pp * 2 + hh
                        sink = sink_ref[head] * LOG2E
                        m = jnp.maximum(m, sink)
                        sink_terms.append(jnp.exp2(sink - m))
                    ps.append(jnp.exp2(s - m).astype(BF16))
                    ms.append(m)
                v_aug = jnp.concatenate([vaug_ref[pair, 0, win, :], vaug_ref[pair, 1, win, :]], axis=0)
                od = jnp.dot(jnp.concatenate(ps, axis=1), v_aug, preferred_element_type=F32)
                den = od[:, PAIR:]
                if has_sink:
                    den = den + jnp.where(lane_lo, sink_terms[0], sink_terms[1])
                o_ref[blk, rows, qcols] = (od[:, :PAIR] * (1.0 / den)).astype(BF16)
                if want_lse:
                    lse_ref[blk, rows, qcols] = (jnp.where(lane_lo, ms[0], ms[1]) + jnp.log2(den)) * LN2


def _dilated_attention(z, bases, group, *, batch, seq, lq):
    window, dil = DIL_GROUPS[group]
    sub_len = seq // dil
    lq = min(lq, sub_len)
    z4 = z.reshape(z.shape[0], batch, sub_len, dil * CB)
    qb, kb, vb = bases
    assert all(base % A_BLOCKS == 0 for base in bases)
    bpq = lq // BLOCK
    slabs = max(1, min(dil, ATTN_ROWS_PER_STEP // lq))
    width = slabs * CB
    cur = lambda base: pl.BlockSpec(
        (A_BLOCKS, None, lq, width), lambda b, r, i: (base // A_BLOCKS, b, i, r))
    prev = lambda base: pl.BlockSpec(
        (A_BLOCKS, None, BLOCK, width), lambda b, r, i: (base // A_BLOCKS, b, jnp.maximum(i * bpq - 1, 0), r))
    out_spec = pl.BlockSpec((A_BLOCKS, None, lq, width), lambda b, r, i: (0, b, i, r))
    o, lse = pl.pallas_call(
        functools.partial(_attn_kernel, lq=lq, max_dist=window // dil, kv_shared=False,
                          has_sink=False, want_lse=True, n_axes=3),
        grid=(batch, dil // slabs, sub_len // lq),
        in_specs=[cur(qb), cur(kb), cur(vb), prev(kb), prev(vb)],
        out_specs=[out_spec, out_spec],
        out_shape=[jax.ShapeDtypeStruct((A_BLOCKS, batch, sub_len, dil * CB), BF16),
                   jax.ShapeDtypeStruct((A_BLOCKS, batch, sub_len, dil * CB), F32)],
        scratch_shapes=[pltpu.VMEM((2 * A_BLOCKS * slabs, 2, lq + BLOCK, 2 * PAIR), BF16)],
        compiler_params=pltpu.CompilerParams(
            dimension_semantics=("arbitrary",) * 3, vmem_limit_bytes=VMEM_LIMIT),
        name=f"dilated_attn_g{group}",
    )(z4, z4, z4, z4, z4)
    rows = batch * sub_len
    return o.reshape(A_BLOCKS, rows, dil * CB), lse.reshape(A_BLOCKS, rows, dil * CB)


def _swa_attention(z_tok, sinks, *, batch, seq, lq):
    z4 = z_tok.reshape(N_TOK_BLOCKS, batch, seq, CB)
    bpq = lq // BLOCK
    n_q_blocks = B_Q_WIDTH // CB
    q_per_kv = n_q_blocks // B_KV_HEADS
    assert ZB_QB % q_per_kv == 0
    q_spec = pl.BlockSpec((q_per_kv, None, lq, CB), lambda b, kvh, i, s: (ZB_QB // q_per_kv + kvh, b, i, 0))
    cur = lambda base: pl.BlockSpec((None, None, lq, PAIR), lambda b, kvh, i, s: (base, b, i, kvh))
    prev = lambda base: pl.BlockSpec(
        (None, None, BLOCK, PAIR), lambda b, kvh, i, s: (base, b, jnp.maximum(i * bpq - 1, 0), kvh))
    o = pl.pallas_call(
        functools.partial(_attn_kernel, lq=lq, max_dist=B_WINDOW - 1, kv_shared=True,
                          has_sink=True, want_lse=False, n_axes=3),
        grid_spec=pltpu.PrefetchScalarGridSpec(
            num_scalar_prefetch=1,
            grid=(batch, B_KV_HEADS, seq // lq),
            in_specs=[q_spec, cur(ZB_KB), cur(ZB_VB), prev(ZB_KB), prev(ZB_VB)],
            out_specs=pl.BlockSpec((q_per_kv, None, lq, CB), lambda b, kvh, i, s: (kvh, b, i, 0)),
            scratch_shapes=[pltpu.VMEM((1, 2, lq + BLOCK, 2 * PAIR), BF16)],
        ),
        out_shape=jax.ShapeDtypeStruct((n_q_blocks, batch, seq, CB), BF16),
        compiler_params=pltpu.CompilerParams(
            dimension_semantics=("arbitrary",) * 3, vmem_limit_bytes=VMEM_LIMIT),
        name="swa_attn",
    )(sinks, z4, z4, z4, z4, z4)
    return o.reshape(n_q_blocks, batch * seq, CB)


def _post_attn_kernel(o0_ref, o1_ref, o2_ref, l0_ref, l1_ref, l2_ref, ob_ref, gate_ref, x_ref,
                      wa_ref, wb_ref, wo_ref, gf_ref, wr_ref, br_ref,
                      po_ref, pl_ref, x1_ref, h2_ref, route_ref, cnt_ref, *, tm):

    def to_token_order(ref, cb, slot, perm_ref, blk):
        dil = DIL_GROUPS[slot + 1][1]
        n = blk // dil
        parts = []
        for tb in range(tm // blk):
            stack = jnp.concatenate(
                [ref[cb, tb * n:(tb + 1) * n, r * CB:(r + 1) * CB] for r in range(dil)], axis=0)
            pieces = (stack,) if stack.dtype == BF16 else _split3(stack)
            moved = [jnp.dot(perm_ref[slot], p.astype(BF16), preferred_element_type=F32) for p in pieces]
            parts.append(functools.reduce(lambda a, b: a + b, moved))
        return jnp.concatenate(parts, axis=0)

    pa = None
    for cb in range(A_BLOCKS):
        l0 = l0_ref[cb]
        l1 = to_token_order(l1_ref, cb, 0, pl_ref, LSE_PERM_BLOCK)
        l2 = to_token_order(l2_ref, cb, 1, pl_ref, LSE_PERM_BLOCK)
        o1 = to_token_order(o1_ref, cb, 0, po_ref, OUT_PERM_BLOCK)
        o2 = to_token_order(o2_ref, cb, 1, po_ref, OUT_PERM_BLOCK)
        mx = jnp.maximum(jnp.maximum(l0, l1), l2)
        e0, e1, e2 = jnp.exp(l0 - mx), jnp.exp(l1 - mx), jnp.exp(l2 - mx)
        inv = 1.0 / (e0 + e1 + e2)
        ya = (e0 * inv) * o0_ref[cb].astype(F32) + (e1 * inv) * o1 + (e2 * inv) * o2
        part = jnp.dot(ya.astype(BF16), wa_ref[cb * CB:(cb + 1) * CB, :], preferred_element_type=F32)
        pa = part if pa is None else pa + part
    pb = None
    for cb in range(B_Q_WIDTH // CB):
        part = jnp.dot(ob_ref[cb], wb_ref[cb * CB:(cb + 1) * CB, :], preferred_element_type=F32)
        pb = part if pb is None else pb + part
    n_gate = D_MODEL // CB
    merged = []
    for cb in range(n_gate):
        cols = slice(cb * CB, (cb + 1) * CB)
        ga = jax.nn.sigmoid(gate_ref[cb].astype(F32))
        gb = jax.nn.sigmoid(gate_ref[n_gate + cb].astype(F32))
        merged.append((ga * pa[:, cols] + gb * pb[:, cols]).astype(BF16))
    merged = jnp.concatenate(merged, axis=1)
    x1 = x_ref[...] + jnp.dot(merged, wo_ref[...], preferred_element_type=F32)
    x1_ref[...] = x1
    h2 = _rms(x1, gf_ref[...])
    h2_ref[...] = h2.astype(BF16)

    h2_hi = h2.astype(BF16)
    h2_lo = (h2 - h2_hi.astype(F32)).astype(BF16)
    wr = wr_ref[...]
    wr_hi = wr.astype(BF16)
    wr_lo = (wr - wr_hi.astype(F32)).astype(BF16)
    nt = (((1,), (1,)), ((), ()))
    hi_terms = lax.dot_general(jnp.concatenate([wr_hi, wr_lo], axis=0), h2_hi, nt, preferred_element_type=F32)
    logits = (hi_terms[:ROUTER_LANES]
              + (lax.dot_general(wr_hi, h2_lo, nt, preferred_element_type=F32) + hi_terms[ROUTER_LANES:])
              ) + br_ref[...]
    sub = lax.broadcasted_iota(jnp.int32, (EXPERTS_PER_GROUP, tm), 0)
    neg_inf = jnp.float32(-jnp.inf)
    top = lambda v: jnp.max(v, axis=0, keepdims=True)
    first = lambda hit: jnp.min(jnp.where(hit, sub, EXPERTS_PER_GROUP), axis=0, keepdims=True)
    lg = jnp.where(sub < N_EXPERT_GROUPS, logits[:EXPERTS_PER_GROUP], neg_inf)
    mg = top(lg)
    gsel = first(lg == mg)
    pg_sel = 1.0 / jnp.sum(jnp.exp(lg - mg), axis=0, keepdims=True)
    le = logits[EXPERT_LANE0:EXPERT_LANE0 + EXPERTS_PER_GROUP]
    for g in range(1, N_EXPERT_GROUPS):
        r0 = EXPERT_LANE0 + g * EXPERTS_PER_GROUP
        le = jnp.where(gsel == g, logits[r0:r0 + EXPERTS_PER_GROUP], le)
    ex = jnp.exp(le - top(le))
    pe = ex / jnp.sum(ex, axis=0, keepdims=True)
    p1 = top(pe)
    i1 = first(pe == p1)
    rest = sub != i1
    p2 = top(jnp.where(rest, pe, -1.0))
    i2 = first(rest & (pe == p2))
    norm = pg_sel / (p1 + p2)
    e1 = gsel * EXPERTS_PER_GROUP + i1
    e2 = gsel * EXPERTS_PER_GROUP + i2
    row = lax.broadcasted_iota(jnp.int32, (ROUTER_LANES, tm), 0)
    route_t = jnp.where(
        row == 0, e1.astype(F32),
        jnp.where(row == 1, e2.astype(F32),
                  jnp.where(row == 2, p1 * norm, jnp.where(row == 3, p2 * norm, 0.0))))
    route_ref[...] = route_t.T
    hits_t = jnp.where((row == e1 + EXPERT_LANE0) | (row == e2 + EXPERT_LANE0), 1.0, 0.0).astype(BF16)
    cnt_ref[0] = lax.dot_general(jnp.ones((cnt_ref.shape[1], tm), BF16), hits_t, nt,
                                 preferred_element_type=F32)


def _post_attn(oa, lse, ob, z_tok, x2, w_proj_a, w_proj_b, w_out, g_ffn, w_router, b_router, *, tm):
    n_tok = x2.shape[0]
    const2 = lambda i: (0, 0)
    blk = lambda nb: pl.BlockSpec((nb, tm, CB), lambda i: (0, i, 0))
    dil_blk = lambda dil: pl.BlockSpec((A_BLOCKS, tm // dil, dil * CB), lambda i: (0, i, 0))
    row = lambda width: pl.BlockSpec((tm, width), lambda i: (i, 0))
    resident = lambda shape: pl.BlockSpec(shape, const2, pipeline_mode=pl.Buffered(1))
    d1, d2 = DIL_GROUPS[1][1], DIL_GROUPS[2][1]
    perm_o = jnp.stack([_dilation_perm(OUT_PERM_BLOCK, d, BF16).T for d in (d1, d2)])
    perm_l = jnp.stack([_dilation_perm(LSE_PERM_BLOCK, d, BF16).T for d in (d1, d2)])
    return pl.pallas_call(
        functools.partial(_post_attn_kernel, tm=tm),
        grid=(n_tok // tm,),
        in_specs=[blk(A_BLOCKS), dil_blk(d1), dil_blk(d2), blk(A_BLOCKS), dil_blk(d1), dil_blk(d2),
                  blk(B_Q_WIDTH // CB), blk(GATE_WIDTH // CB), row(D_MODEL),
                  resident((A_GROUP_WIDTH, D_MODEL)), resident((B_Q_WIDTH, D_MODEL)),
                  resident((D_MODEL, D_MODEL)), resident((1, D_MODEL)),
                  resident((ROUTER_LANES, D_MODEL)), resident((ROUTER_LANES, 1)),
                  pl.BlockSpec((2, OUT_PERM_BLOCK, OUT_PERM_BLOCK), lambda i: (0, 0, 0),
                               pipeline_mode=pl.Buffered(1)),
                  pl.BlockSpec((2, LSE_PERM_BLOCK, LSE_PERM_BLOCK), lambda i: (0, 0, 0),
                               pipeline_mode=pl.Buffered(1))],
        out_specs=[row(D_MODEL), row(D_MODEL), row(ROUTER_LANES),
                   pl.BlockSpec((1, 8, ROUTER_LANES), lambda i: (i, 0, 0))],
        out_shape=[jax.ShapeDtypeStruct((n_tok, D_MODEL), F32),
                   jax.ShapeDtypeStruct((n_tok, D_MODEL), BF16),
                   jax.ShapeDtypeStruct((n_tok, ROUTER_LANES), F32),
                   jax.ShapeDtypeStruct((n_tok // tm, 8, ROUTER_LANES), F32)],
        compiler_params=pltpu.CompilerParams(
            dimension_semantics=("arbitrary",), vmem_limit_bytes=VMEM_LIMIT),
        name="post_attn",
    )(oa[0], oa[1], oa[2], lse[0], lse[1], lse[2], ob, z_tok, x2,
      w_proj_a, w_proj_b, w_out, g_ffn, w_router, b_router, perm_o, perm_l)


def _local_slots(route, lbase_row, ltri):
    lane = lax.broadcasted_iota(jnp.int32, route.shape, 1)
    pick = lambda k: jnp.sum(jnp.where(lane == k, route, 0.0), axis=-1, keepdims=True)
    lanef = lane.astype(F32)
    oh1, oh2 = lanef == pick(0), lanef == pick(1)
    oh = jnp.where(oh1 | oh2, 1.0, 0.0).astype(BF16)
    table = jnp.dot(ltri, oh, preferred_element_type=F32) + lbase_row
    ls1 = jnp.sum(jnp.where(oh1, table, 0.0), axis=-1, keepdims=True)
    ls2 = jnp.sum(jnp.where(oh2, table, 0.0), axis=-1, keepdims=True)
    return ls1, ls2, pick


def _chunk_loop(count, body, unroll=1):
    one = lambda c, carry: (body(c), carry)[1]
    full = 0
    if unroll > 1:
        full = (count // unroll) * unroll

        def group(g, carry):
            for u in range(unroll):
                body(g * unroll + u)
            return carry

        lax.fori_loop(0, count // unroll, group, 0)
    lax.fori_loop(full, count, one, 0)


def _wait_chunks(count, wait_rows):
    _chunk_loop(count // WAIT_GROUP, lambda c: wait_rows(WAIT_GROUP * CHUNK))
    _chunk_loop(count % WAIT_GROUP, lambda c: wait_rows(CHUNK))


def _selection_blocks(slots_a, slots_b, axis, other):
    shape = (SEL_BLOCK, other) if axis == 0 else (other, SEL_BLOCK)
    local = lax.broadcasted_iota(jnp.int32, shape, axis).astype(F32).astype(BF16)
    one, zero = jnp.ones_like(local), jnp.zeros_like(local)

    def block(a):
        rel_a = (slots_a - float(a * SEL_BLOCK)).astype(BF16)
        rel_b = (slots_b - float(a * SEL_BLOCK)).astype(BF16)
        return jnp.where(local == rel_a, one, jnp.where(local == rel_b, one, zero))

    return block


def _for_used_blocks(used_rows, body):
    always = 2 * MOE_TILE // SEL_BLOCK + 1
    body(0, always, True)
    for a in range(always, LOCAL_ROWS // SEL_BLOCK):
        pl.when(used_rows > a * SEL_BLOCK)(functools.partial(body, a, a + 1, False))


def _dispatch_kernel(cdst_ref, tot_ref, tail_start_ref, tail_n_ref,
                     h2_ref, route_ref, lbase_ref, ltri_ref, xs_ref, buf_ref, zero_ref, sem_ref, zsem_ref,
                     *, tm):
    i = pl.program_id(0)
    last = pl.num_programs(0) - 1
    slot = i % 2

    def chunk(slot_, src_row, dst_row, rows=CHUNK):
        return pltpu.make_async_copy(buf_ref.at[slot_, pl.ds(src_row, rows), :],
                                     xs_ref.at[pl.ds(dst_row, rows), :], sem_ref.at[slot_])

    def wait_tile(slot_, tile):
        _wait_chunks(tot_ref[tile], lambda rows: chunk(slot_, 0, 0, rows).wait())

    @pl.when(i == 0)
    def _():
        zero_ref[...] = jnp.zeros_like(zero_ref)

        def zero_chunk(row):
            return pltpu.make_async_copy(zero_ref, xs_ref.at[pl.ds(row, CHUNK), :], zsem_ref)

        def per_expert(e, total):
            start = tail_start_ref[e] * CHUNK
            _chunk_loop(tail_n_ref[e], lambda c: zero_chunk(pl.multiple_of(start + c * CHUNK, CHUNK)).start())
            return total + tail_n_ref[e]

        total = lax.fori_loop(0, N_EXPERTS, per_expert, 0)
        _chunk_loop(total, lambda c: zero_chunk(0).wait())

    @pl.when(i >= 2)
    def _():
        wait_tile(slot, i - 2)

    route = route_ref[...]
    ls1, ls2, pick = _local_slots(route, lbase_ref[0, 0:1, :], ltri_ref[...])
    lane = lax.broadcasted_iota(jnp.int32, route.shape, 1)
    ls_t = jnp.where(lane == 0, ls1, jnp.where(lane == 1, ls2, 0.0)).T
    sel_block = _selection_blocks(ls_t[0:1, :], ls_t[1:2, :], 0, tm)
    lane_expert = (lane // META_PIECES).astype(F32)
    lane_piece = lane % META_PIECES
    by_piece = lambda pieces: jnp.where(lane_piece == 0, pieces[0],
                                        jnp.where(lane_piece == 1, pieces[1], pieces[2]))
    meta = jnp.where(lane_expert == pick(0), by_piece(_split3(pick(2))),
                     jnp.where(lane_expert == pick(1), by_piece(_split3(pick(3))), 0.0))
    meta = meta.astype(BF16)

    def sort_block(a0, a1, first):
        del first
        rows = slice(a0 * SEL_BLOCK, a1 * SEL_BLOCK)
        sel = jnp.concatenate([sel_block(a) for a in range(a0, a1)], axis=0)
        buf_ref[slot, rows, :D_MODEL] = jnp.dot(sel, h2_ref[...], preferred_element_type=F32).astype(BF16)
        buf_ref[slot, rows, D_MODEL:] = jnp.dot(sel, meta, preferred_element_type=F32).astype(BF16)

    _for_used_blocks(tot_ref[i] * CHUNK, sort_block)

    _chunk_loop(tot_ref[i], lambda c: chunk(
        slot, pl.multiple_of(c * CHUNK, CHUNK),
        pl.multiple_of(cdst_ref[i * LOCAL_CHUNKS + c] * CHUNK, CHUNK)).start(), unroll=ISSUE_UNROLL)

    @pl.when(i == last)
    def _():
        @pl.when(i >= 1)
        def _():
            wait_tile(1 - slot, i - 1)
        wait_tile(slot, i)


def _expert_kernel(te_ref, nv_ref, xs_ref, wg_ref, wu_ref, wd_ref, ys_ref, wgb_ref, wub_ref, wdb_ref):
    g = pl.program_id(0)
    e = te_ref[g]

    @pl.when((g == 0) | (te_ref[jnp.maximum(g - 1, 0)] != e))
    def _():
        wgb_ref[...] = wg_ref[...].astype(BF16)
        wub_ref[...] = wu_ref[...].astype(BF16)
        wdb_ref[...] = wd_ref[...].astype(BF16)

    @pl.when(g < nv_ref[0])
    def _():
        x = xs_ref[:, :D_MODEL]
        meta = xs_ref[:, D_MODEL:].astype(F32)
        lane = lax.broadcasted_iota(jnp.int32, meta.shape, 1)
        mine = (lane >= e * META_PIECES) & (lane < (e + 1) * META_PIECES)
        w = jnp.sum(jnp.where(mine, meta, 0.0), axis=-1, keepdims=True)
        hg = jnp.dot(x, wgb_ref[...], preferred_element_type=F32)
        hu = jnp.dot(x, wub_ref[...], preferred_element_type=F32)
        a = (hg * jax.nn.sigmoid(hg)) * hu * w
        ys_ref[...] = jnp.dot(a.astype(BF16), wdb_ref[...], preferred_element_type=F32).astype(BF16)


def _combine_kernel(cdst_ref, tot_ref,
                    x1_ref, route_ref, lbase_ref, ltri_ref, gfin_ref, ys_ref, out_ref, ybuf_ref, sem_ref,
                    *, tm):
    i = pl.program_id(0)
    n_tiles = pl.num_programs(0)
    slot = i % 2

    def chunk(slot_, src_row, dst_row, rows=CHUNK):
        return pltpu.make_async_copy(ys_ref.at[pl.ds(src_row, rows), :],
                                     ybuf_ref.at[slot_, pl.ds(dst_row, rows), :], sem_ref.at[slot_])

    def fetch_tile(tile, slot_):
        _chunk_loop(tot_ref[tile], lambda c: chunk(
            slot_, pl.multiple_of(cdst_ref[tile * LOCAL_CHUNKS + c] * CHUNK, CHUNK),
            pl.multiple_of(c * CHUNK, CHUNK)).start(), unroll=ISSUE_UNROLL)

    @pl.when(i == 0)
    def _():
        ybuf_ref[...] = jnp.zeros_like(ybuf_ref)
        fetch_tile(0, 0)

    @pl.when(i + 1 < n_tiles)
    def _():
        fetch_tile(i + 1, 1 - slot)

    _wait_chunks(tot_ref[i], lambda rows: chunk(slot, 0, 0, rows).wait())

    ls1, ls2, _ = _local_slots(route_ref[...], lbase_ref[0, 0:1, :], ltri_ref[...])
    sel_block = _selection_blocks(ls1, ls2, 1, tm)
    def add_block(a0, a1, first):
        rows = slice(a0 * SEL_BLOCK, a1 * SEL_BLOCK)
        sel = jnp.concatenate([sel_block(a) for a in range(a0, a1)], axis=1)
        y = jnp.dot(sel, ybuf_ref[slot, rows, :], preferred_element_type=F32)
        out_ref[...] = (x1_ref[...] if first else out_ref[...]) + y

    _for_used_blocks(tot_ref[i] * CHUNK, add_block)
    out_ref[...] = _rms(out_ref[...], gfin_ref[...])


def _routing_tables(cnt, n_exp_tiles):
    c16 = (cnt + CHUNK - 1) // CHUNK
    lbase = jnp.cumsum(c16, axis=1) - c16
    tile_off = jnp.cumsum(c16, axis=0) - c16
    tot = jnp.sum(c16, axis=0)
    per = EXP_TILE // CHUNK
    region_tiles = (tot + per - 1) // per
    region = region_tiles * per
    base = jnp.cumsum(region) - region
    dst = base[None, :] + tile_off
    tile_end = jnp.cumsum(region_tiles)
    n_valid = tile_end[-1]
    g = jnp.arange(n_exp_tiles, dtype=jnp.int32)
    tile_expert = jnp.sum(tile_end[None, :] <= jnp.minimum(g, n_valid - 1)[:, None], axis=1).astype(jnp.int32)
    c = jnp.arange(LOCAL_CHUNKS, dtype=jnp.int32)
    lend = lbase + c16
    owns = (lbase[:, None, :] <= c[None, :, None]) & (c[None, :, None] < lend[:, None, :])
    chunk_dst = jnp.sum(jnp.where(owns, (dst - lbase)[:, None, :], 0), axis=2) + c[None, :]
    i32 = lambda a: a.astype(jnp.int32).reshape(-1)
    return dict(chunk_dst=i32(chunk_dst), tot=i32(jnp.sum(c16, axis=1)),
                tail_start=i32(base + tot), tail_n=i32(region - tot),
                tile_expert=tile_expert, n_valid=i32(n_valid),
                lbase_rows=(lbase * CHUNK).astype(F32))


def _moe(h2, route, cnt_tiles, x1, w_gate, w_up, w_down, g_final):
    n_tok = h2.shape[0]
    tm = MOE_TILE
    n_tiles = n_tok // tm
    assert cnt_tiles.shape[0] == n_tiles
    worst_rows = 2 * n_tok + n_tiles * N_EXPERTS * (CHUNK - 1) + N_EXPERTS * (EXP_TILE - CHUNK)
    n_exp_tiles = -(-worst_rows // EXP_TILE)
    n_slots = n_exp_tiles * EXP_TILE

    cnt = cnt_tiles[:, 0, EXPERT_LANE0:EXPERT_LANE0 + N_EXPERTS].astype(jnp.int32)
    tb = _routing_tables(cnt, n_exp_tiles)
    lbase_rows = jnp.zeros((n_tiles, 8, ROUTER_LANES), F32).at[:, :, :N_EXPERTS].set(
        tb["lbase_rows"][:, None, :])
    row_id = np.arange(tm)
    ltri = jnp.asarray((row_id[:, None] > row_id[None, :]).astype(np.float32), dtype=BF16)

    tile_row = lambda width: pl.BlockSpec((tm, width), lambda i, *_: (i, 0))
    lbase_spec = pl.BlockSpec((1, 8, ROUTER_LANES), lambda i, *_: (i, 0, 0))
    ltri_spec = pl.BlockSpec((tm, tm), lambda i, *_: (0, 0), pipeline_mode=pl.Buffered(1))
    hbm = pl.BlockSpec(memory_space=pl.ANY)

    xs = pl.pallas_call(
        functools.partial(_dispatch_kernel, tm=tm),
        grid_spec=pltpu.PrefetchScalarGridSpec(
            num_scalar_prefetch=4, grid=(n_tiles,),
            in_specs=[tile_row(D_MODEL), tile_row(ROUTER_LANES), lbase_spec, ltri_spec],
            out_specs=hbm,
            scratch_shapes=[pltpu.VMEM((2, LOCAL_ROWS, XS_WIDTH), BF16), pltpu.VMEM((CHUNK, XS_WIDTH), BF16),
                            pltpu.SemaphoreType.DMA((2,)), pltpu.SemaphoreType.DMA]),
        out_shape=jax.ShapeDtypeStruct((n_slots, XS_WIDTH), BF16),
        compiler_params=pltpu.CompilerParams(
            dimension_semantics=("arbitrary",), vmem_limit_bytes=VMEM_LIMIT),
        name="moe_dispatch",
    )(tb["chunk_dst"], tb["tot"], tb["tail_start"], tb["tail_n"], h2, route, lbase_rows, ltri)

    row_tile = lambda width: pl.BlockSpec(
        (EXP_TILE, width), lambda g, te, nv: (jnp.maximum(jnp.minimum(g, nv[0] - 1), 0), 0))
    ys = pl.pallas_call(
        _expert_kernel,
        grid_spec=pltpu.PrefetchScalarGridSpec(
            num_scalar_prefetch=2, grid=(n_exp_tiles,),
            in_specs=[row_tile(XS_WIDTH),
                      pl.BlockSpec((None, D_MODEL, D_EXPERT), lambda g, te, nv: (te[g], 0, 0)),
                      pl.BlockSpec((None, D_MODEL, D_EXPERT), lambda g, te, nv: (te[g], 0, 0)),
                      pl.BlockSpec((None, D_EXPERT, D_MODEL), lambda g, te, nv: (te[g], 0, 0))],
            out_specs=row_tile(D_MODEL),
            scratch_shapes=[pltpu.VMEM((D_MODEL, D_EXPERT), BF16), pltpu.VMEM((D_MODEL, D_EXPERT), BF16),
                            pltpu.VMEM((D_EXPERT, D_MODEL), BF16)]),
        out_shape=jax.ShapeDtypeStruct((n_slots, D_MODEL), BF16),
        compiler_params=pltpu.CompilerParams(
            dimension_semantics=("arbitrary",), vmem_limit_bytes=VMEM_LIMIT),
        name="moe_expert",
    )(tb["tile_expert"], tb["n_valid"], xs, w_gate, w_up, w_down)

    return pl.pallas_call(
        functools.partial(_combine_kernel, tm=tm),
        grid_spec=pltpu.PrefetchScalarGridSpec(
            num_scalar_prefetch=2, grid=(n_tiles,),
            in_specs=[tile_row(D_MODEL), tile_row(ROUTER_LANES), lbase_spec, ltri_spec,
                      pl.BlockSpec((1, D_MODEL), lambda i, *_: (0, 0)), hbm],
            out_specs=tile_row(D_MODEL),
            scratch_shapes=[pltpu.VMEM((2, LOCAL_ROWS, D_MODEL), BF16), pltpu.SemaphoreType.DMA((2,))]),
        out_shape=jax.ShapeDtypeStruct((n_tok, D_MODEL), F32),
        compiler_params=pltpu.CompilerParams(
            dimension_semantics=("arbitrary",), vmem_limit_bytes=VMEM_LIMIT),
        name="moe_combine",
    )(tb["chunk_dst"], tb["tot"], x1, route, lbase_rows, ltri, g_final, ys)


def _rope_tables(seq, tm):
    pos = np.arange(seq, dtype=np.float64)
    inv_freq = ROPE_THETA ** (-np.arange(0, HEAD_DIM, 2, dtype=np.float64) / HEAD_DIM)
    ang = pos[:, None] * inv_freq[None, :]
    cos, sin = np.cos(ang), np.sin(ang)
    reps = CB // HEAD_DIM
    cos_t = np.tile(np.concatenate([cos, cos], axis=-1), (1, reps))
    sin_t = np.tile(np.concatenate([-sin, sin], axis=-1), (1, reps))

    def reorder(t, dil):
        return t.reshape(seq // tm, tm // dil, dil, CB).transpose(0, 2, 1, 3).reshape(seq, CB)

    dils = [dil for _, dil in DIL_GROUPS]
    return (jnp.asarray(np.stack([reorder(cos_t, dil) for dil in dils]).astype(np.float32)),
            jnp.asarray(np.stack([reorder(sin_t, dil) for dil in dils]).astype(np.float32)))


def kernel(x, w_in, b_in, sinks, w_proj_a, w_proj_b, w_out, g_mix, g_ffn, w_router_group, b_router_group,
           w_router_expert, b_router_expert, w_exp_gate, w_exp_up, w_exp_down, g_final):
    batch, seq, d = x.shape
    assert d == D_MODEL and w_in.shape[0] == 1, "single-layer kernel"
    n_tok = batch * seq
    x2 = x.reshape(n_tok, d)
    tm_in = 512
    cos_t, sin_t = _rope_tables(seq, tm_in)

    z_tok, z_d1, z_d2 = _in_proj(x2, g_mix[0][None, :], w_in[0].astype(BF16), b_in[0][None, :],
                                 cos_t, sin_t, seq=seq, tm=tm_in)

    a_bases = (0, A_BLOCKS, 2 * A_BLOCKS)
    tok_bases = tuple(ZB_A0 + b for b in a_bases)
    oa, lse = [], []
    for group, (z, bases) in enumerate(((z_tok, tok_bases), (z_d1, a_bases), (z_d2, a_bases))):
        o_g, l_g = _dilated_attention(z, bases, group, batch=batch, seq=seq, lq=ATTN_ROWS_PER_STEP)
        oa.append(o_g)
        lse.append(l_g)
    ob = _swa_attention(z_tok, sinks[0], batch=batch, seq=seq, lq=ATTN_ROWS_PER_STEP)

    gap = EXPERT_LANE0 - N_EXPERT_GROUPS
    tail = ROUTER_LANES - EXPERT_LANE0 - N_EXPERTS
    w_router = jnp.concatenate(
        [w_router_group[0].T, jnp.zeros((gap, d), F32), w_router_expert[0].T, jnp.zeros((tail, d), F32)], axis=0)
    b_router = jnp.concatenate(
        [b_router_group[0], jnp.zeros((gap,), F32), b_router_expert[0], jnp.zeros((tail,), F32)])[:, None]
    x1, h2, route, cnt_tiles = _post_attn(
        oa, lse, ob, z_tok, x2, w_proj_a[0].astype(BF16), w_proj_b[0].astype(BF16), w_out[0].astype(BF16),
        g_ffn[0][None, :], w_router, b_router, tm=MOE_TILE)

    out = _moe(h2, route, cnt_tiles, x1, w_exp_gate[0], w_exp_up[0], w_exp_down[0], g_final[None, :])
    return out.reshape(batch, seq, d)
```

```python
import functools
import math

import jax
import jax.numpy as jnp
import numpy as np
from jax import lax
from jax.experimental import pallas as pl
from jax.experimental.pallas import tpu as pltpu

F32 = jnp.float32
BF16 = jnp.bfloat16

D_MODEL = 1024
HEAD_DIM = 64
HALF = HEAD_DIM // 2
ROPE_THETA = 10000.0
RMS_EPS = 1e-6
LOG2E = math.log2(math.e)
LN2 = math.log(2.0)
Q_SCALE = LOG2E * HEAD_DIM ** -0.5
BLOCK = 128
ATTN_ROWS_PER_STEP = 2048
DIL_GROUPS = ((128, 1), (512, 4), (2048, 16))
N_DIL = len(DIL_GROUPS)
A_GROUP_WIDTH = 512
A_QKV_WIDTH = 3 * N_DIL * A_GROUP_WIDTH
B_Q_HEADS = 16
B_KV_HEADS = 2
B_Q_WIDTH = B_Q_HEADS * HEAD_DIM
B_WINDOW = 128
GATE_WIDTH = 2 * D_MODEL
IN_WIDTH = A_QKV_WIDTH + B_Q_WIDTH + 2 * B_KV_HEADS * HEAD_DIM + GATE_WIDTH
N_EXPERT_GROUPS = 4
EXPERTS_PER_GROUP = 8
N_EXPERTS = N_EXPERT_GROUPS * EXPERTS_PER_GROUP
D_EXPERT = D_MODEL // 4

CB = 256
PAIR = 128
N_IN_BLOCKS = IN_WIDTH // CB
A_BLOCKS = A_GROUP_WIDTH // CB
ZB_GATE = 0
ZB_QB = 8
ZB_KB = 12
ZB_VB = 13
ZB_A0 = 14
N_TOK_BLOCKS = ZB_A0 + 3 * A_BLOCKS
OUT_PERM_BLOCK = 256
LSE_PERM_BLOCK = 128
ROUTER_LANES = 128
EXPERT_LANE0 = EXPERTS_PER_GROUP
assert N_EXPERT_GROUPS <= EXPERTS_PER_GROUP
MOE_TILE = 512
CHUNK = 16
EXP_TILE = 512
LOCAL_ROWS = -(-(2 * MOE_TILE + N_EXPERTS * (CHUNK - 1)) // CB) * CB
LOCAL_CHUNKS = LOCAL_ROWS // CHUNK
SEL_BLOCK = CB
ISSUE_UNROLL = 8
WAIT_GROUP = 8
XS_WIDTH = D_MODEL + ROUTER_LANES
META_PIECES = 3
assert META_PIECES * N_EXPERTS <= ROUTER_LANES

IN_TILE = 512
V7X_VMEM_BYTES = 64 * 1024 * 1024
VMEM_LIMIT = V7X_VMEM_BYTES * 7 // 8


def _in_proj_plan():
    plan = []
    for c in range(N_IN_BLOCKS):
        col = c * CB
        if col < A_QKV_WIDTH:
            part, rem = divmod(col, N_DIL * A_GROUP_WIDTH)
            group, blk = divmod(rem // CB, A_BLOCKS)
            kind = ("q", "k", "v")[part]
            dil = DIL_GROUPS[group][1]
            if dil == 1:
                plan.append((0, ZB_A0 + part * A_BLOCKS + blk, kind, 1))
            else:
                plan.append((group, part * A_BLOCKS + blk, kind, dil))
        elif col < A_QKV_WIDTH + B_Q_WIDTH:
            plan.append((0, ZB_QB + (col - A_QKV_WIDTH) // CB, "q", 1))
        elif col < A_QKV_WIDTH + B_Q_WIDTH + CB:
            plan.append((0, -1, "kvb", 1))
        else:
            plan.append((0, ZB_GATE + (col - (A_QKV_WIDTH + B_Q_WIDTH + CB)) // CB, "v", 1))
    return tuple(plan)


def _rms(x, g):
    return x * lax.rsqrt(jnp.mean(x * x, axis=-1, keepdims=True) + RMS_EPS) * g


def _split3(w):
    hi = w.astype(BF16).astype(F32)
    mid = (w - hi).astype(BF16).astype(F32)
    lo = (w - hi - mid).astype(BF16).astype(F32)
    return hi, mid, lo


def _rope(acc, cos, sin_signed, first_half):
    partner = jnp.where(first_half, pltpu.roll(acc, CB - HALF, 1), pltpu.roll(acc, HALF, 1))
    return acc * cos + partner * sin_signed


def _in_perm_block(dil):
    return max(BLOCK, CHUNK * dil)


def _in_proj_kernel(x_ref, g_ref, w_ref, b_ref, cos_ref, sin_ref, perm1_ref, perm2_ref,
                    zt_ref, zd1_ref, zd2_ref, *, plan, tm):
    out_refs = (zt_ref, zd1_ref, zd2_ref)
    perm_refs = (None, perm1_ref, perm2_ref)
    lane = lax.broadcasted_iota(jnp.int32, (tm, CB), 1)
    first_half = (lane % HEAD_DIM) < HALF

    h = _rms(x_ref[...], g_ref[...]).astype(BF16)
    h_by_dil, tables = {}, {}
    for slot, (_, dil) in enumerate(DIL_GROUPS):
        tables[dil] = slot
        if dil == 1:
            h_by_dil[dil] = h
            continue
        blk = _in_perm_block(dil)
        n = blk // dil
        moved = [jnp.dot(perm_refs[slot][...], h[tb * blk:(tb + 1) * blk], preferred_element_type=F32
                         ).astype(BF16) for tb in range(tm // blk)]
        h_by_dil[dil] = jnp.concatenate(
            [part[r * n:(r + 1) * n] for r in range(dil) for part in moved], axis=0)

    for c, (arr, dst, kind, dil) in enumerate(plan):
        cols = slice(c * CB, (c + 1) * CB)
        acc = jnp.dot(h_by_dil[dil], w_ref[:, cols], preferred_element_type=F32) + b_ref[:, cols]
        if kind in ("q", "k", "kvb"):
            slot = tables[dil]
            rot = _rope(acc, cos_ref[slot], sin_ref[slot], first_half)
        if kind == "q":
            val = (rot * Q_SCALE).astype(BF16)
        elif kind == "k":
            val = rot.astype(BF16)
        elif kind == "v":
            val = acc.astype(BF16)
        else:
            r64 = pltpu.roll(rot, HEAD_DIM, 1)
            r128 = pltpu.roll(rot, 2 * HEAD_DIM, 1)
            kdup = jnp.where(lane < HEAD_DIM, rot, jnp.where(lane < 3 * HEAD_DIM, r64, r128))
            a128 = pltpu.roll(acc, 2 * HEAD_DIM, 1)
            a192 = pltpu.roll(acc, 3 * HEAD_DIM, 1)
            vdup = jnp.where(lane < HEAD_DIM, a128, jnp.where(lane < 3 * HEAD_DIM, a192, acc))
            zt_ref[ZB_KB] = kdup.astype(BF16)
            zt_ref[ZB_VB] = vdup.astype(BF16)
            continue
        if dil == 1:
            out_refs[arr][dst] = val
        else:
            n = tm // dil
            for r in range(dil):
                out_refs[arr][dst, :, r * CB:(r + 1) * CB] = val[r * n:(r + 1) * n, :]


def _dilation_perm(block, dil, dtype):
    j = np.arange(block)
    src = (j % (block // dil)) * dil + j // (block // dil)
    return jnp.asarray((src[:, None] == j[None, :]).astype(np.float32), dtype=dtype)


def _in_proj(x2, g_mix, w_in, b_in, cos_t, sin_t, *, seq, tm):
    n_tok = x2.shape[0]
    tiles_per_seq = seq // tm
    const = lambda i: (0, 0)
    table = pl.BlockSpec((N_DIL, tm, CB), lambda i: (0, i % tiles_per_seq, 0))
    d1, d2 = DIL_GROUPS[1][1], DIL_GROUPS[2][1]
    perms = [_dilation_perm(_in_perm_block(d), d, BF16) for d in (d1, d2)]
    perm_spec = lambda d: pl.BlockSpec((_in_perm_block(d),) * 2, const, pipeline_mode=pl.Buffered(1))
    return pl.pallas_call(
        functools.partial(_in_proj_kernel, plan=_in_proj_plan(), tm=tm),
        grid=(n_tok // tm,),
        in_specs=[
            pl.BlockSpec((tm, D_MODEL), lambda i: (i, 0)),
            pl.BlockSpec((1, D_MODEL), const),
            pl.BlockSpec((D_MODEL, IN_WIDTH), const, pipeline_mode=pl.Buffered(1)),
            pl.BlockSpec((1, IN_WIDTH), const),
            table, table,
            perm_spec(d1), perm_spec(d2),
        ],
        out_specs=[
            pl.BlockSpec((N_TOK_BLOCKS, tm, CB), lambda i: (0, i, 0)),
            pl.BlockSpec((3 * A_BLOCKS, tm // d1, d1 * CB), lambda i: (0, i, 0)),
            pl.BlockSpec((3 * A_BLOCKS, tm // d2, d2 * CB), lambda i: (0, i, 0)),
        ],
        out_shape=[
            jax.ShapeDtypeStruct((N_TOK_BLOCKS, n_tok, CB), BF16),
            jax.ShapeDtypeStruct((3 * A_BLOCKS, n_tok // d1, d1 * CB), BF16),
            jax.ShapeDtypeStruct((3 * A_BLOCKS, n_tok // d2, d2 * CB), BF16),
        ],
        compiler_params=pltpu.CompilerParams(
            dimension_semantics=("arbitrary",), vmem_limit_bytes=VMEM_LIMIT),
        name="in_proj",
    )(x2, g_mix, w_in, b_in, cos_t, sin_t, *perms)


def _attn_kernel(*refs, lq, max_dist, kv_shared, has_sink, want_lse, n_axes):
    refs = list(refs)
    sink_ref = refs.pop(0) if has_sink else None
    q_ref, k_ref, v_ref, kp_ref, vp_ref = refs[:5]
    o_ref = refs[5]
    lse_ref = refs[6] if want_lse else None
    vaug_ref = refs[-1]
    n_qblk = q_ref.shape[0]
    n_kv_pairs = vaug_ref.shape[0]
    rows_kv = lq + BLOCK

    row = lax.broadcasted_iota(jnp.int32, (BLOCK, 2 * BLOCK), 0)
    col = lax.broadcasted_iota(jnp.int32, (BLOCK, 2 * BLOCK), 1)
    dist = row - col + BLOCK
    valid = (dist >= 0) & (dist <= max_dist)
    neg_inf = jnp.float32(-jnp.inf)
    bias = jnp.where(valid, 0.0, neg_inf)
    bias_first = jnp.where(valid & (col >= BLOCK), 0.0, neg_inf)
    bias0 = jnp.where(pl.program_id(n_axes - 1) == 0, bias_first, bias)
    lane_lo = lax.broadcasted_iota(jnp.int32, (BLOCK, PAIR), 1) < HEAD_DIM

    def rd(ref, blk, rows, cols):
        return ref[rows, cols] if kv_shared else ref[blk, rows, cols]

    first_step = functools.reduce(lambda a, b: a & b, [pl.program_id(a) == 0 for a in range(n_axes)])

    @pl.when(first_step)
    def _():
        lane = lax.broadcasted_iota(jnp.int32, (rows_kv, PAIR), 1)
        for pair in range(n_kv_pairs):
            vaug_ref[pair, 0, :, PAIR:] = jnp.where(lane < HEAD_DIM, 1.0, 0.0).astype(BF16)
            vaug_ref[pair, 1, :, PAIR:] = jnp.where(lane < HEAD_DIM, 0.0, 1.0).astype(BF16)

    lane_kv = lax.broadcasted_iota(jnp.int32, (rows_kv, PAIR), 1) < HEAD_DIM
    n_col_pairs = q_ref.shape[-1] // PAIR
    for pair in range(n_kv_pairs):
        cols = slice((pair % n_col_pairs) * PAIR, (pair % n_col_pairs + 1) * PAIR)
        everything = slice(None)
        v_all = jnp.concatenate([rd(vp_ref, pair // n_col_pairs, everything, cols),
                                 rd(v_ref, pair // n_col_pairs, everything, cols)], axis=0)
        zero = jnp.zeros_like(v_all)
        vaug_ref[pair, 0, :, :PAIR] = jnp.where(lane_kv, v_all, zero)
        vaug_ref[pair, 1, :, :PAIR] = jnp.where(lane_kv, zero, v_all)

    for ib in range(lq // BLOCK):
        rows = slice(ib * BLOCK, (ib + 1) * BLOCK)
        win = slice(ib * BLOCK, (ib + 2) * BLOCK)
        b_ib = bias0 if ib == 0 else bias
        for blk in range(n_qblk):
            for pp in range(n_col_pairs):
                qcols = slice(pp * PAIR, (pp + 1) * PAIR)
                pair = 0 if kv_shared else blk * n_col_pairs + pp
                kcols = slice(0, PAIR) if kv_shared else qcols
                if ib == 0:
                    k_win = jnp.concatenate([rd(kp_ref, blk, slice(None), kcols),
                                             rd(k_ref, blk, slice(0, BLOCK), kcols)], axis=0)
                else:
                    k_win = rd(k_ref, blk, slice((ib - 1) * BLOCK, (ib + 1) * BLOCK), kcols)
                q_pair = q_ref[blk, rows, qcols]
                ps, ms, sink_terms = [], [], []
                for hh in range(2):
                    q_h = jnp.where(lane_lo == (hh == 0), q_pair, jnp.zeros_like(q_pair))
                    s = lax.dot_general(q_h, k_win, (((1,), (1,)), ((), ())), preferred_element_type=F32)
                    s = s + b_ib
                    m = jnp.max(s, axis=-1, keepdims=True)
                    if has_sink:
                        head = (pl.program_id(1) * n_qblk + blk) * (CB // HEAD_DIM) + pp * 2 + hh
                        sink = sink_ref[head] * LOG2E
                        m = jnp.maximum(m, sink)
                        sink_terms.append(jnp.exp2(sink - m))
                    ps.append(jnp.exp2(s - m).astype(BF16))
                    ms.append(m)
                v_aug = jnp.concatenate([vaug_ref[pair, 0, win, :], vaug_ref[pair, 1, win, :]], axis=0)
                od = jnp.dot(jnp.concatenate(ps, axis=1), v_aug, preferred_element_type=F32)
                den = od[:, PAIR:]
                if has_sink:
                    den = den + jnp.where(lane_lo, sink_terms[0], sink_terms[1])
                o_ref[blk, rows, qcols] = (od[:, :PAIR] * (1.0 / den)).astype(BF16)
                if want_lse:
                    lse_ref[blk, rows, qcols] = (jnp.where(lane_lo, ms[0], ms[1]) + jnp.log2(den)) * LN2


def _dilated_attention(z, bases, group, *, batch, seq, lq):
    window, dil = DIL_GROUPS[group]
    sub_len = seq // dil
    lq = min(lq, sub_len)
    z4 = z.reshape(z.shape[0], batch, sub_len, dil * CB)
    qb, kb, vb = bases
    assert all(base % A_BLOCKS == 0 for base in bases)
    bpq = lq // BLOCK
    slabs = max(1, min(dil, ATTN_ROWS_PER_STEP // lq))
    width = slabs * CB
    cur = lambda base: pl.BlockSpec(
        (A_BLOCKS, None, lq, width), lambda b, r, i: (base // A_BLOCKS, b, i, r))
    prev = lambda base: pl.BlockSpec(
        (A_BLOCKS, None, BLOCK, width), lambda b, r, i: (base // A_BLOCKS, b, jnp.maximum(i * bpq - 1, 0), r))
    out_spec = pl.BlockSpec((A_BLOCKS, None, lq, width), lambda b, r, i: (0, b, i, r))
    o, lse = pl.pallas_call(
        functools.partial(_attn_kernel, lq=lq, max_dist=window // dil, kv_shared=False,
                          has_sink=False, want_lse=True, n_axes=3),
        grid=(batch, dil // slabs, sub_len // lq),
        in_specs=[cur(qb), cur(kb), cur(vb), prev(kb), prev(vb)],
        out_specs=[out_spec, out_spec],
        out_shape=[jax.ShapeDtypeStruct((A_BLOCKS, batch, sub_len, dil * CB), BF16),
                   jax.ShapeDtypeStruct((A_BLOCKS, batch, sub_len, dil * CB), F32)],
        scratch_shapes=[pltpu.VMEM((2 * A_BLOCKS * slabs, 2, lq + BLOCK, 2 * PAIR), BF16)],
        compiler_params=pltpu.CompilerParams(
            dimension_semantics=("arbitrary",) * 3, vmem_limit_bytes=VMEM_LIMIT),
        name=f"dilated_attn_g{group}",
    )(z4, z4, z4, z4, z4)
    rows = batch * sub_len
    return o.reshape(A_BLOCKS, rows, dil * CB), lse.reshape(A_BLOCKS, rows, dil * CB)


def _swa_attention(z_tok, sinks, *, batch, seq, lq):
    z4 = z_tok.reshape(N_TOK_BLOCKS, batch, seq, CB)
    bpq = lq // BLOCK
    n_q_blocks = B_Q_WIDTH // CB
    q_per_kv = n_q_blocks // B_KV_HEADS
    assert ZB_QB % q_per_kv == 0
    q_spec = pl.BlockSpec((q_per_kv, None, lq, CB), lambda b, kvh, i, s: (ZB_QB // q_per_kv + kvh, b, i, 0))
    cur = lambda base: pl.BlockSpec((None, None, lq, PAIR), lambda b, kvh, i, s: (base, b, i, kvh))
    prev = lambda base: pl.BlockSpec(
        (None, None, BLOCK, PAIR), lambda b, kvh, i, s: (base, b, jnp.maximum(i * bpq - 1, 0), kvh))
    o = pl.pallas_call(
        functools.partial(_attn_kernel, lq=lq, max_dist=B_WINDOW - 1, kv_shared=True,
                          has_sink=True, want_lse=False, n_axes=3),
        grid_spec=pltpu.PrefetchScalarGridSpec(
            num_scalar_prefetch=1,
            grid=(batch, B_KV_HEADS, seq // lq),
            in_specs=[q_spec, cur(ZB_KB), cur(ZB_VB), prev(ZB_KB), prev(ZB_VB)],
            out_specs=pl.BlockSpec((q_per_kv, None, lq, CB), lambda b, kvh, i, s: (kvh, b, i, 0)),
            scratch_shapes=[pltpu.VMEM((1, 2, lq + BLOCK, 2 * PAIR), BF16)],
        ),
        out_shape=jax.ShapeDtypeStruct((n_q_blocks, batch, seq, CB), BF16),
        compiler_params=pltpu.CompilerParams(
            dimension_semantics=("arbitrary",) * 3, vmem_limit_bytes=VMEM_LIMIT),
        name="swa_attn",
    )(sinks, z4, z4, z4, z4, z4)
    return o.reshape(n_q_blocks, batch * seq, CB)


def _post_attn_kernel(o0_ref, o1_ref, o2_ref, l0_ref, l1_ref, l2_ref, ob_ref, gate_ref, x_ref,
                      wa_ref, wb_ref, wo_ref, gf_ref, wr_ref, br_ref,
                      po_ref, pl_ref, x1_ref, h2_ref, route_ref, cnt_ref, *, tm):

    def to_token_order(ref, cb, slot, perm_ref, blk):
        dil = DIL_GROUPS[slot + 1][1]
        n = blk // dil
        parts = []
        for tb in range(tm // blk):
            stack = jnp.concatenate(
                [ref[cb, tb * n:(tb + 1) * n, r * CB:(r + 1) * CB] for r in range(dil)], axis=0)
            pieces = (stack,) if stack.dtype == BF16 else _split3(stack)
            moved = [jnp.dot(perm_ref[slot], p.astype(BF16), preferred_element_type=F32) for p in pieces]
            parts.append(functools.reduce(lambda a, b: a + b, moved))
        return jnp.concatenate(parts, axis=0)

    pa = None
    for cb in range(A_BLOCKS):
        l0 = l0_ref[cb]
        l1 = to_token_order(l1_ref, cb, 0, pl_ref, LSE_PERM_BLOCK)
        l2 = to_token_order(l2_ref, cb, 1, pl_ref, LSE_PERM_BLOCK)
        o1 = to_token_order(o1_ref, cb, 0, po_ref, OUT_PERM_BLOCK)
        o2 = to_token_order(o2_ref, cb, 1, po_ref, OUT_PERM_BLOCK)
        mx = jnp.maximum(jnp.maximum(l0, l1), l2)
        e0, e1, e2 = jnp.exp(l0 - mx), jnp.exp(l1 - mx), jnp.exp(l2 - mx)
        inv = 1.0 / (e0 + e1 + e2)
        ya = (e0 * inv) * o0_ref[cb].astype(F32) + (e1 * inv) * o1 + (e2 * inv) * o2
        part = jnp.dot(ya.astype(BF16), wa_ref[cb * CB:(cb + 1) * CB, :], preferred_element_type=F32)
        pa = part if pa is None else pa + part
    pb = None
    for cb in range(B_Q_WIDTH // CB):
        part = jnp.dot(ob_ref[cb], wb_ref[cb * CB:(cb + 1) * CB, :], preferred_element_type=F32)
        pb = part if pb is None else pb + part
    n_gate = D_MODEL // CB
    merged = []
    for cb in range(n_gate):
        cols = slice(cb * CB, (cb + 1) * CB)
        ga = jax.nn.sigmoid(gate_ref[cb].astype(F32))
        gb = jax.nn.sigmoid(gate_ref[n_gate + cb].astype(F32))
        merged.append((ga * pa[:, cols] + gb * pb[:, cols]).astype(BF16))
    merged = jnp.concatenate(merged, axis=1)
    x1 = x_ref[...] + jnp.dot(merged, wo_ref[...], preferred_element_type=F32)
    x1_ref[...] = x1
    h2 = _rms(x1, gf_ref[...])
    h2_ref[...] = h2.astype(BF16)

    h2_hi = h2.astype(BF16)
    h2_lo = (h2 - h2_hi.astype(F32)).astype(BF16)
    wr = wr_ref[...]
    wr_hi = wr.astype(BF16)
    wr_lo = (wr - wr_hi.astype(F32)).astype(BF16)
    nt = (((1,), (1,)), ((), ()))
    hi_terms = lax.dot_general(jnp.concatenate([wr_hi, wr_lo], axis=0), h2_hi, nt, preferred_element_type=F32)
    logits = (hi_terms[:ROUTER_LANES]
              + (lax.dot_general(wr_hi, h2_lo, nt, preferred_element_type=F32) + hi_terms[ROUTER_LANES:])
              ) + br_ref[...]
    sub = lax.broadcasted_iota(jnp.int32, (EXPERTS_PER_GROUP, tm), 0)
    neg_inf = jnp.float32(-jnp.inf)
    top = lambda v: jnp.max(v, axis=0, keepdims=True)
    first = lambda hit: jnp.min(jnp.where(hit, sub, EXPERTS_PER_GROUP), axis=0, keepdims=True)
    lg = jnp.where(sub < N_EXPERT_GROUPS, logits[:EXPERTS_PER_GROUP], neg_inf)
    mg = top(lg)
    gsel = first(lg == mg)
    pg_sel = 1.0 / jnp.sum(jnp.exp(lg - mg), axis=0, keepdims=True)
    le = logits[EXPERT_LANE0:EXPERT_LANE0 + EXPERTS_PER_GROUP]
    for g in range(1, N_EXPERT_GROUPS):
        r0 = EXPERT_LANE0 + g * EXPERTS_PER_GROUP
        le = jnp.where(gsel == g, logits[r0:r0 + EXPERTS_PER_GROUP], le)
    ex = jnp.exp(le - top(le))
    pe = ex / jnp.sum(ex, axis=0, keepdims=True)
    p1 = top(pe)
    i1 = first(pe == p1)
    rest = sub != i1
    p2 = top(jnp.where(rest, pe, -1.0))
    i2 = first(rest & (pe == p2))
    norm = pg_sel / (p1 + p2)
    e1 = gsel * EXPERTS_PER_GROUP + i1
    e2 = gsel * EXPERTS_PER_GROUP + i2
    row = lax.broadcasted_iota(jnp.int32, (ROUTER_LANES, tm), 0)
    route_t = jnp.where(
        row == 0, e1.astype(F32),
        jnp.where(row == 1, e2.astype(F32),
                  jnp.where(row == 2, p1 * norm, jnp.where(row == 3, p2 * norm, 0.0))))
    route_ref[...] = route_t.T
    hits_t = jnp.where((row == e1 + EXPERT_LANE0) | (row == e2 + EXPERT_LANE0), 1.0, 0.0).astype(BF16)
    cnt_ref[0] = lax.dot_general(jnp.ones((cnt_ref.shape[1], tm), BF16), hits_t, nt,
                                 preferred_element_type=F32)


def _post_attn(oa, lse, ob, z_tok, x2, w_proj_a, w_proj_b, w_out, g_ffn, w_router, b_router, *, tm):
    n_tok = x2.shape[0]
    const2 = lambda i: (0, 0)
    blk = lambda nb: pl.BlockSpec((nb, tm, CB), lambda i: (0, i, 0))
    dil_blk = lambda dil: pl.BlockSpec((A_BLOCKS, tm // dil, dil * CB), lambda i: (0, i, 0))
    row = lambda width: pl.BlockSpec((tm, width), lambda i: (i, 0))
    resident = lambda shape: pl.BlockSpec(shape, const2, pipeline_mode=pl.Buffered(1))
    d1, d2 = DIL_GROUPS[1][1], DIL_GROUPS[2][1]
    perm_o = jnp.stack([_dilation_perm(OUT_PERM_BLOCK, d, BF16).T for d in (d1, d2)])
    perm_l = jnp.stack([_dilation_perm(LSE_PERM_BLOCK, d, BF16).T for d in (d1, d2)])
    return pl.pallas_call(
        functools.partial(_post_attn_kernel, tm=tm),
        grid=(n_tok // tm,),
        in_specs=[blk(A_BLOCKS), dil_blk(d1), dil_blk(d2), blk(A_BLOCKS), dil_blk(d1), dil_blk(d2),
                  blk(B_Q_WIDTH // CB), blk(GATE_WIDTH // CB), row(D_MODEL),
                  resident((A_GROUP_WIDTH, D_MODEL)), resident((B_Q_WIDTH, D_MODEL)),
                  resident((D_MODEL, D_MODEL)), resident((1, D_MODEL)),
                  resident((ROUTER_LANES, D_MODEL)), resident((ROUTER_LANES, 1)),
                  pl.BlockSpec((2, OUT_PERM_BLOCK, OUT_PERM_BLOCK), lambda i: (0, 0, 0),
                               pipeline_mode=pl.Buffered(1)),
                  pl.BlockSpec((2, LSE_PERM_BLOCK, LSE_PERM_BLOCK), lambda i: (0, 0, 0),
                               pipeline_mode=pl.Buffered(1))],
        out_specs=[row(D_MODEL), row(D_MODEL), row(ROUTER_LANES),
                   pl.BlockSpec((1, 8, ROUTER_LANES), lambda i: (i, 0, 0))],
        out_shape=[jax.ShapeDtypeStruct((n_tok, D_MODEL), F32),
                   jax.ShapeDtypeStruct((n_tok, D_MODEL), BF16),
                   jax.ShapeDtypeStruct((n_tok, ROUTER_LANES), F32),
                   jax.ShapeDtypeStruct((n_tok // tm, 8, ROUTER_LANES), F32)],
        compiler_params=pltpu.CompilerParams(
            dimension_semantics=("arbitrary",), vmem_limit_bytes=VMEM_LIMIT),
        name="post_attn",
    )(oa[0], oa[1], oa[2], lse[0], lse[1], lse[2], ob, z_tok, x2,
      w_proj_a, w_proj_b, w_out, g_ffn, w_router, b_router, perm_o, perm_l)


def _local_slots(route, lbase_row, ltri):
    lane = lax.broadcasted_iota(jnp.int32, route.shape, 1)
    pick = lambda k: jnp.sum(jnp.where(lane == k, route, 0.0), axis=-1, keepdims=True)
    lanef = lane.astype(F32)
    oh1, oh2 = lanef == pick(0), lanef == pick(1)
    oh = jnp.where(oh1 | oh2, 1.0, 0.0).astype(BF16)
    table = jnp.dot(ltri, oh, preferred_element_type=F32) + lbase_row
    ls1 = jnp.sum(jnp.where(oh1, table, 0.0), axis=-1, keepdims=True)
    ls2 = jnp.sum(jnp.where(oh2, table, 0.0), axis=-1, keepdims=True)
    return ls1, ls2, pick


def _chunk_loop(count, body, unroll=1):
    one = lambda c, carry: (body(c), carry)[1]
    full = 0
    if unroll > 1:
        full = (count // unroll) * unroll

        def group(g, carry):
            for u in range(unroll):
                body(g * unroll + u)
            return carry

        lax.fori_loop(0, count // unroll, group, 0)
    lax.fori_loop(full, count, one, 0)


def _wait_chunks(count, wait_rows):
    _chunk_loop(count // WAIT_GROUP, lambda c: wait_rows(WAIT_GROUP * CHUNK))
    _chunk_loop(count % WAIT_GROUP, lambda c: wait_rows(CHUNK))


def _selection_blocks(slots_a, slots_b, axis, other):
    shape = (SEL_BLOCK, other) if axis == 0 else (other, SEL_BLOCK)
    local = lax.broadcasted_iota(jnp.int32, shape, axis).astype(F32).astype(BF16)
    one, zero = jnp.ones_like(local), jnp.zeros_like(local)

    def block(a):
        rel_a = (slots_a - float(a * SEL_BLOCK)).astype(BF16)
        rel_b = (slots_b - float(a * SEL_BLOCK)).astype(BF16)
        return jnp.where(local == rel_a, one, jnp.where(local == rel_b, one, zero))

    return block


def _for_used_blocks(used_rows, body):
    always = 2 * MOE_TILE // SEL_BLOCK + 1
    body(0, always, True)
    for a in range(always, LOCAL_ROWS // SEL_BLOCK):
        pl.when(used_rows > a * SEL_BLOCK)(functools.partial(body, a, a + 1, False))


def _dispatch_kernel(cdst_ref, tot_ref, tail_start_ref, tail_n_ref,
                     h2_ref, route_ref, lbase_ref, ltri_ref, xs_ref, buf_ref, zero_ref, sem_ref, zsem_ref,
                     *, tm):
    i = pl.program_id(0)
    last = pl.num_programs(0) - 1
    slot = i % 2

    def chunk(slot_, src_row, dst_row, rows=CHUNK):
        return pltpu.make_async_copy(buf_ref.at[slot_, pl.ds(src_row, rows), :],
                                     xs_ref.at[pl.ds(dst_row, rows), :], sem_ref.at[slot_])

    def wait_tile(slot_, tile):
        _wait_chunks(tot_ref[tile], lambda rows: chunk(slot_, 0, 0, rows).wait())

    @pl.when(i == 0)
    def _():
        zero_ref[...] = jnp.zeros_like(zero_ref)

        def zero_chunk(row):
            return pltpu.make_async_copy(zero_ref, xs_ref.at[pl.ds(row, CHUNK), :], zsem_ref)

        def per_expert(e, total):
            start = tail_start_ref[e] * CHUNK
            _chunk_loop(tail_n_ref[e], lambda c: zero_chunk(pl.multiple_of(start + c * CHUNK, CHUNK)).start())
            return total + tail_n_ref[e]

        total = lax.fori_loop(0, N_EXPERTS, per_expert, 0)
        _chunk_loop(total, lambda c: zero_chunk(0).wait())

    @pl.when(i >= 2)
    def _():
        wait_tile(slot, i - 2)

    route = route_ref[...]
    ls1, ls2, pick = _local_slots(route, lbase_ref[0, 0:1, :], ltri_ref[...])
    lane = lax.broadcasted_iota(jnp.int32, route.shape, 1)
    ls_t = jnp.where(lane == 0, ls1, jnp.where(lane == 1, ls2, 0.0)).T
    sel_block = _selection_blocks(ls_t[0:1, :], ls_t[1:2, :], 0, tm)
    lane_expert = (lane // META_PIECES).astype(F32)
    lane_piece = lane % META_PIECES
    by_piece = lambda pieces: jnp.where(lane_piece == 0, pieces[0],
                                        jnp.where(lane_piece == 1, pieces[1], pieces[2]))
    meta = jnp.where(lane_expert == pick(0), by_piece(_split3(pick(2))),
                     jnp.where(lane_expert == pick(1), by_piece(_split3(pick(3))), 0.0))
    meta = meta.astype(BF16)

    def sort_block(a0, a1, first):
        del first
        rows = slice(a0 * SEL_BLOCK, a1 * SEL_BLOCK)
        sel = jnp.concatenate([sel_block(a) for a in range(a0, a1)], axis=0)
        buf_ref[slot, rows, :D_MODEL] = jnp.dot(sel, h2_ref[...], preferred_element_type=F32).astype(BF16)
        buf_ref[slot, rows, D_MODEL:] = jnp.dot(sel, meta, preferred_element_type=F32).astype(BF16)

    _for_used_blocks(tot_ref[i] * CHUNK, sort_block)

    _chunk_loop(tot_ref[i], lambda c: chunk(
        slot, pl.multiple_of(c * CHUNK, CHUNK),
        pl.multiple_of(cdst_ref[i * LOCAL_CHUNKS + c] * CHUNK, CHUNK)).start(), unroll=ISSUE_UNROLL)

    @pl.when(i == last)
    def _():
        @pl.when(i >= 1)
        def _():
            wait_tile(1 - slot, i - 1)
        wait_tile(slot, i)


def _expert_kernel(te_ref, nv_ref, xs_ref, wg_ref, wu_ref, wd_ref, ys_ref, wgb_ref, wub_ref, wdb_ref):
    g = pl.program_id(0)
    e = te_ref[g]

    @pl.when((g == 0) | (te_ref[jnp.maximum(g - 1, 0)] != e))
    def _():
        wgb_ref[...] = wg_ref[...].astype(BF16)
        wub_ref[...] = wu_ref[...].astype(BF16)
        wdb_ref[...] = wd_ref[...].astype(BF16)

    @pl.when(g < nv_ref[0])
    def _():
        x = xs_ref[:, :D_MODEL]
        meta = xs_ref[:, D_MODEL:].astype(F32)
        lane = lax.broadcasted_iota(jnp.int32, meta.shape, 1)
        mine = (lane >= e * META_PIECES) & (lane < (e + 1) * META_PIECES)
        w = jnp.sum(jnp.where(mine, meta, 0.0), axis=-1, keepdims=True)
        hg = jnp.dot(x, wgb_ref[...], preferred_element_type=F32)
        hu = jnp.dot(x, wub_ref[...], preferred_element_type=F32)
        a = (hg * jax.nn.sigmoid(hg)) * hu * w
        ys_ref[...] = jnp.dot(a.astype(BF16), wdb_ref[...], preferred_element_type=F32).astype(BF16)


def _combine_kernel(cdst_ref, tot_ref,
                    x1_ref, route_ref, lbase_ref, ltri_ref, gfin_ref, ys_ref, out_ref, ybuf_ref, sem_ref,
                    *, tm):
    i = pl.program_id(0)
    n_tiles = pl.num_programs(0)
    slot = i % 2

    def chunk(slot_, src_row, dst_row, rows=CHUNK):
        return pltpu.make_async_copy(ys_ref.at[pl.ds(src_row, rows), :],
                                     ybuf_ref.at[slot_, pl.ds(dst_row, rows), :], sem_ref.at[slot_])

    def fetch_tile(tile, slot_):
        _chunk_loop(tot_ref[tile], lambda c: chunk(
            slot_, pl.multiple_of(cdst_ref[tile * LOCAL_CHUNKS + c] * CHUNK, CHUNK),
            pl.multiple_of(c * CHUNK, CHUNK)).start(), unroll=ISSUE_UNROLL)

    @pl.when(i == 0)
    def _():
        ybuf_ref[...] = jnp.zeros_like(ybuf_ref)
        fetch_tile(0, 0)

    @pl.when(i + 1 < n_tiles)
    def _():
        fetch_tile(i + 1, 1 - slot)

    _wait_chunks(tot_ref[i], lambda rows: chunk(slot, 0, 0, rows).wait())

    ls1, ls2, _ = _local_slots(route_ref[...], lbase_ref[0, 0:1, :], ltri_ref[...])
    sel_block = _selection_blocks(ls1, ls2, 1, tm)
    def add_block(a0, a1, first):
        rows = slice(a0 * SEL_BLOCK, a1 * SEL_BLOCK)
        sel = jnp.concatenate([sel_block(a) for a in range(a0, a1)], axis=1)
        y = jnp.dot(sel, ybuf_ref[slot, rows, :], preferred_element_type=F32)
        out_ref[...] = (x1_ref[...] if first else out_ref[...]) + y

    _for_used_blocks(tot_ref[i] * CHUNK, add_block)
    out_ref[...] = _rms(out_ref[...], gfin_ref[...])


def _routing_tables(cnt, n_exp_tiles):
    c16 = (cnt + CHUNK - 1) // CHUNK
    lbase = jnp.cumsum(c16, axis=1) - c16
    tile_off = jnp.cumsum(c16, axis=0) - c16
    tot = jnp.sum(c16, axis=0)
    per = EXP_TILE // CHUNK
    region_tiles = (tot + per - 1) // per
    region = region_tiles * per
    base = jnp.cumsum(region) - region
    dst = base[None, :] + tile_off
    tile_end = jnp.cumsum(region_tiles)
    n_valid = tile_end[-1]
    g = jnp.arange(n_exp_tiles, dtype=jnp.int32)
    tile_expert = jnp.sum(tile_end[None, :] <= jnp.minimum(g, n_valid - 1)[:, None], axis=1).astype(jnp.int32)
    c = jnp.arange(LOCAL_CHUNKS, dtype=jnp.int32)
    lend = lbase + c16
    owns = (lbase[:, None, :] <= c[None, :, None]) & (c[None, :, None] < lend[:, None, :])
    chunk_dst = jnp.sum(jnp.where(owns, (dst - lbase)[:, None, :], 0), axis=2) + c[None, :]
    i32 = lambda a: a.astype(jnp.int32).reshape(-1)
    return dict(chunk_dst=i32(chunk_dst), tot=i32(jnp.sum(c16, axis=1)),
                tail_start=i32(base + tot), tail_n=i32(region - tot),
                tile_expert=tile_expert, n_valid=i32(n_valid),
                lbase_rows=(lbase * CHUNK).astype(F32))


def _moe(h2, route, cnt_tiles, x1, w_gate, w_up, w_down, g_final):
    n_tok = h2.shape[0]
    tm = MOE_TILE
    n_tiles = n_tok // tm
    assert cnt_tiles.shape[0] == n_tiles
    worst_rows = 2 * n_tok + n_tiles * N_EXPERTS * (CHUNK - 1) + N_EXPERTS * (EXP_TILE - CHUNK)
    n_exp_tiles = -(-worst_rows // EXP_TILE)
    n_slots = n_exp_tiles * EXP_TILE

    cnt = cnt_tiles[:, 0, EXPERT_LANE0:EXPERT_LANE0 + N_EXPERTS].astype(jnp.int32)
    tb = _routing_tables(cnt, n_exp_tiles)
    lbase_rows = jnp.zeros((n_tiles, 8, ROUTER_LANES), F32).at[:, :, :N_EXPERTS].set(
        tb["lbase_rows"][:, None, :])
    row_id = np.arange(tm)
    ltri = jnp.asarray((row_id[:, None] > row_id[None, :]).astype(np.float32), dtype=BF16)

    tile_row = lambda width: pl.BlockSpec((tm, width), lambda i, *_: (i, 0))
    lbase_spec = pl.BlockSpec((1, 8, ROUTER_LANES), lambda i, *_: (i, 0, 0))
    ltri_spec = pl.BlockSpec((tm, tm), lambda i, *_: (0, 0), pipeline_mode=pl.Buffered(1))
    hbm = pl.BlockSpec(memory_space=pl.ANY)

    xs = pl.pallas_call(
        functools.partial(_dispatch_kernel, tm=tm),
        grid_spec=pltpu.PrefetchScalarGridSpec(
            num_scalar_prefetch=4, grid=(n_tiles,),
            in_specs=[tile_row(D_MODEL), tile_row(ROUTER_LANES), lbase_spec, ltri_spec],
            out_specs=hbm,
            scratch_shapes=[pltpu.VMEM((2, LOCAL_ROWS, XS_WIDTH), BF16), pltpu.VMEM((CHUNK, XS_WIDTH), BF16),
                            pltpu.SemaphoreType.DMA((2,)), pltpu.SemaphoreType.DMA]),
        out_shape=jax.ShapeDtypeStruct((n_slots, XS_WIDTH), BF16),
        compiler_params=pltpu.CompilerParams(
            dimension_semantics=("arbitrary",), vmem_limit_bytes=VMEM_LIMIT),
        name="moe_dispatch",
    )(tb["chunk_dst"], tb["tot"], tb["tail_start"], tb["tail_n"], h2, route, lbase_rows, ltri)

    row_tile = lambda width: pl.BlockSpec(
        (EXP_TILE, width), lambda g, te, nv: (jnp.maximum(jnp.minimum(g, nv[0] - 1), 0), 0))
    ys = pl.pallas_call(
        _expert_kernel,
        grid_spec=pltpu.PrefetchScalarGridSpec(
            num_scalar_prefetch=2, grid=(n_exp_tiles,),
            in_specs=[row_tile(XS_WIDTH),
                      pl.BlockSpec((None, D_MODEL, D_EXPERT), lambda g, te, nv: (te[g], 0, 0)),
                      pl.BlockSpec((None, D_MODEL, D_EXPERT), lambda g, te, nv: (te[g], 0, 0)),
                      pl.BlockSpec((None, D_EXPERT, D_MODEL), lambda g, te, nv: (te[g], 0, 0))],
            out_specs=row_tile(D_MODEL),
            scratch_shapes=[pltpu.VMEM((D_MODEL, D_EXPERT), BF16), pltpu.VMEM((D_MODEL, D_EXPERT), BF16),
                            pltpu.VMEM((D_EXPERT, D_MODEL), BF16)]),
        out_shape=jax.ShapeDtypeStruct((n_slots, D_MODEL), BF16),
        compiler_params=pltpu.CompilerParams(
            dimension_semantics=("arbitrary",), vmem_limit_bytes=VMEM_LIMIT),
        name="moe_expert",
    )(tb["tile_expert"], tb["n_valid"], xs, w_gate, w_up, w_down)

    return pl.pallas_call(
        functools.partial(_combine_kernel, tm=tm),
        grid_spec=pltpu.PrefetchScalarGridSpec(
            num_scalar_prefetch=2, grid=(n_tiles,),
            in_specs=[tile_row(D_MODEL), tile_row(ROUTER_LANES), lbase_spec, ltri_spec,
                      pl.BlockSpec((1, D_MODEL), lambda i, *_: (0, 0)), hbm],
            out_specs=tile_row(D_MODEL),
            scratch_shapes=[pltpu.VMEM((2, LOCAL_ROWS, D_MODEL), BF16), pltpu.SemaphoreType.DMA((2,))]),
        out_shape=jax.ShapeDtypeStruct((n_tok, D_MODEL), F32),
        compiler_params=pltpu.CompilerParams(
            dimension_semantics=("arbitrary",), vmem_limit_bytes=VMEM_LIMIT),
        name="moe_combine",
    )(tb["chunk_dst"], tb["tot"], x1, route, lbase_rows, ltri, g_final, ys)


def _rope_tables(seq, tm):
    pos = np.arange(seq, dtype=np.float64)
    inv_freq = ROPE_THETA ** (-np.arange(0, HEAD_DIM, 2, dtype=np.float64) / HEAD_DIM)
    ang = pos[:, None] * inv_freq[None, :]
    cos, sin = np.cos(ang), np.sin(ang)
    reps = CB // HEAD_DIM
    cos_t = np.tile(np.concatenate([cos, cos], axis=-1), (1, reps))
    sin_t = np.tile(np.concatenate([-sin, sin], axis=-1), (1, reps))

    def reorder(t, dil):
        return t.reshape(seq // tm, tm // dil, dil, CB).transpose(0, 2, 1, 3).reshape(seq, CB)

    dils = [dil for _, dil in DIL_GROUPS]
    return (jnp.asarray(np.stack([reorder(cos_t, dil) for dil in dils]).astype(np.float32)),
            jnp.asarray(np.stack([reorder(sin_t, dil) for dil in dils]).astype(np.float32)))


def kernel(x, w_in, b_in, sinks, w_proj_a, w_proj_b, w_out, g_mix, g_ffn, w_router_group, b_router_group,
           w_router_expert, b_router_expert, w_exp_gate, w_exp_up, w_exp_down, g_final):
    batch, seq, d = x.shape
    assert d == D_MODEL and w_in.shape[0] == 1, "single-layer kernel"
    n_tok = batch * seq
    x2 = x.reshape(n_tok, d)
    assert seq % ATTN_ROWS_PER_STEP == 0 and n_tok % MOE_TILE == 0 and seq % IN_TILE == 0
    cos_t, sin_t = _rope_tables(seq, IN_TILE)

    z_tok, z_d1, z_d2 = _in_proj(x2, g_mix[0][None, :], w_in[0].astype(BF16), b_in[0][None, :],
                                 cos_t, sin_t, seq=seq, tm=IN_TILE)

    a_bases = (0, A_BLOCKS, 2 * A_BLOCKS)
    tok_bases = tuple(ZB_A0 + b for b in a_bases)
    oa, lse = [], []
    for group, (z, bases) in enumerate(((z_tok, tok_bases), (z_d1, a_bases), (z_d2, a_bases))):
        o_g, l_g = _dilated_attention(z, bases, group, batch=batch, seq=seq, lq=ATTN_ROWS_PER_STEP)
        oa.append(o_g)
        lse.append(l_g)
    ob = _swa_attention(z_tok, sinks[0], batch=batch, seq=seq, lq=ATTN_ROWS_PER_STEP)

    gap = EXPERT_LANE0 - N_EXPERT_GROUPS
    tail = ROUTER_LANES - EXPERT_LANE0 - N_EXPERTS
    w_router = jnp.concatenate(
        [w_router_group[0].T, jnp.zeros((gap, d), F32), w_router_expert[0].T, jnp.zeros((tail, d), F32)], axis=0)
    b_router = jnp.concatenate(
        [b_router_group[0], jnp.zeros((gap,), F32), b_router_expert[0], jnp.zeros((tail,), F32)])[:, None]
    x1, h2, route, cnt_tiles = _post_attn(
        oa, lse, ob, z_tok, x2, w_proj_a[0].astype(BF16), w_proj_b[0].astype(BF16), w_out[0].astype(BF16),
        g_ffn[0][None, :], w_router, b_router, tm=MOE_TILE)

    out = _moe(h2, route, cnt_tiles, x1, w_exp_gate[0], w_exp_up[0], w_exp_down[0], g_final[None, :])
    return out.reshape(batch, seq, d)
```

```python
import functools
import math

import jax
import jax.numpy as jnp
import numpy as np
from jax import lax
from jax.experimental import pallas as pl
from jax.experimental.pallas import tpu as pltpu

F32 = jnp.float32
BF16 = jnp.bfloat16

D_MODEL = 1024
HEAD_DIM = 64
HALF = HEAD_DIM // 2
ROPE_THETA = 10000.0
RMS_EPS = 1e-6
LOG2E = math.log2(math.e)
LN2 = math.log(2.0)
Q_SCALE = LOG2E * HEAD_DIM ** -0.5
BLOCK = 128
ATTN_ROWS_PER_STEP = 2048
DIL_GROUPS = ((128, 1), (512, 4), (2048, 16))
N_DIL = len(DIL_GROUPS)
A_GROUP_WIDTH = 512
A_QKV_WIDTH = 3 * N_DIL * A_GROUP_WIDTH
B_Q_HEADS = 16
B_KV_HEADS = 2
B_Q_WIDTH = B_Q_HEADS * HEAD_DIM
B_WINDOW = 128
GATE_WIDTH = 2 * D_MODEL
IN_WIDTH = A_QKV_WIDTH + B_Q_WIDTH + 2 * B_KV_HEADS * HEAD_DIM + GATE_WIDTH
N_EXPERT_GROUPS = 4
EXPERTS_PER_GROUP = 8
N_EXPERTS = N_EXPERT_GROUPS * EXPERTS_PER_GROUP
D_EXPERT = D_MODEL // 4

CB = 256
PAIR = 128
N_IN_BLOCKS = IN_WIDTH // CB
A_BLOCKS = A_GROUP_WIDTH // CB
ZB_GATE = 0
ZB_QB = 8
ZB_KB = 12
ZB_VB = 13
ZB_A0 = 14
N_TOK_BLOCKS = ZB_A0 + 3 * A_BLOCKS
OUT_PERM_BLOCK = 256
LSE_PERM_BLOCK = 128
ROUTER_LANES = 128
EXPERT_LANE0 = EXPERTS_PER_GROUP
assert N_EXPERT_GROUPS <= EXPERTS_PER_GROUP
MOE_TILE = 512
CHUNK = 16
EXP_TILE = 512
LOCAL_ROWS = -(-(2 * MOE_TILE + N_EXPERTS * (CHUNK - 1)) // CB) * CB
PAIR_SLOTS = LOCAL_ROWS // CHUNK // 2
SEL_BLOCK = CB
ISSUE_UNROLL = 8
WAIT_GROUP = 8
XS_WIDTH = D_MODEL + ROUTER_LANES
META_PIECES = 3
assert META_PIECES * N_EXPERTS <= ROUTER_LANES

IN_TILE = 512
V7X_VMEM_BYTES = 64 * 1024 * 1024
VMEM_LIMIT = V7X_VMEM_BYTES * 7 // 8


def _in_proj_plan():
    plan = []
    for c in range(N_IN_BLOCKS):
        col = c * CB
        if col < A_QKV_WIDTH:
            part, rem = divmod(col, N_DIL * A_GROUP_WIDTH)
            group, blk = divmod(rem // CB, A_BLOCKS)
            kind = ("q", "k", "v")[part]
            dil = DIL_GROUPS[group][1]
            if dil == 1:
                plan.append((0, ZB_A0 + part * A_BLOCKS + blk, kind, 1))
            else:
                plan.append((group, part * A_BLOCKS + blk, kind, dil))
        elif col < A_QKV_WIDTH + B_Q_WIDTH:
            plan.append((0, ZB_QB + (col - A_QKV_WIDTH) // CB, "q", 1))
        elif col < A_QKV_WIDTH + B_Q_WIDTH + CB:
            plan.append((0, -1, "kvb", 1))
        else:
            plan.append((0, ZB_GATE + (col - (A_QKV_WIDTH + B_Q_WIDTH + CB)) // CB, "v", 1))
    return tuple(plan)


def _rms(x, g):
    return x * lax.rsqrt(jnp.mean(x * x, axis=-1, keepdims=True) + RMS_EPS) * g


def _split3(w):
    hi = w.astype(BF16).astype(F32)
    mid = (w - hi).astype(BF16).astype(F32)
    lo = (w - hi - mid).astype(BF16).astype(F32)
    return hi, mid, lo


def _rope(acc, cos, sin_signed, first_half):
    partner = jnp.where(first_half, pltpu.roll(acc, CB - HALF, 1), pltpu.roll(acc, HALF, 1))
    return acc * cos + partner * sin_signed


def _in_perm_block(dil):
    return max(BLOCK, CHUNK * dil)


def _in_proj_kernel(x_ref, g_ref, w_ref, b_ref, cos_ref, sin_ref, perm1_ref, perm2_ref,
                    zt_ref, zd1_ref, zd2_ref, *, plan, tm):
    out_refs = (zt_ref, zd1_ref, zd2_ref)
    perm_refs = (None, perm1_ref, perm2_ref)
    lane = lax.broadcasted_iota(jnp.int32, (tm, CB), 1)
    first_half = (lane % HEAD_DIM) < HALF

    h = _rms(x_ref[...], g_ref[...]).astype(BF16)
    h_by_dil, tables = {}, {}
    for slot, (_, dil) in enumerate(DIL_GROUPS):
        tables[dil] = slot
        if dil == 1:
            h_by_dil[dil] = h
            continue
        blk = _in_perm_block(dil)
        n = blk // dil
        moved = [jnp.dot(perm_refs[slot][...], h[tb * blk:(tb + 1) * blk], preferred_element_type=F32
                         ).astype(BF16) for tb in range(tm // blk)]
        h_by_dil[dil] = jnp.concatenate(
            [part[r * n:(r + 1) * n] for r in range(dil) for part in moved], axis=0)

    for c, (arr, dst, kind, dil) in enumerate(plan):
        cols = slice(c * CB, (c + 1) * CB)
        acc = jnp.dot(h_by_dil[dil], w_ref[:, cols], preferred_element_type=F32) + b_ref[:, cols]
        if kind in ("q", "k", "kvb"):
            slot = tables[dil]
            rot = _rope(acc, cos_ref[slot], sin_ref[slot], first_half)
        if kind == "q":
            val = (rot * Q_SCALE).astype(BF16)
        elif kind == "k":
            val = rot.astype(BF16)
        elif kind == "v":
            val = acc.astype(BF16)
        else:
            r64 = pltpu.roll(rot, HEAD_DIM, 1)
            r128 = pltpu.roll(rot, 2 * HEAD_DIM, 1)
            kdup = jnp.where(lane < HEAD_DIM, rot, jnp.where(lane < 3 * HEAD_DIM, r64, r128))
            a128 = pltpu.roll(acc, 2 * HEAD_DIM, 1)
            a192 = pltpu.roll(acc, 3 * HEAD_DIM, 1)
            vdup = jnp.where(lane < HEAD_DIM, a128, jnp.where(lane < 3 * HEAD_DIM, a192, acc))
            zt_ref[ZB_KB] = kdup.astype(BF16)
            zt_ref[ZB_VB] = vdup.astype(BF16)
            continue
        if dil == 1:
            out_refs[arr][dst] = val
        else:
            n = tm // dil
            for r in range(dil):
                out_refs[arr][dst, :, r * CB:(r + 1) * CB] = val[r * n:(r + 1) * n, :]


def _dilation_perm(block, dil, dtype):
    j = np.arange(block)
    src = (j % (block // dil)) * dil + j // (block // dil)
    return jnp.asarray((src[:, None] == j[None, :]).astype(np.float32), dtype=dtype)


def _in_proj(x2, g_mix, w_in, b_in, cos_t, sin_t, *, seq, tm):
    n_tok = x2.shape[0]
    tiles_per_seq = seq // tm
    const = lambda i: (0, 0)
    table = pl.BlockSpec((N_DIL, tm, CB), lambda i: (0, i % tiles_per_seq, 0))
    d1, d2 = DIL_GROUPS[1][1], DIL_GROUPS[2][1]
    perms = [_dilation_perm(_in_perm_block(d), d, BF16) for d in (d1, d2)]
    perm_spec = lambda d: pl.BlockSpec((_in_perm_block(d),) * 2, const, pipeline_mode=pl.Buffered(1))
    return pl.pallas_call(
        functools.partial(_in_proj_kernel, plan=_in_proj_plan(), tm=tm),
        grid=(n_tok // tm,),
        in_specs=[
            pl.BlockSpec((tm, D_MODEL), lambda i: (i, 0)),
            pl.BlockSpec((1, D_MODEL), const),
            pl.BlockSpec((D_MODEL, IN_WIDTH), const, pipeline_mode=pl.Buffered(1)),
            pl.BlockSpec((1, IN_WIDTH), const),
            table, table,
            perm_spec(d1), perm_spec(d2),
        ],
        out_specs=[
            pl.BlockSpec((N_TOK_BLOCKS, tm, CB), lambda i: (0, i, 0)),
            pl.BlockSpec((3 * A_BLOCKS, tm // d1, d1 * CB), lambda i: (0, i, 0)),
            pl.BlockSpec((3 * A_BLOCKS, tm // d2, d2 * CB), lambda i: (0, i, 0)),
        ],
        out_shape=[
            jax.ShapeDtypeStruct((N_TOK_BLOCKS, n_tok, CB), BF16),
            jax.ShapeDtypeStruct((3 * A_BLOCKS, n_tok // d1, d1 * CB), BF16),
            jax.ShapeDtypeStruct((3 * A_BLOCKS, n_tok // d2, d2 * CB), BF16),
        ],
        compiler_params=pltpu.CompilerParams(
            dimension_semantics=("arbitrary",), vmem_limit_bytes=VMEM_LIMIT),
        name="in_proj",
    )(x2, g_mix, w_in, b_in, cos_t, sin_t, *perms)


def _attn_kernel(*refs, lq, max_dist, kv_shared, has_sink, want_lse, n_axes):
    refs = list(refs)
    sink_ref = refs.pop(0) if has_sink else None
    q_ref, k_ref, v_ref, kp_ref, vp_ref = refs[:5]
    o_ref = refs[5]
    lse_ref = refs[6] if want_lse else None
    vaug_ref = refs[-1]
    n_qblk = q_ref.shape[0]
    n_kv_pairs = vaug_ref.shape[0]
    rows_kv = lq + BLOCK

    row = lax.broadcasted_iota(jnp.int32, (BLOCK, 2 * BLOCK), 0)
    col = lax.broadcasted_iota(jnp.int32, (BLOCK, 2 * BLOCK), 1)
    dist = row - col + BLOCK
    valid = (dist >= 0) & (dist <= max_dist)
    neg_inf = jnp.float32(-jnp.inf)
    bias = jnp.where(valid, 0.0, neg_inf)
    bias_first = jnp.where(valid & (col >= BLOCK), 0.0, neg_inf)
    bias0 = jnp.where(pl.program_id(n_axes - 1) == 0, bias_first, bias)
    lane_lo = lax.broadcasted_iota(jnp.int32, (BLOCK, PAIR), 1) < HEAD_DIM

    def rd(ref, blk, rows, cols):
        return ref[rows, cols] if kv_shared else ref[blk, rows, cols]

    first_step = functools.reduce(lambda a, b: a & b, [pl.program_id(a) == 0 for a in range(n_axes)])

    @pl.when(first_step)
    def _():
        lane = lax.broadcasted_iota(jnp.int32, (rows_kv, PAIR), 1)
        for pair in range(n_kv_pairs):
            vaug_ref[pair, 0, :, PAIR:] = jnp.where(lane < HEAD_DIM, 1.0, 0.0).astype(BF16)
            vaug_ref[pair, 1, :, PAIR:] = jnp.where(lane < HEAD_DIM, 0.0, 1.0).astype(BF16)

    lane_kv = lax.broadcasted_iota(jnp.int32, (rows_kv, PAIR), 1) < HEAD_DIM
    n_col_pairs = q_ref.shape[-1] // PAIR
    for pair in range(n_kv_pairs):
        cols = slice((pair % n_col_pairs) * PAIR, (pair % n_col_pairs + 1) * PAIR)
        everything = slice(None)
        v_all = jnp.concatenate([rd(vp_ref, pair // n_col_pairs, everything, cols),
                                 rd(v_ref, pair // n_col_pairs, everything, cols)], axis=0)
        zero = jnp.zeros_like(v_all)
        vaug_ref[pair, 0, :, :PAIR] = jnp.where(lane_kv, v_all, zero)
        vaug_ref[pair, 1, :, :PAIR] = jnp.where(lane_kv, zero, v_all)

    for ib in range(lq // BLOCK):
        rows = slice(ib * BLOCK, (ib + 1) * BLOCK)
        win = slice(ib * BLOCK, (ib + 2) * BLOCK)
        b_ib = bias0 if ib == 0 else bias
        for blk in range(n_qblk):
            for pp in range(n_col_pairs):
                qcols = slice(pp * PAIR, (pp + 1) * PAIR)
                pair = 0 if kv_shared else blk * n_col_pairs + pp
                kcols = slice(0, PAIR) if kv_shared else qcols
                if ib == 0:
                    k_win = jnp.concatenate([rd(kp_ref, blk, slice(None), kcols),
                                             rd(k_ref, blk, slice(0, BLOCK), kcols)], axis=0)
                else:
                    k_win = rd(k_ref, blk, slice((ib - 1) * BLOCK, (ib + 1) * BLOCK), kcols)
                q_pair = q_ref[blk, rows, qcols]
                ps, ms, sink_terms = [], [], []
                for hh in range(2):
                    q_h = jnp.where(lane_lo == (hh == 0), q_pair, jnp.zeros_like(q_pair))
                    s = lax.dot_general(q_h, k_win, (((1,), (1,)), ((), ())), preferred_element_type=F32)
                    s = s + b_ib
                    m = jnp.max(s, axis=-1, keepdims=True)
                    if has_sink:
                        head = (pl.program_id(1) * n_qblk + blk) * (CB // HEAD_DIM) + pp * 2 + hh
                        sink = sink_ref[head] * LOG2E
                        m = jnp.maximum(m, sink)
                        sink_terms.append(jnp.exp2(sink - m))
                    ps.append(jnp.exp2(s - m).astype(BF16))
                    ms.append(m)
                v_aug = jnp.concatenate([vaug_ref[pair, 0, win, :], vaug_ref[pair, 1, win, :]], axis=0)
                od = jnp.dot(jnp.concatenate(ps, axis=1), v_aug, preferred_element_type=F32)
                den = od[:, PAIR:]
                if has_sink:
                    den = den + jnp.where(lane_lo, sink_terms[0], sink_terms[1])
                o_ref[blk, rows, qcols] = (od[:, :PAIR] * (1.0 / den)).astype(BF16)
                if want_lse:
                    lse_ref[blk, rows, qcols] = (jnp.where(lane_lo, ms[0], ms[1]) + jnp.log2(den)) * LN2


def _dilated_attention(z, bases, group, *, batch, seq, lq):
    window, dil = DIL_GROUPS[group]
    sub_len = seq // dil
    lq = min(lq, sub_len)
    z4 = z.reshape(z.shape[0], batch, sub_len, dil * CB)
    qb, kb, vb = bases
    assert all(base % A_BLOCKS == 0 for base in bases)
    bpq = lq // BLOCK
    slabs = max(1, min(dil, ATTN_ROWS_PER_STEP // lq))
    width = slabs * CB
    cur = lambda base: pl.BlockSpec(
        (A_BLOCKS, None, lq, width), lambda b, r, i: (base // A_BLOCKS, b, i, r))
    prev = lambda base: pl.BlockSpec(
        (A_BLOCKS, None, BLOCK, width), lambda b, r, i: (base // A_BLOCKS, b, jnp.maximum(i * bpq - 1, 0), r))
    out_spec = pl.BlockSpec((A_BLOCKS, None, lq, width), lambda b, r, i: (0, b, i, r))
    o, lse = pl.pallas_call(
        functools.partial(_attn_kernel, lq=lq, max_dist=window // dil, kv_shared=False,
                          has_sink=False, want_lse=True, n_axes=3),
        grid=(batch, dil // slabs, sub_len // lq),
        in_specs=[cur(qb), cur(kb), cur(vb), prev(kb), prev(vb)],
        out_specs=[out_spec, out_spec],
        out_shape=[jax.ShapeDtypeStruct((A_BLOCKS, batch, sub_len, dil * CB), BF16),
                   jax.ShapeDtypeStruct((A_BLOCKS, batch, sub_len, dil * CB), F32)],
        scratch_shapes=[pltpu.VMEM((2 * A_BLOCKS * slabs, 2, lq + BLOCK, 2 * PAIR), BF16)],
        compiler_params=pltpu.CompilerParams(
            dimension_semantics=("arbitrary",) * 3, vmem_limit_bytes=VMEM_LIMIT),
        name=f"dilated_attn_g{group}",
    )(z4, z4, z4, z4, z4)
    rows = batch * sub_len
    return o.reshape(A_BLOCKS, rows, dil * CB), lse.reshape(A_BLOCKS, rows, dil * CB)


def _swa_attention(z_tok, sinks, *, batch, seq, lq):
    z4 = z_tok.reshape(N_TOK_BLOCKS, batch, seq, CB)
    bpq = lq // BLOCK
    n_q_blocks = B_Q_WIDTH // CB
    q_per_kv = n_q_blocks // B_KV_HEADS
    assert ZB_QB % q_per_kv == 0
    q_spec = pl.BlockSpec((q_per_kv, None, lq, CB), lambda b, kvh, i, s: (ZB_QB // q_per_kv + kvh, b, i, 0))
    cur = lambda base: pl.BlockSpec((None, None, lq, PAIR), lambda b, kvh, i, s: (base, b, i, kvh))
    prev = lambda base: pl.BlockSpec(
        (None, None, BLOCK, PAIR), lambda b, kvh, i, s: (base, b, jnp.maximum(i * bpq - 1, 0), kvh))
    o = pl.pallas_call(
        functools.partial(_attn_kernel, lq=lq, max_dist=B_WINDOW - 1, kv_shared=True,
                          has_sink=True, want_lse=False, n_axes=3),
        grid_spec=pltpu.PrefetchScalarGridSpec(
            num_scalar_prefetch=1,
            grid=(batch, B_KV_HEADS, seq // lq),
            in_specs=[q_spec, cur(ZB_KB), cur(ZB_VB), prev(ZB_KB), prev(ZB_VB)],
            out_specs=pl.BlockSpec((q_per_kv, None, lq, CB), lambda b, kvh, i, s: (kvh, b, i, 0)),
            scratch_shapes=[pltpu.VMEM((1, 2, lq + BLOCK, 2 * PAIR), BF16)],
        ),
        out_shape=jax.ShapeDtypeStruct((n_q_blocks, batch, seq, CB), BF16),
        compiler_params=pltpu.CompilerParams(
            dimension_semantics=("arbitrary",) * 3, vmem_limit_bytes=VMEM_LIMIT),
        name="swa_attn",
    )(sinks, z4, z4, z4, z4, z4)
    return o.reshape(n_q_blocks, batch * seq, CB)


def _post_attn_kernel(o0_ref, o1_ref, o2_ref, l0_ref, l1_ref, l2_ref, ob_ref, gate_ref, x_ref,
                      wa_ref, wb_ref, wo_ref, gf_ref, wr_ref, br_ref,
                      po_ref, pl_ref, x1_ref, h2_ref, route_ref, cnt_ref, *, tm):

    def to_token_order(ref, cb, slot, perm_ref, blk):
        dil = DIL_GROUPS[slot + 1][1]
        n = blk // dil
        parts = []
        for tb in range(tm // blk):
            stack = jnp.concatenate(
                [ref[cb, tb * n:(tb + 1) * n, r * CB:(r + 1) * CB] for r in range(dil)], axis=0)
            pieces = (stack,) if stack.dtype == BF16 else _split3(stack)
            moved = [jnp.dot(perm_ref[slot], p.astype(BF16), preferred_element_type=F32) for p in pieces]
            parts.append(functools.reduce(lambda a, b: a + b, moved))
        return jnp.concatenate(parts, axis=0)

    pa = None
    for cb in range(A_BLOCKS):
        l0 = l0_ref[cb]
        l1 = to_token_order(l1_ref, cb, 0, pl_ref, LSE_PERM_BLOCK)
        l2 = to_token_order(l2_ref, cb, 1, pl_ref, LSE_PERM_BLOCK)
        o1 = to_token_order(o1_ref, cb, 0, po_ref, OUT_PERM_BLOCK)
        o2 = to_token_order(o2_ref, cb, 1, po_ref, OUT_PERM_BLOCK)
        mx = jnp.maximum(jnp.maximum(l0, l1), l2)
        e0, e1, e2 = jnp.exp(l0 - mx), jnp.exp(l1 - mx), jnp.exp(l2 - mx)
        inv = 1.0 / (e0 + e1 + e2)
        ya = (e0 * inv) * o0_ref[cb].astype(F32) + (e1 * inv) * o1 + (e2 * inv) * o2
        part = jnp.dot(ya.astype(BF16), wa_ref[cb * CB:(cb + 1) * CB, :], preferred_element_type=F32)
        pa = part if pa is None else pa + part
    pb = None
    for cb in range(B_Q_WIDTH // CB):
        part = jnp.dot(ob_ref[cb], wb_ref[cb * CB:(cb + 1) * CB, :], preferred_element_type=F32)
        pb = part if pb is None else pb + part
    n_gate = D_MODEL // CB
    merged = []
    for cb in range(n_gate):
        cols = slice(cb * CB, (cb + 1) * CB)
        ga = jax.nn.sigmoid(gate_ref[cb].astype(F32))
        gb = jax.nn.sigmoid(gate_ref[n_gate + cb].astype(F32))
        merged.append((ga * pa[:, cols] + gb * pb[:, cols]).astype(BF16))
    merged = jnp.concatenate(merged, axis=1)
    x1 = x_ref[...] + jnp.dot(merged, wo_ref[...], preferred_element_type=F32)
    x1_ref[...] = x1
    h2 = _rms(x1, gf_ref[...])
    h2_ref[...] = h2.astype(BF16)

    h2_hi = h2.astype(BF16)
    h2_lo = (h2 - h2_hi.astype(F32)).astype(BF16)
    wr = wr_ref[...]
    wr_hi = wr.astype(BF16)
    wr_lo = (wr - wr_hi.astype(F32)).astype(BF16)
    nt = (((1,), (1,)), ((), ()))
    hi_terms = lax.dot_general(jnp.concatenate([wr_hi, wr_lo], axis=0), h2_hi, nt, preferred_element_type=F32)
    logits = (hi_terms[:ROUTER_LANES]
              + (lax.dot_general(wr_hi, h2_lo, nt, preferred_element_type=F32) + hi_terms[ROUTER_LANES:])
              ) + br_ref[...]
    sub = lax.broadcasted_iota(jnp.int32, (EXPERTS_PER_GROUP, tm), 0)
    neg_inf = jnp.float32(-jnp.inf)
    top = lambda v: jnp.max(v, axis=0, keepdims=True)
    first = lambda hit: jnp.min(jnp.where(hit, sub, EXPERTS_PER_GROUP), axis=0, keepdims=True)
    lg = jnp.where(sub < N_EXPERT_GROUPS, logits[:EXPERTS_PER_GROUP], neg_inf)
    mg = top(lg)
    gsel = first(lg == mg)
    pg_sel = 1.0 / jnp.sum(jnp.exp(lg - mg), axis=0, keepdims=True)
    le = logits[EXPERT_LANE0:EXPERT_LANE0 + EXPERTS_PER_GROUP]
    for g in range(1, N_EXPERT_GROUPS):
        r0 = EXPERT_LANE0 + g * EXPERTS_PER_GROUP
        le = jnp.where(gsel == g, logits[r0:r0 + EXPERTS_PER_GROUP], le)
    ex = jnp.exp(le - top(le))
    pe = ex / jnp.sum(ex, axis=0, keepdims=True)
    p1 = top(pe)
    i1 = first(pe == p1)
    rest = sub != i1
    p2 = top(jnp.where(rest, pe, -1.0))
    i2 = first(rest & (pe == p2))
    norm = pg_sel / (p1 + p2)
    e1 = gsel * EXPERTS_PER_GROUP + i1
    e2 = gsel * EXPERTS_PER_GROUP + i2
    row = lax.broadcasted_iota(jnp.int32, (ROUTER_LANES, tm), 0)
    route_t = jnp.where(
        row == 0, e1.astype(F32),
        jnp.where(row == 1, e2.astype(F32),
                  jnp.where(row == 2, p1 * norm, jnp.where(row == 3, p2 * norm, 0.0))))
    route_ref[...] = route_t.T
    hits_t = jnp.where((row == e1 + EXPERT_LANE0) | (row == e2 + EXPERT_LANE0), 1.0, 0.0).astype(BF16)
    cnt_ref[0] = lax.dot_general(jnp.ones((cnt_ref.shape[1], tm), BF16), hits_t, nt,
                                 preferred_element_type=F32)


def _post_attn(oa, lse, ob, z_tok, x2, w_proj_a, w_proj_b, w_out, g_ffn, w_router, b_router, *, tm):
    n_tok = x2.shape[0]
    const2 = lambda i: (0, 0)
    blk = lambda nb: pl.BlockSpec((nb, tm, CB), lambda i: (0, i, 0))
    dil_blk = lambda dil: pl.BlockSpec((A_BLOCKS, tm // dil, dil * CB), lambda i: (0, i, 0))
    row = lambda width: pl.BlockSpec((tm, width), lambda i: (i, 0))
    resident = lambda shape: pl.BlockSpec(shape, const2, pipeline_mode=pl.Buffered(1))
    d1, d2 = DIL_GROUPS[1][1], DIL_GROUPS[2][1]
    perm_o = jnp.stack([_dilation_perm(OUT_PERM_BLOCK, d, BF16).T for d in (d1, d2)])
    perm_l = jnp.stack([_dilation_perm(LSE_PERM_BLOCK, d, BF16).T for d in (d1, d2)])
    return pl.pallas_call(
        functools.partial(_post_attn_kernel, tm=tm),
        grid=(n_tok // tm,),
        in_specs=[blk(A_BLOCKS), dil_blk(d1), dil_blk(d2), blk(A_BLOCKS), dil_blk(d1), dil_blk(d2),
                  blk(B_Q_WIDTH // CB), blk(GATE_WIDTH // CB), row(D_MODEL),
                  resident((A_GROUP_WIDTH, D_MODEL)), resident((B_Q_WIDTH, D_MODEL)),
                  resident((D_MODEL, D_MODEL)), resident((1, D_MODEL)),
                  resident((ROUTER_LANES, D_MODEL)), resident((ROUTER_LANES, 1)),
                  pl.BlockSpec((2, OUT_PERM_BLOCK, OUT_PERM_BLOCK), lambda i: (0, 0, 0),
                               pipeline_mode=pl.Buffered(1)),
                  pl.BlockSpec((2, LSE_PERM_BLOCK, LSE_PERM_BLOCK), lambda i: (0, 0, 0),
                               pipeline_mode=pl.Buffered(1))],
        out_specs=[row(D_MODEL), row(D_MODEL), row(ROUTER_LANES),
                   pl.BlockSpec((1, 8, ROUTER_LANES), lambda i: (i, 0, 0))],
        out_shape=[jax.ShapeDtypeStruct((n_tok, D_MODEL), F32),
                   jax.ShapeDtypeStruct((n_tok, D_MODEL), BF16),
                   jax.ShapeDtypeStruct((n_tok, ROUTER_LANES), F32),
                   jax.ShapeDtypeStruct((n_tok // tm, 8, ROUTER_LANES), F32)],
        compiler_params=pltpu.CompilerParams(
            dimension_semantics=("arbitrary",), vmem_limit_bytes=VMEM_LIMIT),
        name="post_attn",
    )(oa[0], oa[1], oa[2], lse[0], lse[1], lse[2], ob, z_tok, x2,
      w_proj_a, w_proj_b, w_out, g_ffn, w_router, b_router, perm_o, perm_l)


def _local_slots(route, lbase_row, ltri):
    lane = lax.broadcasted_iota(jnp.int32, route.shape, 1)
    pick = lambda k: jnp.sum(jnp.where(lane == k, route, 0.0), axis=-1, keepdims=True)
    lanef = lane.astype(F32)
    oh1, oh2 = lanef == pick(0), lanef == pick(1)
    oh = jnp.where(oh1 | oh2, 1.0, 0.0).astype(BF16)
    table = jnp.dot(ltri, oh, preferred_element_type=F32) + lbase_row
    ls1 = jnp.sum(jnp.where(oh1, table, 0.0), axis=-1, keepdims=True)
    ls2 = jnp.sum(jnp.where(oh2, table, 0.0), axis=-1, keepdims=True)
    return ls1, ls2, pick


def _chunk_loop(count, body, unroll=1):
    one = lambda c, carry: (body(c), carry)[1]
    full = 0
    if unroll > 1:
        full = (count // unroll) * unroll

        def group(g, carry):
            for u in range(unroll):
                body(g * unroll + u)
            return carry

        lax.fori_loop(0, count // unroll, group, 0)
    lax.fori_loop(full, count, one, 0)


def _wait_chunks(count, wait_rows):
    _chunk_loop(count // WAIT_GROUP, lambda c: wait_rows(WAIT_GROUP * CHUNK))
    _chunk_loop(count % WAIT_GROUP, lambda c: wait_rows(CHUNK))


def _selection_blocks(slots_a, slots_b, axis, other):
    shape = (SEL_BLOCK, other) if axis == 0 else (other, SEL_BLOCK)
    local = lax.broadcasted_iota(jnp.int32, shape, axis).astype(F32).astype(BF16)
    one, zero = jnp.ones_like(local), jnp.zeros_like(local)

    def block(a):
        rel_a = (slots_a - float(a * SEL_BLOCK)).astype(BF16)
        rel_b = (slots_b - float(a * SEL_BLOCK)).astype(BF16)
        return jnp.where(local == rel_a, one, jnp.where(local == rel_b, one, zero))

    return block


def _for_used_blocks(used_rows, body):
    always = 2 * MOE_TILE // SEL_BLOCK + 1
    body(0, always, True)
    for a in range(always, LOCAL_ROWS // SEL_BLOCK):
        pl.when(used_rows > a * SEL_BLOCK)(functools.partial(body, a, a + 1, False))


def _issue_copies(copy_refs, tile, start):
    pair_src, pair_dst, n_pair, one_src, one_dst, n_one = copy_refs
    for src_ref, dst_ref, n_ref, slots, size in ((pair_src, pair_dst, n_pair, PAIR_SLOTS, 2),
                                                  (one_src, one_dst, n_one, N_EXPERTS, 1)):
        _chunk_loop(n_ref[tile], lambda c: start(
            pl.multiple_of(src_ref[tile * slots + c] * CHUNK, CHUNK),
            pl.multiple_of(dst_ref[tile * slots + c] * CHUNK, CHUNK), size * CHUNK), unroll=ISSUE_UNROLL)


def _dispatch_kernel(pair_src_ref, pair_dst_ref, n_pair_ref, one_src_ref, one_dst_ref, n_one_ref,
                     tot_ref, tail_start_ref, tail_n_ref,
                     h2_ref, route_ref, lbase_ref, ltri_ref, xs_ref, buf_ref, zero_ref, sem_ref, zsem_ref,
                     *, tm):
    copy_refs = (pair_src_ref, pair_dst_ref, n_pair_ref, one_src_ref, one_dst_ref, n_one_ref)
    i = pl.program_id(0)
    last = pl.num_programs(0) - 1
    slot = i % 2

    def chunk(slot_, src_row, dst_row, rows=CHUNK):
        return pltpu.make_async_copy(buf_ref.at[slot_, pl.ds(src_row, rows), :],
                                     xs_ref.at[pl.ds(dst_row, rows), :], sem_ref.at[slot_])

    def wait_tile(slot_, tile):
        _wait_chunks(tot_ref[tile], lambda rows: chunk(slot_, 0, 0, rows).wait())

    @pl.when(i == 0)
    def _():
        zero_ref[...] = jnp.zeros_like(zero_ref)

        def zero_chunk(row):
            return pltpu.make_async_copy(zero_ref, xs_ref.at[pl.ds(row, CHUNK), :], zsem_ref)

        def per_expert(e, total):
            start = tail_start_ref[e] * CHUNK
            _chunk_loop(tail_n_ref[e], lambda c: zero_chunk(pl.multiple_of(start + c * CHUNK, CHUNK)).start())
            return total + tail_n_ref[e]

        total = lax.fori_loop(0, N_EXPERTS, per_expert, 0)
        _chunk_loop(total, lambda c: zero_chunk(0).wait())

    @pl.when(i >= 2)
    def _():
        wait_tile(slot, i - 2)

    route = route_ref[...]
    ls1, ls2, pick = _local_slots(route, lbase_ref[0, 0:1, :], ltri_ref[...])
    lane = lax.broadcasted_iota(jnp.int32, route.shape, 1)
    ls_t = jnp.where(lane == 0, ls1, jnp.where(lane == 1, ls2, 0.0)).T
    sel_block = _selection_blocks(ls_t[0:1, :], ls_t[1:2, :], 0, tm)
    lane_expert = (lane // META_PIECES).astype(F32)
    lane_piece = lane % META_PIECES
    by_piece = lambda pieces: jnp.where(lane_piece == 0, pieces[0],
                                        jnp.where(lane_piece == 1, pieces[1], pieces[2]))
    meta = jnp.where(lane_expert == pick(0), by_piece(_split3(pick(2))),
                     jnp.where(lane_expert == pick(1), by_piece(_split3(pick(3))), 0.0))
    meta = meta.astype(BF16)

    def sort_block(a0, a1, first):
        del first
        rows = slice(a0 * SEL_BLOCK, a1 * SEL_BLOCK)
        sel = jnp.concatenate([sel_block(a) for a in range(a0, a1)], axis=0)
        buf_ref[slot, rows, :D_MODEL] = jnp.dot(sel, h2_ref[...], preferred_element_type=F32).astype(BF16)
        buf_ref[slot, rows, D_MODEL:] = jnp.dot(sel, meta, preferred_element_type=F32).astype(BF16)

    _for_used_blocks(tot_ref[i] * CHUNK, sort_block)

    _issue_copies(copy_refs, i, lambda local_row, xs_row, rows: chunk(slot, local_row, xs_row, rows).start())

    @pl.when(i == last)
    def _():
        @pl.when(i >= 1)
        def _():
            wait_tile(1 - slot, i - 1)
        wait_tile(slot, i)


def _expert_kernel(te_ref, nv_ref, xs_ref, wg_ref, wu_ref, wd_ref, ys_ref, wgb_ref, wub_ref, wdb_ref):
    g = pl.program_id(0)
    e = te_ref[g]

    @pl.when((g == 0) | (te_ref[jnp.maximum(g - 1, 0)] != e))
    def _():
        wgb_ref[...] = wg_ref[...].astype(BF16)
        wub_ref[...] = wu_ref[...].astype(BF16)
        wdb_ref[...] = wd_ref[...].astype(BF16)

    @pl.when(g < nv_ref[0])
    def _():
        x = xs_ref[:, :D_MODEL]
        meta = xs_ref[:, D_MODEL:].astype(F32)
        lane = lax.broadcasted_iota(jnp.int32, meta.shape, 1)
        mine = (lane >= e * META_PIECES) & (lane < (e + 1) * META_PIECES)
        w = jnp.sum(jnp.where(mine, meta, 0.0), axis=-1, keepdims=True)
        hg = jnp.dot(x, wgb_ref[...], preferred_element_type=F32)
        hu = jnp.dot(x, wub_ref[...], preferred_element_type=F32)
        a = (hg * jax.nn.sigmoid(hg)) * hu * w
        ys_ref[...] = jnp.dot(a.astype(BF16), wdb_ref[...], preferred_element_type=F32).astype(BF16)


def _combine_kernel(pair_src_ref, pair_dst_ref, n_pair_ref, one_src_ref, one_dst_ref, n_one_ref, tot_ref,
                    x1_ref, route_ref, lbase_ref, ltri_ref, gfin_ref, ys_ref, out_ref, ybuf_ref, sem_ref,
                    *, tm):
    copy_refs = (pair_src_ref, pair_dst_ref, n_pair_ref, one_src_ref, one_dst_ref, n_one_ref)
    i = pl.program_id(0)
    n_tiles = pl.num_programs(0)
    slot = i % 2

    def chunk(slot_, src_row, dst_row, rows=CHUNK):
        return pltpu.make_async_copy(ys_ref.at[pl.ds(src_row, rows), :],
                                     ybuf_ref.at[slot_, pl.ds(dst_row, rows), :], sem_ref.at[slot_])

    def fetch_tile(tile, slot_):
        _issue_copies(copy_refs, tile,
                      lambda local_row, ys_row, rows: chunk(slot_, ys_row, local_row, rows).start())

    @pl.when(i == 0)
    def _():
        ybuf_ref[...] = jnp.zeros_like(ybuf_ref)
        fetch_tile(0, 0)

    @pl.when(i + 1 < n_tiles)
    def _():
        fetch_tile(i + 1, 1 - slot)

    _wait_chunks(tot_ref[i], lambda rows: chunk(slot, 0, 0, rows).wait())

    ls1, ls2, _ = _local_slots(route_ref[...], lbase_ref[0, 0:1, :], ltri_ref[...])
    sel_block = _selection_blocks(ls1, ls2, 1, tm)
    def add_block(a0, a1, first):
        rows = slice(a0 * SEL_BLOCK, a1 * SEL_BLOCK)
        sel = jnp.concatenate([sel_block(a) for a in range(a0, a1)], axis=1)
        y = jnp.dot(sel, ybuf_ref[slot, rows, :], preferred_element_type=F32)
        out_ref[...] = (x1_ref[...] if first else out_ref[...]) + y

    _for_used_blocks(tot_ref[i] * CHUNK, add_block)
    out_ref[...] = _rms(out_ref[...], gfin_ref[...])


def _routing_tables(cnt, n_exp_tiles):
    c16 = (cnt + CHUNK - 1) // CHUNK
    lbase = jnp.cumsum(c16, axis=1) - c16
    tile_off = jnp.cumsum(c16, axis=0) - c16
    tot = jnp.sum(c16, axis=0)
    per = EXP_TILE // CHUNK
    region_tiles = (tot + per - 1) // per
    region = region_tiles * per
    base = jnp.cumsum(region) - region
    dst = base[None, :] + tile_off
    tile_end = jnp.cumsum(region_tiles)
    n_valid = tile_end[-1]
    g = jnp.arange(n_exp_tiles, dtype=jnp.int32)
    tile_expert = jnp.sum(tile_end[None, :] <= jnp.minimum(g, n_valid - 1)[:, None], axis=1).astype(jnp.int32)
    i32 = lambda a: a.astype(jnp.int32).reshape(-1)

    def copy_list(per_seg, first, size, slots):
        start = jnp.cumsum(per_seg, axis=1) - per_seg
        k = jnp.arange(slots, dtype=jnp.int32)[None, :, None]
        owns = (start[:, None, :] <= k) & (k < (start + per_seg)[:, None, :])
        inside = (first[:, None, :] + (k - start[:, None, :]) * size)
        pick = lambda seg_start: jnp.sum(jnp.where(owns, seg_start[:, None, :] + inside, 0), axis=2)
        return i32(pick(lbase)), i32(pick(dst)), i32(jnp.sum(per_seg, axis=1))

    pair_src, pair_dst, n_pair = copy_list(c16 // 2, jnp.zeros_like(c16), 2, PAIR_SLOTS)
    one_src, one_dst, n_one = copy_list(c16 % 2, (c16 // 2) * 2, 1, N_EXPERTS)
    return dict(copies=(pair_src, pair_dst, n_pair, one_src, one_dst, n_one), tot=i32(jnp.sum(c16, axis=1)),
                tail_start=i32(base + tot), tail_n=i32(region - tot),
                tile_expert=tile_expert, n_valid=i32(n_valid),
                lbase_rows=(lbase * CHUNK).astype(F32))


def _moe(h2, route, cnt_tiles, x1, w_gate, w_up, w_down, g_final):
    n_tok = h2.shape[0]
    tm = MOE_TILE
    n_tiles = n_tok // tm
    assert cnt_tiles.shape[0] == n_tiles
    worst_rows = 2 * n_tok + n_tiles * N_EXPERTS * (CHUNK - 1) + N_EXPERTS * (EXP_TILE - CHUNK)
    n_exp_tiles = -(-worst_rows // EXP_TILE)
    n_slots = n_exp_tiles * EXP_TILE

    cnt = cnt_tiles[:, 0, EXPERT_LANE0:EXPERT_LANE0 + N_EXPERTS].astype(jnp.int32)
    tb = _routing_tables(cnt, n_exp_tiles)
    lbase_rows = jnp.zeros((n_tiles, 8, ROUTER_LANES), F32).at[:, :, :N_EXPERTS].set(
        tb["lbase_rows"][:, None, :])
    row_id = np.arange(tm)
    ltri = jnp.asarray((row_id[:, None] > row_id[None, :]).astype(np.float32), dtype=BF16)

    tile_row = lambda width: pl.BlockSpec((tm, width), lambda i, *_: (i, 0))
    lbase_spec = pl.BlockSpec((1, 8, ROUTER_LANES), lambda i, *_: (i, 0, 0))
    ltri_spec = pl.BlockSpec((tm, tm), lambda i, *_: (0, 0), pipeline_mode=pl.Buffered(1))
    hbm = pl.BlockSpec(memory_space=pl.ANY)

    xs = pl.pallas_call(
        functools.partial(_dispatch_kernel, tm=tm),
        grid_spec=pltpu.PrefetchScalarGridSpec(
            num_scalar_prefetch=9, grid=(n_tiles,),
            in_specs=[tile_row(D_MODEL), tile_row(ROUTER_LANES), lbase_spec, ltri_spec],
            out_specs=hbm,
            scratch_shapes=[pltpu.VMEM((2, LOCAL_ROWS, XS_WIDTH), BF16), pltpu.VMEM((CHUNK, XS_WIDTH), BF16),
                            pltpu.SemaphoreType.DMA((2,)), pltpu.SemaphoreType.DMA]),
        out_shape=jax.ShapeDtypeStruct((n_slots, XS_WIDTH), BF16),
        compiler_params=pltpu.CompilerParams(
            dimension_semantics=("arbitrary",), vmem_limit_bytes=VMEM_LIMIT),
        name="moe_dispatch",
    )(*tb["copies"], tb["tot"], tb["tail_start"], tb["tail_n"], h2, route, lbase_rows, ltri)

    row_tile = lambda width: pl.BlockSpec(
        (EXP_TILE, width), lambda g, te, nv: (jnp.maximum(jnp.minimum(g, nv[0] - 1), 0), 0))
    ys = pl.pallas_call(
        _expert_kernel,
        grid_spec=pltpu.PrefetchScalarGridSpec(
            num_scalar_prefetch=2, grid=(n_exp_tiles,),
            in_specs=[row_tile(XS_WIDTH),
                      pl.BlockSpec((None, D_MODEL, D_EXPERT), lambda g, te, nv: (te[g], 0, 0)),
                      pl.BlockSpec((None, D_MODEL, D_EXPERT), lambda g, te, nv: (te[g], 0, 0)),
                      pl.BlockSpec((None, D_EXPERT, D_MODEL), lambda g, te, nv: (te[g], 0, 0))],
            out_specs=row_tile(D_MODEL),
            scratch_shapes=[pltpu.VMEM((D_MODEL, D_EXPERT), BF16), pltpu.VMEM((D_MODEL, D_EXPERT), BF16),
                            pltpu.VMEM((D_EXPERT, D_MODEL), BF16)]),
        out_shape=jax.ShapeDtypeStruct((n_slots, D_MODEL), BF16),
        compiler_params=pltpu.CompilerParams(
            dimension_semantics=("arbitrary",), vmem_limit_bytes=VMEM_LIMIT),
        name="moe_expert",
    )(tb["tile_expert"], tb["n_valid"], xs, w_gate, w_up, w_down)

    return pl.pallas_call(
        functools.partial(_combine_kernel, tm=tm),
        grid_spec=pltpu.PrefetchScalarGridSpec(
            num_scalar_prefetch=7, grid=(n_tiles,),
            in_specs=[tile_row(D_MODEL), tile_row(ROUTER_LANES), lbase_spec, ltri_spec,
                      pl.BlockSpec((1, D_MODEL), lambda i, *_: (0, 0)), hbm],
            out_specs=tile_row(D_MODEL),
            scratch_shapes=[pltpu.VMEM((2, LOCAL_ROWS, D_MODEL), BF16), pltpu.SemaphoreType.DMA((2,))]),
        out_shape=jax.ShapeDtypeStruct((n_tok, D_MODEL), F32),
        compiler_params=pltpu.CompilerParams(
            dimension_semantics=("arbitrary",), vmem_limit_bytes=VMEM_LIMIT),
        name="moe_combine",
    )(*tb["copies"], tb["tot"], x1, route, lbase_rows, ltri, g_final, ys)


def _rope_tables(seq, tm):
    pos = np.arange(seq, dtype=np.float64)
    inv_freq = ROPE_THETA ** (-np.arange(0, HEAD_DIM, 2, dtype=np.float64) / HEAD_DIM)
    ang = pos[:, None] * inv_freq[None, :]
    cos, sin = np.cos(ang), np.sin(ang)
    reps = CB // HEAD_DIM
    cos_t = np.tile(np.concatenate([cos, cos], axis=-1), (1, reps))
    sin_t = np.tile(np.concatenate([-sin, sin], axis=-1), (1, reps))

    def reorder(t, dil):
        return t.reshape(seq // tm, tm // dil, dil, CB).transpose(0, 2, 1, 3).reshape(seq, CB)

    dils = [dil for _, dil in DIL_GROUPS]
    return (jnp.asarray(np.stack([reorder(cos_t, dil) for dil in dils]).astype(np.float32)),
            jnp.asarray(np.stack([reorder(sin_t, dil) for dil in dils]).astype(np.float32)))


def kernel(x, w_in, b_in, sinks, w_proj_a, w_proj_b, w_out, g_mix, g_ffn, w_router_group, b_router_group,
           w_router_expert, b_router_expert, w_exp_gate, w_exp_up, w_exp_down, g_final):
    batch, seq, d = x.shape
    assert d == D_MODEL and w_in.shape[0] == 1, "single-layer kernel"
    n_tok = batch * seq
    x2 = x.reshape(n_tok, d)
    assert seq % ATTN_ROWS_PER_STEP == 0 and n_tok % MOE_TILE == 0 and seq % IN_TILE == 0
    cos_t, sin_t = _rope_tables(seq, IN_TILE)

    z_tok, z_d1, z_d2 = _in_proj(x2, g_mix[0][None, :], w_in[0].astype(BF16), b_in[0][None, :],
                                 cos_t, sin_t, seq=seq, tm=IN_TILE)

    a_bases = (0, A_BLOCKS, 2 * A_BLOCKS)
    tok_bases = tuple(ZB_A0 + b for b in a_bases)
    oa, lse = [], []
    for group, (z, bases) in enumerate(((z_tok, tok_bases), (z_d1, a_bases), (z_d2, a_bases))):
        o_g, l_g = _dilated_attention(z, bases, group, batch=batch, seq=seq, lq=ATTN_ROWS_PER_STEP)
        oa.append(o_g)
        lse.append(l_g)
    ob = _swa_attention(z_tok, sinks[0], batch=batch, seq=seq, lq=ATTN_ROWS_PER_STEP)

    gap = EXPERT_LANE0 - N_EXPERT_GROUPS
    tail = ROUTER_LANES - EXPERT_LANE0 - N_EXPERTS
    w_router = jnp.concatenate(
        [w_router_group[0].T, jnp.zeros((gap, d), F32), w_router_expert[0].T, jnp.zeros((tail, d), F32)], axis=0)
    b_router = jnp.concatenate(
        [b_router_group[0], jnp.zeros((gap,), F32), b_router_expert[0], jnp.zeros((tail,), F32)])[:, None]
    x1, h2, route, cnt_tiles = _post_attn(
        oa, lse, ob, z_tok, x2, w_proj_a[0].astype(BF16), w_proj_b[0].astype(BF16), w_out[0].astype(BF16),
        g_ffn[0][None, :], w_router, b_router, tm=MOE_TILE)

    out = _moe(h2, route, cnt_tiles, x1, w_exp_gate[0], w_exp_up[0], w_exp_down[0], g_final[None, :])
    return out.reshape(batch, seq, d)
```

```python
import functools
import math

import jax
import jax.numpy as jnp
import numpy as np
from jax import lax
from jax.experimental import pallas as pl
from jax.experimental.pallas import tpu as pltpu

F32 = jnp.float32
BF16 = jnp.bfloat16

D_MODEL = 1024
HEAD_DIM = 64
HALF = HEAD_DIM // 2
ROPE_THETA = 10000.0
RMS_EPS = 1e-6
LOG2E = math.log2(math.e)
LN2 = math.log(2.0)
Q_SCALE = LOG2E * HEAD_DIM ** -0.5
BLOCK = 128
ATTN_ROWS_PER_STEP = 2048
DIL_GROUPS = ((128, 1), (512, 4), (2048, 16))
N_DIL = len(DIL_GROUPS)
A_GROUP_WIDTH = 512
A_QKV_WIDTH = 3 * N_DIL * A_GROUP_WIDTH
B_Q_HEADS = 16
B_KV_HEADS = 2
B_Q_WIDTH = B_Q_HEADS * HEAD_DIM
B_WINDOW = 128
GATE_WIDTH = 2 * D_MODEL
IN_WIDTH = A_QKV_WIDTH + B_Q_WIDTH + 2 * B_KV_HEADS * HEAD_DIM + GATE_WIDTH
N_EXPERT_GROUPS = 4
EXPERTS_PER_GROUP = 8
N_EXPERTS = N_EXPERT_GROUPS * EXPERTS_PER_GROUP
D_EXPERT = D_MODEL // 4

CB = 256
PAIR = 128
N_IN_BLOCKS = IN_WIDTH // CB
A_BLOCKS = A_GROUP_WIDTH // CB
ZB_GATE = 0
ZB_QB = 8
ZB_KB = 12
ZB_VB = 13
ZB_A0 = 14
N_TOK_BLOCKS = ZB_A0 + 3 * A_BLOCKS
OUT_PERM_BLOCK = 256
LSE_PERM_BLOCK = 128
ROUTER_LANES = 128
EXPERT_LANE0 = EXPERTS_PER_GROUP
assert N_EXPERT_GROUPS <= EXPERTS_PER_GROUP
MOE_TILE = 512
CHUNK = 16
EXP_TILE = 1024
LOCAL_ROWS = -(-(2 * MOE_TILE + N_EXPERTS * (CHUNK - 1)) // CB) * CB
PAIR_SLOTS = LOCAL_ROWS // CHUNK // 2
SEL_BLOCK = CB
ISSUE_UNROLL = 8
WAIT_GROUP = 8
XS_WIDTH = D_MODEL + ROUTER_LANES
META_PIECES = 3
assert META_PIECES * N_EXPERTS <= ROUTER_LANES

IN_TILE = 512
V7X_VMEM_BYTES = 64 * 1024 * 1024
VMEM_LIMIT = V7X_VMEM_BYTES * 7 // 8


def _in_proj_plan():
    plan = []
    for c in range(N_IN_BLOCKS):
        col = c * CB
        if col < A_QKV_WIDTH:
            part, rem = divmod(col, N_DIL * A_GROUP_WIDTH)
            group, blk = divmod(rem // CB, A_BLOCKS)
            kind = ("q", "k", "v")[part]
            dil = DIL_GROUPS[group][1]
            if dil == 1:
                plan.append((0, ZB_A0 + part * A_BLOCKS + blk, kind, 1))
            else:
                plan.append((group, part * A_BLOCKS + blk, kind, dil))
        elif col < A_QKV_WIDTH + B_Q_WIDTH:
            plan.append((0, ZB_QB + (col - A_QKV_WIDTH) // CB, "q", 1))
        elif col < A_QKV_WIDTH + B_Q_WIDTH + CB:
            plan.append((0, -1, "kvb", 1))
        else:
            plan.append((0, ZB_GATE + (col - (A_QKV_WIDTH + B_Q_WIDTH + CB)) // CB, "v", 1))
    return tuple(plan)


def _rms(x, g):
    return x * lax.rsqrt(jnp.mean(x * x, axis=-1, keepdims=True) + RMS_EPS) * g


def _split3(w):
    hi = w.astype(BF16).astype(F32)
    mid = (w - hi).astype(BF16).astype(F32)
    lo = (w - hi - mid).astype(BF16).astype(F32)
    return hi, mid, lo


def _rope(acc, cos, sin_signed, first_half):
    partner = jnp.where(first_half, pltpu.roll(acc, CB - HALF, 1), pltpu.roll(acc, HALF, 1))
    return acc * cos + partner * sin_signed


def _in_perm_block(dil):
    return max(BLOCK, CHUNK * dil)


def _in_proj_kernel(x_ref, g_ref, w_ref, b_ref, cos_ref, sin_ref, perm1_ref, perm2_ref,
                    zt_ref, zd1_ref, zd2_ref, *, plan, tm):
    out_refs = (zt_ref, zd1_ref, zd2_ref)
    perm_refs = (None, perm1_ref, perm2_ref)
    lane = lax.broadcasted_iota(jnp.int32, (tm, CB), 1)
    first_half = (lane % HEAD_DIM) < HALF

    h = _rms(x_ref[...], g_ref[...]).astype(BF16)
    h_by_dil, tables = {}, {}
    for slot, (_, dil) in enumerate(DIL_GROUPS):
        tables[dil] = slot
        if dil == 1:
            h_by_dil[dil] = h
            continue
        blk = _in_perm_block(dil)
        n = blk // dil
        moved = [jnp.dot(perm_refs[slot][...], h[tb * blk:(tb + 1) * blk], preferred_element_type=F32
                         ).astype(BF16) for tb in range(tm // blk)]
        h_by_dil[dil] = jnp.concatenate(
            [part[r * n:(r + 1) * n] for r in range(dil) for part in moved], axis=0)

    for c, (arr, dst, kind, dil) in enumerate(plan):
        cols = slice(c * CB, (c + 1) * CB)
        acc = jnp.dot(h_by_dil[dil], w_ref[:, cols], preferred_element_type=F32) + b_ref[:, cols]
        if kind in ("q", "k", "kvb"):
            slot = tables[dil]
            rot = _rope(acc, cos_ref[slot], sin_ref[slot], first_half)
        if kind == "q":
            val = (rot * Q_SCALE).astype(BF16)
        elif kind == "k":
            val = rot.astype(BF16)
        elif kind == "v":
            val = acc.astype(BF16)
        else:
            r64 = pltpu.roll(rot, HEAD_DIM, 1)
            r128 = pltpu.roll(rot, 2 * HEAD_DIM, 1)
            kdup = jnp.where(lane < HEAD_DIM, rot, jnp.where(lane < 3 * HEAD_DIM, r64, r128))
            a128 = pltpu.roll(acc, 2 * HEAD_DIM, 1)
            a192 = pltpu.roll(acc, 3 * HEAD_DIM, 1)
            vdup = jnp.where(lane < HEAD_DIM, a128, jnp.where(lane < 3 * HEAD_DIM, a192, acc))
            zt_ref[ZB_KB] = kdup.astype(BF16)
            zt_ref[ZB_VB] = vdup.astype(BF16)
            continue
        if dil == 1:
            out_refs[arr][dst] = val
        else:
            n = tm // dil
            for r in range(dil):
                out_refs[arr][dst, :, r * CB:(r + 1) * CB] = val[r * n:(r + 1) * n, :]


def _dilation_perm(block, dil, dtype):
    j = np.arange(block)
    src = (j % (block // dil)) * dil + j // (block // dil)
    return jnp.asarray((src[:, None] == j[None, :]).astype(np.float32), dtype=dtype)


def _in_proj(x2, g_mix, w_in, b_in, cos_t, sin_t, *, seq, tm):
    n_tok = x2.shape[0]
    tiles_per_seq = seq // tm
    const = lambda i: (0, 0)
    table = pl.BlockSpec((N_DIL, tm, CB), lambda i: (0, i % tiles_per_seq, 0))
    d1, d2 = DIL_GROUPS[1][1], DIL_GROUPS[2][1]
    perms = [_dilation_perm(_in_perm_block(d), d, BF16) for d in (d1, d2)]
    perm_spec = lambda d: pl.BlockSpec((_in_perm_block(d),) * 2, const, pipeline_mode=pl.Buffered(1))
    return pl.pallas_call(
        functools.partial(_in_proj_kernel, plan=_in_proj_plan(), tm=tm),
        grid=(n_tok // tm,),
        in_specs=[
            pl.BlockSpec((tm, D_MODEL), lambda i: (i, 0)),
            pl.BlockSpec((1, D_MODEL), const),
            pl.BlockSpec((D_MODEL, IN_WIDTH), const, pipeline_mode=pl.Buffered(1)),
            pl.BlockSpec((1, IN_WIDTH), const),
            table, table,
            perm_spec(d1), perm_spec(d2),
        ],
        out_specs=[
            pl.BlockSpec((N_TOK_BLOCKS, tm, CB), lambda i: (0, i, 0)),
            pl.BlockSpec((3 * A_BLOCKS, tm // d1, d1 * CB), lambda i: (0, i, 0)),
            pl.BlockSpec((3 * A_BLOCKS, tm // d2, d2 * CB), lambda i: (0, i, 0)),
        ],
        out_shape=[
            jax.ShapeDtypeStruct((N_TOK_BLOCKS, n_tok, CB), BF16),
            jax.ShapeDtypeStruct((3 * A_BLOCKS, n_tok // d1, d1 * CB), BF16),
            jax.ShapeDtypeStruct((3 * A_BLOCKS, n_tok // d2, d2 * CB), BF16),
        ],
        compiler_params=pltpu.CompilerParams(
            dimension_semantics=("arbitrary",), vmem_limit_bytes=VMEM_LIMIT),
        name="in_proj",
    )(x2, g_mix, w_in, b_in, cos_t, sin_t, *perms)


def _attn_kernel(*refs, lq, max_dist, kv_shared, has_sink, want_lse, n_axes):
    refs = list(refs)
    sink_ref = refs.pop(0) if has_sink else None
    q_ref, k_ref, v_ref, kp_ref, vp_ref = refs[:5]
    o_ref = refs[5]
    lse_ref = refs[6] if want_lse else None
    vaug_ref = refs[-1]
    n_qblk = q_ref.shape[0]
    n_kv_pairs = vaug_ref.shape[0]
    rows_kv = lq + BLOCK

    row = lax.broadcasted_iota(jnp.int32, (BLOCK, 2 * BLOCK), 0)
    col = lax.broadcasted_iota(jnp.int32, (BLOCK, 2 * BLOCK), 1)
    dist = row - col + BLOCK
    valid = (dist >= 0) & (dist <= max_dist)
    neg_inf = jnp.float32(-jnp.inf)
    bias = jnp.where(valid, 0.0, neg_inf)
    bias_first = jnp.where(valid & (col >= BLOCK), 0.0, neg_inf)
    bias0 = jnp.where(pl.program_id(n_axes - 1) == 0, bias_first, bias)
    lane_lo = lax.broadcasted_iota(jnp.int32, (BLOCK, PAIR), 1) < HEAD_DIM

    def rd(ref, blk, rows, cols):
        return ref[rows, cols] if kv_shared else ref[blk, rows, cols]

    first_step = functools.reduce(lambda a, b: a & b, [pl.program_id(a) == 0 for a in range(n_axes)])

    @pl.when(first_step)
    def _():
        lane = lax.broadcasted_iota(jnp.int32, (rows_kv, PAIR), 1)
        for pair in range(n_kv_pairs):
            vaug_ref[pair, 0, :, PAIR:] = jnp.where(lane < HEAD_DIM, 1.0, 0.0).astype(BF16)
            vaug_ref[pair, 1, :, PAIR:] = jnp.where(lane < HEAD_DIM, 0.0, 1.0).astype(BF16)

    lane_kv = lax.broadcasted_iota(jnp.int32, (rows_kv, PAIR), 1) < HEAD_DIM
    n_col_pairs = q_ref.shape[-1] // PAIR
    for pair in range(n_kv_pairs):
        cols = slice((pair % n_col_pairs) * PAIR, (pair % n_col_pairs + 1) * PAIR)
        everything = slice(None)
        v_all = jnp.concatenate([rd(vp_ref, pair // n_col_pairs, everything, cols),
                                 rd(v_ref, pair // n_col_pairs, everything, cols)], axis=0)
        zero = jnp.zeros_like(v_all)
        vaug_ref[pair, 0, :, :PAIR] = jnp.where(lane_kv, v_all, zero)
        vaug_ref[pair, 1, :, :PAIR] = jnp.where(lane_kv, zero, v_all)

    for ib in range(lq // BLOCK):
        rows = slice(ib * BLOCK, (ib + 1) * BLOCK)
        win = slice(ib * BLOCK, (ib + 2) * BLOCK)
        b_ib = bias0 if ib == 0 else bias
        for blk in range(n_qblk):
            for pp in range(n_col_pairs):
                qcols = slice(pp * PAIR, (pp + 1) * PAIR)
                pair = 0 if kv_shared else blk * n_col_pairs + pp
                kcols = slice(0, PAIR) if kv_shared else qcols
                if ib == 0:
                    k_win = jnp.concatenate([rd(kp_ref, blk, slice(None), kcols),
                                             rd(k_ref, blk, slice(0, BLOCK), kcols)], axis=0)
                else:
                    k_win = rd(k_ref, blk, slice((ib - 1) * BLOCK, (ib + 1) * BLOCK), kcols)
                q_pair = q_ref[blk, rows, qcols]
                ps, ms, sink_terms = [], [], []
                for hh in range(2):
                    q_h = jnp.where(lane_lo == (hh == 0), q_pair, jnp.zeros_like(q_pair))
                    s = lax.dot_general(q_h, k_win, (((1,), (1,)), ((), ())), preferred_element_type=F32)
                    s = s + b_ib
                    m = jnp.max(s, axis=-1, keepdims=True)
                    if has_sink:
                        head = (pl.program_id(1) * n_qblk + blk) * (CB // HEAD_DIM) + pp * 2 + hh
                        sink = sink_ref[head] * LOG2E
                        m = jnp.maximum(m, sink)
                        sink_terms.append(jnp.exp2(sink - m))
                    ps.append(jnp.exp2(s - m).astype(BF16))
                    ms.append(m)
                v_aug = jnp.concatenate([vaug_ref[pair, 0, win, :], vaug_ref[pair, 1, win, :]], axis=0)
                od = jnp.dot(jnp.concatenate(ps, axis=1), v_aug, preferred_element_type=F32)
                den = od[:, PAIR:]
                if has_sink:
                    den = den + jnp.where(lane_lo, sink_terms[0], sink_terms[1])
                o_ref[blk, rows, qcols] = (od[:, :PAIR] * (1.0 / den)).astype(BF16)
                if want_lse:
                    lse_ref[blk, rows, qcols] = (jnp.where(lane_lo, ms[0], ms[1]) + jnp.log2(den)) * LN2


def _dilated_attention(z, bases, group, *, batch, seq, lq):
    window, dil = DIL_GROUPS[group]
    sub_len = seq // dil
    lq = min(lq, sub_len)
    z4 = z.reshape(z.shape[0], batch, sub_len, dil * CB)
    qb, kb, vb = bases
    assert all(base % A_BLOCKS == 0 for base in bases)
    bpq = lq // BLOCK
    slabs = max(1, min(dil, ATTN_ROWS_PER_STEP // lq))
    width = slabs * CB
    cur = lambda base: pl.BlockSpec(
        (A_BLOCKS, None, lq, width), lambda b, r, i: (base // A_BLOCKS, b, i, r))
    prev = lambda base: pl.BlockSpec(
        (A_BLOCKS, None, BLOCK, width), lambda b, r, i: (base // A_BLOCKS, b, jnp.maximum(i * bpq - 1, 0), r))
    out_spec = pl.BlockSpec((A_BLOCKS, None, lq, width), lambda b, r, i: (0, b, i, r))
    o, lse = pl.pallas_call(
        functools.partial(_attn_kernel, lq=lq, max_dist=window // dil, kv_shared=False,
                          has_sink=False, want_lse=True, n_axes=3),
        grid=(batch, dil // slabs, sub_len // lq),
        in_specs=[cur(qb), cur(kb), cur(vb), prev(kb), prev(vb)],
        out_specs=[out_spec, out_spec],
        out_shape=[jax.ShapeDtypeStruct((A_BLOCKS, batch, sub_len, dil * CB), BF16),
                   jax.ShapeDtypeStruct((A_BLOCKS, batch, sub_len, dil * CB), F32)],
        scratch_shapes=[pltpu.VMEM((2 * A_BLOCKS * slabs, 2, lq + BLOCK, 2 * PAIR), BF16)],
        compiler_params=pltpu.CompilerParams(
            dimension_semantics=("arbitrary",) * 3, vmem_limit_bytes=VMEM_LIMIT),
        name=f"dilated_attn_g{group}",
    )(z4, z4, z4, z4, z4)
    rows = batch * sub_len
    return o.reshape(A_BLOCKS, rows, dil * CB), lse.reshape(A_BLOCKS, rows, dil * CB)


def _swa_attention(z_tok, sinks, *, batch, seq, lq):
    z4 = z_tok.reshape(N_TOK_BLOCKS, batch, seq, CB)
    bpq = lq // BLOCK
    n_q_blocks = B_Q_WIDTH // CB
    q_per_kv = n_q_blocks // B_KV_HEADS
    assert ZB_QB % q_per_kv == 0
    q_spec = pl.BlockSpec((q_per_kv, None, lq, CB), lambda b, kvh, i, s: (ZB_QB // q_per_kv + kvh, b, i, 0))
    cur = lambda base: pl.BlockSpec((None, None, lq, PAIR), lambda b, kvh, i, s: (base, b, i, kvh))
    prev = lambda base: pl.BlockSpec(
        (None, None, BLOCK, PAIR), lambda b, kvh, i, s: (base, b, jnp.maximum(i * bpq - 1, 0), kvh))
    o = pl.pallas_call(
        functools.partial(_attn_kernel, lq=lq, max_dist=B_WINDOW - 1, kv_shared=True,
                          has_sink=True, want_lse=False, n_axes=3),
        grid_spec=pltpu.PrefetchScalarGridSpec(
            num_scalar_prefetch=1,
            grid=(batch, B_KV_HEADS, seq // lq),
            in_specs=[q_spec, cur(ZB_KB), cur(ZB_VB), prev(ZB_KB), prev(ZB_VB)],
            out_specs=pl.BlockSpec((q_per_kv, None, lq, CB), lambda b, kvh, i, s: (kvh, b, i, 0)),
            scratch_shapes=[pltpu.VMEM((1, 2, lq + BLOCK, 2 * PAIR), BF16)],
        ),
        out_shape=jax.ShapeDtypeStruct((n_q_blocks, batch, seq, CB), BF16),
        compiler_params=pltpu.CompilerParams(
            dimension_semantics=("arbitrary",) * 3, vmem_limit_bytes=VMEM_LIMIT),
        name="swa_attn",
    )(sinks, z4, z4, z4, z4, z4)
    return o.reshape(n_q_blocks, batch * seq, CB)


def _post_attn_kernel(o0_ref, o1_ref, o2_ref, l0_ref, l1_ref, l2_ref, ob_ref, gate_ref, x_ref,
                      wa_ref, wb_ref, wo_ref, gf_ref, wr_ref, br_ref,
                      po_ref, pl_ref, x1_ref, h2_ref, route_ref, cnt_ref, *, tm):

    def to_token_order(ref, cb, slot, perm_ref, blk):
        dil = DIL_GROUPS[slot + 1][1]
        n = blk // dil
        parts = []
        for tb in range(tm // blk):
            stack = jnp.concatenate(
                [ref[cb, tb * n:(tb + 1) * n, r * CB:(r + 1) * CB] for r in range(dil)], axis=0)
            pieces = (stack,) if stack.dtype == BF16 else _split3(stack)
            moved = [jnp.dot(perm_ref[slot], p.astype(BF16), preferred_element_type=F32) for p in pieces]
            parts.append(functools.reduce(lambda a, b: a + b, moved))
        return jnp.concatenate(parts, axis=0)

    pa = None
    for cb in range(A_BLOCKS):
        l0 = l0_ref[cb]
        l1 = to_token_order(l1_ref, cb, 0, pl_ref, LSE_PERM_BLOCK)
        l2 = to_token_order(l2_ref, cb, 1, pl_ref, LSE_PERM_BLOCK)
        o1 = to_token_order(o1_ref, cb, 0, po_ref, OUT_PERM_BLOCK)
        o2 = to_token_order(o2_ref, cb, 1, po_ref, OUT_PERM_BLOCK)
        mx = jnp.maximum(jnp.maximum(l0, l1), l2)
        e0, e1, e2 = jnp.exp(l0 - mx), jnp.exp(l1 - mx), jnp.exp(l2 - mx)
        inv = 1.0 / (e0 + e1 + e2)
        ya = (e0 * inv) * o0_ref[cb].astype(F32) + (e1 * inv) * o1 + (e2 * inv) * o2
        part = jnp.dot(ya.astype(BF16), wa_ref[cb * CB:(cb + 1) * CB, :], preferred_element_type=F32)
        pa = part if pa is None else pa + part
    pb = None
    for cb in range(B_Q_WIDTH // CB):
        part = jnp.dot(ob_ref[cb], wb_ref[cb * CB:(cb + 1) * CB, :], preferred_element_type=F32)
        pb = part if pb is None else pb + part
    n_gate = D_MODEL // CB
    merged = []
    for cb in range(n_gate):
        cols = slice(cb * CB, (cb + 1) * CB)
        ga = jax.nn.sigmoid(gate_ref[cb].astype(F32))
        gb = jax.nn.sigmoid(gate_ref[n_gate + cb].astype(F32))
        merged.append((ga * pa[:, cols] + gb * pb[:, cols]).astype(BF16))
    merged = jnp.concatenate(merged, axis=1)
    x1 = x_ref[...] + jnp.dot(merged, wo_ref[...], preferred_element_type=F32)
    x1_ref[...] = x1
    h2 = _rms(x1, gf_ref[...])
    h2_ref[...] = h2.astype(BF16)

    h2_hi = h2.astype(BF16)
    h2_lo = (h2 - h2_hi.astype(F32)).astype(BF16)
    wr = wr_ref[...]
    wr_hi = wr.astype(BF16)
    wr_lo = (wr - wr_hi.astype(F32)).astype(BF16)
    nt = (((1,), (1,)), ((), ()))
    hi_terms = lax.dot_general(jnp.concatenate([wr_hi, wr_lo], axis=0), h2_hi, nt, preferred_element_type=F32)
    logits = (hi_terms[:ROUTER_LANES]
              + (lax.dot_general(wr_hi, h2_lo, nt, preferred_element_type=F32) + hi_terms[ROUTER_LANES:])
              ) + br_ref[...]
    sub = lax.broadcasted_iota(jnp.int32, (EXPERTS_PER_GROUP, tm), 0)
    neg_inf = jnp.float32(-jnp.inf)
    top = lambda v: jnp.max(v, axis=0, keepdims=True)
    first = lambda hit: jnp.min(jnp.where(hit, sub, EXPERTS_PER_GROUP), axis=0, keepdims=True)
    lg = jnp.where(sub < N_EXPERT_GROUPS, logits[:EXPERTS_PER_GROUP], neg_inf)
    mg = top(lg)
    gsel = first(lg == mg)
    pg_sel = 1.0 / jnp.sum(jnp.exp(lg - mg), axis=0, keepdims=True)
    le = logits[EXPERT_LANE0:EXPERT_LANE0 + EXPERTS_PER_GROUP]
    for g in range(1, N_EXPERT_GROUPS):
        r0 = EXPERT_LANE0 + g * EXPERTS_PER_GROUP
        le = jnp.where(gsel == g, logits[r0:r0 + EXPERTS_PER_GROUP], le)
    ex = jnp.exp(le - top(le))
    pe = ex / jnp.sum(ex, axis=0, keepdims=True)
    p1 = top(pe)
    i1 = first(pe == p1)
    rest = sub != i1
    p2 = top(jnp.where(rest, pe, -1.0))
    i2 = first(rest & (pe == p2))
    norm = pg_sel / (p1 + p2)
    e1 = gsel * EXPERTS_PER_GROUP + i1
    e2 = gsel * EXPERTS_PER_GROUP + i2
    row = lax.broadcasted_iota(jnp.int32, (ROUTER_LANES, tm), 0)
    route_t = jnp.where(
        row == 0, e1.astype(F32),
        jnp.where(row == 1, e2.astype(F32),
                  jnp.where(row == 2, p1 * norm, jnp.where(row == 3, p2 * norm, 0.0))))
    route_ref[...] = route_t.T
    hits_t = jnp.where((row == e1 + EXPERT_LANE0) | (row == e2 + EXPERT_LANE0), 1.0, 0.0).astype(BF16)
    cnt_ref[0] = lax.dot_general(jnp.ones((cnt_ref.shape[1], tm), BF16), hits_t, nt,
                                 preferred_element_type=F32)


def _post_attn(oa, lse, ob, z_tok, x2, w_proj_a, w_proj_b, w_out, g_ffn, w_router, b_router, *, tm):
    n_tok = x2.shape[0]
    const2 = lambda i: (0, 0)
    blk = lambda nb: pl.BlockSpec((nb, tm, CB), lambda i: (0, i, 0))
    dil_blk = lambda dil: pl.BlockSpec((A_BLOCKS, tm // dil, dil * CB), lambda i: (0, i, 0))
    row = lambda width: pl.BlockSpec((tm, width), lambda i: (i, 0))
    resident = lambda shape: pl.BlockSpec(shape, const2, pipeline_mode=pl.Buffered(1))
    d1, d2 = DIL_GROUPS[1][1], DIL_GROUPS[2][1]
    perm_o = jnp.stack([_dilation_perm(OUT_PERM_BLOCK, d, BF16).T for d in (d1, d2)])
    perm_l = jnp.stack([_dilation_perm(LSE_PERM_BLOCK, d, BF16).T for d in (d1, d2)])
    return pl.pallas_call(
        functools.partial(_post_attn_kernel, tm=tm),
        grid=(n_tok // tm,),
        in_specs=[blk(A_BLOCKS), dil_blk(d1), dil_blk(d2), blk(A_BLOCKS), dil_blk(d1), dil_blk(d2),
                  blk(B_Q_WIDTH // CB), blk(GATE_WIDTH // CB), row(D_MODEL),
                  resident((A_GROUP_WIDTH, D_MODEL)), resident((B_Q_WIDTH, D_MODEL)),
                  resident((D_MODEL, D_MODEL)), resident((1, D_MODEL)),
                  resident((ROUTER_LANES, D_MODEL)), resident((ROUTER_LANES, 1)),
                  pl.BlockSpec((2, OUT_PERM_BLOCK, OUT_PERM_BLOCK), lambda i: (0, 0, 0),
                               pipeline_mode=pl.Buffered(1)),
                  pl.BlockSpec((2, LSE_PERM_BLOCK, LSE_PERM_BLOCK), lambda i: (0, 0, 0),
                               pipeline_mode=pl.Buffered(1))],
        out_specs=[row(D_MODEL), row(D_MODEL), row(ROUTER_LANES),
                   pl.BlockSpec((1, 8, ROUTER_LANES), lambda i: (i, 0, 0))],
        out_shape=[jax.ShapeDtypeStruct((n_tok, D_MODEL), F32),
                   jax.ShapeDtypeStruct((n_tok, D_MODEL), BF16),
                   jax.ShapeDtypeStruct((n_tok, ROUTER_LANES), F32),
                   jax.ShapeDtypeStruct((n_tok // tm, 8, ROUTER_LANES), F32)],
        compiler_params=pltpu.CompilerParams(
            dimension_semantics=("arbitrary",), vmem_limit_bytes=VMEM_LIMIT),
        name="post_attn",
    )(oa[0], oa[1], oa[2], lse[0], lse[1], lse[2], ob, z_tok, x2,
      w_proj_a, w_proj_b, w_out, g_ffn, w_router, b_router, perm_o, perm_l)


def _local_slots(route, lbase_row, ltri):
    lane = lax.broadcasted_iota(jnp.int32, route.shape, 1)
    pick = lambda k: jnp.sum(jnp.where(lane == k, route, 0.0), axis=-1, keepdims=True)
    lanef = lane.astype(F32)
    oh1, oh2 = lanef == pick(0), lanef == pick(1)
    oh = jnp.where(oh1 | oh2, 1.0, 0.0).astype(BF16)
    table = jnp.dot(ltri, oh, preferred_element_type=F32) + lbase_row
    ls1 = jnp.sum(jnp.where(oh1, table, 0.0), axis=-1, keepdims=True)
    ls2 = jnp.sum(jnp.where(oh2, table, 0.0), axis=-1, keepdims=True)
    return ls1, ls2, pick


def _chunk_loop(count, body, unroll=1):
    one = lambda c, carry: (body(c), carry)[1]
    full = 0
    if unroll > 1:
        full = (count // unroll) * unroll

        def group(g, carry):
            for u in range(unroll):
                body(g * unroll + u)
            return carry

        lax.fori_loop(0, count // unroll, group, 0)
    lax.fori_loop(full, count, one, 0)


def _wait_chunks(count, wait_rows):
    _chunk_loop(count // WAIT_GROUP, lambda c: wait_rows(WAIT_GROUP * CHUNK))
    _chunk_loop(count % WAIT_GROUP, lambda c: wait_rows(CHUNK))


def _selection_blocks(slots_a, slots_b, axis, other):
    shape = (SEL_BLOCK, other) if axis == 0 else (other, SEL_BLOCK)
    local = lax.broadcasted_iota(jnp.int32, shape, axis).astype(F32).astype(BF16)
    one, zero = jnp.ones_like(local), jnp.zeros_like(local)

    def block(a):
        rel_a = (slots_a - float(a * SEL_BLOCK)).astype(BF16)
        rel_b = (slots_b - float(a * SEL_BLOCK)).astype(BF16)
        return jnp.where(local == rel_a, one, jnp.where(local == rel_b, one, zero))

    return block


def _for_used_blocks(used_rows, body):
    always = 2 * MOE_TILE // SEL_BLOCK + 1
    body(0, always, True)
    for a in range(always, LOCAL_ROWS // SEL_BLOCK):
        pl.when(used_rows > a * SEL_BLOCK)(functools.partial(body, a, a + 1, False))


def _issue_copies(copy_refs, tile, start):
    pair_src, pair_dst, n_pair, one_src, one_dst, n_one = copy_refs
    for src_ref, dst_ref, n_ref, slots, size in ((pair_src, pair_dst, n_pair, PAIR_SLOTS, 2),
                                                  (one_src, one_dst, n_one, N_EXPERTS, 1)):
        _chunk_loop(n_ref[tile], lambda c: start(
            pl.multiple_of(src_ref[tile * slots + c] * CHUNK, CHUNK),
            pl.multiple_of(dst_ref[tile * slots + c] * CHUNK, CHUNK), size * CHUNK), unroll=ISSUE_UNROLL)


def _dispatch_kernel(pair_src_ref, pair_dst_ref, n_pair_ref, one_src_ref, one_dst_ref, n_one_ref,
                     tot_ref, tail_start_ref, tail_n_ref,
                     h2_ref, route_ref, lbase_ref, ltri_ref, xs_ref, buf_ref, zero_ref, sem_ref, zsem_ref,
                     *, tm):
    copy_refs = (pair_src_ref, pair_dst_ref, n_pair_ref, one_src_ref, one_dst_ref, n_one_ref)
    i = pl.program_id(0)
    last = pl.num_programs(0) - 1
    slot = i % 2

    def chunk(slot_, src_row, dst_row, rows=CHUNK):
        return pltpu.make_async_copy(buf_ref.at[slot_, pl.ds(src_row, rows), :],
                                     xs_ref.at[pl.ds(dst_row, rows), :], sem_ref.at[slot_])

    def wait_tile(slot_, tile):
        _wait_chunks(tot_ref[tile], lambda rows: chunk(slot_, 0, 0, rows).wait())

    @pl.when(i == 0)
    def _():
        zero_ref[...] = jnp.zeros_like(zero_ref)

        def zero_chunk(row):
            return pltpu.make_async_copy(zero_ref, xs_ref.at[pl.ds(row, CHUNK), :], zsem_ref)

        def per_expert(e, total):
            start = tail_start_ref[e] * CHUNK
            _chunk_loop(tail_n_ref[e], lambda c: zero_chunk(pl.multiple_of(start + c * CHUNK, CHUNK)).start())
            return total + tail_n_ref[e]

        total = lax.fori_loop(0, N_EXPERTS, per_expert, 0)
        _chunk_loop(total, lambda c: zero_chunk(0).wait())

    @pl.when(i >= 2)
    def _():
        wait_tile(slot, i - 2)

    route = route_ref[...]
    ls1, ls2, pick = _local_slots(route, lbase_ref[0, 0:1, :], ltri_ref[...])
    lane = lax.broadcasted_iota(jnp.int32, route.shape, 1)
    ls_t = jnp.where(lane == 0, ls1, jnp.where(lane == 1, ls2, 0.0)).T
    sel_block = _selection_blocks(ls_t[0:1, :], ls_t[1:2, :], 0, tm)
    lane_expert = (lane // META_PIECES).astype(F32)
    lane_piece = lane % META_PIECES
    by_piece = lambda pieces: jnp.where(lane_piece == 0, pieces[0],
                                        jnp.where(lane_piece == 1, pieces[1], pieces[2]))
    meta = jnp.where(lane_expert == pick(0), by_piece(_split3(pick(2))),
                     jnp.where(lane_expert == pick(1), by_piece(_split3(pick(3))), 0.0))
    meta = meta.astype(BF16)

    def sort_block(a0, a1, first):
        del first
        rows = slice(a0 * SEL_BLOCK, a1 * SEL_BLOCK)
        sel = jnp.concatenate([sel_block(a) for a in range(a0, a1)], axis=0)
        buf_ref[slot, rows, :D_MODEL] = jnp.dot(sel, h2_ref[...], preferred_element_type=F32).astype(BF16)
        buf_ref[slot, rows, D_MODEL:] = jnp.dot(sel, meta, preferred_element_type=F32).astype(BF16)

    _for_used_blocks(tot_ref[i] * CHUNK, sort_block)

    _issue_copies(copy_refs, i, lambda local_row, xs_row, rows: chunk(slot, local_row, xs_row, rows).start())

    @pl.when(i == last)
    def _():
        @pl.when(i >= 1)
        def _():
            wait_tile(1 - slot, i - 1)
        wait_tile(slot, i)


def _expert_kernel(te_ref, nv_ref, xs_ref, wg_ref, wu_ref, wd_ref, ys_ref, wgb_ref, wub_ref, wdb_ref):
    g = pl.program_id(0)
    e = te_ref[g]

    @pl.when((g == 0) | (te_ref[jnp.maximum(g - 1, 0)] != e))
    def _():
        wgb_ref[...] = wg_ref[...].astype(BF16)
        wub_ref[...] = wu_ref[...].astype(BF16)
        wdb_ref[...] = wd_ref[...].astype(BF16)

    @pl.when(g < nv_ref[0])
    def _():
        x = xs_ref[:, :D_MODEL]
        meta = xs_ref[:, D_MODEL:].astype(F32)
        lane = lax.broadcasted_iota(jnp.int32, meta.shape, 1)
        mine = (lane >= e * META_PIECES) & (lane < (e + 1) * META_PIECES)
        w = jnp.sum(jnp.where(mine, meta, 0.0), axis=-1, keepdims=True)
        hg = jnp.dot(x, wgb_ref[...], preferred_element_type=F32)
        hu = jnp.dot(x, wub_ref[...], preferred_element_type=F32)
        a = (hg * jax.nn.sigmoid(hg)) * hu * w
        ys_ref[...] = jnp.dot(a.astype(BF16), wdb_ref[...], preferred_element_type=F32).astype(BF16)


def _combine_kernel(pair_src_ref, pair_dst_ref, n_pair_ref, one_src_ref, one_dst_ref, n_one_ref, tot_ref,
                    x1_ref, route_ref, lbase_ref, ltri_ref, gfin_ref, ys_ref, out_ref, ybuf_ref, sem_ref,
                    *, tm):
    copy_refs = (pair_src_ref, pair_dst_ref, n_pair_ref, one_src_ref, one_dst_ref, n_one_ref)
    i = pl.program_id(0)
    n_tiles = pl.num_programs(0)
    slot = i % 2

    def chunk(slot_, src_row, dst_row, rows=CHUNK):
        return pltpu.make_async_copy(ys_ref.at[pl.ds(src_row, rows), :],
                                     ybuf_ref.at[slot_, pl.ds(dst_row, rows), :], sem_ref.at[slot_])

    def fetch_tile(tile, slot_):
        _issue_copies(copy_refs, tile,
                      lambda local_row, ys_row, rows: chunk(slot_, ys_row, local_row, rows).start())

    @pl.when(i == 0)
    def _():
        ybuf_ref[...] = jnp.zeros_like(ybuf_ref)
        fetch_tile(0, 0)

    @pl.when(i + 1 < n_tiles)
    def _():
        fetch_tile(i + 1, 1 - slot)

    _wait_chunks(tot_ref[i], lambda rows: chunk(slot, 0, 0, rows).wait())

    ls1, ls2, _ = _local_slots(route_ref[...], lbase_ref[0, 0:1, :], ltri_ref[...])
    sel_block = _selection_blocks(ls1, ls2, 1, tm)
    def add_block(a0, a1, first):
        rows = slice(a0 * SEL_BLOCK, a1 * SEL_BLOCK)
        sel = jnp.concatenate([sel_block(a) for a in range(a0, a1)], axis=1)
        y = jnp.dot(sel, ybuf_ref[slot, rows, :], preferred_element_type=F32)
        out_ref[...] = (x1_ref[...] if first else out_ref[...]) + y

    _for_used_blocks(tot_ref[i] * CHUNK, add_block)
    out_ref[...] = _rms(out_ref[...], gfin_ref[...])


def _routing_tables(cnt, n_exp_tiles):
    c16 = (cnt + CHUNK - 1) // CHUNK
    lbase = jnp.cumsum(c16, axis=1) - c16
    tile_off = jnp.cumsum(c16, axis=0) - c16
    tot = jnp.sum(c16, axis=0)
    per = EXP_TILE // CHUNK
    region_tiles = (tot + per - 1) // per
    region = region_tiles * per
    base = jnp.cumsum(region) - region
    dst = base[None, :] + tile_off
    tile_end = jnp.cumsum(region_tiles)
    n_valid = tile_end[-1]
    g = jnp.arange(n_exp_tiles, dtype=jnp.int32)
    tile_expert = jnp.sum(tile_end[None, :] <= jnp.minimum(g, n_valid - 1)[:, None], axis=1).astype(jnp.int32)
    i32 = lambda a: a.astype(jnp.int32).reshape(-1)

    def copy_list(per_seg, first, size, slots):
        start = jnp.cumsum(per_seg, axis=1) - per_seg
        k = jnp.arange(slots, dtype=jnp.int32)[None, :, None]
        owns = (start[:, None, :] <= k) & (k < (start + per_seg)[:, None, :])
        inside = (first[:, None, :] + (k - start[:, None, :]) * size)
        pick = lambda seg_start: jnp.sum(jnp.where(owns, seg_start[:, None, :] + inside, 0), axis=2)
        return i32(pick(lbase)), i32(pick(dst)), i32(jnp.sum(per_seg, axis=1))

    pair_src, pair_dst, n_pair = copy_list(c16 // 2, jnp.zeros_like(c16), 2, PAIR_SLOTS)
    one_src, one_dst, n_one = copy_list(c16 % 2, (c16 // 2) * 2, 1, N_EXPERTS)
    return dict(copies=(pair_src, pair_dst, n_pair, one_src, one_dst, n_one), tot=i32(jnp.sum(c16, axis=1)),
                tail_start=i32(base + tot), tail_n=i32(region - tot),
                tile_expert=tile_expert, n_valid=i32(n_valid),
                lbase_rows=(lbase * CHUNK).astype(F32))


def _moe(h2, route, cnt_tiles, x1, w_gate, w_up, w_down, g_final):
    n_tok = h2.shape[0]
    tm = MOE_TILE
    n_tiles = n_tok // tm
    assert cnt_tiles.shape[0] == n_tiles
    worst_rows = 2 * n_tok + n_tiles * N_EXPERTS * (CHUNK - 1) + N_EXPERTS * (EXP_TILE - CHUNK)
    n_exp_tiles = -(-worst_rows // EXP_TILE)
    n_slots = n_exp_tiles * EXP_TILE

    cnt = cnt_tiles[:, 0, EXPERT_LANE0:EXPERT_LANE0 + N_EXPERTS].astype(jnp.int32)
    tb = _routing_tables(cnt, n_exp_tiles)
    lbase_rows = jnp.zeros((n_tiles, 8, ROUTER_LANES), F32).at[:, :, :N_EXPERTS].set(
        tb["lbase_rows"][:, None, :])
    row_id = np.arange(tm)
    ltri = jnp.asarray((row_id[:, None] > row_id[None, :]).astype(np.float32), dtype=BF16)

    tile_row = lambda width: pl.BlockSpec((tm, width), lambda i, *_: (i, 0))
    lbase_spec = pl.BlockSpec((1, 8, ROUTER_LANES), lambda i, *_: (i, 0, 0))
    ltri_spec = pl.BlockSpec((tm, tm), lambda i, *_: (0, 0), pipeline_mode=pl.Buffered(1))
    hbm = pl.BlockSpec(memory_space=pl.ANY)

    xs = pl.pallas_call(
        functools.partial(_dispatch_kernel, tm=tm),
        grid_spec=pltpu.PrefetchScalarGridSpec(
            num_scalar_prefetch=9, grid=(n_tiles,),
            in_specs=[tile_row(D_MODEL), tile_row(ROUTER_LANES), lbase_spec, ltri_spec],
            out_specs=hbm,
            scratch_shapes=[pltpu.VMEM((2, LOCAL_ROWS, XS_WIDTH), BF16), pltpu.VMEM((CHUNK, XS_WIDTH), BF16),
                            pltpu.SemaphoreType.DMA((2,)), pltpu.SemaphoreType.DMA]),
        out_shape=jax.ShapeDtypeStruct((n_slots, XS_WIDTH), BF16),
        compiler_params=pltpu.CompilerParams(
            dimension_semantics=("arbitrary",), vmem_limit_bytes=VMEM_LIMIT),
        name="moe_dispatch",
    )(*tb["copies"], tb["tot"], tb["tail_start"], tb["tail_n"], h2, route, lbase_rows, ltri)

    row_tile = lambda width: pl.BlockSpec(
        (EXP_TILE, width), lambda g, te, nv: (jnp.maximum(jnp.minimum(g, nv[0] - 1), 0), 0))
    ys = pl.pallas_call(
        _expert_kernel,
        grid_spec=pltpu.PrefetchScalarGridSpec(
            num_scalar_prefetch=2, grid=(n_exp_tiles,),
            in_specs=[row_tile(XS_WIDTH),
                      pl.BlockSpec((None, D_MODEL, D_EXPERT), lambda g, te, nv: (te[g], 0, 0)),
                      pl.BlockSpec((None, D_MODEL, D_EXPERT), lambda g, te, nv: (te[g], 0, 0)),
                      pl.BlockSpec((None, D_EXPERT, D_MODEL), lambda g, te, nv: (te[g], 0, 0))],
            out_specs=row_tile(D_MODEL),
            scratch_shapes=[pltpu.VMEM((D_MODEL, D_EXPERT), BF16), pltpu.VMEM((D_MODEL, D_EXPERT), BF16),
                            pltpu.VMEM((D_EXPERT, D_MODEL), BF16)]),
        out_shape=jax.ShapeDtypeStruct((n_slots, D_MODEL), BF16),
        compiler_params=pltpu.CompilerParams(
            dimension_semantics=("arbitrary",), vmem_limit_bytes=VMEM_LIMIT),
        name="moe_expert",
    )(tb["tile_expert"], tb["n_valid"], xs, w_gate, w_up, w_down)

    return pl.pallas_call(
        functools.partial(_combine_kernel, tm=tm),
        grid_spec=pltpu.PrefetchScalarGridSpec(
            num_scalar_prefetch=7, grid=(n_tiles,),
            in_specs=[tile_row(D_MODEL), tile_row(ROUTER_LANES), lbase_spec, ltri_spec,
                      pl.BlockSpec((1, D_MODEL), lambda i, *_: (0, 0)), hbm],
            out_specs=tile_row(D_MODEL),
            scratch_shapes=[pltpu.VMEM((2, LOCAL_ROWS, D_MODEL), BF16), pltpu.SemaphoreType.DMA((2,))]),
        out_shape=jax.ShapeDtypeStruct((n_tok, D_MODEL), F32),
        compiler_params=pltpu.CompilerParams(
            dimension_semantics=("arbitrary",), vmem_limit_bytes=VMEM_LIMIT),
        name="moe_combine",
    )(*tb["copies"], tb["tot"], x1, route, lbase_rows, ltri, g_final, ys)


def _rope_tables(seq, tm):
    pos = np.arange(seq, dtype=np.float64)
    inv_freq = ROPE_THETA ** (-np.arange(0, HEAD_DIM, 2, dtype=np.float64) / HEAD_DIM)
    ang = pos[:, None] * inv_freq[None, :]
    cos, sin = np.cos(ang), np.sin(ang)
    reps = CB // HEAD_DIM
    cos_t = np.tile(np.concatenate([cos, cos], axis=-1), (1, reps))
    sin_t = np.tile(np.concatenate([-sin, sin], axis=-1), (1, reps))

    def reorder(t, dil):
        return t.reshape(seq // tm, tm // dil, dil, CB).transpose(0, 2, 1, 3).reshape(seq, CB)

    dils = [dil for _, dil in DIL_GROUPS]
    return (jnp.asarray(np.stack([reorder(cos_t, dil) for dil in dils]).astype(np.float32)),
            jnp.asarray(np.stack([reorder(sin_t, dil) for dil in dils]).astype(np.float32)))


def kernel(x, w_in, b_in, sinks, w_proj_a, w_proj_b, w_out, g_mix, g_ffn, w_router_group, b_router_group,
           w_router_expert, b_router_expert, w_exp_gate, w_exp_up, w_exp_down, g_final):
    batch, seq, d = x.shape
    assert d == D_MODEL and w_in.shape[0] == 1, "single-layer kernel"
    n_tok = batch * seq
    x2 = x.reshape(n_tok, d)
    assert seq % ATTN_ROWS_PER_STEP == 0 and n_tok % MOE_TILE == 0 and seq % IN_TILE == 0
    cos_t, sin_t = _rope_tables(seq, IN_TILE)

    z_tok, z_d1, z_d2 = _in_proj(x2, g_mix[0][None, :], w_in[0].astype(BF16), b_in[0][None, :],
                                 cos_t, sin_t, seq=seq, tm=IN_TILE)

    a_bases = (0, A_BLOCKS, 2 * A_BLOCKS)
    tok_bases = tuple(ZB_A0 + b for b in a_bases)
    oa, lse = [], []
    for group, (z, bases) in enumerate(((z_tok, tok_bases), (z_d1, a_bases), (z_d2, a_bases))):
        o_g, l_g = _dilated_attention(z, bases, group, batch=batch, seq=seq, lq=ATTN_ROWS_PER_STEP)
        oa.append(o_g)
        lse.append(l_g)
    ob = _swa_attention(z_tok, sinks[0], batch=batch, seq=seq, lq=ATTN_ROWS_PER_STEP)

    gap = EXPERT_LANE0 - N_EXPERT_GROUPS
    tail = ROUTER_LANES - EXPERT_LANE0 - N_EXPERTS
    w_router = jnp.concatenate(
        [w_router_group[0].T, jnp.zeros((gap, d), F32), w_router_expert[0].T, jnp.zeros((tail, d), F32)], axis=0)
    b_router = jnp.concatenate(
        [b_router_group[0], jnp.zeros((gap,), F32), b_router_expert[0], jnp.zeros((tail,), F32)])[:, None]
    x1, h2, route, cnt_tiles = _post_attn(
        oa, lse, ob, z_tok, x2, w_proj_a[0].astype(BF16), w_proj_b[0].astype(BF16), w_out[0].astype(BF16),
        g_ffn[0][None, :], w_router, b_router, tm=MOE_TILE)

    out = _moe(h2, route, cnt_tiles, x1, w_exp_gate[0], w_exp_up[0], w_exp_down[0], g_final[None, :])
    return out.reshape(batch, seq, d)
```

```python
import functools
import math

import jax
import jax.numpy as jnp
import numpy as np
from jax import lax
from jax.experimental import pallas as pl
from jax.experimental.pallas import tpu as pltpu

F32 = jnp.float32
BF16 = jnp.bfloat16

D_MODEL = 1024
HEAD_DIM = 64
HALF = HEAD_DIM // 2
ROPE_THETA = 10000.0
RMS_EPS = 1e-6
LOG2E = math.log2(math.e)
LN2 = math.log(2.0)
Q_SCALE = LOG2E * HEAD_DIM ** -0.5
BLOCK = 128
ATTN_ROWS_PER_STEP = 2048
DIL_GROUPS = ((128, 1), (512, 4), (2048, 16))
N_DIL = len(DIL_GROUPS)
A_GROUP_WIDTH = 512
A_QKV_WIDTH = 3 * N_DIL * A_GROUP_WIDTH
B_Q_HEADS = 16
B_KV_HEADS = 2
B_Q_WIDTH = B_Q_HEADS * HEAD_DIM
B_WINDOW = 128
GATE_WIDTH = 2 * D_MODEL
IN_WIDTH = A_QKV_WIDTH + B_Q_WIDTH + 2 * B_KV_HEADS * HEAD_DIM + GATE_WIDTH
N_EXPERT_GROUPS = 4
EXPERTS_PER_GROUP = 8
N_EXPERTS = N_EXPERT_GROUPS * EXPERTS_PER_GROUP
D_EXPERT = D_MODEL // 4

CB = 256
PAIR = 128
N_IN_BLOCKS = IN_WIDTH // CB
A_BLOCKS = A_GROUP_WIDTH // CB
ZB_GATE = 0
ZB_QB = 8
ZB_KB = 12
ZB_VB = 13
ZB_A0 = 14
N_TOK_BLOCKS = ZB_A0 + 3 * A_BLOCKS
OUT_PERM_BLOCK = 256
LSE_PERM_BLOCK = 128
ROUTER_LANES = 128
EXPERT_LANE0 = EXPERTS_PER_GROUP
assert N_EXPERT_GROUPS <= EXPERTS_PER_GROUP
MOE_TILE = 512
CHUNK = 16
EXP_TILE = 1024
LOCAL_ROWS = -(-(2 * MOE_TILE + N_EXPERTS * (CHUNK - 1)) // CB) * CB
PAIR_SLOTS = LOCAL_ROWS // CHUNK // 2
SEL_BLOCK = CB
ISSUE_UNROLL = 8
WAIT_GROUP = 8
XS_WIDTH = D_MODEL + ROUTER_LANES
META_PIECES = 3
assert META_PIECES * N_EXPERTS <= ROUTER_LANES

IN_TILE = 512
V7X_VMEM_BYTES = 64 * 1024 * 1024
VMEM_LIMIT = V7X_VMEM_BYTES * 7 // 8


def _in_proj_plan():
    plan = []
    for c in range(N_IN_BLOCKS):
        col = c * CB
        if col < A_QKV_WIDTH:
            part, rem = divmod(col, N_DIL * A_GROUP_WIDTH)
            group, blk = divmod(rem // CB, A_BLOCKS)
            kind = ("q", "k", "v")[part]
            dil = DIL_GROUPS[group][1]
            if dil == 1:
                plan.append((0, ZB_A0 + part * A_BLOCKS + blk, kind, 1))
            else:
                plan.append((group, part * A_BLOCKS + blk, kind, dil))
        elif col < A_QKV_WIDTH + B_Q_WIDTH:
            plan.append((0, ZB_QB + (col - A_QKV_WIDTH) // CB, "q", 1))
        elif col < A_QKV_WIDTH + B_Q_WIDTH + CB:
            plan.append((0, -1, "kvb", 1))
        else:
            plan.append((0, ZB_GATE + (col - (A_QKV_WIDTH + B_Q_WIDTH + CB)) // CB, "v", 1))
    return tuple(plan)


def _rms(x, g):
    return x * lax.rsqrt(jnp.mean(x * x, axis=-1, keepdims=True) + RMS_EPS) * g


def _split3(w):
    hi = w.astype(BF16).astype(F32)
    mid = (w - hi).astype(BF16).astype(F32)
    lo = (w - hi - mid).astype(BF16).astype(F32)
    return hi, mid, lo


def _rope(acc, cos, sin_signed, first_half):
    partner = jnp.where(first_half, pltpu.roll(acc, CB - HALF, 1), pltpu.roll(acc, HALF, 1))
    return acc * cos + partner * sin_signed


def _in_perm_block(dil):
    return max(BLOCK, CHUNK * dil)


def _in_proj_kernel(x_ref, g_ref, w_ref, b_ref, cos_ref, sin_ref, perm1_ref, perm2_ref,
                    zt_ref, zd1_ref, zd2_ref, *, plan, tm):
    out_refs = (zt_ref, zd1_ref, zd2_ref)
    perm_refs = (None, perm1_ref, perm2_ref)
    lane = lax.broadcasted_iota(jnp.int32, (tm, CB), 1)
    first_half = (lane % HEAD_DIM) < HALF

    h = _rms(x_ref[...], g_ref[...]).astype(BF16)
    h_by_dil, tables = {}, {}
    for slot, (_, dil) in enumerate(DIL_GROUPS):
        tables[dil] = slot
        if dil == 1:
            h_by_dil[dil] = h
            continue
        blk = _in_perm_block(dil)
        n = blk // dil
        moved = [jnp.dot(perm_refs[slot][...], h[tb * blk:(tb + 1) * blk], preferred_element_type=F32
                         ).astype(BF16) for tb in range(tm // blk)]
        h_by_dil[dil] = jnp.concatenate(
            [part[r * n:(r + 1) * n] for r in range(dil) for part in moved], axis=0)

    for c, (arr, dst, kind, dil) in enumerate(plan):
        cols = slice(c * CB, (c + 1) * CB)
        acc = jnp.dot(h_by_dil[dil], w_ref[:, cols], preferred_element_type=F32) + b_ref[:, cols]
        if kind in ("q", "k", "kvb"):
            slot = tables[dil]
            rot = _rope(acc, cos_ref[slot], sin_ref[slot], first_half)
        if kind == "q":
            val = (rot * Q_SCALE).astype(BF16)
        elif kind == "k":
            val = rot.astype(BF16)
        elif kind == "v":
            val = acc.astype(BF16)
        else:
            r64 = pltpu.roll(rot, HEAD_DIM, 1)
            r128 = pltpu.roll(rot, 2 * HEAD_DIM, 1)
            kdup = jnp.where(lane < HEAD_DIM, rot, jnp.where(lane < 3 * HEAD_DIM, r64, r128))
            a128 = pltpu.roll(acc, 2 * HEAD_DIM, 1)
            a192 = pltpu.roll(acc, 3 * HEAD_DIM, 1)
            vdup = jnp.where(lane < HEAD_DIM, a128, jnp.where(lane < 3 * HEAD_DIM, a192, acc))
            zt_ref[ZB_KB] = kdup.astype(BF16)
            zt_ref[ZB_VB] = vdup.astype(BF16)
            continue
        if dil == 1:
            out_refs[arr][dst] = val
        else:
            n = tm // dil
            for r in range(dil):
                out_refs[arr][dst, :, r * CB:(r + 1) * CB] = val[r * n:(r + 1) * n, :]


def _dilation_perm(block, dil, dtype):
    j = np.arange(block)
    src = (j % (block // dil)) * dil + j // (block // dil)
    return jnp.asarray((src[:, None] == j[None, :]).astype(np.float32), dtype=dtype)


def _in_proj(x2, g_mix, w_in, b_in, cos_t, sin_t, *, seq, tm):
    n_tok = x2.shape[0]
    tiles_per_seq = seq // tm
    const = lambda i: (0, 0)
    table = pl.BlockSpec((N_DIL, tm, CB), lambda i: (0, i % tiles_per_seq, 0))
    d1, d2 = DIL_GROUPS[1][1], DIL_GROUPS[2][1]
    perms = [_dilation_perm(_in_perm_block(d), d, BF16) for d in (d1, d2)]
    perm_spec = lambda d: pl.BlockSpec((_in_perm_block(d),) * 2, const, pipeline_mode=pl.Buffered(1))
    return pl.pallas_call(
        functools.partial(_in_proj_kernel, plan=_in_proj_plan(), tm=tm),
        grid=(n_tok // tm,),
        in_specs=[
            pl.BlockSpec((tm, D_MODEL), lambda i: (i, 0)),
            pl.BlockSpec((1, D_MODEL), const),
            pl.BlockSpec((D_MODEL, IN_WIDTH), const, pipeline_mode=pl.Buffered(1)),
            pl.BlockSpec((1, IN_WIDTH), const),
            table, table,
            perm_spec(d1), perm_spec(d2),
        ],
        out_specs=[
            pl.BlockSpec((N_TOK_BLOCKS, tm, CB), lambda i: (0, i, 0)),
            pl.BlockSpec((3 * A_BLOCKS, tm // d1, d1 * CB), lambda i: (0, i, 0)),
            pl.BlockSpec((3 * A_BLOCKS, tm // d2, d2 * CB), lambda i: (0, i, 0)),
        ],
        out_shape=[
            jax.ShapeDtypeStruct((N_TOK_BLOCKS, n_tok, CB), BF16),
            jax.ShapeDtypeStruct((3 * A_BLOCKS, n_tok // d1, d1 * CB), BF16),
            jax.ShapeDtypeStruct((3 * A_BLOCKS, n_tok // d2, d2 * CB), BF16),
        ],
        compiler_params=pltpu.CompilerParams(
            dimension_semantics=("arbitrary",), vmem_limit_bytes=VMEM_LIMIT),
        name="in_proj",
    )(x2, g_mix, w_in, b_in, cos_t, sin_t, *perms)


def _attn_kernel(*refs, lq, max_dist, kv_shared, has_sink, want_lse, n_axes):
    refs = list(refs)
    sink_ref = refs.pop(0) if has_sink else None
    q_ref, k_ref, v_ref, kp_ref, vp_ref = refs[:5]
    o_ref = refs[5]
    lse_ref = refs[6] if want_lse else None
    vaug_ref = refs[-1]
    n_qblk = q_ref.shape[0]
    n_kv_pairs = vaug_ref.shape[0]
    rows_kv = lq + BLOCK

    row = lax.broadcasted_iota(jnp.int32, (BLOCK, 2 * BLOCK), 0)
    col = lax.broadcasted_iota(jnp.int32, (BLOCK, 2 * BLOCK), 1)
    dist = row - col + BLOCK
    valid = (dist >= 0) & (dist <= max_dist)
    neg_inf = jnp.float32(-jnp.inf)
    bias = jnp.where(valid, 0.0, neg_inf)
    bias_first = jnp.where(valid & (col >= BLOCK), 0.0, neg_inf)
    bias0 = jnp.where(pl.program_id(n_axes - 1) == 0, bias_first, bias)
    lane_lo = lax.broadcasted_iota(jnp.int32, (BLOCK, PAIR), 1) < HEAD_DIM

    def rd(ref, blk, rows, cols):
        return ref[rows, cols] if kv_shared else ref[blk, rows, cols]

    first_step = functools.reduce(lambda a, b: a & b, [pl.program_id(a) == 0 for a in range(n_axes)])

    @pl.when(first_step)
    def _():
        lane = lax.broadcasted_iota(jnp.int32, (rows_kv, PAIR), 1)
        for pair in range(n_kv_pairs):
            vaug_ref[pair, 0, :, PAIR:] = jnp.where(lane < HEAD_DIM, 1.0, 0.0).astype(BF16)
            vaug_ref[pair, 1, :, PAIR:] = jnp.where(lane < HEAD_DIM, 0.0, 1.0).astype(BF16)

    lane_kv = lax.broadcasted_iota(jnp.int32, (rows_kv, PAIR), 1) < HEAD_DIM
    n_col_pairs = q_ref.shape[-1] // PAIR
    for pair in range(n_kv_pairs):
        cols = slice((pair % n_col_pairs) * PAIR, (pair % n_col_pairs + 1) * PAIR)
        everything = slice(None)
        v_all = jnp.concatenate([rd(vp_ref, pair // n_col_pairs, everything, cols),
                                 rd(v_ref, pair // n_col_pairs, everything, cols)], axis=0)
        zero = jnp.zeros_like(v_all)
        vaug_ref[pair, 0, :, :PAIR] = jnp.where(lane_kv, v_all, zero)
        vaug_ref[pair, 1, :, :PAIR] = jnp.where(lane_kv, zero, v_all)

    for ib in range(lq // BLOCK):
        rows = slice(ib * BLOCK, (ib + 1) * BLOCK)
        win = slice(ib * BLOCK, (ib + 2) * BLOCK)
        b_ib = bias0 if ib == 0 else bias
        for blk in range(n_qblk):
            for pp in range(n_col_pairs):
                qcols = slice(pp * PAIR, (pp + 1) * PAIR)
                pair = 0 if kv_shared else blk * n_col_pairs + pp
                kcols = slice(0, PAIR) if kv_shared else qcols
                if ib == 0:
                    k_win = jnp.concatenate([rd(kp_ref, blk, slice(None), kcols),
                                             rd(k_ref, blk, slice(0, BLOCK), kcols)], axis=0)
                else:
                    k_win = rd(k_ref, blk, slice((ib - 1) * BLOCK, (ib + 1) * BLOCK), kcols)
                q_pair = q_ref[blk, rows, qcols]
                ps, ms, sink_terms = [], [], []
                for hh in range(2):
                    q_h = jnp.where(lane_lo == (hh == 0), q_pair, jnp.zeros_like(q_pair))
                    s = lax.dot_general(q_h, k_win, (((1,), (1,)), ((), ())), preferred_element_type=F32)
                    s = s + b_ib
                    m = jnp.max(s, axis=-1, keepdims=True)
                    if has_sink:
                        head = (pl.program_id(1) * n_qblk + blk) * (CB // HEAD_DIM) + pp * 2 + hh
                        sink = sink_ref[head] * LOG2E
                        m = jnp.maximum(m, sink)
                        sink_terms.append(jnp.exp2(sink - m))
                    ps.append(jnp.exp2(s - m).astype(BF16))
                    ms.append(m)
                v_aug = jnp.concatenate([vaug_ref[pair, 0, win, :], vaug_ref[pair, 1, win, :]], axis=0)
                od = jnp.dot(jnp.concatenate(ps, axis=1), v_aug, preferred_element_type=F32)
                den = od[:, PAIR:]
                if has_sink:
                    den = den + jnp.where(lane_lo, sink_terms[0], sink_terms[1])
                o_ref[blk, rows, qcols] = (od[:, :PAIR] * (1.0 / den)).astype(BF16)
                if want_lse:
                    lse_ref[blk, rows, qcols] = (jnp.where(lane_lo, ms[0], ms[1]) + jnp.log2(den)) * LN2


def _dilated_attention(z, bases, group, *, batch, seq, lq):
    window, dil = DIL_GROUPS[group]
    sub_len = seq // dil
    lq = min(lq, sub_len)
    z4 = z.reshape(z.shape[0], batch, sub_len, dil * CB)
    qb, kb, vb = bases
    assert all(base % A_BLOCKS == 0 for base in bases)
    bpq = lq // BLOCK
    slabs = max(1, min(dil, ATTN_ROWS_PER_STEP // lq))
    width = slabs * CB
    cur = lambda base: pl.BlockSpec(
        (A_BLOCKS, None, lq, width), lambda b, r, i: (base // A_BLOCKS, b, i, r))
    prev = lambda base: pl.BlockSpec(
        (A_BLOCKS, None, BLOCK, width), lambda b, r, i: (base // A_BLOCKS, b, jnp.maximum(i * bpq - 1, 0), r))
    out_spec = pl.BlockSpec((A_BLOCKS, None, lq, width), lambda b, r, i: (0, b, i, r))
    o, lse = pl.pallas_call(
        functools.partial(_attn_kernel, lq=lq, max_dist=window // dil, kv_shared=False,
                          has_sink=False, want_lse=True, n_axes=3),
        grid=(batch, dil // slabs, sub_len // lq),
        in_specs=[cur(qb), cur(kb), cur(vb), prev(kb), prev(vb)],
        out_specs=[out_spec, out_spec],
        out_shape=[jax.ShapeDtypeStruct((A_BLOCKS, batch, sub_len, dil * CB), BF16),
                   jax.ShapeDtypeStruct((A_BLOCKS, batch, sub_len, dil * CB), F32)],
        scratch_shapes=[pltpu.VMEM((2 * A_BLOCKS * slabs, 2, lq + BLOCK, 2 * PAIR), BF16)],
        compiler_params=pltpu.CompilerParams(
            dimension_semantics=("arbitrary",) * 3, vmem_limit_bytes=VMEM_LIMIT),
        name=f"dilated_attn_g{group}",
    )(z4, z4, z4, z4, z4)
    rows = batch * sub_len
    return o.reshape(A_BLOCKS, rows, dil * CB), lse.reshape(A_BLOCKS, rows, dil * CB)


def _swa_attention(z_tok, sinks, *, batch, seq, lq):
    z4 = z_tok.reshape(N_TOK_BLOCKS, batch, seq, CB)
    bpq = lq // BLOCK
    n_q_blocks = B_Q_WIDTH // CB
    q_per_kv = n_q_blocks // B_KV_HEADS
    assert ZB_QB % q_per_kv == 0
    q_spec = pl.BlockSpec((q_per_kv, None, lq, CB), lambda b, kvh, i, s: (ZB_QB // q_per_kv + kvh, b, i, 0))
    cur = lambda base: pl.BlockSpec((None, None, lq, PAIR), lambda b, kvh, i, s: (base, b, i, kvh))
    prev = lambda base: pl.BlockSpec(
        (None, None, BLOCK, PAIR), lambda b, kvh, i, s: (base, b, jnp.maximum(i * bpq - 1, 0), kvh))
    o = pl.pallas_call(
        functools.partial(_attn_kernel, lq=lq, max_dist=B_WINDOW - 1, kv_shared=True,
                          has_sink=True, want_lse=False, n_axes=3),
        grid_spec=pltpu.PrefetchScalarGridSpec(
            num_scalar_prefetch=1,
            grid=(batch, B_KV_HEADS, seq // lq),
            in_specs=[q_spec, cur(ZB_KB), cur(ZB_VB), prev(ZB_KB), prev(ZB_VB)],
            out_specs=pl.BlockSpec((q_per_kv, None, lq, CB), lambda b, kvh, i, s: (kvh, b, i, 0)),
            scratch_shapes=[pltpu.VMEM((1, 2, lq + BLOCK, 2 * PAIR), BF16)],
        ),
        out_shape=jax.ShapeDtypeStruct((n_q_blocks, batch, seq, CB), BF16),
        compiler_params=pltpu.CompilerParams(
            dimension_semantics=("arbitrary",) * 3, vmem_limit_bytes=VMEM_LIMIT),
        name="swa_attn",
    )(sinks, z4, z4, z4, z4, z4)
    return o.reshape(n_q_blocks, batch * seq, CB)


def _post_attn_kernel(o0_ref, o1_ref, o2_ref, l0_ref, l1_ref, l2_ref, ob_ref, gate_ref, x_ref,
                      wa_ref, wb_ref, wo_ref, gf_ref, wr_ref, br_ref,
                      po_ref, pl_ref, x1_ref, h2_ref, route_ref, cnt_ref, *, tm):

    def to_token_order(ref, cb, slot, perm_ref, blk):
        dil = DIL_GROUPS[slot + 1][1]
        n = blk // dil
        parts = []
        for tb in range(tm // blk):
            stack = jnp.concatenate(
                [ref[cb, tb * n:(tb + 1) * n, r * CB:(r + 1) * CB] for r in range(dil)], axis=0)
            pieces = (stack,) if stack.dtype == BF16 else _split3(stack)
            moved = [jnp.dot(perm_ref[slot], p.astype(BF16), preferred_element_type=F32) for p in pieces]
            parts.append(functools.reduce(lambda a, b: a + b, moved))
        return jnp.concatenate(parts, axis=0)

    pa = None
    for cb in range(A_BLOCKS):
        l0 = l0_ref[cb]
        l1 = to_token_order(l1_ref, cb, 0, pl_ref, LSE_PERM_BLOCK)
        l2 = to_token_order(l2_ref, cb, 1, pl_ref, LSE_PERM_BLOCK)
        o1 = to_token_order(o1_ref, cb, 0, po_ref, OUT_PERM_BLOCK)
        o2 = to_token_order(o2_ref, cb, 1, po_ref, OUT_PERM_BLOCK)
        mx = jnp.maximum(jnp.maximum(l0, l1), l2)
        e0, e1, e2 = jnp.exp(l0 - mx), jnp.exp(l1 - mx), jnp.exp(l2 - mx)
        inv = 1.0 / (e0 + e1 + e2)
        ya = (e0 * inv) * o0_ref[cb].astype(F32) + (e1 * inv) * o1 + (e2 * inv) * o2
        part = jnp.dot(ya.astype(BF16), wa_ref[cb * CB:(cb + 1) * CB, :], preferred_element_type=F32)
        pa = part if pa is None else pa + part
    pb = None
    for cb in range(B_Q_WIDTH // CB):
        part = jnp.dot(ob_ref[cb], wb_ref[cb * CB:(cb + 1) * CB, :], preferred_element_type=F32)
        pb = part if pb is None else pb + part
    n_gate = D_MODEL // CB
    merged = []
    for cb in range(n_gate):
        cols = slice(cb * CB, (cb + 1) * CB)
        ga = jax.nn.sigmoid(gate_ref[cb].astype(F32))
        gb = jax.nn.sigmoid(gate_ref[n_gate + cb].astype(F32))
        merged.append((ga * pa[:, cols] + gb * pb[:, cols]).astype(BF16))
    merged = jnp.concatenate(merged, axis=1)
    x1 = x_ref[...] + jnp.dot(merged, wo_ref[...], preferred_element_type=F32)
    x1_ref[...] = x1
    h2 = _rms(x1, gf_ref[...])
    h2_ref[...] = h2.astype(BF16)

    h2_hi = h2.astype(BF16)
    h2_lo = (h2 - h2_hi.astype(F32)).astype(BF16)
    wr = wr_ref[...]
    wr_hi = wr.astype(BF16)
    wr_lo = (wr - wr_hi.astype(F32)).astype(BF16)
    nt = (((1,), (1,)), ((), ()))
    hi_terms = lax.dot_general(jnp.concatenate([wr_hi, wr_lo], axis=0), h2_hi, nt, preferred_element_type=F32)
    logits = (hi_terms[:ROUTER_LANES]
              + (lax.dot_general(wr_hi, h2_lo, nt, preferred_element_type=F32) + hi_terms[ROUTER_LANES:])
              ) + br_ref[...]
    sub = lax.broadcasted_iota(jnp.int32, (EXPERTS_PER_GROUP, tm), 0)
    neg_inf = jnp.float32(-jnp.inf)
    top = lambda v: jnp.max(v, axis=0, keepdims=True)
    first = lambda hit: jnp.min(jnp.where(hit, sub, EXPERTS_PER_GROUP), axis=0, keepdims=True)
    lg = jnp.where(sub < N_EXPERT_GROUPS, logits[:EXPERTS_PER_GROUP], neg_inf)
    mg = top(lg)
    gsel = first(lg == mg)
    pg_sel = 1.0 / jnp.sum(jnp.exp(lg - mg), axis=0, keepdims=True)
    le = logits[EXPERT_LANE0:EXPERT_LANE0 + EXPERTS_PER_GROUP]
    for g in range(1, N_EXPERT_GROUPS):
        r0 = EXPERT_LANE0 + g * EXPERTS_PER_GROUP
        le = jnp.where(gsel == g, logits[r0:r0 + EXPERTS_PER_GROUP], le)
    ex = jnp.exp(le - top(le))
    pe = ex / jnp.sum(ex, axis=0, keepdims=True)
    p1 = top(pe)
    i1 = first(pe == p1)
    rest = sub != i1
    p2 = top(jnp.where(rest, pe, -1.0))
    i2 = first(rest & (pe == p2))
    norm = pg_sel / (p1 + p2)
    e1 = gsel * EXPERTS_PER_GROUP + i1
    e2 = gsel * EXPERTS_PER_GROUP + i2
    row = lax.broadcasted_iota(jnp.int32, (ROUTER_LANES, tm), 0)
    route_t = jnp.where(
        row == 0, e1.astype(F32),
        jnp.where(row == 1, e2.astype(F32),
                  jnp.where(row == 2, p1 * norm, jnp.where(row == 3, p2 * norm, 0.0))))
    route_ref[...] = route_t.T
    hits_t = jnp.where((row == e1 + EXPERT_LANE0) | (row == e2 + EXPERT_LANE0), 1.0, 0.0).astype(BF16)
    cnt_ref[0] = lax.dot_general(jnp.ones((cnt_ref.shape[1], tm), BF16), hits_t, nt,
                                 preferred_element_type=F32)


def _post_attn(oa, lse, ob, z_tok, x2, w_proj_a, w_proj_b, w_out, g_ffn, w_router, b_router, *, tm):
    n_tok = x2.shape[0]
    const2 = lambda i: (0, 0)
    blk = lambda nb: pl.BlockSpec((nb, tm, CB), lambda i: (0, i, 0))
    dil_blk = lambda dil: pl.BlockSpec((A_BLOCKS, tm // dil, dil * CB), lambda i: (0, i, 0))
    row = lambda width: pl.BlockSpec((tm, width), lambda i: (i, 0))
    resident = lambda shape: pl.BlockSpec(shape, const2, pipeline_mode=pl.Buffered(1))
    d1, d2 = DIL_GROUPS[1][1], DIL_GROUPS[2][1]
    perm_o = jnp.stack([_dilation_perm(OUT_PERM_BLOCK, d, BF16).T for d in (d1, d2)])
    perm_l = jnp.stack([_dilation_perm(LSE_PERM_BLOCK, d, BF16).T for d in (d1, d2)])
    return pl.pallas_call(
        functools.partial(_post_attn_kernel, tm=tm),
        grid=(n_tok // tm,),
        in_specs=[blk(A_BLOCKS), dil_blk(d1), dil_blk(d2), blk(A_BLOCKS), dil_blk(d1), dil_blk(d2),
                  blk(B_Q_WIDTH // CB), blk(GATE_WIDTH // CB), row(D_MODEL),
                  resident((A_GROUP_WIDTH, D_MODEL)), resident((B_Q_WIDTH, D_MODEL)),
                  resident((D_MODEL, D_MODEL)), resident((1, D_MODEL)),
                  resident((ROUTER_LANES, D_MODEL)), resident((ROUTER_LANES, 1)),
                  pl.BlockSpec((2, OUT_PERM_BLOCK, OUT_PERM_BLOCK), lambda i: (0, 0, 0),
                               pipeline_mode=pl.Buffered(1)),
                  pl.BlockSpec((2, LSE_PERM_BLOCK, LSE_PERM_BLOCK), lambda i: (0, 0, 0),
                               pipeline_mode=pl.Buffered(1))],
        out_specs=[row(D_MODEL), row(D_MODEL), row(ROUTER_LANES),
                   pl.BlockSpec((1, 8, ROUTER_LANES), lambda i: (i, 0, 0))],
        out_shape=[jax.ShapeDtypeStruct((n_tok, D_MODEL), F32),
                   jax.ShapeDtypeStruct((n_tok, D_MODEL), BF16),
                   jax.ShapeDtypeStruct((n_tok, ROUTER_LANES), F32),
                   jax.ShapeDtypeStruct((n_tok // tm, 8, ROUTER_LANES), F32)],
        compiler_params=pltpu.CompilerParams(
            dimension_semantics=("arbitrary",), vmem_limit_bytes=VMEM_LIMIT),
        name="post_attn",
    )(oa[0], oa[1], oa[2], lse[0], lse[1], lse[2], ob, z_tok, x2,
      w_proj_a, w_proj_b, w_out, g_ffn, w_router, b_router, perm_o, perm_l)


def _local_slots(route, lbase_row, ltri):
    lane = lax.broadcasted_iota(jnp.int32, route.shape, 1)
    pick = lambda k: jnp.sum(jnp.where(lane == k, route, 0.0), axis=-1, keepdims=True)
    lanef = lane.astype(F32)
    oh1, oh2 = lanef == pick(0), lanef == pick(1)
    oh = jnp.where(oh1 | oh2, 1.0, 0.0).astype(BF16)
    table = jnp.dot(ltri, oh, preferred_element_type=F32) + lbase_row
    ls1 = jnp.sum(jnp.where(oh1, table, 0.0), axis=-1, keepdims=True)
    ls2 = jnp.sum(jnp.where(oh2, table, 0.0), axis=-1, keepdims=True)
    return ls1, ls2, pick


def _chunk_loop(count, body, unroll=1):
    one = lambda c, carry: (body(c), carry)[1]
    full = 0
    if unroll > 1:
        full = (count // unroll) * unroll

        def group(g, carry):
            for u in range(unroll):
                body(g * unroll + u)
            return carry

        lax.fori_loop(0, count // unroll, group, 0)
    lax.fori_loop(full, count, one, 0)


def _wait_chunks(count, wait_rows):
    _chunk_loop(count // WAIT_GROUP, lambda c: wait_rows(WAIT_GROUP * CHUNK))
    _chunk_loop(count % WAIT_GROUP, lambda c: wait_rows(CHUNK))


def _selection_blocks(slots_a, slots_b, axis, other):
    shape = (SEL_BLOCK, other) if axis == 0 else (other, SEL_BLOCK)
    local = lax.broadcasted_iota(jnp.int32, shape, axis).astype(F32).astype(BF16)
    one, zero = jnp.ones_like(local), jnp.zeros_like(local)

    def block(a):
        rel_a = (slots_a - float(a * SEL_BLOCK)).astype(BF16)
        rel_b = (slots_b - float(a * SEL_BLOCK)).astype(BF16)
        return jnp.where(local == rel_a, one, jnp.where(local == rel_b, one, zero))

    return block


def _for_used_blocks(used_rows, body):
    always = 2 * MOE_TILE // SEL_BLOCK + 1
    body(0, always, True)
    for a in range(always, LOCAL_ROWS // SEL_BLOCK):
        pl.when(used_rows > a * SEL_BLOCK)(functools.partial(body, a, a + 1, False))


def _issue_copies(copy_refs, tile, start):
    pair_src, pair_dst, n_pair, one_src, one_dst, n_one = copy_refs
    for src_ref, dst_ref, n_ref, slots, size in ((pair_src, pair_dst, n_pair, PAIR_SLOTS, 2),
                                                  (one_src, one_dst, n_one, N_EXPERTS, 1)):
        _chunk_loop(n_ref[tile], lambda c: start(
            pl.multiple_of(src_ref[tile * slots + c] * CHUNK, CHUNK),
            pl.multiple_of(dst_ref[tile * slots + c] * CHUNK, CHUNK), size * CHUNK), unroll=ISSUE_UNROLL)


def _dispatch_kernel(pair_src_ref, pair_dst_ref, n_pair_ref, one_src_ref, one_dst_ref, n_one_ref,
                     tot_ref, tail_start_ref, tail_n_ref,
                     h2_ref, route_ref, lbase_ref, ltri_ref, xs_ref, buf_ref, zero_ref, sem_ref, zsem_ref,
                     *, tm):
    copy_refs = (pair_src_ref, pair_dst_ref, n_pair_ref, one_src_ref, one_dst_ref, n_one_ref)
    i = pl.program_id(0)
    last = pl.num_programs(0) - 1
    slot = i % 2

    def chunk(slot_, src_row, dst_row, rows=CHUNK):
        return pltpu.make_async_copy(buf_ref.at[slot_, pl.ds(src_row, rows), :],
                                     xs_ref.at[pl.ds(dst_row, rows), :], sem_ref.at[slot_])

    def wait_tile(slot_, tile):
        _wait_chunks(tot_ref[tile], lambda rows: chunk(slot_, 0, 0, rows).wait())

    @pl.when(i == 0)
    def _():
        zero_ref[...] = jnp.zeros_like(zero_ref)

        def zero_rows(row, rows):
            return pltpu.make_async_copy(zero_ref.at[pl.ds(0, rows), :], xs_ref.at[pl.ds(row, rows), :], zsem_ref)

        def per_expert(e, total):
            start = tail_start_ref[e] * CHUNK
            n = tail_n_ref[e]
            big = n // WAIT_GROUP
            _chunk_loop(big, lambda c: zero_rows(
                pl.multiple_of(start + c * (WAIT_GROUP * CHUNK), CHUNK), WAIT_GROUP * CHUNK).start())
            _chunk_loop(n - big * WAIT_GROUP, lambda c: zero_rows(
                pl.multiple_of(start + (big * WAIT_GROUP + c) * CHUNK, CHUNK), CHUNK).start())
            return total + n

        total = lax.fori_loop(0, N_EXPERTS, per_expert, 0)
        _wait_chunks(total, lambda rows: zero_rows(0, rows).wait())

    @pl.when(i >= 2)
    def _():
        wait_tile(slot, i - 2)

    route = route_ref[...]
    ls1, ls2, pick = _local_slots(route, lbase_ref[0, 0:1, :], ltri_ref[...])
    lane = lax.broadcasted_iota(jnp.int32, route.shape, 1)
    ls_t = jnp.where(lane == 0, ls1, jnp.where(lane == 1, ls2, 0.0)).T
    sel_block = _selection_blocks(ls_t[0:1, :], ls_t[1:2, :], 0, tm)
    lane_expert = (lane // META_PIECES).astype(F32)
    lane_piece = lane % META_PIECES
    by_piece = lambda pieces: jnp.where(lane_piece == 0, pieces[0],
                                        jnp.where(lane_piece == 1, pieces[1], pieces[2]))
    meta = jnp.where(lane_expert == pick(0), by_piece(_split3(pick(2))),
                     jnp.where(lane_expert == pick(1), by_piece(_split3(pick(3))), 0.0))
    meta = meta.astype(BF16)

    def sort_block(a0, a1, first):
        del first
        rows = slice(a0 * SEL_BLOCK, a1 * SEL_BLOCK)
        sel = jnp.concatenate([sel_block(a) for a in range(a0, a1)], axis=0)
        buf_ref[slot, rows, :D_MODEL] = jnp.dot(sel, h2_ref[...], preferred_element_type=F32).astype(BF16)
        buf_ref[slot, rows, D_MODEL:] = jnp.dot(sel, meta, preferred_element_type=F32).astype(BF16)

    _for_used_blocks(tot_ref[i] * CHUNK, sort_block)

    _issue_copies(copy_refs, i, lambda local_row, xs_row, rows: chunk(slot, local_row, xs_row, rows).start())

    @pl.when(i == last)
    def _():
        @pl.when(i >= 1)
        def _():
            wait_tile(1 - slot, i - 1)
        wait_tile(slot, i)


def _expert_kernel(te_ref, nv_ref, xs_ref, wg_ref, wu_ref, wd_ref, ys_ref, wgb_ref, wub_ref, wdb_ref):
    g = pl.program_id(0)
    e = te_ref[g]

    @pl.when((g == 0) | (te_ref[jnp.maximum(g - 1, 0)] != e))
    def _():
        wgb_ref[...] = wg_ref[...].astype(BF16)
        wub_ref[...] = wu_ref[...].astype(BF16)
        wdb_ref[...] = wd_ref[...].astype(BF16)

    @pl.when(g < nv_ref[0])
    def _():
        x = xs_ref[:, :D_MODEL]
        meta = xs_ref[:, D_MODEL:].astype(F32)
        lane = lax.broadcasted_iota(jnp.int32, meta.shape, 1)
        mine = (lane >= e * META_PIECES) & (lane < (e + 1) * META_PIECES)
        w = jnp.sum(jnp.where(mine, meta, 0.0), axis=-1, keepdims=True)
        hg = jnp.dot(x, wgb_ref[...], preferred_element_type=F32)
        hu = jnp.dot(x, wub_ref[...], preferred_element_type=F32)
        a = (hg * jax.nn.sigmoid(hg)) * hu * w
        ys_ref[...] = jnp.dot(a.astype(BF16), wdb_ref[...], preferred_element_type=F32).astype(BF16)


def _combine_kernel(pair_src_ref, pair_dst_ref, n_pair_ref, one_src_ref, one_dst_ref, n_one_ref, tot_ref,
                    x1_ref, route_ref, lbase_ref, ltri_ref, gfin_ref, ys_ref, out_ref, ybuf_ref, sem_ref,
                    *, tm):
    copy_refs = (pair_src_ref, pair_dst_ref, n_pair_ref, one_src_ref, one_dst_ref, n_one_ref)
    i = pl.program_id(0)
    n_tiles = pl.num_programs(0)
    slot = i % 2

    def chunk(slot_, src_row, dst_row, rows=CHUNK):
        return pltpu.make_async_copy(ys_ref.at[pl.ds(src_row, rows), :],
                                     ybuf_ref.at[slot_, pl.ds(dst_row, rows), :], sem_ref.at[slot_])

    def fetch_tile(tile, slot_):
        _issue_copies(copy_refs, tile,
                      lambda local_row, ys_row, rows: chunk(slot_, ys_row, local_row, rows).start())

    @pl.when(i == 0)
    def _():
        ybuf_ref[...] = jnp.zeros_like(ybuf_ref)
        fetch_tile(0, 0)

    @pl.when(i + 1 < n_tiles)
    def _():
        fetch_tile(i + 1, 1 - slot)

    _wait_chunks(tot_ref[i], lambda rows: chunk(slot, 0, 0, rows).wait())

    ls1, ls2, _ = _local_slots(route_ref[...], lbase_ref[0, 0:1, :], ltri_ref[...])
    sel_block = _selection_blocks(ls1, ls2, 1, tm)
    def add_block(a0, a1, first):
        rows = slice(a0 * SEL_BLOCK, a1 * SEL_BLOCK)
        sel = jnp.concatenate([sel_block(a) for a in range(a0, a1)], axis=1)
        y = jnp.dot(sel, ybuf_ref[slot, rows, :], preferred_element_type=F32)
        out_ref[...] = (x1_ref[...] if first else out_ref[...]) + y

    _for_used_blocks(tot_ref[i] * CHUNK, add_block)
    out_ref[...] = _rms(out_ref[...], gfin_ref[...])


def _routing_tables(cnt, n_exp_tiles):
    c16 = (cnt + CHUNK - 1) // CHUNK
    lbase = jnp.cumsum(c16, axis=1) - c16
    tile_off = jnp.cumsum(c16, axis=0) - c16
    tot = jnp.sum(c16, axis=0)
    per = EXP_TILE // CHUNK
    region_tiles = (tot + per - 1) // per
    region = region_tiles * per
    base = jnp.cumsum(region) - region
    dst = base[None, :] + tile_off
    tile_end = jnp.cumsum(region_tiles)
    n_valid = tile_end[-1]
    g = jnp.arange(n_exp_tiles, dtype=jnp.int32)
    tile_expert = jnp.sum(tile_end[None, :] <= jnp.minimum(g, n_valid - 1)[:, None], axis=1).astype(jnp.int32)
    i32 = lambda a: a.astype(jnp.int32).reshape(-1)

    def copy_list(per_seg, first, size, slots):
        start = jnp.cumsum(per_seg, axis=1) - per_seg
        k = jnp.arange(slots, dtype=jnp.int32)[None, :, None]
        owns = (start[:, None, :] <= k) & (k < (start + per_seg)[:, None, :])
        inside = (first[:, None, :] + (k - start[:, None, :]) * size)
        pick = lambda seg_start: jnp.sum(jnp.where(owns, seg_start[:, None, :] + inside, 0), axis=2)
        return i32(pick(lbase)), i32(pick(dst)), i32(jnp.sum(per_seg, axis=1))

    pair_src, pair_dst, n_pair = copy_list(c16 // 2, jnp.zeros_like(c16), 2, PAIR_SLOTS)
    one_src, one_dst, n_one = copy_list(c16 % 2, (c16 // 2) * 2, 1, N_EXPERTS)
    return dict(copies=(pair_src, pair_dst, n_pair, one_src, one_dst, n_one), tot=i32(jnp.sum(c16, axis=1)),
                tail_start=i32(base + tot), tail_n=i32(region - tot),
                tile_expert=tile_expert, n_valid=i32(n_valid),
                lbase_rows=(lbase * CHUNK).astype(F32))


def _moe(h2, route, cnt_tiles, x1, w_gate, w_up, w_down, g_final):
    n_tok = h2.shape[0]
    tm = MOE_TILE
    n_tiles = n_tok // tm
    assert cnt_tiles.shape[0] == n_tiles
    worst_rows = 2 * n_tok + n_tiles * N_EXPERTS * (CHUNK - 1) + N_EXPERTS * (EXP_TILE - CHUNK)
    n_exp_tiles = -(-worst_rows // EXP_TILE)
    n_slots = n_exp_tiles * EXP_TILE

    cnt = cnt_tiles[:, 0, EXPERT_LANE0:EXPERT_LANE0 + N_EXPERTS].astype(jnp.int32)
    tb = _routing_tables(cnt, n_exp_tiles)
    lbase_rows = jnp.zeros((n_tiles, 8, ROUTER_LANES), F32).at[:, :, :N_EXPERTS].set(
        tb["lbase_rows"][:, None, :])
    row_id = np.arange(tm)
    ltri = jnp.asarray((row_id[:, None] > row_id[None, :]).astype(np.float32), dtype=BF16)

    tile_row = lambda width: pl.BlockSpec((tm, width), lambda i, *_: (i, 0))
    lbase_spec = pl.BlockSpec((1, 8, ROUTER_LANES), lambda i, *_: (i, 0, 0))
    ltri_spec = pl.BlockSpec((tm, tm), lambda i, *_: (0, 0), pipeline_mode=pl.Buffered(1))
    hbm = pl.BlockSpec(memory_space=pl.ANY)

    xs = pl.pallas_call(
        functools.partial(_dispatch_kernel, tm=tm),
        grid_spec=pltpu.PrefetchScalarGridSpec(
            num_scalar_prefetch=9, grid=(n_tiles,),
            in_specs=[tile_row(D_MODEL), tile_row(ROUTER_LANES), lbase_spec, ltri_spec],
            out_specs=hbm,
            scratch_shapes=[pltpu.VMEM((2, LOCAL_ROWS, XS_WIDTH), BF16),
                            pltpu.VMEM((WAIT_GROUP * CHUNK, XS_WIDTH), BF16),
                            pltpu.SemaphoreType.DMA((2,)), pltpu.SemaphoreType.DMA]),
        out_shape=jax.ShapeDtypeStruct((n_slots, XS_WIDTH), BF16),
        compiler_params=pltpu.CompilerParams(
            dimension_semantics=("arbitrary",), vmem_limit_bytes=VMEM_LIMIT),
        name="moe_dispatch",
    )(*tb["copies"], tb["tot"], tb["tail_start"], tb["tail_n"], h2, route, lbase_rows, ltri)

    row_tile = lambda width: pl.BlockSpec(
        (EXP_TILE, width), lambda g, te, nv: (jnp.maximum(jnp.minimum(g, nv[0] - 1), 0), 0))
    ys = pl.pallas_call(
        _expert_kernel,
        grid_spec=pltpu.PrefetchScalarGridSpec(
            num_scalar_prefetch=2, grid=(n_exp_tiles,),
            in_specs=[row_tile(XS_WIDTH),
                      pl.BlockSpec((None, D_MODEL, D_EXPERT), lambda g, te, nv: (te[g], 0, 0)),
                      pl.BlockSpec((None, D_MODEL, D_EXPERT), lambda g, te, nv: (te[g], 0, 0)),
                      pl.BlockSpec((None, D_EXPERT, D_MODEL), lambda g, te, nv: (te[g], 0, 0))],
            out_specs=row_tile(D_MODEL),
            scratch_shapes=[pltpu.VMEM((D_MODEL, D_EXPERT), BF16), pltpu.VMEM((D_MODEL, D_EXPERT), BF16),
                            pltpu.VMEM((D_EXPERT, D_MODEL), BF16)]),
        out_shape=jax.ShapeDtypeStruct((n_slots, D_MODEL), BF16),
        compiler_params=pltpu.CompilerParams(
            dimension_semantics=("arbitrary",), vmem_limit_bytes=VMEM_LIMIT),
        name="moe_expert",
    )(tb["tile_expert"], tb["n_valid"], xs, w_gate, w_up, w_down)

    return pl.pallas_call(
        functools.partial(_combine_kernel, tm=tm),
        grid_spec=pltpu.PrefetchScalarGridSpec(
            num_scalar_prefetch=7, grid=(n_tiles,),
            in_specs=[tile_row(D_MODEL), tile_row(ROUTER_LANES), lbase_spec, ltri_spec,
                      pl.BlockSpec((1, D_MODEL), lambda i, *_: (0, 0)), hbm],
            out_specs=tile_row(D_MODEL),
            scratch_shapes=[pltpu.VMEM((2, LOCAL_ROWS, D_MODEL), BF16), pltpu.SemaphoreType.DMA((2,))]),
        out_shape=jax.ShapeDtypeStruct((n_tok, D_MODEL), F32),
        compiler_params=pltpu.CompilerParams(
            dimension_semantics=("arbitrary",), vmem_limit_bytes=VMEM_LIMIT),
        name="moe_combine",
    )(*tb["copies"], tb["tot"], x1, route, lbase_rows, ltri, g_final, ys)


def _rope_tables(seq, tm):
    pos = np.arange(seq, dtype=np.float64)
    inv_freq = ROPE_THETA ** (-np.arange(0, HEAD_DIM, 2, dtype=np.float64) / HEAD_DIM)
    ang = pos[:, None] * inv_freq[None, :]
    cos, sin = np.cos(ang), np.sin(ang)
    reps = CB // HEAD_DIM
    cos_t = np.tile(np.concatenate([cos, cos], axis=-1), (1, reps))
    sin_t = np.tile(np.concatenate([-sin, sin], axis=-1), (1, reps))

    def reorder(t, dil):
        return t.reshape(seq // tm, tm // dil, dil, CB).transpose(0, 2, 1, 3).reshape(seq, CB)

    dils = [dil for _, dil in DIL_GROUPS]
    return (jnp.asarray(np.stack([reorder(cos_t, dil) for dil in dils]).astype(np.float32)),
            jnp.asarray(np.stack([reorder(sin_t, dil) for dil in dils]).astype(np.float32)))


def kernel(x, w_in, b_in, sinks, w_proj_a, w_proj_b, w_out, g_mix, g_ffn, w_router_group, b_router_group,
           w_router_expert, b_router_expert, w_exp_gate, w_exp_up, w_exp_down, g_final):
    batch, seq, d = x.shape
    assert d == D_MODEL and w_in.shape[0] == 1, "single-layer kernel"
    n_tok = batch * seq
    x2 = x.reshape(n_tok, d)
    assert seq % ATTN_ROWS_PER_STEP == 0 and n_tok % MOE_TILE == 0 and seq % IN_TILE == 0
    cos_t, sin_t = _rope_tables(seq, IN_TILE)

    z_tok, z_d1, z_d2 = _in_proj(x2, g_mix[0][None, :], w_in[0].astype(BF16), b_in[0][None, :],
                                 cos_t, sin_t, seq=seq, tm=IN_TILE)

    a_bases = (0, A_BLOCKS, 2 * A_BLOCKS)
    tok_bases = tuple(ZB_A0 + b for b in a_bases)
    oa, lse = [], []
    for group, (z, bases) in enumerate(((z_tok, tok_bases), (z_d1, a_bases), (z_d2, a_bases))):
        o_g, l_g = _dilated_attention(z, bases, group, batch=batch, seq=seq, lq=ATTN_ROWS_PER_STEP)
        oa.append(o_g)
        lse.append(l_g)
    ob = _swa_attention(z_tok, sinks[0], batch=batch, seq=seq, lq=ATTN_ROWS_PER_STEP)

    gap = EXPERT_LANE0 - N_EXPERT_GROUPS
    tail = ROUTER_LANES - EXPERT_LANE0 - N_EXPERTS
    w_router = jnp.concatenate(
        [w_router_group[0].T, jnp.zeros((gap, d), F32), w_router_expert[0].T, jnp.zeros((tail, d), F32)], axis=0)
    b_router = jnp.concatenate(
        [b_router_group[0], jnp.zeros((gap,), F32), b_router_expert[0], jnp.zeros((tail,), F32)])[:, None]
    x1, h2, route, cnt_tiles = _post_attn(
        oa, lse, ob, z_tok, x2, w_proj_a[0].astype(BF16), w_proj_b[0].astype(BF16), w_out[0].astype(BF16),
        g_ffn[0][None, :], w_router, b_router, tm=MOE_TILE)

    out = _moe(h2, route, cnt_tiles, x1, w_exp_gate[0], w_exp_up[0], w_exp_down[0], g_final[None, :])
    return out.reshape(batch, seq, d)
```

```python
import functools
import math

import jax
import jax.numpy as jnp
import numpy as np
from jax import lax
from jax.experimental import pallas as pl
from jax.experimental.pallas import tpu as pltpu

F32 = jnp.float32
BF16 = jnp.bfloat16

D_MODEL = 1024
HEAD_DIM = 64
HALF = HEAD_DIM // 2
ROPE_THETA = 10000.0
RMS_EPS = 1e-6
LOG2E = math.log2(math.e)
LN2 = math.log(2.0)
Q_SCALE = LOG2E * HEAD_DIM ** -0.5
BLOCK = 128
ATTN_ROWS_PER_STEP = 2048
DIL_GROUPS = ((128, 1), (512, 4), (2048, 16))
N_DIL = len(DIL_GROUPS)
A_GROUP_WIDTH = 512
A_QKV_WIDTH = 3 * N_DIL * A_GROUP_WIDTH
B_Q_HEADS = 16
B_KV_HEADS = 2
B_Q_WIDTH = B_Q_HEADS * HEAD_DIM
B_WINDOW = 128
GATE_WIDTH = 2 * D_MODEL
IN_WIDTH = A_QKV_WIDTH + B_Q_WIDTH + 2 * B_KV_HEADS * HEAD_DIM + GATE_WIDTH
N_EXPERT_GROUPS = 4
EXPERTS_PER_GROUP = 8
N_EXPERTS = N_EXPERT_GROUPS * EXPERTS_PER_GROUP
D_EXPERT = D_MODEL // 4

CB = 256
PAIR = 128
N_IN_BLOCKS = IN_WIDTH // CB
A_BLOCKS = A_GROUP_WIDTH // CB
ZB_GATE = 0
ZB_QB = 8
ZB_KB = 12
ZB_VB = 13
ZB_A0 = 14
N_TOK_BLOCKS = ZB_A0 + 3 * A_BLOCKS
OUT_PERM_BLOCK = 256
LSE_PERM_BLOCK = 128
ROUTER_LANES = 128
EXPERT_LANE0 = EXPERTS_PER_GROUP
assert N_EXPERT_GROUPS <= EXPERTS_PER_GROUP
MOE_TILE = 512
CHUNK = 16
XS_RING = 3
EXP_TILE = 1024
LOCAL_ROWS = -(-(2 * MOE_TILE + N_EXPERTS * (CHUNK - 1)) // CB) * CB
PAIR_SLOTS = LOCAL_ROWS // CHUNK // 2
SEL_BLOCK = CB
ISSUE_UNROLL = 8
WAIT_GROUP = 8
XS_WIDTH = D_MODEL + ROUTER_LANES
META_PIECES = 3
assert META_PIECES * N_EXPERTS <= ROUTER_LANES

IN_TILE = 512
V7X_VMEM_BYTES = 64 * 1024 * 1024
VMEM_LIMIT = V7X_VMEM_BYTES * 7 // 8


def _in_proj_plan():
    plan = []
    for c in range(N_IN_BLOCKS):
        col = c * CB
        if col < A_QKV_WIDTH:
            part, rem = divmod(col, N_DIL * A_GROUP_WIDTH)
            group, blk = divmod(rem // CB, A_BLOCKS)
            kind = ("q", "k", "v")[part]
            dil = DIL_GROUPS[group][1]
            if dil == 1:
                plan.append((0, ZB_A0 + part * A_BLOCKS + blk, kind, 1))
            else:
                plan.append((group, part * A_BLOCKS + blk, kind, dil))
        elif col < A_QKV_WIDTH + B_Q_WIDTH:
            plan.append((0, ZB_QB + (col - A_QKV_WIDTH) // CB, "q", 1))
        elif col < A_QKV_WIDTH + B_Q_WIDTH + CB:
            plan.append((0, -1, "kvb", 1))
        else:
            plan.append((0, ZB_GATE + (col - (A_QKV_WIDTH + B_Q_WIDTH + CB)) // CB, "v", 1))
    return tuple(plan)


def _rms(x, g):
    return x * lax.rsqrt(jnp.mean(x * x, axis=-1, keepdims=True) + RMS_EPS) * g


def _split3(w):
    hi = w.astype(BF16).astype(F32)
    mid = (w - hi).astype(BF16).astype(F32)
    lo = (w - hi - mid).astype(BF16).astype(F32)
    return hi, mid, lo


def _rope(acc, cos, sin_signed, first_half):
    partner = jnp.where(first_half, pltpu.roll(acc, CB - HALF, 1), pltpu.roll(acc, HALF, 1))
    return acc * cos + partner * sin_signed


def _in_perm_block(dil):
    return max(BLOCK, CHUNK * dil)


def _in_proj_kernel(x_ref, g_ref, w_ref, b_ref, cos_ref, sin_ref, perm1_ref, perm2_ref,
                    zt_ref, zd1_ref, zd2_ref, *, plan, tm):
    out_refs = (zt_ref, zd1_ref, zd2_ref)
    perm_refs = (None, perm1_ref, perm2_ref)
    lane = lax.broadcasted_iota(jnp.int32, (tm, CB), 1)
    first_half = (lane % HEAD_DIM) < HALF

    h = _rms(x_ref[...], g_ref[...]).astype(BF16)
    h_by_dil, tables = {}, {}
    for slot, (_, dil) in enumerate(DIL_GROUPS):
        tables[dil] = slot
        if dil == 1:
            h_by_dil[dil] = h
            continue
        blk = _in_perm_block(dil)
        n = blk // dil
        moved = [jnp.dot(perm_refs[slot][...], h[tb * blk:(tb + 1) * blk], preferred_element_type=F32
                         ).astype(BF16) for tb in range(tm // blk)]
        h_by_dil[dil] = jnp.concatenate(
            [part[r * n:(r + 1) * n] for r in range(dil) for part in moved], axis=0)

    for c, (arr, dst, kind, dil) in enumerate(plan):
        cols = slice(c * CB, (c + 1) * CB)
        acc = jnp.dot(h_by_dil[dil], w_ref[:, cols], preferred_element_type=F32) + b_ref[:, cols]
        if kind in ("q", "k", "kvb"):
            slot = tables[dil]
            rot = _rope(acc, cos_ref[slot], sin_ref[slot], first_half)
        if kind == "q":
            val = (rot * Q_SCALE).astype(BF16)
        elif kind == "k":
            val = rot.astype(BF16)
        elif kind == "v":
            val = acc.astype(BF16)
        else:
            r64 = pltpu.roll(rot, HEAD_DIM, 1)
            r128 = pltpu.roll(rot, 2 * HEAD_DIM, 1)
            kdup = jnp.where(lane < HEAD_DIM, rot, jnp.where(lane < 3 * HEAD_DIM, r64, r128))
            a128 = pltpu.roll(acc, 2 * HEAD_DIM, 1)
            a192 = pltpu.roll(acc, 3 * HEAD_DIM, 1)
            vdup = jnp.where(lane < HEAD_DIM, a128, jnp.where(lane < 3 * HEAD_DIM, a192, acc))
            zt_ref[ZB_KB] = kdup.astype(BF16)
            zt_ref[ZB_VB] = vdup.astype(BF16)
            continue
        if dil == 1:
            out_refs[arr][dst] = val
        else:
            n = tm // dil
            for r in range(dil):
                out_refs[arr][dst, :, r * CB:(r + 1) * CB] = val[r * n:(r + 1) * n, :]


def _dilation_perm(block, dil, dtype):
    j = np.arange(block)
    src = (j % (block // dil)) * dil + j // (block // dil)
    return jnp.asarray((src[:, None] == j[None, :]).astype(np.float32), dtype=dtype)


def _in_proj(x2, g_mix, w_in, b_in, cos_t, sin_t, *, seq, tm):
    n_tok = x2.shape[0]
    tiles_per_seq = seq // tm
    const = lambda i: (0, 0)
    table = pl.BlockSpec((N_DIL, tm, CB), lambda i: (0, i % tiles_per_seq, 0))
    d1, d2 = DIL_GROUPS[1][1], DIL_GROUPS[2][1]
    perms = [_dilation_perm(_in_perm_block(d), d, BF16) for d in (d1, d2)]
    perm_spec = lambda d: pl.BlockSpec((_in_perm_block(d),) * 2, const, pipeline_mode=pl.Buffered(1))
    return pl.pallas_call(
        functools.partial(_in_proj_kernel, plan=_in_proj_plan(), tm=tm),
        grid=(n_tok // tm,),
        in_specs=[
            pl.BlockSpec((tm, D_MODEL), lambda i: (i, 0)),
            pl.BlockSpec((1, D_MODEL), const),
            pl.BlockSpec((D_MODEL, IN_WIDTH), const, pipeline_mode=pl.Buffered(1)),
            pl.BlockSpec((1, IN_WIDTH), const),
            table, table,
            perm_spec(d1), perm_spec(d2),
        ],
        out_specs=[
            pl.BlockSpec((N_TOK_BLOCKS, tm, CB), lambda i: (0, i, 0)),
            pl.BlockSpec((3 * A_BLOCKS, tm // d1, d1 * CB), lambda i: (0, i, 0)),
            pl.BlockSpec((3 * A_BLOCKS, tm // d2, d2 * CB), lambda i: (0, i, 0)),
        ],
        out_shape=[
            jax.ShapeDtypeStruct((N_TOK_BLOCKS, n_tok, CB), BF16),
            jax.ShapeDtypeStruct((3 * A_BLOCKS, n_tok // d1, d1 * CB), BF16),
            jax.ShapeDtypeStruct((3 * A_BLOCKS, n_tok // d2, d2 * CB), BF16),
        ],
        compiler_params=pltpu.CompilerParams(
            dimension_semantics=("arbitrary",), vmem_limit_bytes=VMEM_LIMIT),
        name="in_proj",
    )(x2, g_mix, w_in, b_in, cos_t, sin_t, *perms)


def _attn_kernel(*refs, lq, max_dist, kv_shared, has_sink, want_lse, n_axes):
    refs = list(refs)
    sink_ref = refs.pop(0) if has_sink else None
    q_ref, k_ref, v_ref, kp_ref, vp_ref = refs[:5]
    o_ref = refs[5]
    lse_ref = refs[6] if want_lse else None
    vaug_ref = refs[-1]
    n_qblk = q_ref.shape[0]
    n_kv_pairs = vaug_ref.shape[0]
    rows_kv = lq + BLOCK

    row = lax.broadcasted_iota(jnp.int32, (BLOCK, 2 * BLOCK), 0)
    col = lax.broadcasted_iota(jnp.int32, (BLOCK, 2 * BLOCK), 1)
    dist = row - col + BLOCK
    valid = (dist >= 0) & (dist <= max_dist)
    neg_inf = jnp.float32(-jnp.inf)
    bias = jnp.where(valid, 0.0, neg_inf)
    bias_first = jnp.where(valid & (col >= BLOCK), 0.0, neg_inf)
    bias0 = jnp.where(pl.program_id(n_axes - 1) == 0, bias_first, bias)
    lane_lo = lax.broadcasted_iota(jnp.int32, (BLOCK, PAIR), 1) < HEAD_DIM

    def rd(ref, blk, rows, cols):
        return ref[rows, cols] if kv_shared else ref[blk, rows, cols]

    first_step = functools.reduce(lambda a, b: a & b, [pl.program_id(a) == 0 for a in range(n_axes)])

    @pl.when(first_step)
    def _():
        lane = lax.broadcasted_iota(jnp.int32, (rows_kv, PAIR), 1)
        for pair in range(n_kv_pairs):
            vaug_ref[pair, 0, :, PAIR:] = jnp.where(lane < HEAD_DIM, 1.0, 0.0).astype(BF16)
            vaug_ref[pair, 1, :, PAIR:] = jnp.where(lane < HEAD_DIM, 0.0, 1.0).astype(BF16)

    lane_kv = lax.broadcasted_iota(jnp.int32, (rows_kv, PAIR), 1) < HEAD_DIM
    n_col_pairs = q_ref.shape[-1] // PAIR
    for pair in range(n_kv_pairs):
        cols = slice((pair % n_col_pairs) * PAIR, (pair % n_col_pairs + 1) * PAIR)
        everything = slice(None)
        v_all = jnp.concatenate([rd(vp_ref, pair // n_col_pairs, everything, cols),
                                 rd(v_ref, pair // n_col_pairs, everything, cols)], axis=0)
        zero = jnp.zeros_like(v_all)
        vaug_ref[pair, 0, :, :PAIR] = jnp.where(lane_kv, v_all, zero)
        vaug_ref[pair, 1, :, :PAIR] = jnp.where(lane_kv, zero, v_all)

    for ib in range(lq // BLOCK):
        rows = slice(ib * BLOCK, (ib + 1) * BLOCK)
        win = slice(ib * BLOCK, (ib + 2) * BLOCK)
        b_ib = bias0 if ib == 0 else bias
        for blk in range(n_qblk):
            for pp in range(n_col_pairs):
                qcols = slice(pp * PAIR, (pp + 1) * PAIR)
                pair = 0 if kv_shared else blk * n_col_pairs + pp
                kcols = slice(0, PAIR) if kv_shared else qcols
                if ib == 0:
                    k_win = jnp.concatenate([rd(kp_ref, blk, slice(None), kcols),
                                             rd(k_ref, blk, slice(0, BLOCK), kcols)], axis=0)
                else:
                    k_win = rd(k_ref, blk, slice((ib - 1) * BLOCK, (ib + 1) * BLOCK), kcols)
                q_pair = q_ref[blk, rows, qcols]
                ps, ms, sink_terms = [], [], []
                for hh in range(2):
                    q_h = jnp.where(lane_lo == (hh == 0), q_pair, jnp.zeros_like(q_pair))
                    s = lax.dot_general(q_h, k_win, (((1,), (1,)), ((), ())), preferred_element_type=F32)
                    s = s + b_ib
                    m = jnp.max(s, axis=-1, keepdims=True)
                    if has_sink:
                        head = (pl.program_id(1) * n_qblk + blk) * (CB // HEAD_DIM) + pp * 2 + hh
                        sink = sink_ref[head] * LOG2E
                        m = jnp.maximum(m, sink)
                        sink_terms.append(jnp.exp2(sink - m))
                    ps.append(jnp.exp2(s - m).astype(BF16))
                    ms.append(m)
                v_aug = jnp.concatenate([vaug_ref[pair, 0, win, :], vaug_ref[pair, 1, win, :]], axis=0)
                od = jnp.dot(jnp.concatenate(ps, axis=1), v_aug, preferred_element_type=F32)
                den = od[:, PAIR:]
                if has_sink:
                    den = den + jnp.where(lane_lo, sink_terms[0], sink_terms[1])
                o_ref[blk, rows, qcols] = (od[:, :PAIR] * (1.0 / den)).astype(BF16)
                if want_lse:
                    lse_ref[blk, rows, qcols] = (jnp.where(lane_lo, ms[0], ms[1]) + jnp.log2(den)) * LN2


def _dilated_attention(z, bases, group, *, batch, seq, lq):
    window, dil = DIL_GROUPS[group]
    sub_len = seq // dil
    lq = min(lq, sub_len)
    z4 = z.reshape(z.shape[0], batch, sub_len, dil * CB)
    qb, kb, vb = bases
    assert all(base % A_BLOCKS == 0 for base in bases)
    bpq = lq // BLOCK
    slabs = max(1, min(dil, ATTN_ROWS_PER_STEP // lq))
    width = slabs * CB
    cur = lambda base: pl.BlockSpec(
        (A_BLOCKS, None, lq, width), lambda b, r, i: (base // A_BLOCKS, b, i, r))
    prev = lambda base: pl.BlockSpec(
        (A_BLOCKS, None, BLOCK, width), lambda b, r, i: (base // A_BLOCKS, b, jnp.maximum(i * bpq - 1, 0), r))
    out_spec = pl.BlockSpec((A_BLOCKS, None, lq, width), lambda b, r, i: (0, b, i, r))
    o, lse = pl.pallas_call(
        functools.partial(_attn_kernel, lq=lq, max_dist=window // dil, kv_shared=False,
                          has_sink=False, want_lse=True, n_axes=3),
        grid=(batch, dil // slabs, sub_len // lq),
        in_specs=[cur(qb), cur(kb), cur(vb), prev(kb), prev(vb)],
        out_specs=[out_spec, out_spec],
        out_shape=[jax.ShapeDtypeStruct((A_BLOCKS, batch, sub_len, dil * CB), BF16),
                   jax.ShapeDtypeStruct((A_BLOCKS, batch, sub_len, dil * CB), F32)],
        scratch_shapes=[pltpu.VMEM((2 * A_BLOCKS * slabs, 2, lq + BLOCK, 2 * PAIR), BF16)],
        compiler_params=pltpu.CompilerParams(
            dimension_semantics=("arbitrary",) * 3, vmem_limit_bytes=VMEM_LIMIT),
        name=f"dilated_attn_g{group}",
    )(z4, z4, z4, z4, z4)
    rows = batch * sub_len
    return o.reshape(A_BLOCKS, rows, dil * CB), lse.reshape(A_BLOCKS, rows, dil * CB)


def _swa_attention(z_tok, sinks, *, batch, seq, lq):
    z4 = z_tok.reshape(N_TOK_BLOCKS, batch, seq, CB)
    bpq = lq // BLOCK
    n_q_blocks = B_Q_WIDTH // CB
    q_per_kv = n_q_blocks // B_KV_HEADS
    assert ZB_QB % q_per_kv == 0
    q_spec = pl.BlockSpec((q_per_kv, None, lq, CB), lambda b, kvh, i, s: (ZB_QB // q_per_kv + kvh, b, i, 0))
    cur = lambda base: pl.BlockSpec((None, None, lq, PAIR), lambda b, kvh, i, s: (base, b, i, kvh))
    prev = lambda base: pl.BlockSpec(
        (None, None, BLOCK, PAIR), lambda b, kvh, i, s: (base, b, jnp.maximum(i * bpq - 1, 0), kvh))
    o = pl.pallas_call(
        functools.partial(_attn_kernel, lq=lq, max_dist=B_WINDOW - 1, kv_shared=True,
                          has_sink=True, want_lse=False, n_axes=3),
        grid_spec=pltpu.PrefetchScalarGridSpec(
            num_scalar_prefetch=1,
            grid=(batch, B_KV_HEADS, seq // lq),
            in_specs=[q_spec, cur(ZB_KB), cur(ZB_VB), prev(ZB_KB), prev(ZB_VB)],
            out_specs=pl.BlockSpec((q_per_kv, None, lq, CB), lambda b, kvh, i, s: (kvh, b, i, 0)),
            scratch_shapes=[pltpu.VMEM((1, 2, lq + BLOCK, 2 * PAIR), BF16)],
        ),
        out_shape=jax.ShapeDtypeStruct((n_q_blocks, batch, seq, CB), BF16),
        compiler_params=pltpu.CompilerParams(
            dimension_semantics=("arbitrary",) * 3, vmem_limit_bytes=VMEM_LIMIT),
        name="swa_attn",
    )(sinks, z4, z4, z4, z4, z4)
    return o.reshape(n_q_blocks, batch * seq, CB)


def _post_attn_kernel(o0_ref, o1_ref, o2_ref, l0_ref, l1_ref, l2_ref, ob_ref, gate_ref, x_ref,
                      wa_ref, wb_ref, wo_ref, gf_ref, wr_ref, br_ref,
                      po_ref, pl_ref, x1_ref, h2_ref, route_ref, cnt_ref, *, tm):

    def to_token_order(ref, cb, slot, perm_ref, blk):
        dil = DIL_GROUPS[slot + 1][1]
        n = blk // dil
        parts = []
        for tb in range(tm // blk):
            stack = jnp.concatenate(
                [ref[cb, tb * n:(tb + 1) * n, r * CB:(r + 1) * CB] for r in range(dil)], axis=0)
            pieces = (stack,) if stack.dtype == BF16 else _split3(stack)
            moved = [jnp.dot(perm_ref[slot], p.astype(BF16), preferred_element_type=F32) for p in pieces]
            parts.append(functools.reduce(lambda a, b: a + b, moved))
        return jnp.concatenate(parts, axis=0)

    pa = None
    for cb in range(A_BLOCKS):
        l0 = l0_ref[cb]
        l1 = to_token_order(l1_ref, cb, 0, pl_ref, LSE_PERM_BLOCK)
        l2 = to_token_order(l2_ref, cb, 1, pl_ref, LSE_PERM_BLOCK)
        o1 = to_token_order(o1_ref, cb, 0, po_ref, OUT_PERM_BLOCK)
        o2 = to_token_order(o2_ref, cb, 1, po_ref, OUT_PERM_BLOCK)
        mx = jnp.maximum(jnp.maximum(l0, l1), l2)
        e0, e1, e2 = jnp.exp(l0 - mx), jnp.exp(l1 - mx), jnp.exp(l2 - mx)
        inv = 1.0 / (e0 + e1 + e2)
        ya = (e0 * inv) * o0_ref[cb].astype(F32) + (e1 * inv) * o1 + (e2 * inv) * o2
        part = jnp.dot(ya.astype(BF16), wa_ref[cb * CB:(cb + 1) * CB, :], preferred_element_type=F32)
        pa = part if pa is None else pa + part
    pb = None
    for cb in range(B_Q_WIDTH // CB):
        part = jnp.dot(ob_ref[cb], wb_ref[cb * CB:(cb + 1) * CB, :], preferred_element_type=F32)
        pb = part if pb is None else pb + part
    n_gate = D_MODEL // CB
    merged = []
    for cb in range(n_gate):
        cols = slice(cb * CB, (cb + 1) * CB)
        ga = jax.nn.sigmoid(gate_ref[cb].astype(F32))
        gb = jax.nn.sigmoid(gate_ref[n_gate + cb].astype(F32))
        merged.append((ga * pa[:, cols] + gb * pb[:, cols]).astype(BF16))
    merged = jnp.concatenate(merged, axis=1)
    x1 = x_ref[...] + jnp.dot(merged, wo_ref[...], preferred_element_type=F32)
    x1_ref[...] = x1
    h2 = _rms(x1, gf_ref[...])
    h2_ref[...] = h2.astype(BF16)

    h2_hi = h2.astype(BF16)
    h2_lo = (h2 - h2_hi.astype(F32)).astype(BF16)
    wr = wr_ref[...]
    wr_hi = wr.astype(BF16)
    wr_lo = (wr - wr_hi.astype(F32)).astype(BF16)
    nt = (((1,), (1,)), ((), ()))
    hi_terms = lax.dot_general(jnp.concatenate([wr_hi, wr_lo], axis=0), h2_hi, nt, preferred_element_type=F32)
    logits = (hi_terms[:ROUTER_LANES]
              + (lax.dot_general(wr_hi, h2_lo, nt, preferred_element_type=F32) + hi_terms[ROUTER_LANES:])
              ) + br_ref[...]
    sub = lax.broadcasted_iota(jnp.int32, (EXPERTS_PER_GROUP, tm), 0)
    neg_inf = jnp.float32(-jnp.inf)
    top = lambda v: jnp.max(v, axis=0, keepdims=True)
    first = lambda hit: jnp.min(jnp.where(hit, sub, EXPERTS_PER_GROUP), axis=0, keepdims=True)
    lg = jnp.where(sub < N_EXPERT_GROUPS, logits[:EXPERTS_PER_GROUP], neg_inf)
    mg = top(lg)
    gsel = first(lg == mg)
    pg_sel = 1.0 / jnp.sum(jnp.exp(lg - mg), axis=0, keepdims=True)
    le = logits[EXPERT_LANE0:EXPERT_LANE0 + EXPERTS_PER_GROUP]
    for g in range(1, N_EXPERT_GROUPS):
        r0 = EXPERT_LANE0 + g * EXPERTS_PER_GROUP
        le = jnp.where(gsel == g, logits[r0:r0 + EXPERTS_PER_GROUP], le)
    ex = jnp.exp(le - top(le))
    pe = ex / jnp.sum(ex, axis=0, keepdims=True)
    p1 = top(pe)
    i1 = first(pe == p1)
    rest = sub != i1
    p2 = top(jnp.where(rest, pe, -1.0))
    i2 = first(rest & (pe == p2))
    norm = pg_sel / (p1 + p2)
    e1 = gsel * EXPERTS_PER_GROUP + i1
    e2 = gsel * EXPERTS_PER_GROUP + i2
    row = lax.broadcasted_iota(jnp.int32, (ROUTER_LANES, tm), 0)
    route_t = jnp.where(
        row == 0, e1.astype(F32),
        jnp.where(row == 1, e2.astype(F32),
                  jnp.where(row == 2, p1 * norm, jnp.where(row == 3, p2 * norm, 0.0))))
    route_ref[...] = route_t.T
    hits_t = jnp.where((row == e1 + EXPERT_LANE0) | (row == e2 + EXPERT_LANE0), 1.0, 0.0).astype(BF16)
    cnt_ref[0] = lax.dot_general(jnp.ones((cnt_ref.shape[1], tm), BF16), hits_t, nt,
                                 preferred_element_type=F32)


def _post_attn(oa, lse, ob, z_tok, x2, w_proj_a, w_proj_b, w_out, g_ffn, w_router, b_router, *, tm):
    n_tok = x2.shape[0]
    const2 = lambda i: (0, 0)
    blk = lambda nb: pl.BlockSpec((nb, tm, CB), lambda i: (0, i, 0))
    dil_blk = lambda dil: pl.BlockSpec((A_BLOCKS, tm // dil, dil * CB), lambda i: (0, i, 0))
    row = lambda width: pl.BlockSpec((tm, width), lambda i: (i, 0))
    resident = lambda shape: pl.BlockSpec(shape, const2, pipeline_mode=pl.Buffered(1))
    d1, d2 = DIL_GROUPS[1][1], DIL_GROUPS[2][1]
    perm_o = jnp.stack([_dilation_perm(OUT_PERM_BLOCK, d, BF16).T for d in (d1, d2)])
    perm_l = jnp.stack([_dilation_perm(LSE_PERM_BLOCK, d, BF16).T for d in (d1, d2)])
    return pl.pallas_call(
        functools.partial(_post_attn_kernel, tm=tm),
        grid=(n_tok // tm,),
        in_specs=[blk(A_BLOCKS), dil_blk(d1), dil_blk(d2), blk(A_BLOCKS), dil_blk(d1), dil_blk(d2),
                  blk(B_Q_WIDTH // CB), blk(GATE_WIDTH // CB), row(D_MODEL),
                  resident((A_GROUP_WIDTH, D_MODEL)), resident((B_Q_WIDTH, D_MODEL)),
                  resident((D_MODEL, D_MODEL)), resident((1, D_MODEL)),
                  resident((ROUTER_LANES, D_MODEL)), resident((ROUTER_LANES, 1)),
                  pl.BlockSpec((2, OUT_PERM_BLOCK, OUT_PERM_BLOCK), lambda i: (0, 0, 0),
                               pipeline_mode=pl.Buffered(1)),
                  pl.BlockSpec((2, LSE_PERM_BLOCK, LSE_PERM_BLOCK), lambda i: (0, 0, 0),
                               pipeline_mode=pl.Buffered(1))],
        out_specs=[row(D_MODEL), row(D_MODEL), row(ROUTER_LANES),
                   pl.BlockSpec((1, 8, ROUTER_LANES), lambda i: (i, 0, 0))],
        out_shape=[jax.ShapeDtypeStruct((n_tok, D_MODEL), F32),
                   jax.ShapeDtypeStruct((n_tok, D_MODEL), BF16),
                   jax.ShapeDtypeStruct((n_tok, ROUTER_LANES), F32),
                   jax.ShapeDtypeStruct((n_tok // tm, 8, ROUTER_LANES), F32)],
        compiler_params=pltpu.CompilerParams(
            dimension_semantics=("arbitrary",), vmem_limit_bytes=VMEM_LIMIT),
        name="post_attn",
    )(oa[0], oa[1], oa[2], lse[0], lse[1], lse[2], ob, z_tok, x2,
      w_proj_a, w_proj_b, w_out, g_ffn, w_router, b_router, perm_o, perm_l)


def _local_slots(route, lbase_row, ltri):
    lane = lax.broadcasted_iota(jnp.int32, route.shape, 1)
    pick = lambda k: jnp.sum(jnp.where(lane == k, route, 0.0), axis=-1, keepdims=True)
    lanef = lane.astype(F32)
    oh1, oh2 = lanef == pick(0), lanef == pick(1)
    oh = jnp.where(oh1 | oh2, 1.0, 0.0).astype(BF16)
    table = jnp.dot(ltri, oh, preferred_element_type=F32) + lbase_row
    ls1 = jnp.sum(jnp.where(oh1, table, 0.0), axis=-1, keepdims=True)
    ls2 = jnp.sum(jnp.where(oh2, table, 0.0), axis=-1, keepdims=True)
    return ls1, ls2, pick


def _chunk_loop(count, body, unroll=1):
    one = lambda c, carry: (body(c), carry)[1]
    full = 0
    if unroll > 1:
        full = (count // unroll) * unroll

        def group(g, carry):
            for u in range(unroll):
                body(g * unroll + u)
            return carry

        lax.fori_loop(0, count // unroll, group, 0)
    lax.fori_loop(full, count, one, 0)


def _wait_chunks(count, wait_rows):
    _chunk_loop(count // WAIT_GROUP, lambda c: wait_rows(WAIT_GROUP * CHUNK))
    _chunk_loop(count % WAIT_GROUP, lambda c: wait_rows(CHUNK))


def _selection_blocks(slots_a, slots_b, axis, other):
    shape = (SEL_BLOCK, other) if axis == 0 else (other, SEL_BLOCK)
    local = lax.broadcasted_iota(jnp.int32, shape, axis).astype(F32).astype(BF16)
    one, zero = jnp.ones_like(local), jnp.zeros_like(local)

    def block(a):
        rel_a = (slots_a - float(a * SEL_BLOCK)).astype(BF16)
        rel_b = (slots_b - float(a * SEL_BLOCK)).astype(BF16)
        return jnp.where(local == rel_a, one, jnp.where(local == rel_b, one, zero))

    return block


def _for_used_blocks(used_rows, body):
    always = 2 * MOE_TILE // SEL_BLOCK + 1
    body(0, always, True)
    for a in range(always, LOCAL_ROWS // SEL_BLOCK):
        pl.when(used_rows > a * SEL_BLOCK)(functools.partial(body, a, a + 1, False))


def _issue_copies(copy_refs, tile, start):
    pair_src, pair_dst, n_pair, one_src, one_dst, n_one = copy_refs
    for src_ref, dst_ref, n_ref, slots, size in ((pair_src, pair_dst, n_pair, PAIR_SLOTS, 2),
                                                  (one_src, one_dst, n_one, N_EXPERTS, 1)):
        _chunk_loop(n_ref[tile], lambda c: start(
            pl.multiple_of(src_ref[tile * slots + c] * CHUNK, CHUNK),
            pl.multiple_of(dst_ref[tile * slots + c] * CHUNK, CHUNK), size * CHUNK), unroll=ISSUE_UNROLL)


def _dispatch_kernel(pair_src_ref, pair_dst_ref, n_pair_ref, one_src_ref, one_dst_ref, n_one_ref,
                     tot_ref, tail_start_ref, tail_n_ref,
                     h2_ref, route_ref, lbase_ref, ltri_ref, xs_ref, buf_ref, zero_ref, sem_ref, zsem_ref,
                     *, tm):
    copy_refs = (pair_src_ref, pair_dst_ref, n_pair_ref, one_src_ref, one_dst_ref, n_one_ref)
    i = pl.program_id(0)
    last = pl.num_programs(0) - 1
    slot = i % 2

    def chunk(slot_, src_row, dst_row, rows=CHUNK):
        return pltpu.make_async_copy(buf_ref.at[slot_, pl.ds(src_row, rows), :],
                                     xs_ref.at[pl.ds(dst_row, rows), :], sem_ref.at[slot_])

    def wait_tile(slot_, tile):
        _wait_chunks(tot_ref[tile], lambda rows: chunk(slot_, 0, 0, rows).wait())

    @pl.when(i == 0)
    def _():
        zero_ref[...] = jnp.zeros_like(zero_ref)

        def zero_chunk(row):
            return pltpu.make_async_copy(zero_ref, xs_ref.at[pl.ds(row, CHUNK), :], zsem_ref)

        def per_expert(e, total):
            start = tail_start_ref[e] * CHUNK
            _chunk_loop(tail_n_ref[e], lambda c: zero_chunk(pl.multiple_of(start + c * CHUNK, CHUNK)).start())
            return total + tail_n_ref[e]

        total = lax.fori_loop(0, N_EXPERTS, per_expert, 0)
        _chunk_loop(total, lambda c: zero_chunk(0).wait())

    @pl.when(i >= 2)
    def _():
        wait_tile(slot, i - 2)

    route = route_ref[...]
    ls1, ls2, pick = _local_slots(route, lbase_ref[0, 0:1, :], ltri_ref[...])
    lane = lax.broadcasted_iota(jnp.int32, route.shape, 1)
    ls_t = jnp.where(lane == 0, ls1, jnp.where(lane == 1, ls2, 0.0)).T
    sel_block = _selection_blocks(ls_t[0:1, :], ls_t[1:2, :], 0, tm)
    lane_expert = (lane // META_PIECES).astype(F32)
    lane_piece = lane % META_PIECES
    by_piece = lambda pieces: jnp.where(lane_piece == 0, pieces[0],
                                        jnp.where(lane_piece == 1, pieces[1], pieces[2]))
    meta = jnp.where(lane_expert == pick(0), by_piece(_split3(pick(2))),
                     jnp.where(lane_expert == pick(1), by_piece(_split3(pick(3))), 0.0))
    meta = meta.astype(BF16)

    def sort_block(a0, a1, first):
        del first
        rows = slice(a0 * SEL_BLOCK, a1 * SEL_BLOCK)
        sel = jnp.concatenate([sel_block(a) for a in range(a0, a1)], axis=0)
        buf_ref[slot, rows, :D_MODEL] = jnp.dot(sel, h2_ref[...], preferred_element_type=F32).astype(BF16)
        buf_ref[slot, rows, D_MODEL:] = jnp.dot(sel, meta, preferred_element_type=F32).astype(BF16)

    _for_used_blocks(tot_ref[i] * CHUNK, sort_block)

    _issue_copies(copy_refs, i, lambda local_row, xs_row, rows: chunk(slot, local_row, xs_row, rows).start())

    @pl.when(i == last)
    def _():
        @pl.when(i >= 1)
        def _():
            wait_tile(1 - slot, i - 1)
        wait_tile(slot, i)


def _expert_kernel(te_ref, nv_ref, xs_hbm_ref, wg_ref, wu_ref, wd_ref, ys_ref, wgb_ref, wub_ref, wdb_ref,
                   xbuf_ref, xsem_ref):
    g = pl.program_id(0)
    e = te_ref[g]
    n_valid = nv_ref[0]

    def fetch(tile):
        return pltpu.make_async_copy(
            xs_hbm_ref.at[pl.ds(pl.multiple_of(tile * EXP_TILE, EXP_TILE), EXP_TILE), :],
            xbuf_ref.at[tile % XS_RING], xsem_ref.at[tile % XS_RING])

    @pl.when(g == 0)
    def _():
        for ahead in range(XS_RING - 1):
            pl.when(ahead < n_valid)(lambda ahead=ahead: fetch(ahead).start())

    @pl.when(g + (XS_RING - 1) < n_valid)
    def _():
        fetch(g + (XS_RING - 1)).start()

    xs_ref = xbuf_ref.at[g % XS_RING]

    @pl.when((g == 0) | (te_ref[jnp.maximum(g - 1, 0)] != e))
    def _():
        wgb_ref[...] = wg_ref[...].astype(BF16)
        wub_ref[...] = wu_ref[...].astype(BF16)
        wdb_ref[...] = wd_ref[...].astype(BF16)

    @pl.when(g < n_valid)
    def _():
        fetch(g).wait()
        x = xs_ref[:, :D_MODEL]
        meta = xs_ref[:, D_MODEL:].astype(F32)
        lane = lax.broadcasted_iota(jnp.int32, meta.shape, 1)
        mine = (lane >= e * META_PIECES) & (lane < (e + 1) * META_PIECES)
        w = jnp.sum(jnp.where(mine, meta, 0.0), axis=-1, keepdims=True)
        hg = jnp.dot(x, wgb_ref[...], preferred_element_type=F32)
        hu = jnp.dot(x, wub_ref[...], preferred_element_type=F32)
        a = (hg * jax.nn.sigmoid(hg)) * hu * w
        ys_ref[...] = jnp.dot(a.astype(BF16), wdb_ref[...], preferred_element_type=F32).astype(BF16)


def _combine_kernel(pair_src_ref, pair_dst_ref, n_pair_ref, one_src_ref, one_dst_ref, n_one_ref, tot_ref,
                    x1_ref, route_ref, lbase_ref, ltri_ref, gfin_ref, ys_ref, out_ref, ybuf_ref, sem_ref,
                    *, tm):
    copy_refs = (pair_src_ref, pair_dst_ref, n_pair_ref, one_src_ref, one_dst_ref, n_one_ref)
    i = pl.program_id(0)
    n_tiles = pl.num_programs(0)
    slot = i % 2

    def chunk(slot_, src_row, dst_row, rows=CHUNK):
        return pltpu.make_async_copy(ys_ref.at[pl.ds(src_row, rows), :],
                                     ybuf_ref.at[slot_, pl.ds(dst_row, rows), :], sem_ref.at[slot_])

    def fetch_tile(tile, slot_):
        _issue_copies(copy_refs, tile,
                      lambda local_row, ys_row, rows: chunk(slot_, ys_row, local_row, rows).start())

    @pl.when(i == 0)
    def _():
        ybuf_ref[...] = jnp.zeros_like(ybuf_ref)
        fetch_tile(0, 0)

    @pl.when(i + 1 < n_tiles)
    def _():
        fetch_tile(i + 1, 1 - slot)

    _wait_chunks(tot_ref[i], lambda rows: chunk(slot, 0, 0, rows).wait())

    ls1, ls2, _ = _local_slots(route_ref[...], lbase_ref[0, 0:1, :], ltri_ref[...])
    sel_block = _selection_blocks(ls1, ls2, 1, tm)
    def add_block(a0, a1, first):
        rows = slice(a0 * SEL_BLOCK, a1 * SEL_BLOCK)
        sel = jnp.concatenate([sel_block(a) for a in range(a0, a1)], axis=1)
        y = jnp.dot(sel, ybuf_ref[slot, rows, :], preferred_element_type=F32)
        out_ref[...] = (x1_ref[...] if first else out_ref[...]) + y

    _for_used_blocks(tot_ref[i] * CHUNK, add_block)
    out_ref[...] = _rms(out_ref[...], gfin_ref[...])


def _routing_tables(cnt, n_exp_tiles):
    c16 = (cnt + CHUNK - 1) // CHUNK
    lbase = jnp.cumsum(c16, axis=1) - c16
    tile_off = jnp.cumsum(c16, axis=0) - c16
    tot = jnp.sum(c16, axis=0)
    per = EXP_TILE // CHUNK
    region_tiles = (tot + per - 1) // per
    region = region_tiles * per
    base = jnp.cumsum(region) - region
    dst = base[None, :] + tile_off
    tile_end = jnp.cumsum(region_tiles)
    n_valid = tile_end[-1]
    g = jnp.arange(n_exp_tiles, dtype=jnp.int32)
    tile_expert = jnp.sum(tile_end[None, :] <= jnp.minimum(g, n_valid - 1)[:, None], axis=1).astype(jnp.int32)
    i32 = lambda a: a.astype(jnp.int32).reshape(-1)

    def copy_list(per_seg, first, size, slots):
        start = jnp.cumsum(per_seg, axis=1) - per_seg
        k = jnp.arange(slots, dtype=jnp.int32)[None, :, None]
        owns = (start[:, None, :] <= k) & (k < (start + per_seg)[:, None, :])
        inside = (first[:, None, :] + (k - start[:, None, :]) * size)
        pick = lambda seg_start: jnp.sum(jnp.where(owns, seg_start[:, None, :] + inside, 0), axis=2)
        return i32(pick(lbase)), i32(pick(dst)), i32(jnp.sum(per_seg, axis=1))

    pair_src, pair_dst, n_pair = copy_list(c16 // 2, jnp.zeros_like(c16), 2, PAIR_SLOTS)
    one_src, one_dst, n_one = copy_list(c16 % 2, (c16 // 2) * 2, 1, N_EXPERTS)
    return dict(copies=(pair_src, pair_dst, n_pair, one_src, one_dst, n_one), tot=i32(jnp.sum(c16, axis=1)),
                tail_start=i32(base + tot), tail_n=i32(region - tot),
                tile_expert=tile_expert, n_valid=i32(n_valid),
                lbase_rows=(lbase * CHUNK).astype(F32))


def _moe(h2, route, cnt_tiles, x1, w_gate, w_up, w_down, g_final):
    n_tok = h2.shape[0]
    tm = MOE_TILE
    n_tiles = n_tok // tm
    assert cnt_tiles.shape[0] == n_tiles
    worst_rows = 2 * n_tok + n_tiles * N_EXPERTS * (CHUNK - 1) + N_EXPERTS * (EXP_TILE - CHUNK)
    n_exp_tiles = -(-worst_rows // EXP_TILE)
    n_slots = n_exp_tiles * EXP_TILE

    cnt = cnt_tiles[:, 0, EXPERT_LANE0:EXPERT_LANE0 + N_EXPERTS].astype(jnp.int32)
    tb = _routing_tables(cnt, n_exp_tiles)
    lbase_rows = jnp.zeros((n_tiles, 8, ROUTER_LANES), F32).at[:, :, :N_EXPERTS].set(
        tb["lbase_rows"][:, None, :])
    row_id = np.arange(tm)
    ltri = jnp.asarray((row_id[:, None] > row_id[None, :]).astype(np.float32), dtype=BF16)

    tile_row = lambda width: pl.BlockSpec((tm, width), lambda i, *_: (i, 0))
    lbase_spec = pl.BlockSpec((1, 8, ROUTER_LANES), lambda i, *_: (i, 0, 0))
    ltri_spec = pl.BlockSpec((tm, tm), lambda i, *_: (0, 0), pipeline_mode=pl.Buffered(1))
    hbm = pl.BlockSpec(memory_space=pl.ANY)

    xs = pl.pallas_call(
        functools.partial(_dispatch_kernel, tm=tm),
        grid_spec=pltpu.PrefetchScalarGridSpec(
            num_scalar_prefetch=9, grid=(n_tiles,),
            in_specs=[tile_row(D_MODEL), tile_row(ROUTER_LANES), lbase_spec, ltri_spec],
            out_specs=hbm,
            scratch_shapes=[pltpu.VMEM((2, LOCAL_ROWS, XS_WIDTH), BF16), pltpu.VMEM((CHUNK, XS_WIDTH), BF16),
                            pltpu.SemaphoreType.DMA((2,)), pltpu.SemaphoreType.DMA]),
        out_shape=jax.ShapeDtypeStruct((n_slots, XS_WIDTH), BF16),
        compiler_params=pltpu.CompilerParams(
            dimension_semantics=("arbitrary",), vmem_limit_bytes=VMEM_LIMIT),
        name="moe_dispatch",
    )(*tb["copies"], tb["tot"], tb["tail_start"], tb["tail_n"], h2, route, lbase_rows, ltri)

    row_tile = lambda width: pl.BlockSpec(
        (EXP_TILE, width), lambda g, te, nv: (jnp.maximum(jnp.minimum(g, nv[0] - 1), 0), 0))
    ys = pl.pallas_call(
        _expert_kernel,
        grid_spec=pltpu.PrefetchScalarGridSpec(
            num_scalar_prefetch=2, grid=(n_exp_tiles,),
            in_specs=[pl.BlockSpec(memory_space=pl.ANY),
                      pl.BlockSpec((None, D_MODEL, D_EXPERT), lambda g, te, nv: (te[g], 0, 0)),
                      pl.BlockSpec((None, D_MODEL, D_EXPERT), lambda g, te, nv: (te[g], 0, 0)),
                      pl.BlockSpec((None, D_EXPERT, D_MODEL), lambda g, te, nv: (te[g], 0, 0))],
            out_specs=row_tile(D_MODEL),
            scratch_shapes=[pltpu.VMEM((D_MODEL, D_EXPERT), BF16), pltpu.VMEM((D_MODEL, D_EXPERT), BF16),
                            pltpu.VMEM((D_EXPERT, D_MODEL), BF16),
                            pltpu.VMEM((XS_RING, EXP_TILE, XS_WIDTH), BF16), pltpu.SemaphoreType.DMA((XS_RING,))]),
        out_shape=jax.ShapeDtypeStruct((n_slots, D_MODEL), BF16),
        compiler_params=pltpu.CompilerParams(
            dimension_semantics=("arbitrary",), vmem_limit_bytes=VMEM_LIMIT),
        name="moe_expert",
    )(tb["tile_expert"], tb["n_valid"], xs, w_gate, w_up, w_down)

    return pl.pallas_call(
        functools.partial(_combine_kernel, tm=tm),
        grid_spec=pltpu.PrefetchScalarGridSpec(
            num_scalar_prefetch=7, grid=(n_tiles,),
            in_specs=[tile_row(D_MODEL), tile_row(ROUTER_LANES), lbase_spec, ltri_spec,
                      pl.BlockSpec((1, D_MODEL), lambda i, *_: (0, 0)), hbm],
            out_specs=tile_row(D_MODEL),
            scratch_shapes=[pltpu.VMEM((2, LOCAL_ROWS, D_MODEL), BF16), pltpu.SemaphoreType.DMA((2,))]),
        out_shape=jax.ShapeDtypeStruct((n_tok, D_MODEL), F32),
        compiler_params=pltpu.CompilerParams(
            dimension_semantics=("arbitrary",), vmem_limit_bytes=VMEM_LIMIT),
        name="moe_combine",
    )(*tb["copies"], tb["tot"], x1, route, lbase_rows, ltri, g_final, ys)


def _rope_tables(seq, tm):
    pos = np.arange(seq, dtype=np.float64)
    inv_freq = ROPE_THETA ** (-np.arange(0, HEAD_DIM, 2, dtype=np.float64) / HEAD_DIM)
    ang = pos[:, None] * inv_freq[None, :]
    cos, sin = np.cos(ang), np.sin(ang)
    reps = CB // HEAD_DIM
    cos_t = np.tile(np.concatenate([cos, cos], axis=-1), (1, reps))
    sin_t = np.tile(np.concatenate([-sin, sin], axis=-1), (1, reps))

    def reorder(t, dil):
        return t.reshape(seq // tm, tm // dil, dil, CB).transpose(0, 2, 1, 3).reshape(seq, CB)

    dils = [dil for _, dil in DIL_GROUPS]
    return (jnp.asarray(np.stack([reorder(cos_t, dil) for dil in dils]).astype(np.float32)),
            jnp.asarray(np.stack([reorder(sin_t, dil) for dil in dils]).astype(np.float32)))


def kernel(x, w_in, b_in, sinks, w_proj_a, w_proj_b, w_out, g_mix, g_ffn, w_router_group, b_router_group,
           w_router_expert, b_router_expert, w_exp_gate, w_exp_up, w_exp_down, g_final):
    batch, seq, d = x.shape
    assert d == D_MODEL and w_in.shape[0] == 1, "single-layer kernel"
    n_tok = batch * seq
    x2 = x.reshape(n_tok, d)
    assert seq % ATTN_ROWS_PER_STEP == 0 and n_tok % MOE_TILE == 0 and seq % IN_TILE == 0
    cos_t, sin_t = _rope_tables(seq, IN_TILE)

    z_tok, z_d1, z_d2 = _in_proj(x2, g_mix[0][None, :], w_in[0].astype(BF16), b_in[0][None, :],
                                 cos_t, sin_t, seq=seq, tm=IN_TILE)

    a_bases = (0, A_BLOCKS, 2 * A_BLOCKS)
    tok_bases = tuple(ZB_A0 + b for b in a_bases)
    oa, lse = [], []
    for group, (z, bases) in enumerate(((z_tok, tok_bases), (z_d1, a_bases), (z_d2, a_bases))):
        o_g, l_g = _dilated_attention(z, bases, group, batch=batch, seq=seq, lq=ATTN_ROWS_PER_STEP)
        oa.append(o_g)
        lse.append(l_g)
    ob = _swa_attention(z_tok, sinks[0], batch=batch, seq=seq, lq=ATTN_ROWS_PER_STEP)

    gap = EXPERT_LANE0 - N_EXPERT_GROUPS
    tail = ROUTER_LANES - EXPERT_LANE0 - N_EXPERTS
    w_router = jnp.concatenate(
        [w_router_group[0].T, jnp.zeros((gap, d), F32), w_router_expert[0].T, jnp.zeros((tail, d), F32)], axis=0)
    b_router = jnp.concatenate(
        [b_router_group[0], jnp.zeros((gap,), F32), b_router_expert[0], jnp.zeros((tail,), F32)])[:, None]
    x1, h2, route, cnt_tiles = _post_attn(
        oa, lse, ob, z_tok, x2, w_proj_a[0].astype(BF16), w_proj_b[0].astype(BF16), w_out[0].astype(BF16),
        g_ffn[0][None, :], w_router, b_router, tm=MOE_TILE)

    out = _moe(h2, route, cnt_tiles, x1, w_exp_gate[0], w_exp_up[0], w_exp_down[0], g_final[None, :])
    return out.reshape(batch, seq, d)
```
